```python
import jax, jax.numpy as jnp
from jax import lax
import numpy as np

D_MODEL = 1024
BATCH = 4
SEQ = 4096
DEPTH = 1

CHUNK = 64
N_HEADS = 8
HEAD_DIM = 64
D_ATTN = N_HEADS * HEAD_DIM
LEFT_CHUNKS = 8
BAND = (LEFT_CHUNKS + 1) * CHUNK
REL_CLIP = 128
N_REL = 2 * REL_CLIP + 1
D_CONV = D_MODEL // 2
CONV_W = 31
N_BRANCH = 2
D_IN = 2 * D_CONV + 3 * D_ATTN + N_BRANCH * D_MODEL
PEER_HEADS = 8
PEER_NKEYS = 128
PEER_N = PEER_NKEYS * PEER_NKEYS
PEER_DQ = 256
PEER_DH = PEER_DQ // 2
PEER_TOPK = 16
PEER_BLOCK = 128
EPS = 1e-6
NEG_INF = -1e30

kernel_name = 'hybrid_conv_chunkattn_peer_adaln'


def rms_norm(x, g):
    xf = x.astype(jnp.float32)
    y = xf * lax.rsqrt(jnp.mean(xf * xf, axis=-1, keepdims=True) + EPS)
    return (y * g.astype(jnp.float32)).astype(x.dtype)


def layer_norm(x, g, b):
    xf = x.astype(jnp.float32)
    mu = jnp.mean(xf, axis=-1, keepdims=True)
    var = jnp.mean(jnp.square(xf - mu), axis=-1, keepdims=True)
    y = (xf - mu) * lax.rsqrt(var + EPS)
    return (y * g.astype(jnp.float32) + b.astype(jnp.float32)).astype(x.dtype)


def conv_module(a, b, conv_dw, conv_b, ln_g, ln_b, w_conv_out):
    u = a * jax.nn.sigmoid(b)
    y = lax.conv_general_dilated(
        u, conv_dw[:, None, :].astype(u.dtype), window_strides=(1,),
        padding=[(CONV_W - 1, 0)], dimension_numbers=('NWC', 'WIO', 'NWC'),
        feature_group_count=D_CONV) + conv_b
    y = jax.nn.silu(layer_norm(y, ln_g, ln_b))
    return y @ w_conv_out


def chunked_attention(q, k, v, q_norm_g, k_norm_g, rel_bias, w_attn_out):
    B, S, _ = q.shape
    nc = S // CHUNK
    q = rms_norm(q.reshape(B, S, N_HEADS, HEAD_DIM), q_norm_g)
    k = rms_norm(k.reshape(B, S, N_HEADS, HEAD_DIM), k_norm_g)
    v = v.reshape(B, S, N_HEADS, HEAD_DIM)
    qc = q.reshape(B, nc, CHUNK, N_HEADS, HEAD_DIM)
    pad = ((0, 0), (LEFT_CHUNKS * CHUNK, 0), (0, 0), (0, 0))
    kp = jnp.pad(k, pad).reshape(B, nc + LEFT_CHUNKS, CHUNK, N_HEADS, HEAD_DIM)
    vp = jnp.pad(v, pad).reshape(B, nc + LEFT_CHUNKS, CHUNK, N_HEADS, HEAD_DIM)
    band_idx = jnp.arange(nc)[:, None] + jnp.arange(LEFT_CHUNKS + 1)[None, :]
    kb = kp[:, band_idx].reshape(B, nc, BAND, N_HEADS, HEAD_DIM)
    vb = vp[:, band_idx].reshape(B, nc, BAND, N_HEADS, HEAD_DIM)
    s = jnp.einsum('bnqhd,bnkhd->bnhqk', qc, kb,
                   preferred_element_type=jnp.float32) * (HEAD_DIM ** -0.5)
    qpos = LEFT_CHUNKS * CHUNK + jnp.arange(CHUNK)
    kpos = jnp.arange(BAND)
    rel = jnp.clip(qpos[:, None] - kpos[None, :], -REL_CLIP, REL_CLIP) + REL_CLIP
    bias = rel_bias[:, rel].astype(jnp.float32)
    key_chunk = jnp.arange(nc)[:, None] - LEFT_CHUNKS + kpos[None, :] // CHUNK
    valid = key_chunk >= 0
    s = jnp.where(valid[None, :, None, None, :], s + bias[None, None], NEG_INF)
    p = jax.nn.softmax(s, axis=-1).astype(v.dtype)
    o = jnp.einsum('bnhqk,bnkhd->bnqhd', p, vb).reshape(B, S, D_ATTN)
    return o @ w_attn_out


def peer_ffn(h, peer_wq, peer_keys, peer_u, peer_v):
    B, S, D = h.shape
    T = B * S
    ht = h.reshape(T, D)
    q = (ht @ peer_wq).reshape(T, PEER_HEADS, 2, PEER_DH)
    s = jnp.einsum('thpd,hpkd->thpk', q, peer_keys,
                   preferred_element_type=jnp.float32)
    sv, si = lax.top_k(s, PEER_TOPK)
    cand = (sv[..., 0, :, None] + sv[..., 1, None, :]).reshape(T, PEER_HEADS, PEER_TOPK * PEER_TOPK)
    cidx = (si[..., 0, :, None] * PEER_NKEYS + si[..., 1, None, :]).reshape(T, PEER_HEADS, PEER_TOPK * PEER_TOPK)
    top, pos = lax.top_k(cand, PEER_TOPK)
    eidx = jnp.take_along_axis(cidx, pos, axis=-1)
    g = jax.nn.softmax(top, axis=-1).astype(h.dtype)
    nb = T // PEER_BLOCK

    def expert_block(args):
        hb, ib, gb = args
        u = peer_u[ib]
        a = jax.nn.gelu(jnp.einsum('td,thkd->thk', hb, u), approximate=False)
        return jnp.einsum('thk,thkd->td', gb * a, peer_v[ib])

    out = lax.map(expert_block, (ht.reshape(nb, PEER_BLOCK, D),
                                 eidx.reshape(nb, PEER_BLOCK, PEER_HEADS, PEER_TOPK),
                                 g.reshape(nb, PEER_BLOCK, PEER_HEADS, PEER_TOPK)))
    return out.reshape(B, S, D)


def setup_inputs(seed: int = 0) -> dict:
    key = jax.random.key(seed)
    ks = jax.random.split(key, 24)
    f32 = jnp.float32

    def nrm(k, shape, scale):
        return jax.random.normal(k, shape, f32) * scale

    L, D = DEPTH, D_MODEL
    return {
        'x': nrm(ks[0], (BATCH, SEQ, D), 1.0),
        'c': nrm(ks[1], (BATCH, D), 1.0),
        'w_ada': nrm(ks[2], (L, D, 6 * D), 0.5 * D ** -0.5),
        'b_ada': nrm(ks[3], (L, 6 * D), 0.02),
        'norm1_g': 1.0 + nrm(ks[4], (L, D), 0.1),
        'norm2_g': 1.0 + nrm(ks[5], (L, D), 0.1),
        'w_in': nrm(ks[6], (L, D, D_IN), D ** -0.5),
        'conv_dw': nrm(ks[7], (L, CONV_W, D_CONV), CONV_W ** -0.5),
        'conv_b': nrm(ks[8], (L, D_CONV), 0.02),
        'conv_ln_g': 1.0 + nrm(ks[9], (L, D_CONV), 0.1),
        'conv_ln_b': nrm(ks[10], (L, D_CONV), 0.02),
        'w_conv_out': nrm(ks[11], (L, D_CONV, D), D_CONV ** -0.5),
        'q_norm_g': 1.0 + nrm(ks[12], (L, HEAD_DIM), 0.1),
        'k_norm_g': 1.0 + nrm(ks[13], (L, HEAD_DIM), 0.1),
        'rel_bias': nrm(ks[14], (L, N_HEADS, N_REL), 0.5),
        'w_attn_out': nrm(ks[15], (L, D_ATTN, D), D_ATTN ** -0.5),
        'w_out': nrm(ks[16], (L, D, D), D ** -0.5),
        'peer_wq': nrm(ks[17], (L, D, PEER_HEADS * PEER_DQ), D ** -0.5),
        'peer_keys': nrm(ks[18], (L, PEER_HEADS, 2, PEER_NKEYS, PEER_DH), PEER_DH ** -0.5),
        'peer_u': nrm(ks[19], (L, PEER_N, D), D ** -0.5),
        'peer_v': nrm(ks[20], (L, PEER_N, D), PEER_HEADS ** -0.5),
    }


def reference(x, c, w_ada, b_ada, norm1_g, norm2_g, w_in, conv_dw, conv_b, conv_ln_g,
              conv_ln_b, w_conv_out, q_norm_g, k_norm_g, rel_bias, w_attn_out, w_out,
              peer_wq, peer_keys, peer_u, peer_v):
    split_pts = [D_CONV, 2 * D_CONV, 2 * D_CONV + D_ATTN,
                 2 * D_CONV + 2 * D_ATTN, 2 * D_CONV + 3 * D_ATTN, 2 * D_CONV + 3 * D_ATTN + D_MODEL]
    cond = jax.nn.silu(c)
    for l in range(DEPTH):
        mod = (cond @ w_ada[l] + b_ada[l])[:, None, :]
        sh1, sc1, ga1, sh2, sc2, ga2 = jnp.split(mod, 6, axis=-1)
        h = rms_norm(x, norm1_g[l]) * (1.0 + sc1) + sh1
        proj = h @ w_in[l]
        glu_a, glu_b, q, k, v, gl_conv, gl_attn = jnp.split(proj, split_pts, axis=-1)
        y_conv = conv_module(glu_a, glu_b, conv_dw[l], conv_b[l], conv_ln_g[l],
                             conv_ln_b[l], w_conv_out[l])
        y_attn = chunked_attention(q, k, v, q_norm_g[l], k_norm_g[l], rel_bias[l], w_attn_out[l])
        merged = jax.nn.sigmoid(gl_conv) * y_conv + jax.nn.sigmoid(gl_attn) * y_attn
        x = x + ga1 * (merged @ w_out[l])
        h2 = rms_norm(x, norm2_g[l]) * (1.0 + sc2) + sh2
        x = x + ga2 * peer_ffn(h2, peer_wq[l], peer_keys[l], peer_u[l], peer_v[l])
    return x
```

```python
import functools

import jax
import jax.numpy as jnp
import numpy as np
from jax import lax
from jax.experimental import pallas as pl
from jax.experimental.pallas import tpu as pltpu

F32 = jnp.float32
BF16 = jnp.bfloat16

D_MODEL = 1024
CHUNK = 64
N_HEADS = 8
HEAD_DIM = 64
D_ATTN = N_HEADS * HEAD_DIM
LEFT_CHUNKS = 8
REL_CLIP = 128
D_CONV = D_MODEL // 2
CONV_W = 31
PEER_HEADS = 8
PEER_NKEYS = 128
PEER_N = PEER_NKEYS * PEER_NKEYS
PEER_DH = 128
PEER_TOPK = 16
EPS = 1e-6
NEG_INF = -1e30

LANES = 128
VMEM_LIMIT = 56 * 1024 * 1024

TM_IN = 256
TS_CONV = 512
HALO = 32
TQ = 256
NKB = 3
TM_MERGE = 256
TB_ROUTE = 512
TB_EXP = 512
EC = 1024


def _cparams(sem):
    return pltpu.CompilerParams(dimension_semantics=sem, vmem_limit_bytes=VMEM_LIMIT)


def _ada_kernel(c_ref, w_ref, b_ref, o_ref):
    c = c_ref[...]
    cond = c * jax.nn.sigmoid(c)
    o_ref[...] = jnp.dot(cond.astype(BF16), w_ref[...].astype(BF16),
                         preferred_element_type=F32) + b_ref[...]


def _ada(c, w, b):
    bsz, d = c.shape
    n = w.shape[1]
    tn = 1024
    return pl.pallas_call(
        _ada_kernel,
        grid=(n // tn,),
        in_specs=[pl.BlockSpec((bsz, d), lambda j: (0, 0)),
                  pl.BlockSpec((d, tn), lambda j: (0, j)),
                  pl.BlockSpec((1, tn), lambda j: (0, j))],
        out_specs=pl.BlockSpec((bsz, tn), lambda j: (0, j)),
        out_shape=jax.ShapeDtypeStruct((bsz, n), F32),
        compiler_params=_cparams(("arbitrary",)),
        name="ada",
    )(c, w, b.reshape(1, n))


def _head_rms(t, bd_ref, gain):
    t2 = t * t
    hi = t2.astype(BF16)
    lo = (t2 - hi.astype(F32)).astype(BF16)
    ms = (jnp.dot(hi, bd_ref[...], preferred_element_type=F32)
          + jnp.dot(lo, bd_ref[...], preferred_element_type=F32))
    return t * lax.rsqrt(ms + EPS) * gain


def _inproj_kernel(x_ref, g_ref, sc_ref, sh_ref, w_ref, bd_ref, qg_ref, kg_ref,
                   u_ref, q_ref, k_ref, v_ref, gc_ref, ga_ref):
    x = x_ref[0]
    ms = jnp.mean(x * x, axis=-1, keepdims=True)
    h = (x * lax.rsqrt(ms + EPS)) * g_ref[...]
    h = h * (1.0 + sc_ref[0]) + sh_ref[0]
    hb = h.astype(BF16)

    def seg(lo, hi):
        return jnp.dot(hb, w_ref[:, lo:hi], preferred_element_type=F32)

    o = 0
    a = seg(o, o + D_CONV); o += D_CONV
    b = seg(o, o + D_CONV); o += D_CONV
    u_ref[0] = a * jax.nn.sigmoid(b)
    q = seg(o, o + D_ATTN); o += D_ATTN
    q_ref[0] = (_head_rms(q, bd_ref, qg_ref[...]) * (HEAD_DIM ** -0.5)).astype(BF16)
    k = seg(o, o + D_ATTN); o += D_ATTN
    k_ref[0] = _head_rms(k, bd_ref, kg_ref[...]).astype(BF16)
    v_ref[0] = seg(o, o + D_ATTN).astype(BF16); o += D_ATTN
    gc_ref[0] = jax.nn.sigmoid(seg(o, o + D_MODEL)); o += D_MODEL
    ga_ref[0] = jax.nn.sigmoid(seg(o, o + D_MODEL))


def _inproj(x, g1, sc1, sh1, w_in, bd, qg, kg):
    bsz, s, d = x.shape
    tm = TM_IN
    n_in = w_in.shape[1]
    row = lambda w: pl.BlockSpec((1, tm, w), lambda b, i: (b, i, 0))
    vec = lambda w: pl.BlockSpec((1, w), lambda b, i: (0, 0))
    mod = pl.BlockSpec((1, 1, d), lambda b, i: (b, 0, 0))
    return pl.pallas_call(
        _inproj_kernel,
        grid=(bsz, s // tm),
        in_specs=[row(d), vec(d), mod, mod,
                  pl.BlockSpec((d, n_in), lambda b, i: (0, 0)),
                  pl.BlockSpec((D_ATTN, D_ATTN), lambda b, i: (0, 0)),
                  vec(D_ATTN), vec(D_ATTN)],
        out_specs=[row(D_CONV), row(D_ATTN), row(D_ATTN), row(D_ATTN), row(d), row(d)],
        out_shape=[jax.ShapeDtypeStruct((bsz, s, D_CONV), F32),
                   jax.ShapeDtypeStruct((bsz, s, D_ATTN), BF16),
                   jax.ShapeDtypeStruct((bsz, s, D_ATTN), BF16),
                   jax.ShapeDtypeStruct((bsz, s, D_ATTN), BF16),
                   jax.ShapeDtypeStruct((bsz, s, d), F32),
                   jax.ShapeDtypeStruct((bsz, s, d), F32)],
        compiler_params=_cparams(("arbitrary", "arbitrary")),
        name="inproj",
    )(x, g1, sc1, sh1, w_in, bd, qg, kg)


CONV_ROWS = 64


def _conv_kernel(u_ref, up_ref, dw_ref, cb_ref, lg_ref, lb_ref, w_ref, gc_ref,
                 o_ref, ext_ref, y_ref):
    i = pl.program_id(1)
    ts = u_ref.shape[1]
    prev = up_ref[0, ts - HALO:, :]
    ext_ref[0:HALO, :] = jnp.where(i > 0, prev, 0.0)
    ext_ref[HALO:, :] = u_ref[0]
    base = HALO - (CONV_W - 1)
    for r0 in range(0, ts, CONV_ROWS):
        acc = jnp.zeros((CONV_ROWS, D_CONV), F32) + cb_ref[...]
        for w in range(CONV_W):
            acc = acc + ext_ref[r0 + base + w:r0 + base + w + CONV_ROWS, :] * dw_ref[w:w + 1, :]
        y_ref[r0:r0 + CONV_ROWS, :] = acc
    y = y_ref[...]
    mu = jnp.mean(y, axis=-1, keepdims=True)
    yc = y - mu
    var = jnp.mean(yc * yc, axis=-1, keepdims=True)
    z = yc * lax.rsqrt(var + EPS) * lg_ref[...] + lb_ref[...]
    z = z * jax.nn.sigmoid(z)
    o = jnp.dot(z.astype(BF16), w_ref[...], preferred_element_type=F32)
    o_ref[0] = gc_ref[0] * o


def _conv(u, dw, cb, lg, lb, w_co, gc):
    bsz, s, dc = u.shape
    d = w_co.shape[1]
    ts = TS_CONV
    vec = lambda w: pl.BlockSpec((1, w), lambda b, i: (0, 0))
    return pl.pallas_call(
        _conv_kernel,
        grid=(bsz, s // ts),
        in_specs=[pl.BlockSpec((1, ts, dc), lambda b, i: (b, i, 0)),
                  pl.BlockSpec((1, ts, dc), lambda b, i: (b, jnp.maximum(i - 1, 0), 0)),
                  pl.BlockSpec((CONV_W, dc), lambda b, i: (0, 0)),
                  vec(dc), vec(dc), vec(dc),
                  pl.BlockSpec((dc, d), lambda b, i: (0, 0)),
                  pl.BlockSpec((1, ts, d), lambda b, i: (b, i, 0))],
        out_specs=pl.BlockSpec((1, ts, d), lambda b, i: (b, i, 0)),
        out_shape=jax.ShapeDtypeStruct((bsz, s, d), F32),
        scratch_shapes=[pltpu.VMEM((ts + HALO, dc), F32), pltpu.VMEM((ts, dc), F32)],
        compiler_params=_cparams(("arbitrary", "arbitrary")),
        name="conv",
    )(u, u, dw, cb, lg, lb, w_co, gc)


def _attn_kernel(q_ref, k0_ref, k1_ref, k2_ref, v0_ref, v1_ref, v2_ref, bias_ref, o_ref):
    i = pl.program_id(1)
    k_refs = (k0_ref, k1_ref, k2_ref)
    v_refs = (v0_ref, v1_ref, v2_ref)
    pens = [jnp.where(i - (NKB - 1) + j >= 0, 0.0, NEG_INF).astype(F32) for j in range(NKB)]
    for h in range(N_HEADS):
        lo = h * HEAD_DIM
        qh = q_ref[0, :, lo:lo + HEAD_DIM]
        ss = []
        for j in range(NKB):
            kh = k_refs[j][0, :, lo:lo + HEAD_DIM]
            s = lax.dot_general(qh, kh, (((1,), (1,)), ((), ())), preferred_element_type=F32)
            ss.append(s + bias_ref[h, :, j * TQ:(j + 1) * TQ] + pens[j])
        m = jnp.maximum(jnp.maximum(jnp.max(ss[0], axis=-1, keepdims=True),
                                    jnp.max(ss[1], axis=-1, keepdims=True)),
                        jnp.max(ss[2], axis=-1, keepdims=True))
        l = jnp.zeros_like(m)
        acc = jnp.zeros((TQ, HEAD_DIM), F32)
        for j in range(NKB):
            p = jnp.exp(ss[j] - m)
            l = l + jnp.sum(p, axis=-1, keepdims=True)
            vh = v_refs[j][0, :, lo:lo + HEAD_DIM]
            acc = acc + jnp.dot(p.astype(BF16), vh, preferred_element_type=F32)
        o_ref[0, :, lo:lo + HEAD_DIM] = (acc / l).astype(BF16)


def _attn(q, k, v, bias):
    bsz, s, da = q.shape
    kspec = lambda j: pl.BlockSpec(
        (1, TQ, da), lambda b, i: (b, jnp.maximum(i - (NKB - 1) + j, 0), 0))
    return pl.pallas_call(
        _attn_kernel,
        grid=(bsz, s // TQ),
        in_specs=[pl.BlockSpec((1, TQ, da), lambda b, i: (b, i, 0)),
                  kspec(0), kspec(1), kspec(2), kspec(0), kspec(1), kspec(2),
                  pl.BlockSpec((N_HEADS, TQ, NKB * TQ), lambda b, i: (0, 0, 0))],
        out_specs=pl.BlockSpec((1, TQ, da), lambda b, i: (b, i, 0)),
        out_shape=jax.ShapeDtypeStruct((bsz, s, da), BF16),
        compiler_params=_cparams(("arbitrary", "arbitrary")),
        name="attn",
    )(q, k, k, k, v, v, v, bias)


def _bias_table(rel_bias):
    qi = np.arange(TQ)[:, None]
    kj = np.arange(NKB * TQ)[None, :]
    d = qi + (NKB - 1) * TQ - kj
    rel = np.clip(d, -REL_CLIP, REL_CLIP) + REL_CLIP
    qc = qi // CHUNK + (NKB - 1) * TQ // CHUNK
    kc = kj // CHUNK
    band = (kc >= qc - LEFT_CHUNKS) & (kc <= qc)
    tab = rel_bias[:, rel].astype(F32)
    return jnp.where(band[None], tab, NEG_INF)


def _merge_kernel(o_ref, mc_ref, ga_ref, x_ref, wa_ref, wo_ref, ga1_ref, g2_ref, sc_ref, sh_ref,
                  x1_ref, h2t_ref):
    ya = jnp.dot(o_ref[0], wa_ref[...], preferred_element_type=F32)
    merged = mc_ref[0] + ga_ref[0] * ya
    y = jnp.dot(merged.astype(BF16), wo_ref[...], preferred_element_type=F32)
    x1 = x_ref[0] + ga1_ref[0] * y
    x1_ref[0] = x1
    ms = jnp.mean(x1 * x1, axis=-1, keepdims=True)
    h2 = (x1 * lax.rsqrt(ms + EPS)) * g2_ref[...]
    h2 = h2 * (1.0 + sc_ref[0]) + sh_ref[0]
    h2t_ref[...] = h2.T.astype(BF16)


def _merge(o, mc, ga, x, wa, wo, ga1, g2, sc2, sh2):
    bsz, s, d = x.shape
    tm = TM_MERGE
    nb = s // tm
    row = lambda w: pl.BlockSpec((1, tm, w), lambda b, i: (b, i, 0))
    mod = pl.BlockSpec((1, 1, d), lambda b, i: (b, 0, 0))
    return pl.pallas_call(
        _merge_kernel,
        grid=(bsz, nb),
        in_specs=[row(D_ATTN), row(d), row(d), row(d),
                  pl.BlockSpec((D_ATTN, d), lambda b, i: (0, 0)),
                  pl.BlockSpec((d, d), lambda b, i: (0, 0)),
                  mod, pl.BlockSpec((1, d), lambda b, i: (0, 0)), mod, mod],
        out_specs=[row(d), pl.BlockSpec((d, tm), lambda b, i: (0, b * nb + i))],
        out_shape=[jax.ShapeDtypeStruct((bsz, s, d), F32),
                   jax.ShapeDtypeStruct((d, bsz * s), BF16)],
        compiler_params=_cparams(("arbitrary", "arbitrary")),
        name="merge",
    )(o, mc, ga, x, wa, wo, ga1, g2, sc2, sh2)


_CAND = [(a, b) for a in range(PEER_TOPK) for b in range(PEER_TOPK)
         if (a + 1) * (b + 1) <= PEER_TOPK]


def _route_kernel(h2t_ref, wqt_ref, keys_ref, rank1_ref, limit_ref, e0_ref, e1_ref,
                  s_scr, rank_scr, v_scr, b_scr, zi_scr):
    tb = h2t_ref.shape[1]
    ncol = tb // LANES
    n_hp = 2 * PEER_HEADS
    qt = jnp.dot(wqt_ref[...], h2t_ref[...], preferred_element_type=F32).astype(BF16)
    for hp in range(n_hp):
        s_scr[hp] = jnp.dot(keys_ref[hp], qt[hp * PEER_DH:(hp + 1) * PEER_DH, :],
                            preferred_element_type=F32)

    def stage1(it, carry):
        hp = it // ncol
        col = it % ncol
        cs = pl.ds(pl.multiple_of(col * LANES, LANES), LANES)
        work = s_scr[hp, :, cs]
        rank = jnp.full(work.shape, float(PEER_TOPK), F32)
        for r in range(PEER_TOPK):
            m = jnp.max(work, axis=0, keepdims=True)
            eq = work == m
            rank = jnp.where(eq, float(r), rank)
            work = jnp.where(eq, -jnp.inf, work)
            v_scr[hp, r, pl.ds(col, 1), :] = m
        rank_scr[hp, :, cs] = rank
        return carry

    lax.fori_loop(0, n_hp * ncol, stage1, 0)

    def stage2(h, carry):
        v0 = [v_scr[2 * h, a] for a in range(PEER_TOPK)]
        v1 = [v_scr[2 * h + 1, b] for b in range(PEER_TOPK)]
        cand = [v0[a] + v1[b] for (a, b) in _CAND]
        top = cand[0]
        work = list(cand)
        for r in range(PEER_TOPK):
            m = functools.reduce(jnp.maximum, work)
            work = [jnp.where(w == m, -jnp.inf, w) for w in work]
        z = jnp.zeros_like(top)
        cnt = [jnp.zeros_like(top) for _ in range(PEER_TOPK)]
        for (a, b), c, w in zip(_CAND, cand, work):
            sel = w == -jnp.inf
            z = z + jnp.where(sel, jnp.exp(c - top), 0.0)
            cnt[a] = cnt[a] + jnp.where(sel, 1.0, 0.0)
        for a in range(PEER_TOPK):
            b_scr[h, a] = cnt[a]
        zi_scr[h] = 1.0 / z
        return carry

    lax.fori_loop(0, PEER_HEADS, stage2, 0)

    def stage3(it, carry):
        h = it // ncol
        col = it % ncol
        cs = pl.ds(pl.multiple_of(col * LANES, LANES), LANES)
        rw = pl.ds(col, 1)
        rank0 = rank_scr[2 * h, :, cs]
        limit = jnp.zeros(rank0.shape, F32)
        for a in range(PEER_TOPK):
            limit = jnp.where(rank0 == float(a), b_scr[h, a, rw, :], limit)
        limit_ref[h, :, cs] = limit
        rank1_ref[h, :, cs] = rank_scr[2 * h + 1, :, cs]
        e0_ref[h, :, cs] = jnp.exp(s_scr[2 * h, :, cs] - v_scr[2 * h, 0, rw, :])
        e1_ref[h, :, cs] = (jnp.exp(s_scr[2 * h + 1, :, cs] - v_scr[2 * h + 1, 0, rw, :])
                            * zi_scr[h, rw, :])
        return carry

    lax.fori_loop(0, PEER_HEADS * ncol, stage3, 0)


def _route(h2t, wqt, keys):
    d, t = h2t.shape
    tb = TB_ROUTE
    ncol = tb // LANES
    n_hp = 2 * PEER_HEADS
    tab = pl.BlockSpec((PEER_HEADS, PEER_NKEYS, tb), lambda i: (0, 0, i))
    tab_shape = jax.ShapeDtypeStruct((PEER_HEADS, PEER_NKEYS, t), F32)
    return pl.pallas_call(
        _route_kernel,
        grid=(t // tb,),
        in_specs=[pl.BlockSpec((d, tb), lambda i: (0, i)),
                  pl.BlockSpec(wqt.shape, lambda i: (0, 0)),
                  pl.BlockSpec(keys.shape, lambda i: (0, 0, 0))],
        out_specs=[tab, tab, tab, tab],
        out_shape=[tab_shape] * 4,
        scratch_shapes=[pltpu.VMEM((n_hp, PEER_NKEYS, tb), F32),
                        pltpu.VMEM((n_hp, PEER_NKEYS, tb), F32),
                        pltpu.VMEM((n_hp, PEER_TOPK, ncol, LANES), F32),
                        pltpu.VMEM((PEER_HEADS, PEER_TOPK, ncol, LANES), F32),
                        pltpu.VMEM((PEER_HEADS, ncol, LANES), F32)],
        compiler_params=_cparams(("arbitrary",)),
        name="route",
    )(h2t, wqt, keys)


_INV_SQRT2 = float(1.0 / np.sqrt(2.0))


def _experts_kernel(h2t_ref, u_ref, vt_ref, rank1_ref, limit_ref, e0_ref, e1_ref,
                    x1_ref, ga2_ref, o_ref, a_scr, p_scr, acc_ref):
    c = pl.program_id(1)
    tb = h2t_ref.shape[1]
    ncol = tb // LANES
    rows_per_step = EC // PEER_NKEYS

    @pl.when(c == 0)
    def _():
        acc_ref[...] = jnp.zeros_like(acc_ref)

    a_scr[...] = jnp.dot(u_ref[...], h2t_ref[...], preferred_element_type=F32)

    def col_body(col, carry):
        cs = pl.ds(pl.multiple_of(col * LANES, LANES), LANES)
        for ii in range(rows_per_step):
            row = slice(ii, ii + 1)
            g = jnp.zeros((PEER_NKEYS, LANES), F32)
            for h in range(PEER_HEADS):
                sel = rank1_ref[h, :, cs] < limit_ref[h, row, cs]
                g = g + jnp.where(sel, e1_ref[h, :, cs], 0.0) * e0_ref[h, row, cs]
            a = a_scr[ii * PEER_NKEYS:(ii + 1) * PEER_NKEYS, cs]
            act = 0.5 * a * (1.0 + lax.erf(a * _INV_SQRT2))
            p_scr[ii * PEER_NKEYS:(ii + 1) * PEER_NKEYS, cs] = (act * g).astype(BF16)
        return carry

    lax.fori_loop(0, ncol, col_body, 0)
    acc_ref[...] += jnp.dot(vt_ref[...], p_scr[...], preferred_element_type=F32)

    @pl.when(c == pl.num_programs(1) - 1)
    def _():
        o_ref[...] = x1_ref[...] + ga2_ref[0] * acc_ref[...].T


def _experts(h2t, u, vt, rank1, limit, e0, e1, x1, ga2, seq):
    d, t = h2t.shape
    tb = TB_EXP
    n_e = u.shape[0]
    per_batch = seq // tb
    tab = pl.BlockSpec((PEER_HEADS, PEER_NKEYS, tb), lambda i, c: (0, 0, i))
    rowtab = pl.BlockSpec((PEER_HEADS, EC // PEER_NKEYS, tb), lambda i, c: (0, c, i))
    return pl.pallas_call(
        _experts_kernel,
        grid=(t // tb, n_e // EC),
        in_specs=[pl.BlockSpec((d, tb), lambda i, c: (0, i)),
                  pl.BlockSpec((EC, d), lambda i, c: (c, 0)),
                  pl.BlockSpec((d, EC), lambda i, c: (0, c)),
                  tab, rowtab, rowtab, tab,
                  pl.BlockSpec((tb, d), lambda i, c: (i, 0)),
                  pl.BlockSpec((1, 1, d), lambda i, c: (i // per_batch, 0, 0))],
        out_specs=pl.BlockSpec((tb, d), lambda i, c: (i, 0)),
        out_shape=jax.ShapeDtypeStruct((t, d), F32),
        scratch_shapes=[pltpu.VMEM((EC, tb), F32), pltpu.VMEM((EC, tb), BF16),
                        pltpu.VMEM((d, tb), F32)],
        compiler_params=_cparams(("arbitrary", "arbitrary")),
        name="experts",
    )(h2t, u, vt, rank1, limit, e0, e1, x1, ga2)


def kernel(x, c, w_ada, b_ada, norm1_g, norm2_g, w_in, conv_dw, conv_b, conv_ln_g, conv_ln_b,
           w_conv_out, q_norm_g, k_norm_g, rel_bias, w_attn_out, w_out, peer_wq, peer_keys,
           peer_u, peer_v):
    bsz, s, d = x.shape
    depth = w_ada.shape[0]
    bd = jnp.asarray(np.kron(np.eye(N_HEADS), np.full((HEAD_DIM, HEAD_DIM), 1.0 / HEAD_DIM)), BF16)
    for l in range(depth):
        mod = _ada(c, w_ada[l], b_ada[l])
        sh1, sc1, ga1, sh2, sc2, ga2 = [m.reshape(bsz, 1, d) for m in jnp.split(mod, 6, axis=-1)]
        u, q, k, v, gc, ga = _inproj(
            x, norm1_g[l].reshape(1, d), sc1, sh1, w_in[l].astype(BF16), bd,
            jnp.tile(q_norm_g[l], N_HEADS).reshape(1, D_ATTN),
            jnp.tile(k_norm_g[l], N_HEADS).reshape(1, D_ATTN))
        mc = _conv(u, conv_dw[l], conv_b[l].reshape(1, D_CONV), conv_ln_g[l].reshape(1, D_CONV),
                   conv_ln_b[l].reshape(1, D_CONV), w_conv_out[l].astype(BF16), gc)
        o = _attn(q, k, v, _bias_table(rel_bias[l]))
        x1, h2t = _merge(o, mc, ga, x, w_attn_out[l].astype(BF16), w_out[l].astype(BF16),
                         ga1, norm2_g[l].reshape(1, d), sc2, sh2)
        wqt = peer_wq[l].T.astype(BF16)
        keys = peer_keys[l].reshape(2 * PEER_HEADS, PEER_NKEYS, PEER_DH).astype(BF16)
        rank1, limit, e0, e1 = _route(h2t, wqt, keys)
        out = _experts(h2t, peer_u[l].astype(BF16), peer_v[l].T.astype(BF16),
                       rank1, limit, e0, e1, x1.reshape(bsz * s, d), ga2, s)
        x = out.reshape(bsz, s, d)
    return x
```

```python
import functools

import jax
import jax.numpy as jnp
import numpy as np
from jax import lax
from jax.experimental import pallas as pl
from jax.experimental.pallas import tpu as pltpu

F32 = jnp.float32
BF16 = jnp.bfloat16

D_MODEL = 1024
CHUNK = 64
N_HEADS = 8
HEAD_DIM = 64
D_ATTN = N_HEADS * HEAD_DIM
LEFT_CHUNKS = 8
REL_CLIP = 128
D_CONV = D_MODEL // 2
CONV_W = 31
PEER_HEADS = 8
PEER_NKEYS = 128
PEER_N = PEER_NKEYS * PEER_NKEYS
PEER_DH = 128
PEER_TOPK = 16
EPS = 1e-6
NEG_INF = -1e30

LANES = 128
VMEM_LIMIT = 56 * 1024 * 1024

TM_IN = 256
TS_CONV = 512
HALO = 32
TQ = 256
NKB = 3
TM_MERGE = 256
TB_ROUTE = 512
TB_EXP = 512
EC = 1024


def _cparams(sem):
    return pltpu.CompilerParams(dimension_semantics=sem, vmem_limit_bytes=VMEM_LIMIT)


def _ada_kernel(c_ref, w_ref, b_ref, o_ref):
    c = c_ref[...]
    cond = c * jax.nn.sigmoid(c)
    o_ref[...] = jnp.dot(cond.astype(BF16), w_ref[...].astype(BF16),
                         preferred_element_type=F32) + b_ref[...]


def _ada(c, w, b):
    bsz, d = c.shape
    n = w.shape[1]
    tn = 1024
    return pl.pallas_call(
        _ada_kernel,
        grid=(n // tn,),
        in_specs=[pl.BlockSpec((bsz, d), lambda j: (0, 0)),
                  pl.BlockSpec((d, tn), lambda j: (0, j)),
                  pl.BlockSpec((1, tn), lambda j: (0, j))],
        out_specs=pl.BlockSpec((bsz, tn), lambda j: (0, j)),
        out_shape=jax.ShapeDtypeStruct((bsz, n), F32),
        compiler_params=_cparams(("arbitrary",)),
        name="ada",
    )(c, w, b.reshape(1, n))


def _head_rms(t, bd_ref, gain):
    t2 = t * t
    hi = t2.astype(BF16)
    lo = (t2 - hi.astype(F32)).astype(BF16)
    ms = (jnp.dot(hi, bd_ref[...], preferred_element_type=F32)
          + jnp.dot(lo, bd_ref[...], preferred_element_type=F32))
    return t * lax.rsqrt(ms + EPS) * gain


def _inproj_kernel(x_ref, g_ref, sc_ref, sh_ref, w_ref, bd_ref, qg_ref, kg_ref,
                   u_ref, q_ref, k_ref, v_ref, gc_ref, ga_ref):
    x = x_ref[0]
    ms = jnp.mean(x * x, axis=-1, keepdims=True)
    h = (x * lax.rsqrt(ms + EPS)) * g_ref[...]
    h = h * (1.0 + sc_ref[0]) + sh_ref[0]
    hb = h.astype(BF16)

    def seg(lo, hi):
        return jnp.dot(hb, w_ref[:, lo:hi], preferred_element_type=F32)

    o = 0
    a = seg(o, o + D_CONV); o += D_CONV
    b = seg(o, o + D_CONV); o += D_CONV
    u_ref[0] = a * jax.nn.sigmoid(b)
    q = seg(o, o + D_ATTN); o += D_ATTN
    q_ref[0] = (_head_rms(q, bd_ref, qg_ref[...]) * (HEAD_DIM ** -0.5)).astype(BF16)
    k = seg(o, o + D_ATTN); o += D_ATTN
    k_ref[0] = _head_rms(k, bd_ref, kg_ref[...]).astype(BF16)
    v_ref[0] = seg(o, o + D_ATTN).astype(BF16); o += D_ATTN
    gc_ref[0] = jax.nn.sigmoid(seg(o, o + D_MODEL)); o += D_MODEL
    ga_ref[0] = jax.nn.sigmoid(seg(o, o + D_MODEL))


def _inproj(x, g1, sc1, sh1, w_in, bd, qg, kg):
    bsz, s, d = x.shape
    tm = TM_IN
    n_in = w_in.shape[1]
    row = lambda w: pl.BlockSpec((1, tm, w), lambda b, i: (b, i, 0))
    vec = lambda w: pl.BlockSpec((1, w), lambda b, i: (0, 0))
    mod = pl.BlockSpec((1, 1, d), lambda b, i: (b, 0, 0))
    return pl.pallas_call(
        _inproj_kernel,
        grid=(bsz, s // tm),
        in_specs=[row(d), vec(d), mod, mod,
                  pl.BlockSpec((d, n_in), lambda b, i: (0, 0)),
                  pl.BlockSpec((D_ATTN, D_ATTN), lambda b, i: (0, 0)),
                  vec(D_ATTN), vec(D_ATTN)],
        out_specs=[row(D_CONV), row(D_ATTN), row(D_ATTN), row(D_ATTN), row(d), row(d)],
        out_shape=[jax.ShapeDtypeStruct((bsz, s, D_CONV), F32),
                   jax.ShapeDtypeStruct((bsz, s, D_ATTN), BF16),
                   jax.ShapeDtypeStruct((bsz, s, D_ATTN), BF16),
                   jax.ShapeDtypeStruct((bsz, s, D_ATTN), BF16),
                   jax.ShapeDtypeStruct((bsz, s, d), F32),
                   jax.ShapeDtypeStruct((bsz, s, d), F32)],
        compiler_params=_cparams(("arbitrary", "arbitrary")),
        name="inproj",
    )(x, g1, sc1, sh1, w_in, bd, qg, kg)


CONV_ROWS = 64


def _conv_kernel(u_ref, up_ref, dw_ref, cb_ref, lg_ref, lb_ref, w_ref, gc_ref,
                 o_ref, ext_ref, y_ref):
    i = pl.program_id(1)
    ts = u_ref.shape[1]
    prev = up_ref[0, ts - HALO:, :]
    ext_ref[0:HALO, :] = jnp.where(i > 0, prev, 0.0)
    ext_ref[HALO:, :] = u_ref[0]
    base = HALO - (CONV_W - 1)
    for r0 in range(0, ts, CONV_ROWS):
        acc = jnp.zeros((CONV_ROWS, D_CONV), F32) + cb_ref[...]
        for w in range(CONV_W):
            acc = acc + ext_ref[r0 + base + w:r0 + base + w + CONV_ROWS, :] * dw_ref[w:w + 1, :]
        y_ref[r0:r0 + CONV_ROWS, :] = acc
    y = y_ref[...]
    mu = jnp.mean(y, axis=-1, keepdims=True)
    yc = y - mu
    var = jnp.mean(yc * yc, axis=-1, keepdims=True)
    z = yc * lax.rsqrt(var + EPS) * lg_ref[...] + lb_ref[...]
    z = z * jax.nn.sigmoid(z)
    o = jnp.dot(z.astype(BF16), w_ref[...], preferred_element_type=F32)
    o_ref[0] = gc_ref[0] * o


def _conv(u, dw, cb, lg, lb, w_co, gc):
    bsz, s, dc = u.shape
    d = w_co.shape[1]
    ts = TS_CONV
    vec = lambda w: pl.BlockSpec((1, w), lambda b, i: (0, 0))
    return pl.pallas_call(
        _conv_kernel,
        grid=(bsz, s // ts),
        in_specs=[pl.BlockSpec((1, ts, dc), lambda b, i: (b, i, 0)),
                  pl.BlockSpec((1, ts, dc), lambda b, i: (b, jnp.maximum(i - 1, 0), 0)),
                  pl.BlockSpec((CONV_W, dc), lambda b, i: (0, 0)),
                  vec(dc), vec(dc), vec(dc),
                  pl.BlockSpec((dc, d), lambda b, i: (0, 0)),
                  pl.BlockSpec((1, ts, d), lambda b, i: (b, i, 0))],
        out_specs=pl.BlockSpec((1, ts, d), lambda b, i: (b, i, 0)),
        out_shape=jax.ShapeDtypeStruct((bsz, s, d), F32),
        scratch_shapes=[pltpu.VMEM((ts + HALO, dc), F32), pltpu.VMEM((ts, dc), F32)],
        compiler_params=_cparams(("arbitrary", "arbitrary")),
        name="conv",
    )(u, u, dw, cb, lg, lb, w_co, gc)


def _attn_kernel(q_ref, k0_ref, k1_ref, k2_ref, v0_ref, v1_ref, v2_ref, bias_ref, o_ref):
    i = pl.program_id(1)
    k_refs = (k0_ref, k1_ref, k2_ref)
    v_refs = (v0_ref, v1_ref, v2_ref)
    pens = [jnp.where(i - (NKB - 1) + j >= 0, 0.0, NEG_INF).astype(F32) for j in range(NKB)]
    for h in range(N_HEADS):
        lo = h * HEAD_DIM
        qh = q_ref[0, :, lo:lo + HEAD_DIM]
        ss = []
        for j in range(NKB):
            kh = k_refs[j][0, :, lo:lo + HEAD_DIM]
            s = lax.dot_general(qh, kh, (((1,), (1,)), ((), ())), preferred_element_type=F32)
            ss.append(s + bias_ref[h, :, j * TQ:(j + 1) * TQ] + pens[j])
        m = jnp.maximum(jnp.maximum(jnp.max(ss[0], axis=-1, keepdims=True),
                                    jnp.max(ss[1], axis=-1, keepdims=True)),
                        jnp.max(ss[2], axis=-1, keepdims=True))
        l = jnp.zeros_like(m)
        acc = jnp.zeros((TQ, HEAD_DIM), F32)
        for j in range(NKB):
            p = jnp.exp(ss[j] - m)
            l = l + jnp.sum(p, axis=-1, keepdims=True)
            vh = v_refs[j][0, :, lo:lo + HEAD_DIM]
            acc = acc + jnp.dot(p.astype(BF16), vh, preferred_element_type=F32)
        o_ref[0, :, lo:lo + HEAD_DIM] = (acc / l).astype(BF16)


def _attn(q, k, v, bias):
    bsz, s, da = q.shape
    kspec = lambda j: pl.BlockSpec(
        (1, TQ, da), lambda b, i: (b, jnp.maximum(i - (NKB - 1) + j, 0), 0))
    return pl.pallas_call(
        _attn_kernel,
        grid=(bsz, s // TQ),
        in_specs=[pl.BlockSpec((1, TQ, da), lambda b, i: (b, i, 0)),
                  kspec(0), kspec(1), kspec(2), kspec(0), kspec(1), kspec(2),
                  pl.BlockSpec((N_HEADS, TQ, NKB * TQ), lambda b, i: (0, 0, 0))],
        out_specs=pl.BlockSpec((1, TQ, da), lambda b, i: (b, i, 0)),
        out_shape=jax.ShapeDtypeStruct((bsz, s, da), BF16),
        compiler_params=_cparams(("arbitrary", "arbitrary")),
        name="attn",
    )(q, k, k, k, v, v, v, bias)


def _bias_table(rel_bias):
    nk = NKB * TQ
    lw = TQ + nk - 1
    n_lo = (TQ - 1) - REL_CLIP
    n_hi = (nk - 1) - REL_CLIP
    w = jnp.concatenate([jnp.repeat(rel_bias[:, :1], n_lo, axis=1), rel_bias,
                         jnp.repeat(rel_bias[:, -1:], n_hi, axis=1)], axis=1).astype(F32)
    w2 = jnp.roll(w[:, ::-1], -(TQ - 1), axis=1)
    flat = jnp.tile(w2, (1, TQ))[:, :TQ * (lw - 1)]
    tab = flat.reshape(-1, TQ, lw - 1)[:, :, :nk]
    qi = np.arange(TQ)[:, None]
    kj = np.arange(nk)[None, :]
    qc = qi // CHUNK + (NKB - 1) * TQ // CHUNK
    kc = kj // CHUNK
    band = (kc >= qc - LEFT_CHUNKS) & (kc <= qc)
    return jnp.where(band[None], tab, NEG_INF)


def _merge_kernel(o_ref, mc_ref, ga_ref, x_ref, wa_ref, wo_ref, ga1_ref, g2_ref, sc_ref, sh_ref,
                  x1_ref, h2t_ref):
    ya = jnp.dot(o_ref[0], wa_ref[...], preferred_element_type=F32)
    merged = mc_ref[0] + ga_ref[0] * ya
    y = jnp.dot(merged.astype(BF16), wo_ref[...], preferred_element_type=F32)
    x1 = x_ref[0] + ga1_ref[0] * y
    x1_ref[0] = x1
    ms = jnp.mean(x1 * x1, axis=-1, keepdims=True)
    h2 = (x1 * lax.rsqrt(ms + EPS)) * g2_ref[...]
    h2 = h2 * (1.0 + sc_ref[0]) + sh_ref[0]
    h2t_ref[...] = h2.T.astype(BF16)


def _merge(o, mc, ga, x, wa, wo, ga1, g2, sc2, sh2):
    bsz, s, d = x.shape
    tm = TM_MERGE
    nb = s // tm
    row = lambda w: pl.BlockSpec((1, tm, w), lambda b, i: (b, i, 0))
    mod = pl.BlockSpec((1, 1, d), lambda b, i: (b, 0, 0))
    return pl.pallas_call(
        _merge_kernel,
        grid=(bsz, nb),
        in_specs=[row(D_ATTN), row(d), row(d), row(d),
                  pl.BlockSpec((D_ATTN, d), lambda b, i: (0, 0)),
                  pl.BlockSpec((d, d), lambda b, i: (0, 0)),
                  mod, pl.BlockSpec((1, d), lambda b, i: (0, 0)), mod, mod],
        out_specs=[row(d), pl.BlockSpec((d, tm), lambda b, i: (0, b * nb + i))],
        out_shape=[jax.ShapeDtypeStruct((bsz, s, d), F32),
                   jax.ShapeDtypeStruct((d, bsz * s), BF16)],
        compiler_params=_cparams(("arbitrary", "arbitrary")),
        name="merge",
    )(o, mc, ga, x, wa, wo, ga1, g2, sc2, sh2)


_CAND = [(a, b) for a in range(PEER_TOPK) for b in range(PEER_TOPK)
         if (a + 1) * (b + 1) <= PEER_TOPK]


def _route_kernel(h2t_ref, wqt_ref, keys_ref, rank1_ref, limit_ref, e0_ref, e1_ref,
                  s_scr, rank_scr, v_scr, b_scr, zi_scr):
    tb = h2t_ref.shape[1]
    ncol = tb // LANES
    n_hp = 2 * PEER_HEADS
    qt = jnp.dot(wqt_ref[...], h2t_ref[...], preferred_element_type=F32).astype(BF16)
    for hp in range(n_hp):
        s_scr[hp] = jnp.dot(keys_ref[hp], qt[hp * PEER_DH:(hp + 1) * PEER_DH, :],
                            preferred_element_type=F32)

    def stage1(it, carry):
        hp = it // ncol
        col = it % ncol
        cs = pl.ds(pl.multiple_of(col * LANES, LANES), LANES)
        work = s_scr[hp, :, cs]
        rank = jnp.full(work.shape, float(PEER_TOPK), F32)
        for r in range(PEER_TOPK):
            m = jnp.max(work, axis=0, keepdims=True)
            eq = work == m
            rank = jnp.where(eq, float(r), rank)
            work = jnp.where(eq, -jnp.inf, work)
            v_scr[hp, r, pl.ds(col, 1), :] = m
        rank_scr[hp, :, cs] = rank
        return carry

    lax.fori_loop(0, n_hp * ncol, stage1, 0)

    def stage2(h, carry):
        v0 = [v_scr[2 * h, a] for a in range(PEER_TOPK)]
        v1 = [v_scr[2 * h + 1, b] for b in range(PEER_TOPK)]
        cand = [v0[a] + v1[b] for (a, b) in _CAND]
        top = cand[0]
        work = list(cand)
        for r in range(PEER_TOPK):
            m = functools.reduce(jnp.maximum, work)
            work = [jnp.where(w == m, -jnp.inf, w) for w in work]
        z = jnp.zeros_like(top)
        cnt = [jnp.zeros_like(top) for _ in range(PEER_TOPK)]
        for (a, b), c, w in zip(_CAND, cand, work):
            sel = w == -jnp.inf
            z = z + jnp.where(sel, jnp.exp(c - top), 0.0)
            cnt[a] = cnt[a] + jnp.where(sel, 1.0, 0.0)
        for a in range(PEER_TOPK):
            b_scr[h, a] = cnt[a]
        zi_scr[h] = 1.0 / z
        return carry

    lax.fori_loop(0, PEER_HEADS, stage2, 0)

    def stage3(it, carry):
        h = it // ncol
        col = it % ncol
        cs = pl.ds(pl.multiple_of(col * LANES, LANES), LANES)
        rw = pl.ds(col, 1)
        rank0 = rank_scr[2 * h, :, cs]
        limit = jnp.zeros(rank0.shape, F32)
        for a in range(PEER_TOPK):
            limit = jnp.where(rank0 == float(a), b_scr[h, a, rw, :], limit)
        limit_ref[h, :, cs] = limit
        rank1_ref[h, :, cs] = rank_scr[2 * h + 1, :, cs]
        e0_ref[h, :, cs] = jnp.exp(s_scr[2 * h, :, cs] - v_scr[2 * h, 0, rw, :])
        e1_ref[h, :, cs] = (jnp.exp(s_scr[2 * h + 1, :, cs] - v_scr[2 * h + 1, 0, rw, :])
                            * zi_scr[h, rw, :])
        return carry

    lax.fori_loop(0, PEER_HEADS * ncol, stage3, 0)


def _route(h2t, wqt, keys):
    d, t = h2t.shape
    tb = TB_ROUTE
    ncol = tb // LANES
    n_hp = 2 * PEER_HEADS
    tab = pl.BlockSpec((PEER_HEADS, PEER_NKEYS, tb), lambda i: (0, 0, i))
    tab_shape = lambda dt: jax.ShapeDtypeStruct((PEER_HEADS, PEER_NKEYS, t), dt)
    return pl.pallas_call(
        _route_kernel,
        grid=(t // tb,),
        in_specs=[pl.BlockSpec((d, tb), lambda i: (0, i)),
                  pl.BlockSpec(wqt.shape, lambda i: (0, 0)),
                  pl.BlockSpec(keys.shape, lambda i: (0, 0, 0))],
        out_specs=[tab, tab, tab, tab],
        out_shape=[tab_shape(F32)] * 4,
        scratch_shapes=[pltpu.VMEM((n_hp, PEER_NKEYS, tb), F32),
                        pltpu.VMEM((n_hp, PEER_NKEYS, tb), F32),
                        pltpu.VMEM((n_hp, PEER_TOPK, ncol, LANES), F32),
                        pltpu.VMEM((PEER_HEADS, PEER_TOPK, ncol, LANES), F32),
                        pltpu.VMEM((PEER_HEADS, ncol, LANES), F32)],
        compiler_params=_cparams(("arbitrary",)),
        name="route",
    )(h2t, wqt, keys)


_INV_SQRT2 = float(1.0 / np.sqrt(2.0))


def _experts_kernel(h2t_ref, u_ref, vt_ref, rank1_ref, limit_ref, e0_ref, e1_ref,
                    x1_ref, ga2_ref, o_ref, a_scr, p_scr, acc_ref, r1_scr, e1_scr):
    c = pl.program_id(1)
    tb = h2t_ref.shape[1]
    ncol = tb // LANES
    rows_per_step = EC // PEER_NKEYS

    @pl.when(c == 0)
    def _():
        acc_ref[...] = jnp.zeros_like(acc_ref)
        for h in range(PEER_HEADS):
            r1_scr[h] = rank1_ref[h].astype(BF16)
            e1_scr[h] = e1_ref[h].astype(BF16)

    a_scr[...] = jnp.dot(u_ref[...], h2t_ref[...], preferred_element_type=F32)

    shape = (PEER_NKEYS, LANES)
    zero = jnp.zeros(shape, BF16)
    for col in range(ncol):
        cs = slice(col * LANES, (col + 1) * LANES)
        for ii in range(rows_per_step):
            row = slice(ii, ii + 1)
            g = zero
            for h in range(PEER_HEADS):
                lim = jnp.broadcast_to(limit_ref[h, row, cs], shape).astype(BF16)
                e0 = jnp.broadcast_to(e0_ref[h, row, cs], shape).astype(BF16)
                g = g + jnp.where(r1_scr[h, :, cs] < lim, e1_scr[h, :, cs], zero) * e0
            a = a_scr[ii * PEER_NKEYS:(ii + 1) * PEER_NKEYS, cs]
            half = 0.5 * a
            act = half + half * lax.erf(a * _INV_SQRT2)
            p_scr[ii * PEER_NKEYS:(ii + 1) * PEER_NKEYS, cs] = act.astype(BF16) * g
    acc_ref[...] += jnp.dot(vt_ref[...], p_scr[...], preferred_element_type=F32)

    @pl.when(c == pl.num_programs(1) - 1)
    def _():
        o_ref[...] = x1_ref[...] + ga2_ref[0] * acc_ref[...].T


def _experts(h2t, u, vt, rank1, limit, e0, e1, x1, ga2, seq):
    d, t = h2t.shape
    tb = TB_EXP
    n_e = u.shape[0]
    per_batch = seq // tb
    tab = pl.BlockSpec((PEER_HEADS, PEER_NKEYS, tb), lambda i, c: (0, 0, i))
    rowtab = pl.BlockSpec((PEER_HEADS, EC // PEER_NKEYS, tb), lambda i, c: (0, c, i))
    return pl.pallas_call(
        _experts_kernel,
        grid=(t // tb, n_e // EC),
        in_specs=[pl.BlockSpec((d, tb), lambda i, c: (0, i)),
                  pl.BlockSpec((EC, d), lambda i, c: (c, 0)),
                  pl.BlockSpec((d, EC), lambda i, c: (0, c)),
                  tab, rowtab, rowtab, tab,
                  pl.BlockSpec((tb, d), lambda i, c: (i, 0)),
                  pl.BlockSpec((1, 1, d), lambda i, c: (i // per_batch, 0, 0))],
        out_specs=pl.BlockSpec((tb, d), lambda i, c: (i, 0)),
        out_shape=jax.ShapeDtypeStruct((t, d), F32),
        scratch_shapes=[pltpu.VMEM((EC, tb), F32), pltpu.VMEM((EC, tb), BF16),
                        pltpu.VMEM((d, tb), F32),
                        pltpu.VMEM((PEER_HEADS, PEER_NKEYS, tb), BF16),
                        pltpu.VMEM((PEER_HEADS, PEER_NKEYS, tb), BF16)],
        compiler_params=_cparams(("arbitrary", "arbitrary")),
        name="experts",
    )(h2t, u, vt, rank1, limit, e0, e1, x1, ga2)


def kernel(x, c, w_ada, b_ada, norm1_g, norm2_g, w_in, conv_dw, conv_b, conv_ln_g, conv_ln_b,
           w_conv_out, q_norm_g, k_norm_g, rel_bias, w_attn_out, w_out, peer_wq, peer_keys,
           peer_u, peer_v):
    bsz, s, d = x.shape
    depth = w_ada.shape[0]
    bd = jnp.asarray(np.kron(np.eye(N_HEADS), np.full((HEAD_DIM, HEAD_DIM), 1.0 / HEAD_DIM)), BF16)
    for l in range(depth):
        mod = _ada(c, w_ada[l], b_ada[l])
        sh1, sc1, ga1, sh2, sc2, ga2 = [m.reshape(bsz, 1, d) for m in jnp.split(mod, 6, axis=-1)]
        u, q, k, v, gc, ga = _inproj(
            x, norm1_g[l].reshape(1, d), sc1, sh1, w_in[l].astype(BF16), bd,
            jnp.tile(q_norm_g[l], N_HEADS).reshape(1, D_ATTN),
            jnp.tile(k_norm_g[l], N_HEADS).reshape(1, D_ATTN))
        mc = _conv(u, conv_dw[l], conv_b[l].reshape(1, D_CONV), conv_ln_g[l].reshape(1, D_CONV),
                   conv_ln_b[l].reshape(1, D_CONV), w_conv_out[l].astype(BF16), gc)
        o = _attn(q, k, v, _bias_table(rel_bias[l]))
        x1, h2t = _merge(o, mc, ga, x, w_attn_out[l].astype(BF16), w_out[l].astype(BF16),
                         ga1, norm2_g[l].reshape(1, d), sc2, sh2)
        wqt = peer_wq[l].T.astype(BF16)
        keys = peer_keys[l].reshape(2 * PEER_HEADS, PEER_NKEYS, PEER_DH).astype(BF16)
        rank1, limit, e0, e1 = _route(h2t, wqt, keys)
        out = _experts(h2t, peer_u[l].astype(BF16), peer_v[l].T.astype(BF16),
                       rank1, limit, e0, e1, x1.reshape(bsz * s, d), ga2, s)
        x = out.reshape(bsz, s, d)
    return x
```

```python
import functools

import jax
import jax.numpy as jnp
import numpy as np
from jax import lax
from jax.experimental import pallas as pl
from jax.experimental.pallas import tpu as pltpu

F32 = jnp.float32
BF16 = jnp.bfloat16

D_MODEL = 1024
CHUNK = 64
N_HEADS = 8
HEAD_DIM = 64
D_ATTN = N_HEADS * HEAD_DIM
LEFT_CHUNKS = 8
REL_CLIP = 128
D_CONV = D_MODEL // 2
CONV_W = 31
PEER_HEADS = 8
PEER_NKEYS = 128
PEER_N = PEER_NKEYS * PEER_NKEYS
PEER_DH = 128
PEER_TOPK = 16
EPS = 1e-6
NEG_INF = -1e30

LANES = 128
VMEM_LIMIT = 56 * 1024 * 1024

TM_IN = 256
TS_CONV = 512
HALO = 32
TQ = 256
NKB = 3
TM_MERGE = 256
TB_ROUTE = 512
TB_EXP = 1024
EC = 512
ROWS_PER_STEP = EC // PEER_NKEYS
_BIG = 2.0 ** 100


def _cparams(sem):
    return pltpu.CompilerParams(dimension_semantics=sem, vmem_limit_bytes=VMEM_LIMIT)


def _ada_kernel(c_ref, w_ref, b_ref, o_ref):
    c = c_ref[...]
    cond = c * jax.nn.sigmoid(c)
    o_ref[...] = jnp.dot(cond.astype(BF16), w_ref[...].astype(BF16),
                         preferred_element_type=F32) + b_ref[...]


def _ada(c, w, b):
    bsz, d = c.shape
    n = w.shape[1]
    tn = 1024
    return pl.pallas_call(
        _ada_kernel,
        grid=(n // tn,),
        in_specs=[pl.BlockSpec((bsz, d), lambda j: (0, 0)),
                  pl.BlockSpec((d, tn), lambda j: (0, j)),
                  pl.BlockSpec((1, tn), lambda j: (0, j))],
        out_specs=pl.BlockSpec((bsz, tn), lambda j: (0, j)),
        out_shape=jax.ShapeDtypeStruct((bsz, n), F32),
        compiler_params=_cparams(("arbitrary",)),
        name="ada",
    )(c, w, b.reshape(1, n))


def _head_rms(t, bd_ref, gain):
    t2 = t * t
    hi = t2.astype(BF16)
    lo = (t2 - hi.astype(F32)).astype(BF16)
    ms = (jnp.dot(hi, bd_ref[...], preferred_element_type=F32)
          + jnp.dot(lo, bd_ref[...], preferred_element_type=F32))
    return t * lax.rsqrt(ms + EPS) * gain


def _inproj_kernel(x_ref, g_ref, sc_ref, sh_ref, w_ref, bd_ref, qg_ref, kg_ref,
                   u_ref, q_ref, k_ref, v_ref, gc_ref, ga_ref):
    x = x_ref[0]
    ms = jnp.mean(x * x, axis=-1, keepdims=True)
    h = (x * lax.rsqrt(ms + EPS)) * g_ref[...]
    h = h * (1.0 + sc_ref[0]) + sh_ref[0]
    hb = h.astype(BF16)

    def seg(lo, hi):
        return jnp.dot(hb, w_ref[:, lo:hi], preferred_element_type=F32)

    o = 0
    a = seg(o, o + D_CONV); o += D_CONV
    b = seg(o, o + D_CONV); o += D_CONV
    u_ref[0] = a * jax.nn.sigmoid(b)
    q = seg(o, o + D_ATTN); o += D_ATTN
    q_ref[0] = (_head_rms(q, bd_ref, qg_ref[...]) * (HEAD_DIM ** -0.5)).astype(BF16)
    k = seg(o, o + D_ATTN); o += D_ATTN
    k_ref[0] = _head_rms(k, bd_ref, kg_ref[...]).astype(BF16)
    v_ref[0] = seg(o, o + D_ATTN).astype(BF16); o += D_ATTN
    gc_ref[0] = jax.nn.sigmoid(seg(o, o + D_MODEL)); o += D_MODEL
    ga_ref[0] = jax.nn.sigmoid(seg(o, o + D_MODEL))


def _inproj(x, g1, sc1, sh1, w_in, bd, qg, kg):
    bsz, s, d = x.shape
    tm = TM_IN
    n_in = w_in.shape[1]
    row = lambda w: pl.BlockSpec((1, tm, w), lambda b, i: (b, i, 0))
    vec = lambda w: pl.BlockSpec((1, w), lambda b, i: (0, 0))
    mod = pl.BlockSpec((1, 1, d), lambda b, i: (b, 0, 0))
    return pl.pallas_call(
        _inproj_kernel,
        grid=(bsz, s // tm),
        in_specs=[row(d), vec(d), mod, mod,
                  pl.BlockSpec((d, n_in), lambda b, i: (0, 0)),
                  pl.BlockSpec((D_ATTN, D_ATTN), lambda b, i: (0, 0)),
                  vec(D_ATTN), vec(D_ATTN)],
        out_specs=[row(D_CONV), row(D_ATTN), row(D_ATTN), row(D_ATTN), row(d), row(d)],
        out_shape=[jax.ShapeDtypeStruct((bsz, s, D_CONV), F32),
                   jax.ShapeDtypeStruct((bsz, s, D_ATTN), BF16),
                   jax.ShapeDtypeStruct((bsz, s, D_ATTN), BF16),
                   jax.ShapeDtypeStruct((bsz, s, D_ATTN), BF16),
                   jax.ShapeDtypeStruct((bsz, s, d), F32),
                   jax.ShapeDtypeStruct((bsz, s, d), F32)],
        compiler_params=_cparams(("arbitrary", "arbitrary")),
        name="inproj",
    )(x, g1, sc1, sh1, w_in, bd, qg, kg)


CONV_ROWS = 64


def _conv_kernel(u_ref, up_ref, dw_ref, cb_ref, lg_ref, lb_ref, w_ref, gc_ref,
                 o_ref, ext_ref, y_ref):
    i = pl.program_id(1)
    ts = u_ref.shape[1]
    prev = up_ref[0, ts - HALO:, :]
    ext_ref[0:HALO, :] = jnp.where(i > 0, prev, 0.0)
    ext_ref[HALO:, :] = u_ref[0]
    base = HALO - (CONV_W - 1)
    for r0 in range(0, ts, CONV_ROWS):
        acc = jnp.zeros((CONV_ROWS, D_CONV), F32) + cb_ref[...]
        for w in range(CONV_W):
            acc = acc + ext_ref[r0 + base + w:r0 + base + w + CONV_ROWS, :] * dw_ref[w:w + 1, :]
        y_ref[r0:r0 + CONV_ROWS, :] = acc
    y = y_ref[...]
    mu = jnp.mean(y, axis=-1, keepdims=True)
    yc = y - mu
    var = jnp.mean(yc * yc, axis=-1, keepdims=True)
    z = yc * lax.rsqrt(var + EPS) * lg_ref[...] + lb_ref[...]
    z = z * jax.nn.sigmoid(z)
    o = jnp.dot(z.astype(BF16), w_ref[...], preferred_element_type=F32)
    o_ref[0] = gc_ref[0] * o


def _conv(u, dw, cb, lg, lb, w_co, gc):
    bsz, s, dc = u.shape
    d = w_co.shape[1]
    ts = TS_CONV
    vec = lambda w: pl.BlockSpec((1, w), lambda b, i: (0, 0))
    return pl.pallas_call(
        _conv_kernel,
        grid=(bsz, s // ts),
        in_specs=[pl.BlockSpec((1, ts, dc), lambda b, i: (b, i, 0)),
                  pl.BlockSpec((1, ts, dc), lambda b, i: (b, jnp.maximum(i - 1, 0), 0)),
                  pl.BlockSpec((CONV_W, dc), lambda b, i: (0, 0)),
                  vec(dc), vec(dc), vec(dc),
                  pl.BlockSpec((dc, d), lambda b, i: (0, 0)),
                  pl.BlockSpec((1, ts, d), lambda b, i: (b, i, 0))],
        out_specs=pl.BlockSpec((1, ts, d), lambda b, i: (b, i, 0)),
        out_shape=jax.ShapeDtypeStruct((bsz, s, d), F32),
        scratch_shapes=[pltpu.VMEM((ts + HALO, dc), F32), pltpu.VMEM((ts, dc), F32)],
        compiler_params=_cparams(("arbitrary", "arbitrary")),
        name="conv",
    )(u, u, dw, cb, lg, lb, w_co, gc)


def _attn_kernel(q_ref, k0_ref, k1_ref, k2_ref, v0_ref, v1_ref, v2_ref, bias_ref, o_ref):
    i = pl.program_id(1)
    k_refs = (k0_ref, k1_ref, k2_ref)
    v_refs = (v0_ref, v1_ref, v2_ref)
    pens = [jnp.where(i - (NKB - 1) + j >= 0, 0.0, NEG_INF).astype(F32) for j in range(NKB)]
    for h in range(N_HEADS):
        lo = h * HEAD_DIM
        qh = q_ref[0, :, lo:lo + HEAD_DIM]
        ss = []
        for j in range(NKB):
            kh = k_refs[j][0, :, lo:lo + HEAD_DIM]
            s = lax.dot_general(qh, kh, (((1,), (1,)), ((), ())), preferred_element_type=F32)
            ss.append(s + bias_ref[h, :, j * TQ:(j + 1) * TQ] + pens[j])
        m = jnp.maximum(jnp.maximum(jnp.max(ss[0], axis=-1, keepdims=True),
                                    jnp.max(ss[1], axis=-1, keepdims=True)),
                        jnp.max(ss[2], axis=-1, keepdims=True))
        l = jnp.zeros_like(m)
        acc = jnp.zeros((TQ, HEAD_DIM), F32)
        for j in range(NKB):
            p = jnp.exp(ss[j] - m)
            l = l + jnp.sum(p, axis=-1, keepdims=True)
            vh = v_refs[j][0, :, lo:lo + HEAD_DIM]
            acc = acc + jnp.dot(p.astype(BF16), vh, preferred_element_type=F32)
        o_ref[0, :, lo:lo + HEAD_DIM] = (acc / l).astype(BF16)


def _attn(q, k, v, bias):
    bsz, s, da = q.shape
    kspec = lambda j: pl.BlockSpec(
        (1, TQ, da), lambda b, i: (b, jnp.maximum(i - (NKB - 1) + j, 0), 0))
    return pl.pallas_call(
        _attn_kernel,
        grid=(bsz, s // TQ),
        in_specs=[pl.BlockSpec((1, TQ, da), lambda b, i: (b, i, 0)),
                  kspec(0), kspec(1), kspec(2), kspec(0), kspec(1), kspec(2),
                  pl.BlockSpec((N_HEADS, TQ, NKB * TQ), lambda b, i: (0, 0, 0))],
        out_specs=pl.BlockSpec((1, TQ, da), lambda b, i: (b, i, 0)),
        out_shape=jax.ShapeDtypeStruct((bsz, s, da), BF16),
        compiler_params=_cparams(("arbitrary", "arbitrary")),
        name="attn",
    )(q, k, k, k, v, v, v, bias)


def _bias_table(rel_bias):
    nk = NKB * TQ
    lw = TQ + nk - 1
    n_lo = (TQ - 1) - REL_CLIP
    n_hi = (nk - 1) - REL_CLIP
    w = jnp.concatenate([jnp.repeat(rel_bias[:, :1], n_lo, axis=1), rel_bias,
                         jnp.repeat(rel_bias[:, -1:], n_hi, axis=1)], axis=1).astype(F32)
    w2 = jnp.roll(w[:, ::-1], -(TQ - 1), axis=1)
    flat = jnp.tile(w2, (1, TQ))[:, :TQ * (lw - 1)]
    tab = flat.reshape(-1, TQ, lw - 1)[:, :, :nk]
    qi = np.arange(TQ)[:, None]
    kj = np.arange(nk)[None, :]
    qc = qi // CHUNK + (NKB - 1) * TQ // CHUNK
    kc = kj // CHUNK
    band = (kc >= qc - LEFT_CHUNKS) & (kc <= qc)
    return jnp.where(band[None], tab, NEG_INF)


def _merge_kernel(o_ref, mc_ref, ga_ref, x_ref, wa_ref, wo_ref, ga1_ref, g2_ref, sc_ref, sh_ref,
                  x1_ref, h2t_ref):
    ya = jnp.dot(o_ref[0], wa_ref[...], preferred_element_type=F32)
    merged = mc_ref[0] + ga_ref[0] * ya
    y = jnp.dot(merged.astype(BF16), wo_ref[...], preferred_element_type=F32)
    x1 = x_ref[0] + ga1_ref[0] * y
    x1_ref[0] = x1
    ms = jnp.mean(x1 * x1, axis=-1, keepdims=True)
    h2 = (x1 * lax.rsqrt(ms + EPS)) * g2_ref[...]
    h2 = h2 * (1.0 + sc_ref[0]) + sh_ref[0]
    h2t_ref[...] = h2.T.astype(BF16)


def _merge(o, mc, ga, x, wa, wo, ga1, g2, sc2, sh2):
    bsz, s, d = x.shape
    tm = TM_MERGE
    nb = s // tm
    row = lambda w: pl.BlockSpec((1, tm, w), lambda b, i: (b, i, 0))
    mod = pl.BlockSpec((1, 1, d), lambda b, i: (b, 0, 0))
    return pl.pallas_call(
        _merge_kernel,
        grid=(bsz, nb),
        in_specs=[row(D_ATTN), row(d), row(d), row(d),
                  pl.BlockSpec((D_ATTN, d), lambda b, i: (0, 0)),
                  pl.BlockSpec((d, d), lambda b, i: (0, 0)),
                  mod, pl.BlockSpec((1, d), lambda b, i: (0, 0)), mod, mod],
        out_specs=[row(d), pl.BlockSpec((d, tm), lambda b, i: (0, b * nb + i))],
        out_shape=[jax.ShapeDtypeStruct((bsz, s, d), F32),
                   jax.ShapeDtypeStruct((d, bsz * s), BF16)],
        compiler_params=_cparams(("arbitrary", "arbitrary")),
        name="merge",
    )(o, mc, ga, x, wa, wo, ga1, g2, sc2, sh2)


_CAND = [(a, b) for a in range(PEER_TOPK) for b in range(PEER_TOPK)
         if (a + 1) * (b + 1) <= PEER_TOPK]


def _route_kernel(h2t_ref, wqt_ref, keys_ref, rank1_ref, limit_ref, e0_ref, e1_ref,
                  s_scr, rank_scr, v_scr, b_scr, zi_scr):
    tb = h2t_ref.shape[1]
    ncol = tb // LANES
    n_hp = 2 * PEER_HEADS
    qt = jnp.dot(wqt_ref[...], h2t_ref[...], preferred_element_type=F32).astype(BF16)
    for hp in range(n_hp):
        s_scr[hp] = jnp.dot(keys_ref[hp], qt[hp * PEER_DH:(hp + 1) * PEER_DH, :],
                            preferred_element_type=F32)

    def stage1(it, carry):
        h = it // ncol
        col = it % ncol
        cs = pl.ds(pl.multiple_of(col * LANES, LANES), LANES)
        rw = pl.ds(h * ncol + col, 1)
        works = [s_scr[2 * h + p, :, cs] for p in range(2)]
        for r in range(PEER_TOPK):
            for p in range(2):
                m = jnp.max(works[p], axis=0, keepdims=True)
                works[p] = jnp.where(works[p] == m, -(r + 1) * _BIG, works[p])
                v_scr[p, r, rw, :] = m
        for p in range(2):
            rank_scr[2 * h + p, :, cs] = jnp.where(
                works[p] <= -_BIG, works[p] * (-1.0 / _BIG) - 1.0, float(PEER_TOPK))
        return carry

    lax.fori_loop(0, PEER_HEADS * ncol, stage1, 0)

    v0 = [v_scr[0, a] for a in range(PEER_TOPK)]
    v1 = [v_scr[1, b] for b in range(PEER_TOPK)]
    cand = [v0[a] + v1[b] for (a, b) in _CAND]
    top = cand[0]
    work = list(cand)
    for r in range(PEER_TOPK):
        m = functools.reduce(jnp.maximum, work)
        work = [jnp.where(w == m, -jnp.inf, w) for w in work]
    z = jnp.zeros_like(top)
    cnt = [jnp.zeros_like(top) for _ in range(PEER_TOPK)]
    for (a, b), c, w in zip(_CAND, cand, work):
        sel = w == -jnp.inf
        z = z + jnp.where(sel, jnp.exp(c - top), 0.0)
        cnt[a] = cnt[a] + jnp.where(sel, 1.0, 0.0)
    for a in range(PEER_TOPK):
        b_scr[a] = cnt[a]
    zi_scr[...] = 1.0 / z

    def stage3(it, carry):
        h = it // ncol
        col = it % ncol
        cs = pl.ds(pl.multiple_of(col * LANES, LANES), LANES)
        rw = pl.ds(h * ncol + col, 1)
        rank0 = rank_scr[2 * h, :, cs]
        limit = jnp.zeros(rank0.shape, F32)
        for a in range(PEER_TOPK):
            limit = jnp.where(rank0 == float(a), b_scr[a, rw, :], limit)
        e0 = jnp.exp(s_scr[2 * h, :, cs] - v_scr[0, 0, rw, :])
        for g in range(PEER_NKEYS // ROWS_PER_STEP):
            rows = slice(g * ROWS_PER_STEP, (g + 1) * ROWS_PER_STEP)
            limit_ref[h, g, :, cs] = limit[rows]
            e0_ref[h, g, :, cs] = e0[rows]
        rank1_ref[h, :, cs] = rank_scr[2 * h + 1, :, cs].astype(BF16)
        e1_ref[h, :, cs] = (jnp.exp(s_scr[2 * h + 1, :, cs] - v_scr[1, 0, rw, :])
                            * zi_scr[rw, :]).astype(BF16)
        return carry

    lax.fori_loop(0, PEER_HEADS * ncol, stage3, 0)


def _route(h2t, wqt, keys):
    d, t = h2t.shape
    tb = TB_ROUTE
    ncol = tb // LANES
    n_hp = 2 * PEER_HEADS
    n_grp = PEER_NKEYS // ROWS_PER_STEP
    tab = pl.BlockSpec((PEER_HEADS, PEER_NKEYS, tb), lambda i: (0, 0, i))
    rowtab = pl.BlockSpec((PEER_HEADS, n_grp, ROWS_PER_STEP, tb), lambda i: (0, 0, 0, i))
    tab_shape = jax.ShapeDtypeStruct((PEER_HEADS, PEER_NKEYS, t), BF16)
    rowtab_shape = jax.ShapeDtypeStruct((PEER_HEADS, n_grp, ROWS_PER_STEP, t), F32)
    return pl.pallas_call(
        _route_kernel,
        grid=(t // tb,),
        in_specs=[pl.BlockSpec((d, tb), lambda i: (0, i)),
                  pl.BlockSpec(wqt.shape, lambda i: (0, 0)),
                  pl.BlockSpec(keys.shape, lambda i: (0, 0, 0))],
        out_specs=[tab, rowtab, rowtab, tab],
        out_shape=[tab_shape, rowtab_shape, rowtab_shape, tab_shape],
        scratch_shapes=[pltpu.VMEM((n_hp, PEER_NKEYS, tb), F32),
                        pltpu.VMEM((n_hp, PEER_NKEYS, tb), F32),
                        pltpu.VMEM((2, PEER_TOPK, PEER_HEADS * ncol, LANES), F32),
                        pltpu.VMEM((PEER_TOPK, PEER_HEADS * ncol, LANES), F32),
                        pltpu.VMEM((PEER_HEADS * ncol, LANES), F32)],
        compiler_params=_cparams(("arbitrary",)),
        name="route",
    )(h2t, wqt, keys)


_INV_SQRT2 = float(1.0 / np.sqrt(2.0))


def _experts_kernel(h2t_ref, u_ref, vt_ref, rank1_ref, limit_ref, e0_ref, e1_ref,
                    x1_ref, ga2_ref, o_ref, a_scr, p_scr, acc_ref, r1_scr, e1_scr):
    c = pl.program_id(1)
    tb = h2t_ref.shape[1]
    ncol = tb // LANES

    @pl.when(c == 0)
    def _():
        acc_ref[...] = jnp.zeros_like(acc_ref)
        for h in range(PEER_HEADS):
            r1_scr[h] = rank1_ref[h]
            e1_scr[h] = e1_ref[h]

    a_scr[...] = jnp.dot(u_ref[...], h2t_ref[...], preferred_element_type=F32)

    shape = (PEER_NKEYS, LANES)
    zero = jnp.zeros(shape, BF16)
    for col in range(ncol):
        cs = slice(col * LANES, (col + 1) * LANES)
        for ii in range(ROWS_PER_STEP):
            row = slice(ii, ii + 1)
            g = zero
            for h in range(PEER_HEADS):
                lim = jnp.broadcast_to(limit_ref[h, 0, row, cs], shape).astype(BF16)
                e0 = jnp.broadcast_to(e0_ref[h, 0, row, cs], shape).astype(BF16)
                g = g + jnp.where(r1_scr[h, :, cs] < lim, e1_scr[h, :, cs], zero) * e0
            a = a_scr[ii * PEER_NKEYS:(ii + 1) * PEER_NKEYS, cs]
            half = 0.5 * a
            act = half + half * lax.erf(a * _INV_SQRT2)
            p_scr[ii * PEER_NKEYS:(ii + 1) * PEER_NKEYS, cs] = act.astype(BF16) * g
    acc_ref[...] += jnp.dot(vt_ref[...], p_scr[...], preferred_element_type=F32)

    @pl.when(c == pl.num_programs(1) - 1)
    def _():
        o_ref[...] = x1_ref[...] + ga2_ref[0] * acc_ref[...].T


def _experts(h2t, u, vt, rank1, limit, e0, e1, x1, ga2, seq):
    d, t = h2t.shape
    tb = TB_EXP
    n_e = u.shape[0]
    per_batch = seq // tb
    tab = pl.BlockSpec((PEER_HEADS, PEER_NKEYS, tb), lambda i, c: (0, 0, i))
    rowtab = pl.BlockSpec((PEER_HEADS, 1, ROWS_PER_STEP, tb), lambda i, c: (0, c, 0, i))
    return pl.pallas_call(
        _experts_kernel,
        grid=(t // tb, n_e // EC),
        in_specs=[pl.BlockSpec((d, tb), lambda i, c: (0, i)),
                  pl.BlockSpec((EC, d), lambda i, c: (c, 0)),
                  pl.BlockSpec((d, EC), lambda i, c: (0, c)),
                  tab, rowtab, rowtab, tab,
                  pl.BlockSpec((tb, d), lambda i, c: (i, 0)),
                  pl.BlockSpec((1, 1, d), lambda i, c: (i // per_batch, 0, 0))],
        out_specs=pl.BlockSpec((tb, d), lambda i, c: (i, 0)),
        out_shape=jax.ShapeDtypeStruct((t, d), F32),
        scratch_shapes=[pltpu.VMEM((EC, tb), F32), pltpu.VMEM((EC, tb), BF16),
                        pltpu.VMEM((d, tb), F32),
                        pltpu.VMEM((PEER_HEADS, PEER_NKEYS, tb), BF16),
                        pltpu.VMEM((PEER_HEADS, PEER_NKEYS, tb), BF16)],
        compiler_params=_cparams(("arbitrary", "arbitrary")),
        name="experts",
    )(h2t, u, vt, rank1, limit, e0, e1, x1, ga2)


def kernel(x, c, w_ada, b_ada, norm1_g, norm2_g, w_in, conv_dw, conv_b, conv_ln_g, conv_ln_b,
           w_conv_out, q_norm_g, k_norm_g, rel_bias, w_attn_out, w_out, peer_wq, peer_keys,
           peer_u, peer_v):
    bsz, s, d = x.shape
    depth = w_ada.shape[0]
    bd = jnp.asarray(np.kron(np.eye(N_HEADS), np.full((HEAD_DIM, HEAD_DIM), 1.0 / HEAD_DIM)), BF16)
    for l in range(depth):
        mod = _ada(c, w_ada[l], b_ada[l])
        sh1, sc1, ga1, sh2, sc2, ga2 = [m.reshape(bsz, 1, d) for m in jnp.split(mod, 6, axis=-1)]
        u, q, k, v, gc, ga = _inproj(
            x, norm1_g[l].reshape(1, d), sc1, sh1, w_in[l].astype(BF16), bd,
            jnp.tile(q_norm_g[l], N_HEADS).reshape(1, D_ATTN),
            jnp.tile(k_norm_g[l], N_HEADS).reshape(1, D_ATTN))
        mc = _conv(u, conv_dw[l], conv_b[l].reshape(1, D_CONV), conv_ln_g[l].reshape(1, D_CONV),
                   conv_ln_b[l].reshape(1, D_CONV), w_conv_out[l].astype(BF16), gc)
        o = _attn(q, k, v, _bias_table(rel_bias[l]))
        x1, h2t = _merge(o, mc, ga, x, w_attn_out[l].astype(BF16), w_out[l].astype(BF16),
                         ga1, norm2_g[l].reshape(1, d), sc2, sh2)
        wqt = peer_wq[l].T.astype(BF16)
        keys = peer_keys[l].reshape(2 * PEER_HEADS, PEER_NKEYS, PEER_DH).astype(BF16)
        rank1, limit, e0, e1 = _route(h2t, wqt, keys)
        out = _experts(h2t, peer_u[l].astype(BF16), peer_v[l].T.astype(BF16),
                       rank1, limit, e0, e1, x1.reshape(bsz * s, d), ga2, s)
        x = out.reshape(bsz, s, d)
    return x
```

```python
import functools

import jax
import jax.numpy as jnp
import numpy as np
from jax import lax
from jax.experimental import pallas as pl
from jax.experimental.pallas import tpu as pltpu

F32 = jnp.float32
BF16 = jnp.bfloat16

D_MODEL = 1024
CHUNK = 64
N_HEADS = 8
HEAD_DIM = 64
D_ATTN = N_HEADS * HEAD_DIM
LEFT_CHUNKS = 8
REL_CLIP = 128
D_CONV = D_MODEL // 2
CONV_W = 31
PEER_HEADS = 8
PEER_NKEYS = 128
PEER_N = PEER_NKEYS * PEER_NKEYS
PEER_DH = 128
PEER_TOPK = 16
EPS = 1e-6
NEG_INF = -1e30

LANES = 128
MXU_N = 256
VMEM_LIMIT = 56 * 1024 * 1024

TM_IN = 256
TS_CONV = 512
HALO = 32
TQ = 256
NKB = 3
TM_MERGE = 256
TB_ROUTE = 512
TB_EXP = 1024
EC = 512
ROWS_PER_STEP = EC // PEER_NKEYS
_BIG = 2.0 ** 100


def _cparams(sem):
    return pltpu.CompilerParams(dimension_semantics=sem, vmem_limit_bytes=VMEM_LIMIT)


def _ada_kernel(c_ref, w_ref, b_ref, o_ref):
    c = c_ref[...]
    cond = c * jax.nn.sigmoid(c)
    o_ref[...] = jnp.dot(cond.astype(BF16), w_ref[...].astype(BF16),
                         preferred_element_type=F32) + b_ref[...]


def _ada(c, w, b):
    bsz, d = c.shape
    n = w.shape[1]
    tn = 1024
    return pl.pallas_call(
        _ada_kernel,
        grid=(n // tn,),
        in_specs=[pl.BlockSpec((bsz, d), lambda j: (0, 0)),
                  pl.BlockSpec((d, tn), lambda j: (0, j)),
                  pl.BlockSpec((1, tn), lambda j: (0, j))],
        out_specs=pl.BlockSpec((bsz, tn), lambda j: (0, j)),
        out_shape=jax.ShapeDtypeStruct((bsz, n), F32),
        compiler_params=_cparams(("arbitrary",)),
        name="ada",
    )(c, w, b.reshape(1, n))


def _head_rms(t, bd_ref, gain):
    t2 = t * t
    hi = t2.astype(BF16)
    lo = (t2 - hi.astype(F32)).astype(BF16)
    ms = (jnp.dot(hi, bd_ref[...], preferred_element_type=F32)
          + jnp.dot(lo, bd_ref[...], preferred_element_type=F32))
    return t * lax.rsqrt(ms + EPS) * gain


def _inproj_kernel(x_ref, g_ref, sc_ref, sh_ref, w_ref, bd_ref, qg_ref, kg_ref,
                   u_ref, q_ref, k_ref, v_ref, gc_ref, ga_ref):
    x = x_ref[0]
    ms = jnp.mean(x * x, axis=-1, keepdims=True)
    h = (x * lax.rsqrt(ms + EPS)) * g_ref[...]
    h = h * (1.0 + sc_ref[0]) + sh_ref[0]
    hb = h.astype(BF16)

    def seg(lo, hi):
        return jnp.dot(hb, w_ref[:, lo:hi], preferred_element_type=F32)

    o = 0
    a = seg(o, o + D_CONV); o += D_CONV
    b = seg(o, o + D_CONV); o += D_CONV
    u_ref[0] = a * jax.nn.sigmoid(b)
    q = seg(o, o + D_ATTN); o += D_ATTN
    q_ref[0] = (_head_rms(q, bd_ref, qg_ref[...]) * (HEAD_DIM ** -0.5)).astype(BF16)
    k = seg(o, o + D_ATTN); o += D_ATTN
    k_ref[0] = _head_rms(k, bd_ref, kg_ref[...]).astype(BF16)
    v_ref[0] = seg(o, o + D_ATTN).astype(BF16); o += D_ATTN
    gc_ref[0] = jax.nn.sigmoid(seg(o, o + D_MODEL)); o += D_MODEL
    ga_ref[0] = jax.nn.sigmoid(seg(o, o + D_MODEL))


def _inproj(x, g1, sc1, sh1, w_in, bd, qg, kg):
    bsz, s, d = x.shape
    tm = TM_IN
    n_in = w_in.shape[1]
    row = lambda w: pl.BlockSpec((1, tm, w), lambda b, i: (b, i, 0))
    vec = lambda w: pl.BlockSpec((1, w), lambda b, i: (0, 0))
    mod = pl.BlockSpec((1, 1, d), lambda b, i: (b, 0, 0))
    return pl.pallas_call(
        _inproj_kernel,
        grid=(bsz, s // tm),
        in_specs=[row(d), vec(d), mod, mod,
                  pl.BlockSpec((d, n_in), lambda b, i: (0, 0)),
                  pl.BlockSpec((D_ATTN, D_ATTN), lambda b, i: (0, 0)),
                  vec(D_ATTN), vec(D_ATTN)],
        out_specs=[row(D_CONV), row(D_ATTN), row(D_ATTN), row(D_ATTN), row(d), row(d)],
        out_shape=[jax.ShapeDtypeStruct((bsz, s, D_CONV), F32),
                   jax.ShapeDtypeStruct((bsz, s, D_ATTN), BF16),
                   jax.ShapeDtypeStruct((bsz, s, D_ATTN), BF16),
                   jax.ShapeDtypeStruct((bsz, s, D_ATTN), BF16),
                   jax.ShapeDtypeStruct((bsz, s, d), F32),
                   jax.ShapeDtypeStruct((bsz, s, d), F32)],
        compiler_params=_cparams(("arbitrary", "arbitrary")),
        name="inproj",
    )(x, g1, sc1, sh1, w_in, bd, qg, kg)


CONV_ROWS = 64


def _conv_kernel(u_ref, up_ref, dw_ref, cb_ref, lg_ref, lb_ref, w_ref, gc_ref,
                 o_ref, ext_ref, y_ref):
    i = pl.program_id(1)
    ts = u_ref.shape[1]
    prev = up_ref[0, ts - HALO:, :]
    ext_ref[0:HALO, :] = jnp.where(i > 0, prev, 0.0)
    ext_ref[HALO:, :] = u_ref[0]
    base = HALO - (CONV_W - 1)
    for r0 in range(0, ts, CONV_ROWS):
        acc = jnp.zeros((CONV_ROWS, D_CONV), F32) + cb_ref[...]
        for w in range(CONV_W):
            acc = acc + ext_ref[r0 + base + w:r0 + base + w + CONV_ROWS, :] * dw_ref[w:w + 1, :]
        y_ref[r0:r0 + CONV_ROWS, :] = acc
    y = y_ref[...]
    mu = jnp.mean(y, axis=-1, keepdims=True)
    yc = y - mu
    var = jnp.mean(yc * yc, axis=-1, keepdims=True)
    z = yc * lax.rsqrt(var + EPS) * lg_ref[...] + lb_ref[...]
    z = z * jax.nn.sigmoid(z)
    o = jnp.dot(z.astype(BF16), w_ref[...], preferred_element_type=F32)
    o_ref[0] = gc_ref[0] * o


def _conv(u, dw, cb, lg, lb, w_co, gc):
    bsz, s, dc = u.shape
    d = w_co.shape[1]
    ts = TS_CONV
    vec = lambda w: pl.BlockSpec((1, w), lambda b, i: (0, 0))
    return pl.pallas_call(
        _conv_kernel,
        grid=(bsz, s // ts),
        in_specs=[pl.BlockSpec((1, ts, dc), lambda b, i: (b, i, 0)),
                  pl.BlockSpec((1, ts, dc), lambda b, i: (b, jnp.maximum(i - 1, 0), 0)),
                  pl.BlockSpec((CONV_W, dc), lambda b, i: (0, 0)),
                  vec(dc), vec(dc), vec(dc),
                  pl.BlockSpec((dc, d), lambda b, i: (0, 0)),
                  pl.BlockSpec((1, ts, d), lambda b, i: (b, i, 0))],
        out_specs=pl.BlockSpec((1, ts, d), lambda b, i: (b, i, 0)),
        out_shape=jax.ShapeDtypeStruct((bsz, s, d), F32),
        scratch_shapes=[pltpu.VMEM((ts + HALO, dc), F32), pltpu.VMEM((ts, dc), F32)],
        compiler_params=_cparams(("arbitrary", "arbitrary")),
        name="conv",
    )(u, u, dw, cb, lg, lb, w_co, gc)


def _attn_kernel(q_ref, k0_ref, k1_ref, k2_ref, v0_ref, v1_ref, v2_ref, bias_ref, o_ref):
    i = pl.program_id(1)
    k_refs = (k0_ref, k1_ref, k2_ref)
    v_refs = (v0_ref, v1_ref, v2_ref)
    pens = [jnp.where(i - (NKB - 1) + j >= 0, 0.0, NEG_INF).astype(F32) for j in range(NKB)]
    for h in range(N_HEADS):
        lo = h * HEAD_DIM
        qh = q_ref[0, :, lo:lo + HEAD_DIM]
        ss = []
        for j in range(NKB):
            kh = k_refs[j][0, :, lo:lo + HEAD_DIM]
            s = lax.dot_general(qh, kh, (((1,), (1,)), ((), ())), preferred_element_type=F32)
            ss.append(s + bias_ref[h, :, j * TQ:(j + 1) * TQ] + pens[j])
        m = jnp.maximum(jnp.maximum(jnp.max(ss[0], axis=-1, keepdims=True),
                                    jnp.max(ss[1], axis=-1, keepdims=True)),
                        jnp.max(ss[2], axis=-1, keepdims=True))
        l = jnp.zeros_like(m)
        acc = jnp.zeros((TQ, HEAD_DIM), F32)
        for j in range(NKB):
            p = jnp.exp(ss[j] - m)
            l = l + jnp.sum(p, axis=-1, keepdims=True)
            vh = v_refs[j][0, :, lo:lo + HEAD_DIM]
            acc = acc + jnp.dot(p.astype(BF16), vh, preferred_element_type=F32)
        o_ref[0, :, lo:lo + HEAD_DIM] = (acc / l).astype(BF16)


def _attn(q, k, v, bias):
    bsz, s, da = q.shape
    kspec = lambda j: pl.BlockSpec(
        (1, TQ, da), lambda b, i: (b, jnp.maximum(i - (NKB - 1) + j, 0), 0))
    return pl.pallas_call(
        _attn_kernel,
        grid=(bsz, s // TQ),
        in_specs=[pl.BlockSpec((1, TQ, da), lambda b, i: (b, i, 0)),
                  kspec(0), kspec(1), kspec(2), kspec(0), kspec(1), kspec(2),
                  pl.BlockSpec((N_HEADS, TQ, NKB * TQ), lambda b, i: (0, 0, 0))],
        out_specs=pl.BlockSpec((1, TQ, da), lambda b, i: (b, i, 0)),
        out_shape=jax.ShapeDtypeStruct((bsz, s, da), BF16),
        compiler_params=_cparams(("arbitrary", "arbitrary")),
        name="attn",
    )(q, k, k, k, v, v, v, bias)


def _bias_table(rel_bias):
    nk = NKB * TQ
    lw = TQ + nk - 1
    n_lo = (TQ - 1) - REL_CLIP
    n_hi = (nk - 1) - REL_CLIP
    w = jnp.concatenate([jnp.repeat(rel_bias[:, :1], n_lo, axis=1), rel_bias,
                         jnp.repeat(rel_bias[:, -1:], n_hi, axis=1)], axis=1).astype(F32)
    w2 = jnp.roll(w[:, ::-1], -(TQ - 1), axis=1)
    flat = jnp.tile(w2, (1, TQ))[:, :TQ * (lw - 1)]
    tab = flat.reshape(-1, TQ, lw - 1)[:, :, :nk]
    qi = np.arange(TQ)[:, None]
    kj = np.arange(nk)[None, :]
    qc = qi // CHUNK + (NKB - 1) * TQ // CHUNK
    kc = kj // CHUNK
    band = (kc >= qc - LEFT_CHUNKS) & (kc <= qc)
    return jnp.where(band[None], tab, NEG_INF)


def _merge_kernel(o_ref, mc_ref, ga_ref, x_ref, wa_ref, wo_ref, ga1_ref, g2_ref, sc_ref, sh_ref,
                  x1_ref, h2t_ref):
    ya = jnp.dot(o_ref[0], wa_ref[...], preferred_element_type=F32)
    merged = mc_ref[0] + ga_ref[0] * ya
    y = jnp.dot(merged.astype(BF16), wo_ref[...], preferred_element_type=F32)
    x1 = x_ref[0] + ga1_ref[0] * y
    x1_ref[0] = x1
    ms = jnp.mean(x1 * x1, axis=-1, keepdims=True)
    h2 = (x1 * lax.rsqrt(ms + EPS)) * g2_ref[...]
    h2 = h2 * (1.0 + sc_ref[0]) + sh_ref[0]
    h2t_ref[...] = h2.T.astype(BF16)


def _merge(o, mc, ga, x, wa, wo, ga1, g2, sc2, sh2):
    bsz, s, d = x.shape
    tm = TM_MERGE
    nb = s // tm
    row = lambda w: pl.BlockSpec((1, tm, w), lambda b, i: (b, i, 0))
    mod = pl.BlockSpec((1, 1, d), lambda b, i: (b, 0, 0))
    return pl.pallas_call(
        _merge_kernel,
        grid=(bsz, nb),
        in_specs=[row(D_ATTN), row(d), row(d), row(d),
                  pl.BlockSpec((D_ATTN, d), lambda b, i: (0, 0)),
                  pl.BlockSpec((d, d), lambda b, i: (0, 0)),
                  mod, pl.BlockSpec((1, d), lambda b, i: (0, 0)), mod, mod],
        out_specs=[row(d), pl.BlockSpec((d, tm), lambda b, i: (0, b * nb + i))],
        out_shape=[jax.ShapeDtypeStruct((bsz, s, d), F32),
                   jax.ShapeDtypeStruct((d, bsz * s), BF16)],
        compiler_params=_cparams(("arbitrary", "arbitrary")),
        name="merge",
    )(o, mc, ga, x, wa, wo, ga1, g2, sc2, sh2)


_CAND = [(a, b) for a in range(PEER_TOPK) for b in range(PEER_TOPK)
         if (a + 1) * (b + 1) <= PEER_TOPK]


def _route_kernel(h2t_ref, wqt_ref, keys_ref, rank1_ref, limit_ref, e0_ref, e1_ref,
                  s_scr, rank_scr, v_scr, b_scr, zi_scr):
    tb = h2t_ref.shape[1]
    ncol = tb // LANES
    n_hp = 2 * PEER_HEADS
    qt = jnp.dot(wqt_ref[...], h2t_ref[...], preferred_element_type=F32).astype(BF16)
    for hp in range(n_hp):
        s_scr[hp] = jnp.dot(keys_ref[hp], qt[hp * PEER_DH:(hp + 1) * PEER_DH, :],
                            preferred_element_type=F32)

    def stage1(it, carry):
        h = it // ncol
        col = it % ncol
        cs = pl.ds(pl.multiple_of(col * LANES, LANES), LANES)
        rw = pl.ds(h * ncol + col, 1)
        works = [s_scr[2 * h + p, :, cs] for p in range(2)]
        for r in range(PEER_TOPK):
            for p in range(2):
                m = jnp.max(works[p], axis=0, keepdims=True)
                works[p] = jnp.where(works[p] == m, -(r + 1) * _BIG, works[p])
                v_scr[p, r, rw, :] = m
        for p in range(2):
            rank_scr[2 * h + p, :, cs] = jnp.where(
                works[p] <= -_BIG, works[p] * (-1.0 / _BIG) - 1.0, float(PEER_TOPK))
        return carry

    lax.fori_loop(0, PEER_HEADS * ncol, stage1, 0)

    v0 = [v_scr[0, a] for a in range(PEER_TOPK)]
    v1 = [v_scr[1, b] for b in range(PEER_TOPK)]
    cand = [v0[a] + v1[b] for (a, b) in _CAND]
    top = cand[0]
    work = list(cand)
    for r in range(PEER_TOPK):
        m = functools.reduce(jnp.maximum, work)
        work = [jnp.where(w == m, -jnp.inf, w) for w in work]
    z = jnp.zeros_like(top)
    cnt = [jnp.zeros_like(top) for _ in range(PEER_TOPK)]
    for (a, b), c, w in zip(_CAND, cand, work):
        sel = w == -jnp.inf
        z = z + jnp.where(sel, jnp.exp(c - top), 0.0)
        cnt[a] = cnt[a] + jnp.where(sel, 1.0, 0.0)
    for a in range(PEER_TOPK):
        b_scr[a] = cnt[a]
    zi_scr[...] = 1.0 / z

    def stage3(it, carry):
        h = it // ncol
        col = it % ncol
        cs = pl.ds(pl.multiple_of(col * LANES, LANES), LANES)
        rw = pl.ds(h * ncol + col, 1)
        rank0 = rank_scr[2 * h, :, cs]
        limit = jnp.zeros(rank0.shape, F32)
        for a in range(PEER_TOPK):
            limit = jnp.where(rank0 == float(a), b_scr[a, rw, :], limit)
        e0 = jnp.exp(s_scr[2 * h, :, cs] - v_scr[0, 0, rw, :])
        for g in range(PEER_NKEYS // ROWS_PER_STEP):
            rows = slice(g * ROWS_PER_STEP, (g + 1) * ROWS_PER_STEP)
            limit_ref[h, g, :, cs] = limit[rows]
            e0_ref[h, g, :, cs] = e0[rows]
        rank1_ref[h, :, cs] = rank_scr[2 * h + 1, :, cs].astype(BF16)
        e1_ref[h, :, cs] = (jnp.exp(s_scr[2 * h + 1, :, cs] - v_scr[1, 0, rw, :])
                            * zi_scr[rw, :]).astype(BF16)
        return carry

    lax.fori_loop(0, PEER_HEADS * ncol, stage3, 0)


def _route(h2t, wqt, keys):
    d, t = h2t.shape
    tb = TB_ROUTE
    ncol = tb // LANES
    n_hp = 2 * PEER_HEADS
    n_grp = PEER_NKEYS // ROWS_PER_STEP
    tab = pl.BlockSpec((PEER_HEADS, PEER_NKEYS, tb), lambda i: (0, 0, i))
    rowtab = pl.BlockSpec((PEER_HEADS, n_grp, ROWS_PER_STEP, tb), lambda i: (0, 0, 0, i))
    tab_shape = jax.ShapeDtypeStruct((PEER_HEADS, PEER_NKEYS, t), BF16)
    rowtab_shape = jax.ShapeDtypeStruct((PEER_HEADS, n_grp, ROWS_PER_STEP, t), F32)
    return pl.pallas_call(
        _route_kernel,
        grid=(t // tb,),
        in_specs=[pl.BlockSpec((d, tb), lambda i: (0, i)),
                  pl.BlockSpec(wqt.shape, lambda i: (0, 0)),
                  pl.BlockSpec(keys.shape, lambda i: (0, 0, 0))],
        out_specs=[tab, rowtab, rowtab, tab],
        out_shape=[tab_shape, rowtab_shape, rowtab_shape, tab_shape],
        scratch_shapes=[pltpu.VMEM((n_hp, PEER_NKEYS, tb), F32),
                        pltpu.VMEM((n_hp, PEER_NKEYS, tb), F32),
                        pltpu.VMEM((2, PEER_TOPK, PEER_HEADS * ncol, LANES), F32),
                        pltpu.VMEM((PEER_TOPK, PEER_HEADS * ncol, LANES), F32),
                        pltpu.VMEM((PEER_HEADS * ncol, LANES), F32)],
        compiler_params=_cparams(("arbitrary",)),
        name="route",
    )(h2t, wqt, keys)


_INV_SQRT2 = float(1.0 / np.sqrt(2.0))
JG_BLOCK = 8


def _experts_kernel(h2t_ref, u_ref, vt_ref, rank1_ref, limit_ref, e0_ref, e1_ref,
                    x1_ref, ga2_ref, o_ref, a0_scr, a1_scr, p0_scr, p1_scr, acc_ref,
                    r1_scr, e1_scr):
    c = pl.program_id(1)
    tb = h2t_ref.shape[1]

    @pl.when(c == 0)
    def _():
        acc_ref[...] = jnp.zeros_like(acc_ref)
        a1_scr[...] = jnp.zeros_like(a1_scr)
        p0_scr[...] = jnp.zeros_like(p0_scr)
        for h in range(PEER_HEADS):
            r1_scr[h] = rank1_ref[h].astype(F32)
            e1_scr[h] = e1_ref[h].astype(F32)

    sub = (8, LANES)
    n_jg = PEER_NKEYS // sub[0]
    zero = jnp.zeros(sub, F32)

    def pair_body(k, refs):
        a_new, a_old, p_new, p_old = refs
        base = k * MXU_N
        ps = pl.ds(base, MXU_N)
        a_new[:, ps] = jnp.dot(u_ref[...], h2t_ref[:, ps], preferred_element_type=F32)
        acc_ref[:, ps] += jnp.dot(vt_ref[...], p_old[:, ps], preferred_element_type=F32)
        for half in range(MXU_N // LANES):
            cs = pl.ds(base + half * LANES, LANES)
            for ip, j0 in [(ip, j0) for ip in range(ROWS_PER_STEP // 2)
                           for j0 in range(0, n_jg, JG_BLOCK)]:
                rows = (2 * ip, 2 * ip + 1)
                g = [[zero] * JG_BLOCK for _ in rows]
                for h in range(PEER_HEADS):
                    lim = [jnp.broadcast_to(limit_ref[h, 0, r:r + 1, cs], sub) for r in rows]
                    e0 = [jnp.broadcast_to(e0_ref[h, 0, r:r + 1, cs], sub) for r in rows]
                    for jg in range(JG_BLOCK):
                        js = slice((j0 + jg) * sub[0], (j0 + jg + 1) * sub[0])
                        r1 = r1_scr[h, js, cs]
                        e1 = e1_scr[h, js, cs]
                        for q in range(2):
                            g[q][jg] = g[q][jg] + jnp.where(r1 < lim[q], e1, zero) * e0[q]
                for q, r in enumerate(rows):
                    for jg in range(0, JG_BLOCK, 2):
                        lo = r * PEER_NKEYS + (j0 + jg) * sub[0]
                        a = a_old[lo:lo + 2 * sub[0], cs]
                        hf = 0.5 * a
                        act = hf + hf * lax.erf(a * _INV_SQRT2)
                        gg = jnp.concatenate([g[q][jg], g[q][jg + 1]], axis=0)
                        p_new[lo:lo + 2 * sub[0], cs] = (act * gg).astype(BF16)

    even = (a0_scr, a1_scr, p1_scr, p0_scr)
    odd = (a1_scr, a0_scr, p0_scr, p1_scr)
    for parity, refs in ((0, even), (1, odd)):
        @pl.when(c % 2 == parity)
        def _(refs=refs):
            for k in range(tb // MXU_N):
                pair_body(k, refs)

    @pl.when(c == pl.num_programs(1) - 1)
    def _():
        o_ref[...] = x1_ref[...] + ga2_ref[0] * acc_ref[...].T


def _experts(h2t, u, vt, rank1, limit, e0, e1, x1, ga2, seq):
    d, t = h2t.shape
    tb = TB_EXP
    n_chunks = u.shape[0] // EC
    last = n_chunks - 1
    per_batch = seq // tb
    once = dict(pipeline_mode=pl.Buffered(1))
    tab = pl.BlockSpec((PEER_HEADS, PEER_NKEYS, tb), lambda i, c: (0, 0, i), **once)
    rowtab = pl.BlockSpec((PEER_HEADS, 1, ROWS_PER_STEP, tb),
                          lambda i, c: (0, jnp.clip(c - 1, 0, last), 0, i))
    return pl.pallas_call(
        _experts_kernel,
        grid=(t // tb, n_chunks + 2),
        in_specs=[pl.BlockSpec((d, tb), lambda i, c: (0, i)),
                  pl.BlockSpec((EC, d), lambda i, c: (jnp.minimum(c, last), 0)),
                  pl.BlockSpec((d, EC), lambda i, c: (0, jnp.clip(c - 2, 0, last))),
                  tab, rowtab, rowtab, tab,
                  pl.BlockSpec((tb, d), lambda i, c: (i, 0), **once),
                  pl.BlockSpec((1, 1, d), lambda i, c: (i // per_batch, 0, 0))],
        out_specs=pl.BlockSpec((tb, d), lambda i, c: (i, 0)),
        out_shape=jax.ShapeDtypeStruct((t, d), F32),
        scratch_shapes=[pltpu.VMEM((EC, tb), F32), pltpu.VMEM((EC, tb), F32),
                        pltpu.VMEM((EC, tb), BF16), pltpu.VMEM((EC, tb), BF16),
                        pltpu.VMEM((d, tb), F32),
                        pltpu.VMEM((PEER_HEADS, PEER_NKEYS, tb), F32),
                        pltpu.VMEM((PEER_HEADS, PEER_NKEYS, tb), F32)],
        compiler_params=_cparams(("arbitrary", "arbitrary")),
        name="experts",
    )(h2t, u, vt, rank1, limit, e0, e1, x1, ga2)


def kernel(x, c, w_ada, b_ada, norm1_g, norm2_g, w_in, conv_dw, conv_b, conv_ln_g, conv_ln_b,
           w_conv_out, q_norm_g, k_norm_g, rel_bias, w_attn_out, w_out, peer_wq, peer_keys,
           peer_u, peer_v):
    bsz, s, d = x.shape
    depth = w_ada.shape[0]
    bd = jnp.asarray(np.kron(np.eye(N_HEADS), np.full((HEAD_DIM, HEAD_DIM), 1.0 / HEAD_DIM)), BF16)
    for l in range(depth):
        mod = _ada(c, w_ada[l], b_ada[l])
        sh1, sc1, ga1, sh2, sc2, ga2 = [m.reshape(bsz, 1, d) for m in jnp.split(mod, 6, axis=-1)]
        u, q, k, v, gc, ga = _inproj(
            x, norm1_g[l].reshape(1, d), sc1, sh1, w_in[l].astype(BF16), bd,
            jnp.tile(q_norm_g[l], N_HEADS).reshape(1, D_ATTN),
            jnp.tile(k_norm_g[l], N_HEADS).reshape(1, D_ATTN))
        mc = _conv(u, conv_dw[l], conv_b[l].reshape(1, D_CONV), conv_ln_g[l].reshape(1, D_CONV),
                   conv_ln_b[l].reshape(1, D_CONV), w_conv_out[l].astype(BF16), gc)
        o = _attn(q, k, v, _bias_table(rel_bias[l]))
        x1, h2t = _merge(o, mc, ga, x, w_attn_out[l].astype(BF16), w_out[l].astype(BF16),
                         ga1, norm2_g[l].reshape(1, d), sc2, sh2)
        wqt = peer_wq[l].T.astype(BF16)
        keys = peer_keys[l].reshape(2 * PEER_HEADS, PEER_NKEYS, PEER_DH).astype(BF16)
        rank1, limit, e0, e1 = _route(h2t, wqt, keys)
        out = _experts(h2t, peer_u[l].astype(BF16), peer_v[l].T.astype(BF16),
                       rank1, limit, e0, e1, x1.reshape(bsz * s, d), ga2, s)
        x = out.reshape(bsz, s, d)
    return x
```

```python
import functools

import jax
import jax.numpy as jnp
import numpy as np
from jax import lax
from jax.experimental import pallas as pl
from jax.experimental.pallas import tpu as pltpu

F32 = jnp.float32
BF16 = jnp.bfloat16

D_MODEL = 1024
CHUNK = 64
N_HEADS = 8
HEAD_DIM = 64
D_ATTN = N_HEADS * HEAD_DIM
LEFT_CHUNKS = 8
REL_CLIP = 128
D_CONV = D_MODEL // 2
CONV_W = 31
PEER_HEADS = 8
PEER_NKEYS = 128
PEER_N = PEER_NKEYS * PEER_NKEYS
PEER_DH = 128
PEER_TOPK = 16
EPS = 1e-6
NEG_INF = -1e30

LANES = 128
MXU_N = 256
VMEM_LIMIT = 56 * 1024 * 1024

TM_IN = 256
TS_CONV = 512
HALO = 32
TQ = 256
NKB = 3
TM_MERGE = 256
TB_ROUTE = 512
TB_EXP = 1024
EC = 512
ROWS_PER_STEP = EC // PEER_NKEYS
_BIG = 2.0 ** 100


def _cparams(sem):
    return pltpu.CompilerParams(dimension_semantics=sem, vmem_limit_bytes=VMEM_LIMIT)


def _ada_kernel(c_ref, w_ref, b_ref, o_ref):
    c = c_ref[...]
    cond = c * jax.nn.sigmoid(c)
    o_ref[...] = jnp.dot(cond.astype(BF16), w_ref[...].astype(BF16),
                         preferred_element_type=F32) + b_ref[...]


def _ada(c, w, b):
    bsz, d = c.shape
    n = w.shape[1]
    tn = 1024
    return pl.pallas_call(
        _ada_kernel,
        grid=(n // tn,),
        in_specs=[pl.BlockSpec((bsz, d), lambda j: (0, 0)),
                  pl.BlockSpec((d, tn), lambda j: (0, j)),
                  pl.BlockSpec((1, tn), lambda j: (0, j))],
        out_specs=pl.BlockSpec((bsz, tn), lambda j: (0, j)),
        out_shape=jax.ShapeDtypeStruct((bsz, n), F32),
        compiler_params=_cparams(("arbitrary",)),
        name="ada",
    )(c, w, b.reshape(1, n))


def _head_rms(t, bd_ref, gain):
    t2 = t * t
    hi = t2.astype(BF16)
    lo = (t2 - hi.astype(F32)).astype(BF16)
    ms = (jnp.dot(hi, bd_ref[...], preferred_element_type=F32)
          + jnp.dot(lo, bd_ref[...], preferred_element_type=F32))
    return t * lax.rsqrt(ms + EPS) * gain


def _inproj_kernel(x_ref, g_ref, sc_ref, sh_ref, w_ref, bd_ref, qg_ref, kg_ref,
                   u_ref, q_ref, k_ref, v_ref, gc_ref, ga_ref):
    x = x_ref[0]
    ms = jnp.mean(x * x, axis=-1, keepdims=True)
    h = (x * lax.rsqrt(ms + EPS)) * g_ref[...]
    h = h * (1.0 + sc_ref[0]) + sh_ref[0]
    hb = h.astype(BF16)

    def seg(lo, hi):
        return jnp.dot(hb, w_ref[:, lo:hi], preferred_element_type=F32)

    o = 0
    a = seg(o, o + D_CONV); o += D_CONV
    b = seg(o, o + D_CONV); o += D_CONV
    u_ref[0] = a * jax.nn.sigmoid(b)
    q = seg(o, o + D_ATTN); o += D_ATTN
    q_ref[0] = (_head_rms(q, bd_ref, qg_ref[...]) * (HEAD_DIM ** -0.5)).astype(BF16)
    k = seg(o, o + D_ATTN); o += D_ATTN
    k_ref[0] = _head_rms(k, bd_ref, kg_ref[...]).astype(BF16)
    v_ref[0] = seg(o, o + D_ATTN).astype(BF16); o += D_ATTN
    gc_ref[0] = jax.nn.sigmoid(seg(o, o + D_MODEL)); o += D_MODEL
    ga_ref[0] = jax.nn.sigmoid(seg(o, o + D_MODEL))


def _inproj(x, g1, sc1, sh1, w_in, bd, qg, kg):
    bsz, s, d = x.shape
    tm = TM_IN
    n_in = w_in.shape[1]
    row = lambda w: pl.BlockSpec((1, tm, w), lambda b, i: (b, i, 0))
    vec = lambda w: pl.BlockSpec((1, w), lambda b, i: (0, 0))
    mod = pl.BlockSpec((1, 1, d), lambda b, i: (b, 0, 0))
    return pl.pallas_call(
        _inproj_kernel,
        grid=(bsz, s // tm),
        in_specs=[row(d), vec(d), mod, mod,
                  pl.BlockSpec((d, n_in), lambda b, i: (0, 0)),
                  pl.BlockSpec((D_ATTN, D_ATTN), lambda b, i: (0, 0)),
                  vec(D_ATTN), vec(D_ATTN)],
        out_specs=[row(D_CONV), row(D_ATTN), row(D_ATTN), row(D_ATTN), row(d), row(d)],
        out_shape=[jax.ShapeDtypeStruct((bsz, s, D_CONV), F32),
                   jax.ShapeDtypeStruct((bsz, s, D_ATTN), BF16),
                   jax.ShapeDtypeStruct((bsz, s, D_ATTN), BF16),
                   jax.ShapeDtypeStruct((bsz, s, D_ATTN), BF16),
                   jax.ShapeDtypeStruct((bsz, s, d), F32),
                   jax.ShapeDtypeStruct((bsz, s, d), F32)],
        compiler_params=_cparams(("arbitrary", "arbitrary")),
        name="inproj",
    )(x, g1, sc1, sh1, w_in, bd, qg, kg)


CONV_ROWS = 64


def _conv_kernel(u_ref, up_ref, dw_ref, cb_ref, lg_ref, lb_ref, w_ref, gc_ref,
                 o_ref, ext_ref, y_ref):
    i = pl.program_id(1)
    ts = u_ref.shape[1]
    prev = up_ref[0, ts - HALO:, :]
    ext_ref[0:HALO, :] = jnp.where(i > 0, prev, 0.0)
    ext_ref[HALO:, :] = u_ref[0]
    base = HALO - (CONV_W - 1)
    for r0 in range(0, ts, CONV_ROWS):
        acc = jnp.zeros((CONV_ROWS, D_CONV), F32) + cb_ref[...]
        for w in range(CONV_W):
            acc = acc + ext_ref[r0 + base + w:r0 + base + w + CONV_ROWS, :] * dw_ref[w:w + 1, :]
        y_ref[r0:r0 + CONV_ROWS, :] = acc
    y = y_ref[...]
    mu = jnp.mean(y, axis=-1, keepdims=True)
    yc = y - mu
    var = jnp.mean(yc * yc, axis=-1, keepdims=True)
    z = yc * lax.rsqrt(var + EPS) * lg_ref[...] + lb_ref[...]
    z = z * jax.nn.sigmoid(z)
    o = jnp.dot(z.astype(BF16), w_ref[...], preferred_element_type=F32)
    o_ref[0] = gc_ref[0] * o


def _conv(u, dw, cb, lg, lb, w_co, gc):
    bsz, s, dc = u.shape
    d = w_co.shape[1]
    ts = TS_CONV
    vec = lambda w: pl.BlockSpec((1, w), lambda b, i: (0, 0))
    return pl.pallas_call(
        _conv_kernel,
        grid=(bsz, s // ts),
        in_specs=[pl.BlockSpec((1, ts, dc), lambda b, i: (b, i, 0)),
                  pl.BlockSpec((1, ts, dc), lambda b, i: (b, jnp.maximum(i - 1, 0), 0)),
                  pl.BlockSpec((CONV_W, dc), lambda b, i: (0, 0)),
                  vec(dc), vec(dc), vec(dc),
                  pl.BlockSpec((dc, d), lambda b, i: (0, 0)),
                  pl.BlockSpec((1, ts, d), lambda b, i: (b, i, 0))],
        out_specs=pl.BlockSpec((1, ts, d), lambda b, i: (b, i, 0)),
        out_shape=jax.ShapeDtypeStruct((bsz, s, d), F32),
        scratch_shapes=[pltpu.VMEM((ts + HALO, dc), F32), pltpu.VMEM((ts, dc), F32)],
        compiler_params=_cparams(("arbitrary", "arbitrary")),
        name="conv",
    )(u, u, dw, cb, lg, lb, w_co, gc)


def _attn_kernel(q_ref, k0_ref, k1_ref, k2_ref, v0_ref, v1_ref, v2_ref, bias_ref, o_ref):
    i = pl.program_id(1)
    k_refs = (k0_ref, k1_ref, k2_ref)
    v_refs = (v0_ref, v1_ref, v2_ref)
    pens = [jnp.where(i - (NKB - 1) + j >= 0, 0.0, NEG_INF).astype(F32) for j in range(NKB)]
    for h in range(N_HEADS):
        lo = h * HEAD_DIM
        qh = q_ref[0, :, lo:lo + HEAD_DIM]
        ss = []
        for j in range(NKB):
            kh = k_refs[j][0, :, lo:lo + HEAD_DIM]
            s = lax.dot_general(qh, kh, (((1,), (1,)), ((), ())), preferred_element_type=F32)
            ss.append(s + bias_ref[h, :, j * TQ:(j + 1) * TQ] + pens[j])
        m = jnp.maximum(jnp.maximum(jnp.max(ss[0], axis=-1, keepdims=True),
                                    jnp.max(ss[1], axis=-1, keepdims=True)),
                        jnp.max(ss[2], axis=-1, keepdims=True))
        l = jnp.zeros_like(m)
        acc = jnp.zeros((TQ, HEAD_DIM), F32)
        for j in range(NKB):
            p = jnp.exp(ss[j] - m)
            l = l + jnp.sum(p, axis=-1, keepdims=True)
            vh = v_refs[j][0, :, lo:lo + HEAD_DIM]
            acc = acc + jnp.dot(p.astype(BF16), vh, preferred_element_type=F32)
        o_ref[0, :, lo:lo + HEAD_DIM] = (acc / l).astype(BF16)


def _attn(q, k, v, bias):
    bsz, s, da = q.shape
    kspec = lambda j: pl.BlockSpec(
        (1, TQ, da), lambda b, i: (b, jnp.maximum(i - (NKB - 1) + j, 0), 0))
    return pl.pallas_call(
        _attn_kernel,
        grid=(bsz, s // TQ),
        in_specs=[pl.BlockSpec((1, TQ, da), lambda b, i: (b, i, 0)),
                  kspec(0), kspec(1), kspec(2), kspec(0), kspec(1), kspec(2),
                  pl.BlockSpec((N_HEADS, TQ, NKB * TQ), lambda b, i: (0, 0, 0))],
        out_specs=pl.BlockSpec((1, TQ, da), lambda b, i: (b, i, 0)),
        out_shape=jax.ShapeDtypeStruct((bsz, s, da), BF16),
        compiler_params=_cparams(("arbitrary", "arbitrary")),
        name="attn",
    )(q, k, k, k, v, v, v, bias)


def _bias_table(rel_bias):
    nk = NKB * TQ
    lw = TQ + nk - 1
    n_lo = (TQ - 1) - REL_CLIP
    n_hi = (nk - 1) - REL_CLIP
    w = jnp.concatenate([jnp.repeat(rel_bias[:, :1], n_lo, axis=1), rel_bias,
                         jnp.repeat(rel_bias[:, -1:], n_hi, axis=1)], axis=1).astype(F32)
    w2 = jnp.roll(w[:, ::-1], -(TQ - 1), axis=1)
    flat = jnp.tile(w2, (1, TQ))[:, :TQ * (lw - 1)]
    tab = flat.reshape(-1, TQ, lw - 1)[:, :, :nk]
    qi = np.arange(TQ)[:, None]
    kj = np.arange(nk)[None, :]
    qc = qi // CHUNK + (NKB - 1) * TQ // CHUNK
    kc = kj // CHUNK
    band = (kc >= qc - LEFT_CHUNKS) & (kc <= qc)
    return jnp.where(band[None], tab, NEG_INF)


def _merge_kernel(o_ref, mc_ref, ga_ref, x_ref, wa_ref, wo_ref, ga1_ref, g2_ref, sc_ref, sh_ref,
                  x1_ref, h2t_ref):
    ya = jnp.dot(o_ref[0], wa_ref[...], preferred_element_type=F32)
    merged = mc_ref[0] + ga_ref[0] * ya
    y = jnp.dot(merged.astype(BF16), wo_ref[...], preferred_element_type=F32)
    x1 = x_ref[0] + ga1_ref[0] * y
    x1_ref[0] = x1
    ms = jnp.mean(x1 * x1, axis=-1, keepdims=True)
    h2 = (x1 * lax.rsqrt(ms + EPS)) * g2_ref[...]
    h2 = h2 * (1.0 + sc_ref[0]) + sh_ref[0]
    h2t_ref[...] = h2.T.astype(BF16)


def _merge(o, mc, ga, x, wa, wo, ga1, g2, sc2, sh2):
    bsz, s, d = x.shape
    tm = TM_MERGE
    nb = s // tm
    row = lambda w: pl.BlockSpec((1, tm, w), lambda b, i: (b, i, 0))
    mod = pl.BlockSpec((1, 1, d), lambda b, i: (b, 0, 0))
    return pl.pallas_call(
        _merge_kernel,
        grid=(bsz, nb),
        in_specs=[row(D_ATTN), row(d), row(d), row(d),
                  pl.BlockSpec((D_ATTN, d), lambda b, i: (0, 0)),
                  pl.BlockSpec((d, d), lambda b, i: (0, 0)),
                  mod, pl.BlockSpec((1, d), lambda b, i: (0, 0)), mod, mod],
        out_specs=[row(d), pl.BlockSpec((d, tm), lambda b, i: (0, b * nb + i))],
        out_shape=[jax.ShapeDtypeStruct((bsz, s, d), F32),
                   jax.ShapeDtypeStruct((d, bsz * s), BF16)],
        compiler_params=_cparams(("arbitrary", "arbitrary")),
        name="merge",
    )(o, mc, ga, x, wa, wo, ga1, g2, sc2, sh2)


_CAND = [(a, b) for a in range(PEER_TOPK) for b in range(PEER_TOPK)
         if (a + 1) * (b + 1) <= PEER_TOPK]


def _route_kernel(h2t_ref, wqt_ref, keys_ref, rank1_ref, limit_ref, e0_ref, e1_ref,
                  s_scr, rank_scr, v_scr, b_scr, zi_scr):
    tb = h2t_ref.shape[1]
    ncol = tb // LANES
    n_hp = 2 * PEER_HEADS
    qt = jnp.dot(wqt_ref[...], h2t_ref[...], preferred_element_type=F32).astype(BF16)
    for hp in range(n_hp):
        s_scr[hp] = jnp.dot(keys_ref[hp], qt[hp * PEER_DH:(hp + 1) * PEER_DH, :],
                            preferred_element_type=F32)

    def stage1(it, carry):
        h = it // ncol
        col = it % ncol
        cs = pl.ds(pl.multiple_of(col * LANES, LANES), LANES)
        rw = pl.ds(h * ncol + col, 1)
        works = [s_scr[2 * h + p, :, cs] for p in range(2)]
        for r in range(PEER_TOPK):
            for p in range(2):
                m = jnp.max(works[p], axis=0, keepdims=True)
                works[p] = jnp.where(works[p] == m, -(r + 1) * _BIG, works[p])
                v_scr[p, r, rw, :] = m
        for p in range(2):
            rank_scr[2 * h + p, :, cs] = jnp.where(
                works[p] <= -_BIG, works[p] * (-1.0 / _BIG) - 1.0, float(PEER_TOPK))
        return carry

    lax.fori_loop(0, PEER_HEADS * ncol, stage1, 0)

    v0 = [v_scr[0, a] for a in range(PEER_TOPK)]
    v1 = [v_scr[1, b] for b in range(PEER_TOPK)]
    cand = [v0[a] + v1[b] for (a, b) in _CAND]
    top = cand[0]
    work = list(cand)
    for r in range(PEER_TOPK):
        m = functools.reduce(jnp.maximum, work)
        work = [jnp.where(w == m, -jnp.inf, w) for w in work]
    z = jnp.zeros_like(top)
    cnt = [jnp.zeros_like(top) for _ in range(PEER_TOPK)]
    for (a, b), c, w in zip(_CAND, cand, work):
        sel = w == -jnp.inf
        z = z + jnp.where(sel, jnp.exp(c - top), 0.0)
        cnt[a] = cnt[a] + jnp.where(sel, 1.0, 0.0)
    for a in range(PEER_TOPK):
        b_scr[a] = cnt[a]
    zi_scr[...] = 1.0 / z

    def stage3(it, carry):
        h = it // ncol
        col = it % ncol
        cs = pl.ds(pl.multiple_of(col * LANES, LANES), LANES)
        rw = pl.ds(h * ncol + col, 1)
        rank0 = rank_scr[2 * h, :, cs]
        limit = jnp.zeros(rank0.shape, F32)
        for a in range(PEER_TOPK):
            limit = jnp.where(rank0 == float(a), b_scr[a, rw, :], limit)
        e0 = jnp.exp(s_scr[2 * h, :, cs] - v_scr[0, 0, rw, :])
        for g in range(PEER_NKEYS // ROWS_PER_STEP):
            rows = slice(g * ROWS_PER_STEP, (g + 1) * ROWS_PER_STEP)
            limit_ref[h, g, :, cs] = limit[rows]
            e0_ref[h, g, :, cs] = e0[rows]
        rank1_ref[h, :, cs] = rank_scr[2 * h + 1, :, cs].astype(BF16)
        e1_ref[h, :, cs] = (jnp.exp(s_scr[2 * h + 1, :, cs] - v_scr[1, 0, rw, :])
                            * zi_scr[rw, :]).astype(BF16)
        return carry

    lax.fori_loop(0, PEER_HEADS * ncol, stage3, 0)


def _route(h2t, wqt, keys):
    d, t = h2t.shape
    tb = TB_ROUTE
    ncol = tb // LANES
    n_hp = 2 * PEER_HEADS
    n_grp = PEER_NKEYS // ROWS_PER_STEP
    tab = pl.BlockSpec((PEER_HEADS, PEER_NKEYS, tb), lambda i: (0, 0, i))
    rowtab = pl.BlockSpec((PEER_HEADS, n_grp, ROWS_PER_STEP, tb), lambda i: (0, 0, 0, i))
    tab_shape = jax.ShapeDtypeStruct((PEER_HEADS, PEER_NKEYS, t), BF16)
    rowtab_shape = jax.ShapeDtypeStruct((PEER_HEADS, n_grp, ROWS_PER_STEP, t), F32)
    return pl.pallas_call(
        _route_kernel,
        grid=(t // tb,),
        in_specs=[pl.BlockSpec((d, tb), lambda i: (0, i)),
                  pl.BlockSpec(wqt.shape, lambda i: (0, 0)),
                  pl.BlockSpec(keys.shape, lambda i: (0, 0, 0))],
        out_specs=[tab, rowtab, rowtab, tab],
        out_shape=[tab_shape, rowtab_shape, rowtab_shape, tab_shape],
        scratch_shapes=[pltpu.VMEM((n_hp, PEER_NKEYS, tb), F32),
                        pltpu.VMEM((n_hp, PEER_NKEYS, tb), F32),
                        pltpu.VMEM((2, PEER_TOPK, PEER_HEADS * ncol, LANES), F32),
                        pltpu.VMEM((PEER_TOPK, PEER_HEADS * ncol, LANES), F32),
                        pltpu.VMEM((PEER_HEADS * ncol, LANES), F32)],
        compiler_params=_cparams(("arbitrary",)),
        name="route",
    )(h2t, wqt, keys)


_INV_SQRT2 = float(1.0 / np.sqrt(2.0))
K_PIECE = 256
JG_BLOCK = 8


def _experts_kernel(h2t_ref, u_ref, vt_ref, rank1_ref, limit_ref, e0_ref, e1_ref,
                    x1_ref, ga2_ref, o_ref, a0_scr, a1_scr, p0_scr, p1_scr, acc_ref,
                    r1_scr, e1_scr):
    c = pl.program_id(1)
    tb = h2t_ref.shape[1]

    @pl.when(c == 0)
    def _():
        acc_ref[...] = jnp.zeros_like(acc_ref)
        a1_scr[...] = jnp.zeros_like(a1_scr)
        p0_scr[...] = jnp.zeros_like(p0_scr)
        for h in range(PEER_HEADS):
            r1_scr[h] = rank1_ref[h].astype(F32)
            e1_scr[h] = e1_ref[h].astype(F32)

    sub = (8, LANES)
    n_jg = PEER_NKEYS // sub[0]
    zero = jnp.zeros(sub, F32)

    def pair_body(k, refs):
        a_new, a_old, p_new, p_old = refs
        base = k * MXU_N
        ps = pl.ds(base, MXU_N)

        def mm_a(kc):
            ks = slice(kc * K_PIECE, (kc + 1) * K_PIECE)
            part = jnp.dot(u_ref[:, ks], h2t_ref[ks, ps], preferred_element_type=F32)
            if kc == 0:
                a_new[:, ps] = part
            else:
                a_new[:, ps] += part

        def mm_acc(kc, mh):
            ks = slice(kc * K_PIECE, (kc + 1) * K_PIECE)
            ms = slice(mh * EC, (mh + 1) * EC)
            acc_ref[ms, ps] += jnp.dot(vt_ref[ms, ks], p_old[ks, ps],
                                       preferred_element_type=F32)

        mm_pieces = ([functools.partial(mm_a, kc) for kc in range(D_MODEL // K_PIECE)]
                     + [functools.partial(mm_acc, kc, mh) for kc in range(EC // K_PIECE)
                        for mh in range(D_MODEL // EC)])
        blocks = [(half, ip, j0) for half in range(MXU_N // LANES)
                  for ip in range(ROWS_PER_STEP // 2) for j0 in range(0, n_jg, JG_BLOCK)]
        every = len(blocks) // len(mm_pieces)
        for bi, (half, ip, j0) in enumerate(blocks):
            if bi % every == 0:
                mm_pieces[bi // every]()
            cs = pl.ds(base + half * LANES, LANES)
            if True:
                rows = (2 * ip, 2 * ip + 1)
                g = [[zero] * JG_BLOCK for _ in rows]
                for h in range(PEER_HEADS):
                    lim = [jnp.broadcast_to(limit_ref[h, 0, r:r + 1, cs], sub) for r in rows]
                    e0 = [jnp.broadcast_to(e0_ref[h, 0, r:r + 1, cs], sub) for r in rows]
                    for jg in range(JG_BLOCK):
                        js = slice((j0 + jg) * sub[0], (j0 + jg + 1) * sub[0])
                        r1 = r1_scr[h, js, cs]
                        e1 = e1_scr[h, js, cs]
                        for q in range(2):
                            g[q][jg] = g[q][jg] + jnp.where(r1 < lim[q], e1, zero) * e0[q]
                for q, r in enumerate(rows):
                    for jg in range(0, JG_BLOCK, 2):
                        lo = r * PEER_NKEYS + (j0 + jg) * sub[0]
                        a = a_old[lo:lo + 2 * sub[0], cs]
                        hf = 0.5 * a
                        act = hf + hf * lax.erf(a * _INV_SQRT2)
                        gg = jnp.concatenate([g[q][jg], g[q][jg + 1]], axis=0)
                        p_new[lo:lo + 2 * sub[0], cs] = (act * gg).astype(BF16)

    even = (a0_scr, a1_scr, p1_scr, p0_scr)
    odd = (a1_scr, a0_scr, p0_scr, p1_scr)
    for parity, refs in ((0, even), (1, odd)):
        @pl.when(c % 2 == parity)
        def _(refs=refs):
            for k in range(tb // MXU_N):
                pair_body(k, refs)

    @pl.when(c == pl.num_programs(1) - 1)
    def _():
        o_ref[...] = x1_ref[...] + ga2_ref[0] * acc_ref[...].T


def _experts(h2t, u, vt, rank1, limit, e0, e1, x1, ga2, seq):
    d, t = h2t.shape
    tb = TB_EXP
    n_chunks = u.shape[0] // EC
    last = n_chunks - 1
    per_batch = seq // tb
    once = dict(pipeline_mode=pl.Buffered(1))
    tab = pl.BlockSpec((PEER_HEADS, PEER_NKEYS, tb), lambda i, c: (0, 0, i), **once)
    rowtab = pl.BlockSpec((PEER_HEADS, 1, ROWS_PER_STEP, tb),
                          lambda i, c: (0, jnp.clip(c - 1, 0, last), 0, i))
    return pl.pallas_call(
        _experts_kernel,
        grid=(t // tb, n_chunks + 2),
        in_specs=[pl.BlockSpec((d, tb), lambda i, c: (0, i)),
                  pl.BlockSpec((EC, d), lambda i, c: (jnp.minimum(c, last), 0)),
                  pl.BlockSpec((d, EC), lambda i, c: (0, jnp.clip(c - 2, 0, last))),
                  tab, rowtab, rowtab, tab,
                  pl.BlockSpec((tb, d), lambda i, c: (i, 0), **once),
                  pl.BlockSpec((1, 1, d), lambda i, c: (i // per_batch, 0, 0))],
        out_specs=pl.BlockSpec((tb, d), lambda i, c: (i, 0)),
        out_shape=jax.ShapeDtypeStruct((t, d), F32),
        scratch_shapes=[pltpu.VMEM((EC, tb), F32), pltpu.VMEM((EC, tb), F32),
                        pltpu.VMEM((EC, tb), BF16), pltpu.VMEM((EC, tb), BF16),
                        pltpu.VMEM((d, tb), F32),
                        pltpu.VMEM((PEER_HEADS, PEER_NKEYS, tb), F32),
                        pltpu.VMEM((PEER_HEADS, PEER_NKEYS, tb), F32)],
        compiler_params=_cparams(("arbitrary", "arbitrary")),
        name="experts",
    )(h2t, u, vt, rank1, limit, e0, e1, x1, ga2)


def kernel(x, c, w_ada, b_ada, norm1_g, norm2_g, w_in, conv_dw, conv_b, conv_ln_g, conv_ln_b,
           w_conv_out, q_norm_g, k_norm_g, rel_bias, w_attn_out, w_out, peer_wq, peer_keys,
           peer_u, peer_v):
    bsz, s, d = x.shape
    depth = w_ada.shape[0]
    bd = jnp.asarray(np.kron(np.eye(N_HEADS), np.full((HEAD_DIM, HEAD_DIM), 1.0 / HEAD_DIM)), BF16)
    for l in range(depth):
        mod = _ada(c, w_ada[l], b_ada[l])
        sh1, sc1, ga1, sh2, sc2, ga2 = [m.reshape(bsz, 1, d) for m in jnp.split(mod, 6, axis=-1)]
        u, q, k, v, gc, ga = _inproj(
            x, norm1_g[l].reshape(1, d), sc1, sh1, w_in[l].astype(BF16), bd,
            jnp.tile(q_norm_g[l], N_HEADS).reshape(1, D_ATTN),
            jnp.tile(k_norm_g[l], N_HEADS).reshape(1, D_ATTN))
        mc = _conv(u, conv_dw[l], conv_b[l].reshape(1, D_CONV), conv_ln_g[l].reshape(1, D_CONV),
                   conv_ln_b[l].reshape(1, D_CONV), w_conv_out[l].astype(BF16), gc)
        o = _attn(q, k, v, _bias_table(rel_bias[l]))
        x1, h2t = _merge(o, mc, ga, x, w_attn_out[l].astype(BF16), w_out[l].astype(BF16),
                         ga1, norm2_g[l].reshape(1, d), sc2, sh2)
        wqt = peer_wq[l].T.astype(BF16)
        keys = peer_keys[l].reshape(2 * PEER_HEADS, PEER_NKEYS, PEER_DH).astype(BF16)
        rank1, limit, e0, e1 = _route(h2t, wqt, keys)
        out = _experts(h2t, peer_u[l].astype(BF16), peer_v[l].T.astype(BF16),
                       rank1, limit, e0, e1, x1.reshape(bsz * s, d), ga2, s)
        x = out.reshape(bsz, s, d)
    return x
```

```python
import functools

import jax
import jax.numpy as jnp
import numpy as np
from jax import lax
from jax.experimental import pallas as pl
from jax.experimental.pallas import tpu as pltpu

F32 = jnp.float32
BF16 = jnp.bfloat16

D_MODEL = 1024
CHUNK = 64
N_HEADS = 8
HEAD_DIM = 64
D_ATTN = N_HEADS * HEAD_DIM
LEFT_CHUNKS = 8
REL_CLIP = 128
D_CONV = D_MODEL // 2
CONV_W = 31
PEER_HEADS = 8
PEER_NKEYS = 128
PEER_N = PEER_NKEYS * PEER_NKEYS
PEER_DH = 128
PEER_TOPK = 16
EPS = 1e-6
NEG_INF = -1e30

LANES = 128
MXU_N = 256
VMEM_LIMIT = 56 * 1024 * 1024

TM_IN = 512
TS_CONV = 512
HALO = 32
TQ = 256
NKB = 3
TM_MERGE = 256
TB_ROUTE = 512
TB_EXP = 1024
EC = 512
ROWS_PER_STEP = EC // PEER_NKEYS

def _cparams(sem):
    return pltpu.CompilerParams(dimension_semantics=sem, vmem_limit_bytes=VMEM_LIMIT)


def _ada_kernel(c_ref, w_ref, b_ref, o_ref):
    c = c_ref[...]
    cond = c * jax.nn.sigmoid(c)
    o_ref[...] = jnp.dot(cond.astype(BF16), w_ref[...].astype(BF16),
                         preferred_element_type=F32) + b_ref[...]


def _ada(c, w, b):
    bsz, d = c.shape
    n = w.shape[1]
    tn = 1024
    return pl.pallas_call(
        _ada_kernel,
        grid=(n // tn,),
        in_specs=[pl.BlockSpec((bsz, d), lambda j: (0, 0)),
                  pl.BlockSpec((d, tn), lambda j: (0, j)),
                  pl.BlockSpec((1, tn), lambda j: (0, j))],
        out_specs=pl.BlockSpec((bsz, tn), lambda j: (0, j)),
        out_shape=jax.ShapeDtypeStruct((bsz, n), F32),
        compiler_params=_cparams(("arbitrary",)),
        name="ada",
    )(c, w, b.reshape(1, n))


def _head_rms(t, bd_ref, gain):
    t2 = t * t
    hi = t2.astype(BF16)
    lo = (t2 - hi.astype(F32)).astype(BF16)
    ms = (jnp.dot(hi, bd_ref[...], preferred_element_type=F32)
          + jnp.dot(lo, bd_ref[...], preferred_element_type=F32))
    return t * lax.rsqrt(ms + EPS) * gain


def _inproj_kernel(x_ref, g_ref, sc_ref, sh_ref, w_ref, bd_ref, qg_ref, kg_ref,
                   u_ref, q_ref, k_ref, v_ref, gc_ref, ga_ref):
    x = x_ref[0]
    ms = jnp.mean(x * x, axis=-1, keepdims=True)
    h = (x * lax.rsqrt(ms + EPS)) * g_ref[...]
    h = h * (1.0 + sc_ref[0]) + sh_ref[0]
    hb = h.astype(BF16)

    def seg(lo, hi):
        return jnp.dot(hb, w_ref[:, lo:hi], preferred_element_type=F32)

    o = 0
    a = seg(o, o + D_CONV); o += D_CONV
    b = seg(o, o + D_CONV); o += D_CONV
    u_ref[0] = a * jax.nn.sigmoid(b)
    q = seg(o, o + D_ATTN); o += D_ATTN
    q_ref[0] = (_head_rms(q, bd_ref, qg_ref[...]) * (HEAD_DIM ** -0.5)).astype(BF16)
    k = seg(o, o + D_ATTN); o += D_ATTN
    k_ref[0] = _head_rms(k, bd_ref, kg_ref[...]).astype(BF16)
    v_ref[0] = seg(o, o + D_ATTN).astype(BF16); o += D_ATTN
    gc_ref[0] = jax.nn.sigmoid(seg(o, o + D_MODEL)); o += D_MODEL
    ga_ref[0] = jax.nn.sigmoid(seg(o, o + D_MODEL))


def _inproj(x, g1, sc1, sh1, w_in, bd, qg, kg):
    bsz, s, d = x.shape
    tm = TM_IN
    n_in = w_in.shape[1]
    row = lambda w: pl.BlockSpec((1, tm, w), lambda b, i: (b, i, 0))
    vec = lambda w: pl.BlockSpec((1, w), lambda b, i: (0, 0))
    mod = pl.BlockSpec((1, 1, d), lambda b, i: (b, 0, 0))
    return pl.pallas_call(
        _inproj_kernel,
        grid=(bsz, s // tm),
        in_specs=[row(d), vec(d), mod, mod,
                  pl.BlockSpec((d, n_in), lambda b, i: (0, 0), pipeline_mode=pl.Buffered(1)),
                  pl.BlockSpec((D_ATTN, D_ATTN), lambda b, i: (0, 0)),
                  vec(D_ATTN), vec(D_ATTN)],
        out_specs=[row(D_CONV), row(D_ATTN), row(D_ATTN), row(D_ATTN), row(d), row(d)],
        out_shape=[jax.ShapeDtypeStruct((bsz, s, D_CONV), F32),
                   jax.ShapeDtypeStruct((bsz, s, D_ATTN), BF16),
                   jax.ShapeDtypeStruct((bsz, s, D_ATTN), BF16),
                   jax.ShapeDtypeStruct((bsz, s, D_ATTN), BF16),
                   jax.ShapeDtypeStruct((bsz, s, d), F32),
                   jax.ShapeDtypeStruct((bsz, s, d), F32)],
        compiler_params=_cparams(("arbitrary", "arbitrary")),
        name="inproj",
    )(x, g1, sc1, sh1, w_in, bd, qg, kg)


CONV_ROWS = 64


def _conv_kernel(u_ref, up_ref, dw_ref, cb_ref, lg_ref, lb_ref, w_ref, gc_ref,
                 o_ref, ext_ref, y_ref):
    i = pl.program_id(1)
    ts = u_ref.shape[1]
    prev = up_ref[0, ts - HALO:, :]
    ext_ref[0:HALO, :] = jnp.where(i > 0, prev, 0.0)
    ext_ref[HALO:, :] = u_ref[0]
    base = HALO - (CONV_W - 1)
    for r0 in range(0, ts, CONV_ROWS):
        acc = jnp.zeros((CONV_ROWS, D_CONV), F32) + cb_ref[...]
        for w in range(CONV_W):
            acc = acc + ext_ref[r0 + base + w:r0 + base + w + CONV_ROWS, :] * dw_ref[w:w + 1, :]
        y_ref[r0:r0 + CONV_ROWS, :] = acc
    y = y_ref[...]
    mu = jnp.mean(y, axis=-1, keepdims=True)
    yc = y - mu
    var = jnp.mean(yc * yc, axis=-1, keepdims=True)
    z = yc * lax.rsqrt(var + EPS) * lg_ref[...] + lb_ref[...]
    z = z * jax.nn.sigmoid(z)
    o = jnp.dot(z.astype(BF16), w_ref[...], preferred_element_type=F32)
    o_ref[0] = gc_ref[0] * o


def _conv(u, dw, cb, lg, lb, w_co, gc):
    bsz, s, dc = u.shape
    d = w_co.shape[1]
    ts = TS_CONV
    vec = lambda w: pl.BlockSpec((1, w), lambda b, i: (0, 0))
    return pl.pallas_call(
        _conv_kernel,
        grid=(bsz, s // ts),
        in_specs=[pl.BlockSpec((1, ts, dc), lambda b, i: (b, i, 0)),
                  pl.BlockSpec((1, ts, dc), lambda b, i: (b, jnp.maximum(i - 1, 0), 0)),
                  pl.BlockSpec((CONV_W, dc), lambda b, i: (0, 0)),
                  vec(dc), vec(dc), vec(dc),
                  pl.BlockSpec((dc, d), lambda b, i: (0, 0)),
                  pl.BlockSpec((1, ts, d), lambda b, i: (b, i, 0))],
        out_specs=pl.BlockSpec((1, ts, d), lambda b, i: (b, i, 0)),
        out_shape=jax.ShapeDtypeStruct((bsz, s, d), F32),
        scratch_shapes=[pltpu.VMEM((ts + HALO, dc), F32), pltpu.VMEM((ts, dc), F32)],
        compiler_params=_cparams(("arbitrary", "arbitrary")),
        name="conv",
    )(u, u, dw, cb, lg, lb, w_co, gc)


def _attn_kernel(q_ref, k0_ref, k1_ref, k2_ref, v0_ref, v1_ref, v2_ref, bias_ref, o_ref):
    i = pl.program_id(1)
    k_refs = (k0_ref, k1_ref, k2_ref)
    v_refs = (v0_ref, v1_ref, v2_ref)
    pens = [jnp.where(i - (NKB - 1) + j >= 0, 0.0, NEG_INF).astype(F32) for j in range(NKB)]
    for h in range(N_HEADS):
        lo = h * HEAD_DIM
        qh = q_ref[0, :, lo:lo + HEAD_DIM]
        ss = []
        for j in range(NKB):
            kh = k_refs[j][0, :, lo:lo + HEAD_DIM]
            s = lax.dot_general(qh, kh, (((1,), (1,)), ((), ())), preferred_element_type=F32)
            ss.append(s + bias_ref[h, :, j * TQ:(j + 1) * TQ] + pens[j])
        m = jnp.maximum(jnp.maximum(jnp.max(ss[0], axis=-1, keepdims=True),
                                    jnp.max(ss[1], axis=-1, keepdims=True)),
                        jnp.max(ss[2], axis=-1, keepdims=True))
        l = jnp.zeros_like(m)
        acc = jnp.zeros((TQ, HEAD_DIM), F32)
        for j in range(NKB):
            p = jnp.exp(ss[j] - m)
            l = l + jnp.sum(p, axis=-1, keepdims=True)
            vh = v_refs[j][0, :, lo:lo + HEAD_DIM]
            acc = acc + jnp.dot(p.astype(BF16), vh, preferred_element_type=F32)
        o_ref[0, :, lo:lo + HEAD_DIM] = (acc / l).astype(BF16)


def _attn(q, k, v, bias):
    bsz, s, da = q.shape
    kspec = lambda j: pl.BlockSpec(
        (1, TQ, da), lambda b, i: (b, jnp.maximum(i - (NKB - 1) + j, 0), 0))
    return pl.pallas_call(
        _attn_kernel,
        grid=(bsz, s // TQ),
        in_specs=[pl.BlockSpec((1, TQ, da), lambda b, i: (b, i, 0)),
                  kspec(0), kspec(1), kspec(2), kspec(0), kspec(1), kspec(2),
                  pl.BlockSpec((N_HEADS, TQ, NKB * TQ), lambda b, i: (0, 0, 0))],
        out_specs=pl.BlockSpec((1, TQ, da), lambda b, i: (b, i, 0)),
        out_shape=jax.ShapeDtypeStruct((bsz, s, da), BF16),
        compiler_params=_cparams(("arbitrary", "arbitrary")),
        name="attn",
    )(q, k, k, k, v, v, v, bias)


def _bias_table(rel_bias):
    nk = NKB * TQ
    lw = TQ + nk - 1
    n_lo = (TQ - 1) - REL_CLIP
    n_hi = (nk - 1) - REL_CLIP
    w = jnp.concatenate([jnp.repeat(rel_bias[:, :1], n_lo, axis=1), rel_bias,
                         jnp.repeat(rel_bias[:, -1:], n_hi, axis=1)], axis=1).astype(F32)
    w2 = jnp.roll(w[:, ::-1], -(TQ - 1), axis=1)
    flat = jnp.tile(w2, (1, TQ))[:, :TQ * (lw - 1)]
    tab = flat.reshape(-1, TQ, lw - 1)[:, :, :nk]
    qi = np.arange(TQ)[:, None]
    kj = np.arange(nk)[None, :]
    qc = qi // CHUNK + (NKB - 1) * TQ // CHUNK
    kc = kj // CHUNK
    band = (kc >= qc - LEFT_CHUNKS) & (kc <= qc)
    return jnp.where(band[None], tab, NEG_INF)


def _merge_kernel(o_ref, mc_ref, ga_ref, x_ref, wa_ref, wo_ref, ga1_ref, g2_ref, sc_ref, sh_ref,
                  x1_ref, h2t_ref):
    ya = jnp.dot(o_ref[0], wa_ref[...], preferred_element_type=F32)
    merged = mc_ref[0] + ga_ref[0] * ya
    y = jnp.dot(merged.astype(BF16), wo_ref[...], preferred_element_type=F32)
    x1 = x_ref[0] + ga1_ref[0] * y
    x1_ref[0] = x1
    ms = jnp.mean(x1 * x1, axis=-1, keepdims=True)
    h2 = (x1 * lax.rsqrt(ms + EPS)) * g2_ref[...]
    h2 = h2 * (1.0 + sc_ref[0]) + sh_ref[0]
    h2t_ref[...] = h2.T.astype(BF16)


def _merge(o, mc, ga, x, wa, wo, ga1, g2, sc2, sh2):
    bsz, s, d = x.shape
    tm = TM_MERGE
    nb = s // tm
    row = lambda w: pl.BlockSpec((1, tm, w), lambda b, i: (b, i, 0))
    mod = pl.BlockSpec((1, 1, d), lambda b, i: (b, 0, 0))
    return pl.pallas_call(
        _merge_kernel,
        grid=(bsz, nb),
        in_specs=[row(D_ATTN), row(d), row(d), row(d),
                  pl.BlockSpec((D_ATTN, d), lambda b, i: (0, 0)),
                  pl.BlockSpec((d, d), lambda b, i: (0, 0)),
                  mod, pl.BlockSpec((1, d), lambda b, i: (0, 0)), mod, mod],
        out_specs=[row(d), pl.BlockSpec((d, tm), lambda b, i: (0, b * nb + i))],
        out_shape=[jax.ShapeDtypeStruct((bsz, s, d), F32),
                   jax.ShapeDtypeStruct((d, bsz * s), BF16)],
        compiler_params=_cparams(("arbitrary", "arbitrary")),
        name="merge",
    )(o, mc, ga, x, wa, wo, ga1, g2, sc2, sh2)


_CAND = [(a, b) for a in range(PEER_TOPK) for b in range(PEER_TOPK)
         if (a + 1) * (b + 1) <= PEER_TOPK]


def _sort16_network():
    n, pairs, p = PEER_TOPK, [], 1
    while p < n:
        k = p
        while k >= 1:
            for j in range(k % p, n - k, 2 * k):
                for i in range(min(k, n - j - k)):
                    if (i + j) // (2 * p) == (i + j + k) // (2 * p):
                        pairs.append((i + j, i + j + k))
            k //= 2
        p *= 2
    return pairs


_SORT16 = _sort16_network()


def _top16_sorted(problems, store_row):
    sub = 8
    lvs = [[s[g * sub:(g + 1) * sub, :] for g in range(PEER_NKEYS // sub)] for s in problems]
    for a, b in _SORT16:
        for lv in lvs:
            lv[a], lv[b] = jnp.maximum(lv[a], lv[b]), jnp.minimum(lv[a], lv[b])
    for r in range(PEER_TOPK):
        for p, lv in enumerate(lvs):
            m = jnp.max(lv[0], axis=0, keepdims=True)
            store_row(p, r, m)
            eq = lv[0] == m
            for k in range(PEER_TOPK - 1 - r):
                lv[k] = jnp.where(eq, lv[k + 1], lv[k])


def _route_kernel(h2t_ref, wqt_ref, keys_ref, s1_ref, theta_ref, e0_ref, e1_ref,
                  s_scr, v_scr, tau_scr, zi_scr):
    tb = h2t_ref.shape[1]
    ncol = tb // LANES
    n_hp = 2 * PEER_HEADS
    qt = jnp.dot(wqt_ref[...], h2t_ref[...], preferred_element_type=F32).astype(BF16)
    for hp in range(n_hp):
        s_scr[hp] = jnp.dot(keys_ref[hp], qt[hp * PEER_DH:(hp + 1) * PEER_DH, :],
                            preferred_element_type=F32)

    def stage1(it, carry):
        h = it // ncol
        col = it % ncol
        cs = pl.ds(pl.multiple_of(col * LANES, LANES), LANES)
        rw = pl.ds(h * ncol + col, 1)

        def store_row(p, r, m):
            v_scr[p, r, rw, :] = m

        _top16_sorted([s_scr[2 * h + p, :, cs] for p in range(2)], store_row)
        return carry

    lax.fori_loop(0, PEER_HEADS * ncol, stage1, 0)

    v0 = [v_scr[0, a] for a in range(PEER_TOPK)]
    v1 = [v_scr[1, b] for b in range(PEER_TOPK)]
    cand = [v0[a] + v1[b] for (a, b) in _CAND]
    top = cand[0]
    work = list(cand)
    for r in range(PEER_TOPK):
        c16 = functools.reduce(jnp.maximum, work)
        work = [jnp.where(w == c16, -jnp.inf, w) for w in work]
    c17 = functools.reduce(jnp.maximum, work)
    z = jnp.zeros_like(top)
    for c in cand:
        z = z + jnp.where(c >= c16, jnp.exp(c - top), 0.0)
    tau_scr[...] = 0.5 * (c16 + c17)
    zi_scr[...] = 1.0 / z

    def stage3(it, carry):
        h = it // ncol
        col = it % ncol
        cs = pl.ds(pl.multiple_of(col * LANES, LANES), LANES)
        rw = pl.ds(h * ncol + col, 1)
        s0 = s_scr[2 * h, :, cs]
        s1 = s_scr[2 * h + 1, :, cs]
        theta = jnp.where(s0 >= v_scr[0, PEER_TOPK - 1, rw, :], tau_scr[rw, :] - s0, jnp.inf)
        e0 = jnp.exp(s0 - v_scr[0, 0, rw, :])
        for g in range(PEER_NKEYS // ROWS_PER_STEP):
            rows = slice(g * ROWS_PER_STEP, (g + 1) * ROWS_PER_STEP)
            theta_ref[h, g, :, cs] = theta[rows]
            e0_ref[h, g, :, cs] = e0[rows]
        s1_ref[h, :, cs] = jnp.where(s1 >= v_scr[1, PEER_TOPK - 1, rw, :], s1, -jnp.inf)
        e1_ref[h, :, cs] = jnp.exp(s1 - v_scr[1, 0, rw, :]) * zi_scr[rw, :]
        return carry

    lax.fori_loop(0, PEER_HEADS * ncol, stage3, 0)


def _route(h2t, wqt, keys):
    d, t = h2t.shape
    tb = TB_ROUTE
    ncol = tb // LANES
    n_hp = 2 * PEER_HEADS
    n_grp = PEER_NKEYS // ROWS_PER_STEP
    tab = pl.BlockSpec((PEER_HEADS, PEER_NKEYS, tb), lambda i: (0, 0, i))
    rowtab = pl.BlockSpec((PEER_HEADS, n_grp, ROWS_PER_STEP, tb), lambda i: (0, 0, 0, i))
    tab_shape = jax.ShapeDtypeStruct((PEER_HEADS, PEER_NKEYS, t), F32)
    rowtab_shape = jax.ShapeDtypeStruct((PEER_HEADS, n_grp, ROWS_PER_STEP, t), F32)
    return pl.pallas_call(
        _route_kernel,
        grid=(t // tb,),
        in_specs=[pl.BlockSpec((d, tb), lambda i: (0, i)),
                  pl.BlockSpec(wqt.shape, lambda i: (0, 0)),
                  pl.BlockSpec(keys.shape, lambda i: (0, 0, 0))],
        out_specs=[tab, rowtab, rowtab, tab],
        out_shape=[tab_shape, rowtab_shape, rowtab_shape, tab_shape],
        scratch_shapes=[pltpu.VMEM((n_hp, PEER_NKEYS, tb), F32),
                        pltpu.VMEM((2, PEER_TOPK, PEER_HEADS * ncol, LANES), F32),
                        pltpu.VMEM((PEER_HEADS * ncol, LANES), F32),
                        pltpu.VMEM((PEER_HEADS * ncol, LANES), F32)],
        compiler_params=_cparams(("arbitrary",)),
        name="route",
    )(h2t, wqt, keys)


_INV_SQRT2 = float(1.0 / np.sqrt(2.0))
K_PIECE = 256
JG_BLOCK = 8


def _experts_kernel(h2t_ref, u_ref, vt_ref, s1_ref, theta_ref, e0_ref, e1_ref,
                    x1_ref, ga2_ref, o_ref, a0_scr, a1_scr, p0_scr, p1_scr, acc_ref):
    c = pl.program_id(1)
    tb = h2t_ref.shape[1]

    @pl.when(c == 0)
    def _():
        acc_ref[...] = jnp.zeros_like(acc_ref)
        a1_scr[...] = jnp.zeros_like(a1_scr)
        p0_scr[...] = jnp.zeros_like(p0_scr)

    sub = (8, LANES)
    n_jg = PEER_NKEYS // sub[0]
    zero = jnp.zeros(sub, F32)

    def pair_body(k, refs):
        a_new, a_old, p_new, p_old = refs
        base = k * MXU_N
        ps = pl.ds(base, MXU_N)

        def mm_a(kc):
            ks = slice(kc * K_PIECE, (kc + 1) * K_PIECE)
            part = jnp.dot(u_ref[:, ks], h2t_ref[ks, ps], preferred_element_type=F32)
            if kc == 0:
                a_new[:, ps] = part
            else:
                a_new[:, ps] += part

        def mm_acc(kc, mh):
            ks = slice(kc * K_PIECE, (kc + 1) * K_PIECE)
            ms = slice(mh * EC, (mh + 1) * EC)
            acc_ref[ms, ps] += jnp.dot(vt_ref[ms, ks], p_old[ks, ps],
                                       preferred_element_type=F32)

        mm_pieces = ([functools.partial(mm_a, kc) for kc in range(D_MODEL // K_PIECE)]
                     + [functools.partial(mm_acc, kc, mh) for kc in range(EC // K_PIECE)
                        for mh in range(D_MODEL // EC)])
        blocks = [(half, ip, j0) for half in range(MXU_N // LANES)
                  for ip in range(ROWS_PER_STEP // 2) for j0 in range(0, n_jg, JG_BLOCK)]
        every = len(blocks) // len(mm_pieces)
        for bi, (half, ip, j0) in enumerate(blocks):
            if bi % every == 0:
                mm_pieces[bi // every]()
            cs = pl.ds(base + half * LANES, LANES)
            if True:
                rows = (2 * ip, 2 * ip + 1)
                g = [[zero] * JG_BLOCK for _ in rows]
                for h in range(PEER_HEADS):
                    th = [jnp.broadcast_to(theta_ref[h, 0, r:r + 1, cs], sub) for r in rows]
                    e0 = [jnp.broadcast_to(e0_ref[h, 0, r:r + 1, cs], sub) for r in rows]
                    for jg in range(JG_BLOCK):
                        js = slice((j0 + jg) * sub[0], (j0 + jg + 1) * sub[0])
                        s1 = s1_ref[h, js, cs]
                        e1 = e1_ref[h, js, cs]
                        for q in range(2):
                            g[q][jg] = g[q][jg] + jnp.where(s1 >= th[q], e1, zero) * e0[q]
                for q, r in enumerate(rows):
                    for jg in range(0, JG_BLOCK, 2):
                        lo = r * PEER_NKEYS + (j0 + jg) * sub[0]
                        a = a_old[lo:lo + 2 * sub[0], cs]
                        hf = 0.5 * a
                        act = hf + hf * lax.erf(a * _INV_SQRT2)
                        gg = jnp.concatenate([g[q][jg], g[q][jg + 1]], axis=0)
                        p_new[lo:lo + 2 * sub[0], cs] = (act * gg).astype(BF16)

    even = (a0_scr, a1_scr, p1_scr, p0_scr)
    odd = (a1_scr, a0_scr, p0_scr, p1_scr)
    for parity, refs in ((0, even), (1, odd)):
        @pl.when(c % 2 == parity)
        def _(refs=refs):
            for k in range(tb // MXU_N):
                pair_body(k, refs)

    @pl.when(c == pl.num_programs(1) - 1)
    def _():
        o_ref[...] = x1_ref[...] + ga2_ref[0] * acc_ref[...].T


def _experts(h2t, u, vt, s1, theta, e0, e1, x1, ga2, seq):
    d, t = h2t.shape
    tb = TB_EXP
    n_chunks = u.shape[0] // EC
    last = n_chunks - 1
    per_batch = seq // tb
    once = dict(pipeline_mode=pl.Buffered(1))
    tab = pl.BlockSpec((PEER_HEADS, PEER_NKEYS, tb), lambda i, c: (0, 0, i), **once)
    rowtab = pl.BlockSpec((PEER_HEADS, 1, ROWS_PER_STEP, tb),
                          lambda i, c: (0, jnp.clip(c - 1, 0, last), 0, i))
    return pl.pallas_call(
        _experts_kernel,
        grid=(t // tb, n_chunks + 2),
        in_specs=[pl.BlockSpec((d, tb), lambda i, c: (0, i)),
                  pl.BlockSpec((EC, d), lambda i, c: (jnp.minimum(c, last), 0)),
                  pl.BlockSpec((d, EC), lambda i, c: (0, jnp.clip(c - 2, 0, last))),
                  tab, rowtab, rowtab, tab,
                  pl.BlockSpec((tb, d), lambda i, c: (i, 0), **once),
                  pl.BlockSpec((1, 1, d), lambda i, c: (i // per_batch, 0, 0))],
        out_specs=pl.BlockSpec((tb, d), lambda i, c: (i, 0)),
        out_shape=jax.ShapeDtypeStruct((t, d), F32),
        scratch_shapes=[pltpu.VMEM((EC, tb), F32), pltpu.VMEM((EC, tb), F32),
                        pltpu.VMEM((EC, tb), BF16), pltpu.VMEM((EC, tb), BF16),
                        pltpu.VMEM((d, tb), F32)],
        compiler_params=_cparams(("arbitrary", "arbitrary")),
        name="experts",
    )(h2t, u, vt, s1, theta, e0, e1, x1, ga2)


def kernel(x, c, w_ada, b_ada, norm1_g, norm2_g, w_in, conv_dw, conv_b, conv_ln_g, conv_ln_b,
           w_conv_out, q_norm_g, k_norm_g, rel_bias, w_attn_out, w_out, peer_wq, peer_keys,
           peer_u, peer_v):
    bsz, s, d = x.shape
    depth = w_ada.shape[0]
    bd = jnp.asarray(np.kron(np.eye(N_HEADS), np.full((HEAD_DIM, HEAD_DIM), 1.0 / HEAD_DIM)), BF16)
    for l in range(depth):
        mod = _ada(c, w_ada[l], b_ada[l])
        sh1, sc1, ga1, sh2, sc2, ga2 = [m.reshape(bsz, 1, d) for m in jnp.split(mod, 6, axis=-1)]
        u, q, k, v, gc, ga = _inproj(
            x, norm1_g[l].reshape(1, d), sc1, sh1, w_in[l].astype(BF16), bd,
            jnp.tile(q_norm_g[l], N_HEADS).reshape(1, D_ATTN),
            jnp.tile(k_norm_g[l], N_HEADS).reshape(1, D_ATTN))
        mc = _conv(u, conv_dw[l], conv_b[l].reshape(1, D_CONV), conv_ln_g[l].reshape(1, D_CONV),
                   conv_ln_b[l].reshape(1, D_CONV), w_conv_out[l].astype(BF16), gc)
        o = _attn(q, k, v, _bias_table(rel_bias[l]))
        x1, h2t = _merge(o, mc, ga, x, w_attn_out[l].astype(BF16), w_out[l].astype(BF16),
                         ga1, norm2_g[l].reshape(1, d), sc2, sh2)
        wqt = peer_wq[l].T.astype(BF16)
        keys = peer_keys[l].reshape(2 * PEER_HEADS, PEER_NKEYS, PEER_DH).astype(BF16)
        s1, theta, e0, e1 = _route(h2t, wqt, keys)
        out = _experts(h2t, peer_u[l].astype(BF16), peer_v[l].T.astype(BF16),
                       s1, theta, e0, e1, x1.reshape(bsz * s, d), ga2, s)
        x = out.reshape(bsz, s, d)
    return x
```

```python
import functools

import jax
import jax.numpy as jnp
import numpy as np
from jax import lax
from jax.experimental import pallas as pl
from jax.experimental.pallas import tpu as pltpu

F32 = jnp.float32
BF16 = jnp.bfloat16

D_MODEL = 1024
CHUNK = 64
N_HEADS = 8
HEAD_DIM = 64
D_ATTN = N_HEADS * HEAD_DIM
LEFT_CHUNKS = 8
REL_CLIP = 128
D_CONV = D_MODEL // 2
CONV_W = 31
PEER_HEADS = 8
PEER_NKEYS = 128
PEER_N = PEER_NKEYS * PEER_NKEYS
PEER_DH = 128
PEER_TOPK = 16
EPS = 1e-6
NEG_INF = -1e30
LOG2E = float(np.log2(np.e))

LANES = 128
SUBLANES = 8
MXU_N = 256
VMEM_LIMIT = 56 * 1024 * 1024

TM_IN = 512
TS_CONV = 512
HALO = 32
TQ = 256
NKB = 3
TM_MERGE = 256
TB_ROUTE = 512
TB_EXP = 1024
EC = 512
ROWS_PER_STEP = EC // PEER_NKEYS

def _cparams(sem):
    return pltpu.CompilerParams(dimension_semantics=sem, vmem_limit_bytes=VMEM_LIMIT)


def _ada_kernel(c_ref, w_ref, b_ref, o_ref):
    c = c_ref[...]
    cond = c * jax.nn.sigmoid(c)
    o_ref[...] = jnp.dot(cond.astype(BF16), w_ref[...].astype(BF16),
                         preferred_element_type=F32) + b_ref[...]


def _ada(c, w, b):
    bsz, d = c.shape
    n = w.shape[1]
    tn = 1024
    return pl.pallas_call(
        _ada_kernel,
        grid=(n // tn,),
        in_specs=[pl.BlockSpec((bsz, d), lambda j: (0, 0)),
                  pl.BlockSpec((d, tn), lambda j: (0, j)),
                  pl.BlockSpec((1, tn), lambda j: (0, j))],
        out_specs=pl.BlockSpec((bsz, tn), lambda j: (0, j)),
        out_shape=jax.ShapeDtypeStruct((bsz, n), F32),
        compiler_params=_cparams(("arbitrary",)),
        name="ada",
    )(c, w, b.reshape(1, n))


def _head_rms(t, bd_ref, gain):
    t2 = t * t
    hi = t2.astype(BF16)
    lo = (t2 - hi.astype(F32)).astype(BF16)
    ms = (jnp.dot(hi, bd_ref[...], preferred_element_type=F32)
          + jnp.dot(lo, bd_ref[...], preferred_element_type=F32))
    return t * lax.rsqrt(ms + EPS) * gain


def _inproj_kernel(x_ref, g_ref, sc_ref, sh_ref, w_ref, bd_ref, qg_ref, kg_ref,
                   u_ref, q_ref, k_ref, v_ref, gc_ref, ga_ref):
    x = x_ref[0]
    ms = jnp.mean(x * x, axis=-1, keepdims=True)
    h = (x * lax.rsqrt(ms + EPS)) * g_ref[...]
    h = h * (1.0 + sc_ref[0]) + sh_ref[0]
    hb = h.astype(BF16)

    def seg(lo, hi):
        return jnp.dot(hb, w_ref[:, lo:hi], preferred_element_type=F32)

    o = 0
    a = seg(o, o + D_CONV); o += D_CONV
    b = seg(o, o + D_CONV); o += D_CONV
    u_ref[0] = a * jax.nn.sigmoid(b)
    q = seg(o, o + D_ATTN); o += D_ATTN
    q_ref[0] = (_head_rms(q, bd_ref, qg_ref[...]) * (HEAD_DIM ** -0.5 * LOG2E)).astype(BF16)
    k = seg(o, o + D_ATTN); o += D_ATTN
    k_ref[0] = _head_rms(k, bd_ref, kg_ref[...]).astype(BF16)
    v_ref[0] = seg(o, o + D_ATTN).astype(BF16); o += D_ATTN
    gc_ref[0] = jax.nn.sigmoid(seg(o, o + D_MODEL)); o += D_MODEL
    ga_ref[0] = jax.nn.sigmoid(seg(o, o + D_MODEL))


def _inproj(x, g1, sc1, sh1, w_in, bd, qg, kg):
    bsz, s, d = x.shape
    tm = TM_IN
    n_in = w_in.shape[1]
    row = lambda w: pl.BlockSpec((1, tm, w), lambda b, i: (b, i, 0))
    vec = lambda w: pl.BlockSpec((1, w), lambda b, i: (0, 0))
    mod = pl.BlockSpec((1, 1, d), lambda b, i: (b, 0, 0))
    return pl.pallas_call(
        _inproj_kernel,
        grid=(bsz, s // tm),
        in_specs=[row(d), vec(d), mod, mod,
                  pl.BlockSpec((d, n_in), lambda b, i: (0, 0), pipeline_mode=pl.Buffered(1)),
                  pl.BlockSpec((D_ATTN, D_ATTN), lambda b, i: (0, 0)),
                  vec(D_ATTN), vec(D_ATTN)],
        out_specs=[row(D_CONV), row(D_ATTN), row(D_ATTN), row(D_ATTN), row(d), row(d)],
        out_shape=[jax.ShapeDtypeStruct((bsz, s, D_CONV), F32),
                   jax.ShapeDtypeStruct((bsz, s, D_ATTN), BF16),
                   jax.ShapeDtypeStruct((bsz, s, D_ATTN), BF16),
                   jax.ShapeDtypeStruct((bsz, s, D_ATTN), BF16),
                   jax.ShapeDtypeStruct((bsz, s, d), F32),
                   jax.ShapeDtypeStruct((bsz, s, d), F32)],
        compiler_params=_cparams(("arbitrary", "arbitrary")),
        name="inproj",
    )(x, g1, sc1, sh1, w_in, bd, qg, kg)


CONV_ROWS = 64


def _conv_kernel(u_ref, up_ref, dw_ref, cb_ref, lg_ref, lb_ref, w_ref, gc_ref,
                 o_ref, ext_ref, y_ref):
    i = pl.program_id(1)
    ts = u_ref.shape[1]
    prev = up_ref[0, ts - HALO:, :]
    ext_ref[0, 0:HALO, :] = jnp.where(i > 0, prev, 0.0)
    ext_ref[0, HALO:, :] = u_ref[0]
    n_sh = ts + HALO - SUBLANES
    for s in range(1, SUBLANES):
        for r0 in range(0, n_sh, CONV_ROWS):
            n = min(CONV_ROWS, n_sh - r0)
            ext_ref[s, r0:r0 + n, :] = ext_ref[0, r0 + s:r0 + s + n, :]
    base = HALO - (CONV_W - 1)
    for r0 in range(0, ts, CONV_ROWS):
        acc = jnp.zeros((CONV_ROWS, D_CONV), F32) + cb_ref[...]
        for w in range(CONV_W):
            s = (base + w) % SUBLANES
            a = r0 + base + w - s
            acc = acc + ext_ref[s, a:a + CONV_ROWS, :] * dw_ref[w:w + 1, :]
        y_ref[r0:r0 + CONV_ROWS, :] = acc
    y = y_ref[...]
    mu = jnp.mean(y, axis=-1, keepdims=True)
    yc = y - mu
    var = jnp.mean(yc * yc, axis=-1, keepdims=True)
    z = yc * lax.rsqrt(var + EPS) * lg_ref[...] + lb_ref[...]
    z = z * jax.nn.sigmoid(z)
    o = jnp.dot(z.astype(BF16), w_ref[...], preferred_element_type=F32)
    o_ref[0] = gc_ref[0] * o


def _conv(u, dw, cb, lg, lb, w_co, gc):
    bsz, s, dc = u.shape
    d = w_co.shape[1]
    ts = TS_CONV
    vec = lambda w: pl.BlockSpec((1, w), lambda b, i: (0, 0))
    return pl.pallas_call(
        _conv_kernel,
        grid=(bsz, s // ts),
        in_specs=[pl.BlockSpec((1, ts, dc), lambda b, i: (b, i, 0)),
                  pl.BlockSpec((1, ts, dc), lambda b, i: (b, jnp.maximum(i - 1, 0), 0)),
                  pl.BlockSpec((CONV_W, dc), lambda b, i: (0, 0)),
                  vec(dc), vec(dc), vec(dc),
                  pl.BlockSpec((dc, d), lambda b, i: (0, 0)),
                  pl.BlockSpec((1, ts, d), lambda b, i: (b, i, 0))],
        out_specs=pl.BlockSpec((1, ts, d), lambda b, i: (b, i, 0)),
        out_shape=jax.ShapeDtypeStruct((bsz, s, d), F32),
        scratch_shapes=[pltpu.VMEM((SUBLANES, ts + HALO, dc), F32), pltpu.VMEM((ts, dc), F32)],
        compiler_params=_cparams(("arbitrary", "arbitrary")),
        name="conv",
    )(u, u, dw, cb, lg, lb, w_co, gc)


def _attn_kernel(q_ref, k0_ref, k1_ref, k2_ref, v0_ref, v1_ref, v2_ref, bias_ref, o_ref):
    i = pl.program_id(1)
    k_refs = (k0_ref, k1_ref, k2_ref)
    v_refs = (v0_ref, v1_ref, v2_ref)
    low = lax.broadcasted_iota(jnp.int32, (TQ, LANES), 1) < HEAD_DIM

    def body(pens):
        for hp in range(N_HEADS * HEAD_DIM // LANES):
            lo = hp * LANES
            q2 = q_ref[0, :, lo:lo + LANES]
            outs = []
            for half in range(LANES // HEAD_DIM):
                h = hp * (LANES // HEAD_DIM) + half
                qh = jnp.where(low if half == 0 else jnp.logical_not(low), q2, jnp.zeros_like(q2))
                ss = []
                for j in range(NKB):
                    s = lax.dot_general(qh, k_refs[j][0, :, lo:lo + LANES],
                                        (((1,), (1,)), ((), ())), preferred_element_type=F32)
                    s = s + bias_ref[h, :, j * TQ:(j + 1) * TQ]
                    ss.append(s if pens is None else s + pens[j])
                m = jnp.maximum(jnp.maximum(jnp.max(ss[0], axis=-1, keepdims=True),
                                            jnp.max(ss[1], axis=-1, keepdims=True)),
                                jnp.max(ss[2], axis=-1, keepdims=True))
                l = jnp.zeros_like(m)
                acc = jnp.zeros((TQ, LANES), F32)
                for j in range(NKB):
                    p = jnp.exp2(ss[j] - m)
                    l = l + jnp.sum(p, axis=-1, keepdims=True)
                    acc = acc + jnp.dot(p.astype(BF16), v_refs[j][0, :, lo:lo + LANES],
                                        preferred_element_type=F32)
                outs.append(acc / l)
            o_ref[0, :, lo:lo + LANES] = jnp.where(low, outs[0], outs[1]).astype(BF16)

    @pl.when(i >= NKB - 1)
    def _():
        body(None)

    @pl.when(i < NKB - 1)
    def _():
        body([jnp.where(i - (NKB - 1) + j >= 0, 0.0, NEG_INF).astype(F32) for j in range(NKB)])


def _attn(q, k, v, bias):
    bsz, s, da = q.shape
    kspec = lambda j: pl.BlockSpec(
        (1, TQ, da), lambda b, i: (b, jnp.maximum(i - (NKB - 1) + j, 0), 0))
    return pl.pallas_call(
        _attn_kernel,
        grid=(bsz, s // TQ),
        in_specs=[pl.BlockSpec((1, TQ, da), lambda b, i: (b, i, 0)),
                  kspec(0), kspec(1), kspec(2), kspec(0), kspec(1), kspec(2),
                  pl.BlockSpec((N_HEADS, TQ, NKB * TQ), lambda b, i: (0, 0, 0))],
        out_specs=pl.BlockSpec((1, TQ, da), lambda b, i: (b, i, 0)),
        out_shape=jax.ShapeDtypeStruct((bsz, s, da), BF16),
        compiler_params=_cparams(("arbitrary", "arbitrary")),
        name="attn",
    )(q, k, k, k, v, v, v, bias)


def _bias_table(rel_bias):
    nk = NKB * TQ
    lw = TQ + nk - 1
    n_lo = (TQ - 1) - REL_CLIP
    n_hi = (nk - 1) - REL_CLIP
    w = jnp.concatenate([jnp.repeat(rel_bias[:, :1], n_lo, axis=1), rel_bias,
                         jnp.repeat(rel_bias[:, -1:], n_hi, axis=1)], axis=1).astype(F32)
    w2 = jnp.roll(w[:, ::-1], -(TQ - 1), axis=1)
    flat = jnp.tile(w2, (1, TQ))[:, :TQ * (lw - 1)]
    tab = flat.reshape(-1, TQ, lw - 1)[:, :, :nk]
    qi = np.arange(TQ)[:, None]
    kj = np.arange(nk)[None, :]
    qc = qi // CHUNK + (NKB - 1) * TQ // CHUNK
    kc = kj // CHUNK
    band = (kc >= qc - LEFT_CHUNKS) & (kc <= qc)
    return jnp.where(band[None], tab * LOG2E, NEG_INF)


def _merge_kernel(o_ref, mc_ref, ga_ref, x_ref, wa_ref, wo_ref, ga1_ref, g2_ref, sc_ref, sh_ref,
                  x1_ref, h2t_ref):
    ya = jnp.dot(o_ref[0], wa_ref[...], preferred_element_type=F32)
    merged = mc_ref[0] + ga_ref[0] * ya
    y = jnp.dot(merged.astype(BF16), wo_ref[...], preferred_element_type=F32)
    x1 = x_ref[0] + ga1_ref[0] * y
    x1_ref[0] = x1
    ms = jnp.mean(x1 * x1, axis=-1, keepdims=True)
    h2 = (x1 * lax.rsqrt(ms + EPS)) * g2_ref[...]
    h2 = h2 * (1.0 + sc_ref[0]) + sh_ref[0]
    h2t_ref[...] = h2.T.astype(BF16)


def _merge(o, mc, ga, x, wa, wo, ga1, g2, sc2, sh2):
    bsz, s, d = x.shape
    tm = TM_MERGE
    nb = s // tm
    row = lambda w: pl.BlockSpec((1, tm, w), lambda b, i: (b, i, 0))
    mod = pl.BlockSpec((1, 1, d), lambda b, i: (b, 0, 0))
    return pl.pallas_call(
        _merge_kernel,
        grid=(bsz, nb),
        in_specs=[row(D_ATTN), row(d), row(d), row(d),
                  pl.BlockSpec((D_ATTN, d), lambda b, i: (0, 0)),
                  pl.BlockSpec((d, d), lambda b, i: (0, 0)),
                  mod, pl.BlockSpec((1, d), lambda b, i: (0, 0)), mod, mod],
        out_specs=[row(d), pl.BlockSpec((d, tm), lambda b, i: (0, b * nb + i))],
        out_shape=[jax.ShapeDtypeStruct((bsz, s, d), F32),
                   jax.ShapeDtypeStruct((d, bsz * s), BF16)],
        compiler_params=_cparams(("arbitrary", "arbitrary")),
        name="merge",
    )(o, mc, ga, x, wa, wo, ga1, g2, sc2, sh2)


_CAND = [(a, b) for a in range(PEER_TOPK) for b in range(PEER_TOPK)
         if (a + 1) * (b + 1) <= PEER_TOPK]


def _sort16_network():
    n, pairs, p = PEER_TOPK, [], 1
    while p < n:
        k = p
        while k >= 1:
            for j in range(k % p, n - k, 2 * k):
                for i in range(min(k, n - j - k)):
                    if (i + j) // (2 * p) == (i + j + k) // (2 * p):
                        pairs.append((i + j, i + j + k))
            k //= 2
        p *= 2
    return pairs


_SORT16 = _sort16_network()


def _top16_sorted(problems, store_row):
    sub = 8
    lvs = [[s[g * sub:(g + 1) * sub, :] for g in range(PEER_NKEYS // sub)] for s in problems]
    for a, b in _SORT16:
        for lv in lvs:
            lv[a], lv[b] = jnp.maximum(lv[a], lv[b]), jnp.minimum(lv[a], lv[b])
    for r in range(PEER_TOPK):
        for p, lv in enumerate(lvs):
            m = jnp.max(lv[0], axis=0, keepdims=True)
            store_row(p, r, m)
            eq = lv[0] == m
            for k in range(PEER_TOPK - 1 - r):
                lv[k] = jnp.where(eq, lv[k + 1], lv[k])


def _route_kernel(h2t_ref, wqt_ref, keys_ref, s1_ref, theta_ref, e0_ref, e1_ref,
                  s_scr, v_scr, tau_scr, zi_scr):
    tb = h2t_ref.shape[1]
    ncol = tb // LANES
    n_hp = 2 * PEER_HEADS
    qt = jnp.dot(wqt_ref[...], h2t_ref[...], preferred_element_type=F32).astype(BF16)
    for hp in range(n_hp):
        s_scr[hp] = jnp.dot(keys_ref[hp], qt[hp * PEER_DH:(hp + 1) * PEER_DH, :],
                            preferred_element_type=F32)

    def stage1(it, carry):
        h = it // ncol
        col = it % ncol
        cs = pl.ds(pl.multiple_of(col * LANES, LANES), LANES)
        rw = pl.ds(h * ncol + col, 1)

        def store_row(p, r, m):
            v_scr[p, r, rw, :] = m

        _top16_sorted([s_scr[2 * h + p, :, cs] for p in range(2)], store_row)
        return carry

    lax.fori_loop(0, PEER_HEADS * ncol, stage1, 0)

    v0 = [v_scr[0, a] for a in range(PEER_TOPK)]
    v1 = [v_scr[1, b] for b in range(PEER_TOPK)]
    cand = [v0[a] + v1[b] for (a, b) in _CAND]
    top = cand[0]
    work = list(cand)
    for r in range(PEER_TOPK):
        c16 = functools.reduce(jnp.maximum, work)
        work = [jnp.where(w == c16, -jnp.inf, w) for w in work]
    c17 = functools.reduce(jnp.maximum, work)
    z = jnp.zeros_like(top)
    for c in cand:
        z = z + jnp.where(c >= c16, jnp.exp(c - top), 0.0)
    tau_scr[...] = 0.5 * (c16 + c17)
    zi_scr[...] = 0.5 / z

    def stage3(it, carry):
        h = it // ncol
        col = it % ncol
        cs = pl.ds(pl.multiple_of(col * LANES, LANES), LANES)
        rw = pl.ds(h * ncol + col, 1)
        s0 = s_scr[2 * h, :, cs]
        s1 = s_scr[2 * h + 1, :, cs]
        theta = jnp.where(s0 >= v_scr[0, PEER_TOPK - 1, rw, :], tau_scr[rw, :] - s0, jnp.inf)
        e0 = jnp.exp(s0 - v_scr[0, 0, rw, :])
        for g in range(PEER_NKEYS // ROWS_PER_STEP):
            rows = slice(g * ROWS_PER_STEP, (g + 1) * ROWS_PER_STEP)
            theta_ref[h, g, :, cs] = theta[rows]
            e0_ref[h, g, :, cs] = e0[rows]
        s1_ref[h, :, cs] = jnp.where(s1 >= v_scr[1, PEER_TOPK - 1, rw, :], s1, -jnp.inf)
        e1_ref[h, :, cs] = jnp.exp(s1 - v_scr[1, 0, rw, :]) * zi_scr[rw, :]
        return carry

    lax.fori_loop(0, PEER_HEADS * ncol, stage3, 0)


def _route(h2t, wqt, keys):
    d, t = h2t.shape
    tb = TB_ROUTE
    ncol = tb // LANES
    n_hp = 2 * PEER_HEADS
    n_grp = PEER_NKEYS // ROWS_PER_STEP
    tab = pl.BlockSpec((PEER_HEADS, PEER_NKEYS, tb), lambda i: (0, 0, i))
    rowtab = pl.BlockSpec((PEER_HEADS, n_grp, ROWS_PER_STEP, tb), lambda i: (0, 0, 0, i))
    tab_shape = jax.ShapeDtypeStruct((PEER_HEADS, PEER_NKEYS, t), F32)
    rowtab_shape = jax.ShapeDtypeStruct((PEER_HEADS, n_grp, ROWS_PER_STEP, t), F32)
    return pl.pallas_call(
        _route_kernel,
        grid=(t // tb,),
        in_specs=[pl.BlockSpec((d, tb), lambda i: (0, i)),
                  pl.BlockSpec(wqt.shape, lambda i: (0, 0)),
                  pl.BlockSpec(keys.shape, lambda i: (0, 0, 0))],
        out_specs=[tab, rowtab, rowtab, tab],
        out_shape=[tab_shape, rowtab_shape, rowtab_shape, tab_shape],
        scratch_shapes=[pltpu.VMEM((n_hp, PEER_NKEYS, tb), F32),
                        pltpu.VMEM((2, PEER_TOPK, PEER_HEADS * ncol, LANES), F32),
                        pltpu.VMEM((PEER_HEADS * ncol, LANES), F32),
                        pltpu.VMEM((PEER_HEADS * ncol, LANES), F32)],
        compiler_params=_cparams(("arbitrary",)),
        name="route",
    )(h2t, wqt, keys)


_INV_SQRT2 = float(1.0 / np.sqrt(2.0))
K_PIECE = 256
JG_BLOCK = 8


def _experts_kernel(h2t_ref, u_ref, vt_ref, s1_ref, theta_ref, e0_ref, e1_ref,
                    x1_ref, ga2_ref, o_ref, a0_scr, a1_scr, p0_scr, p1_scr, acc_ref):
    c = pl.program_id(1)
    tb = h2t_ref.shape[1]

    @pl.when(c == 0)
    def _():
        acc_ref[...] = jnp.zeros_like(acc_ref)
        a1_scr[...] = jnp.zeros_like(a1_scr)
        p0_scr[...] = jnp.zeros_like(p0_scr)

    sub = (8, LANES)
    n_jg = PEER_NKEYS // sub[0]
    zero = jnp.zeros(sub, F32)

    def pair_body(k, refs):
        a_new, a_old, p_new, p_old = refs
        base = k * MXU_N
        ps = pl.ds(base, MXU_N)

        def mm_a(kc):
            ks = slice(kc * K_PIECE, (kc + 1) * K_PIECE)
            part = jnp.dot(u_ref[:, ks], h2t_ref[ks, ps], preferred_element_type=F32)
            if kc == 0:
                a_new[:, ps] = part
            else:
                a_new[:, ps] += part

        def mm_acc(kc, mh):
            ks = slice(kc * K_PIECE, (kc + 1) * K_PIECE)
            ms = slice(mh * EC, (mh + 1) * EC)
            acc_ref[ms, ps] += jnp.dot(vt_ref[ms, ks], p_old[ks, ps],
                                       preferred_element_type=F32)

        mm_pieces = ([functools.partial(mm_a, kc) for kc in range(D_MODEL // K_PIECE)]
                     + [functools.partial(mm_acc, kc, mh) for kc in range(EC // K_PIECE)
                        for mh in range(D_MODEL // EC)])
        blocks = [(half, ip, j0) for half in range(MXU_N // LANES)
                  for ip in range(ROWS_PER_STEP // 2) for j0 in range(0, n_jg, JG_BLOCK)]
        every = len(blocks) // len(mm_pieces)
        for bi, (half, ip, j0) in enumerate(blocks):
            if bi % every == 0:
                mm_pieces[bi // every]()
            cs = pl.ds(base + half * LANES, LANES)
            if True:
                rows = (2 * ip, 2 * ip + 1)
                g = [[zero] * JG_BLOCK for _ in rows]
                for h in range(PEER_HEADS):
                    th = [jnp.broadcast_to(theta_ref[h, 0, r:r + 1, cs], sub) for r in rows]
                    e0 = [jnp.broadcast_to(e0_ref[h, 0, r:r + 1, cs], sub) for r in rows]
                    for jg in range(JG_BLOCK):
                        js = slice((j0 + jg) * sub[0], (j0 + jg + 1) * sub[0])
                        s1 = s1_ref[h, js, cs]
                        e1 = e1_ref[h, js, cs]
                        for q in range(2):
                            g[q][jg] = g[q][jg] + jnp.where(s1 >= th[q], e1, zero) * e0[q]
                for q, r in enumerate(rows):
                    for jg in range(0, JG_BLOCK, 2):
                        lo = r * PEER_NKEYS + (j0 + jg) * sub[0]
                        a = a_old[lo:lo + 2 * sub[0], cs]
                        act = a + a * lax.erf(a * _INV_SQRT2)
                        gg = jnp.concatenate([g[q][jg], g[q][jg + 1]], axis=0)
                        p_new[lo:lo + 2 * sub[0], cs] = (act * gg).astype(BF16)

    even = (a0_scr, a1_scr, p1_scr, p0_scr)
    odd = (a1_scr, a0_scr, p0_scr, p1_scr)
    for parity, refs in ((0, even), (1, odd)):
        @pl.when(c % 2 == parity)
        def _(refs=refs):
            for k in range(tb // MXU_N):
                pair_body(k, refs)

    @pl.when(c == pl.num_programs(1) - 1)
    def _():
        o_ref[...] = x1_ref[...] + ga2_ref[0] * acc_ref[...].T


def _experts(h2t, u, vt, s1, theta, e0, e1, x1, ga2, seq):
    d, t = h2t.shape
    tb = TB_EXP
    n_chunks = u.shape[0] // EC
    last = n_chunks - 1
    per_batch = seq // tb
    once = dict(pipeline_mode=pl.Buffered(1))
    tab = pl.BlockSpec((PEER_HEADS, PEER_NKEYS, tb), lambda i, c: (0, 0, i), **once)
    rowtab = pl.BlockSpec((PEER_HEADS, 1, ROWS_PER_STEP, tb),
                          lambda i, c: (0, jnp.clip(c - 1, 0, last), 0, i))
    return pl.pallas_call(
        _experts_kernel,
        grid=(t // tb, n_chunks + 2),
        in_specs=[pl.BlockSpec((d, tb), lambda i, c: (0, i)),
                  pl.BlockSpec((EC, d), lambda i, c: (jnp.minimum(c, last), 0)),
                  pl.BlockSpec((d, EC), lambda i, c: (0, jnp.clip(c - 2, 0, last))),
                  tab, rowtab, rowtab, tab,
                  pl.BlockSpec((tb, d), lambda i, c: (i, 0), **once),
                  pl.BlockSpec((1, 1, d), lambda i, c: (i // per_batch, 0, 0))],
        out_specs=pl.BlockSpec((tb, d), lambda i, c: (i, 0)),
        out_shape=jax.ShapeDtypeStruct((t, d), F32),
        scratch_shapes=[pltpu.VMEM((EC, tb), F32), pltpu.VMEM((EC, tb), F32),
                        pltpu.VMEM((EC, tb), BF16), pltpu.VMEM((EC, tb), BF16),
                        pltpu.VMEM((d, tb), F32)],
        compiler_params=_cparams(("arbitrary", "arbitrary")),
        name="experts",
    )(h2t, u, vt, s1, theta, e0, e1, x1, ga2)


def kernel(x, c, w_ada, b_ada, norm1_g, norm2_g, w_in, conv_dw, conv_b, conv_ln_g, conv_ln_b,
           w_conv_out, q_norm_g, k_norm_g, rel_bias, w_attn_out, w_out, peer_wq, peer_keys,
           peer_u, peer_v):
    bsz, s, d = x.shape
    depth = w_ada.shape[0]
    bd = jnp.asarray(np.kron(np.eye(N_HEADS), np.full((HEAD_DIM, HEAD_DIM), 1.0 / HEAD_DIM)), BF16)
    for l in range(depth):
        mod = _ada(c, w_ada[l], b_ada[l])
        sh1, sc1, ga1, sh2, sc2, ga2 = [m.reshape(bsz, 1, d) for m in jnp.split(mod, 6, axis=-1)]
        u, q, k, v, gc, ga = _inproj(
            x, norm1_g[l].reshape(1, d), sc1, sh1, w_in[l].astype(BF16), bd,
            jnp.tile(q_norm_g[l], N_HEADS).reshape(1, D_ATTN),
            jnp.tile(k_norm_g[l], N_HEADS).reshape(1, D_ATTN))
        mc = _conv(u, conv_dw[l], conv_b[l].reshape(1, D_CONV), conv_ln_g[l].reshape(1, D_CONV),
                   conv_ln_b[l].reshape(1, D_CONV), w_conv_out[l].astype(BF16), gc)
        o = _attn(q, k, v, _bias_table(rel_bias[l]))
        x1, h2t = _merge(o, mc, ga, x, w_attn_out[l].astype(BF16), w_out[l].astype(BF16),
                         ga1, norm2_g[l].reshape(1, d), sc2, sh2)
        wqt = peer_wq[l].T.astype(BF16)
        keys = peer_keys[l].reshape(2 * PEER_HEADS, PEER_NKEYS, PEER_DH).astype(BF16)
        s1, theta, e0, e1 = _route(h2t, wqt, keys)
        out = _experts(h2t, peer_u[l].astype(BF16), peer_v[l].T.astype(BF16),
                       s1, theta, e0, e1, x1.reshape(bsz * s, d), ga2, s)
        x = out.reshape(bsz, s, d)
    return x
```

```python
import functools

import jax
import jax.numpy as jnp
import numpy as np
from jax import lax
from jax.experimental import pallas as pl
from jax.experimental.pallas import tpu as pltpu

F32 = jnp.float32
BF16 = jnp.bfloat16

D_MODEL = 1024
CHUNK = 64
N_HEADS = 8
HEAD_DIM = 64
D_ATTN = N_HEADS * HEAD_DIM
LEFT_CHUNKS = 8
REL_CLIP = 128
D_CONV = D_MODEL // 2
CONV_W = 31
PEER_HEADS = 8
PEER_NKEYS = 128
PEER_N = PEER_NKEYS * PEER_NKEYS
PEER_DH = 128
PEER_TOPK = 16
EPS = 1e-6
NEG_INF = -1e30
LOG2E = float(np.log2(np.e))

LANES = 128
SUBLANES = 8
MXU_N = 256
VMEM_LIMIT = 56 * 1024 * 1024

TM_IN = 512
TS_CONV = 512
HALO = 32
TQ = 256
NKB = 3
TM_MERGE = 512
TB_ROUTE = 512
TB_EXP = 1024
EC = 512
ROWS_PER_STEP = EC // PEER_NKEYS

def _cparams(sem):
    return pltpu.CompilerParams(dimension_semantics=sem, vmem_limit_bytes=VMEM_LIMIT)


def _ada_kernel(c_ref, w_ref, b_ref, o_ref):
    c = c_ref[...]
    cond = c * jax.nn.sigmoid(c)
    o_ref[...] = jnp.dot(cond.astype(BF16), w_ref[...].astype(BF16),
                         preferred_element_type=F32) + b_ref[...]


def _ada(c, w, b):
    bsz, d = c.shape
    n = w.shape[1]
    tn = 1024
    return pl.pallas_call(
        _ada_kernel,
        grid=(n // tn,),
        in_specs=[pl.BlockSpec((bsz, d), lambda j: (0, 0)),
                  pl.BlockSpec((d, tn), lambda j: (0, j)),
                  pl.BlockSpec((1, tn), lambda j: (0, j))],
        out_specs=pl.BlockSpec((bsz, tn), lambda j: (0, j)),
        out_shape=jax.ShapeDtypeStruct((bsz, n), F32),
        compiler_params=_cparams(("arbitrary",)),
        name="ada",
    )(c, w, b.reshape(1, n))


def _head_rms(t, bd_ref, gain):
    t2 = t * t
    hi = t2.astype(BF16)
    lo = (t2 - hi.astype(F32)).astype(BF16)
    ms = (jnp.dot(hi, bd_ref[...], preferred_element_type=F32)
          + jnp.dot(lo, bd_ref[...], preferred_element_type=F32))
    return t * lax.rsqrt(ms + EPS) * gain


def _inproj_kernel(x_ref, g_ref, sc_ref, sh_ref, w_ref, bd_ref, qg_ref, kg_ref,
                   u_ref, q_ref, k_ref, v_ref, gc_ref, ga_ref):
    x = x_ref[0]
    ms = jnp.mean(x * x, axis=-1, keepdims=True)
    h = (x * lax.rsqrt(ms + EPS)) * g_ref[...]
    h = h * (1.0 + sc_ref[0]) + sh_ref[0]
    hb = h.astype(BF16)

    def seg(lo, hi):
        return jnp.dot(hb, w_ref[:, lo:hi], preferred_element_type=F32)

    o = 0
    a = seg(o, o + D_CONV); o += D_CONV
    b = seg(o, o + D_CONV); o += D_CONV
    u_ref[0] = a * jax.nn.sigmoid(b)
    q = seg(o, o + D_ATTN); o += D_ATTN
    q_ref[0] = (_head_rms(q, bd_ref, qg_ref[...]) * (HEAD_DIM ** -0.5 * LOG2E)).astype(BF16)
    k = seg(o, o + D_ATTN); o += D_ATTN
    k_ref[0] = _head_rms(k, bd_ref, kg_ref[...]).astype(BF16)
    v_ref[0] = seg(o, o + D_ATTN).astype(BF16); o += D_ATTN
    gc_ref[0] = jax.nn.sigmoid(seg(o, o + D_MODEL)); o += D_MODEL
    ga_ref[0] = jax.nn.sigmoid(seg(o, o + D_MODEL))


def _inproj(x, g1, sc1, sh1, w_in, bd, qg, kg):
    bsz, s, d = x.shape
    tm = TM_IN
    n_in = w_in.shape[1]
    row = lambda w: pl.BlockSpec((1, tm, w), lambda b, i: (b, i, 0))
    vec = lambda w: pl.BlockSpec((1, w), lambda b, i: (0, 0))
    mod = pl.BlockSpec((1, 1, d), lambda b, i: (b, 0, 0))
    return pl.pallas_call(
        _inproj_kernel,
        grid=(bsz, s // tm),
        in_specs=[row(d), vec(d), mod, mod,
                  pl.BlockSpec((d, n_in), lambda b, i: (0, 0), pipeline_mode=pl.Buffered(1)),
                  pl.BlockSpec((D_ATTN, D_ATTN), lambda b, i: (0, 0)),
                  vec(D_ATTN), vec(D_ATTN)],
        out_specs=[row(D_CONV), row(D_ATTN), row(D_ATTN), row(D_ATTN), row(d), row(d)],
        out_shape=[jax.ShapeDtypeStruct((bsz, s, D_CONV), F32),
                   jax.ShapeDtypeStruct((bsz, s, D_ATTN), BF16),
                   jax.ShapeDtypeStruct((bsz, s, D_ATTN), BF16),
                   jax.ShapeDtypeStruct((bsz, s, D_ATTN), BF16),
                   jax.ShapeDtypeStruct((bsz, s, d), F32),
                   jax.ShapeDtypeStruct((bsz, s, d), F32)],
        compiler_params=_cparams(("arbitrary", "arbitrary")),
        name="inproj",
    )(x, g1, sc1, sh1, w_in, bd, qg, kg)


CONV_ROWS = 64


def _conv_kernel(u_ref, up_ref, dw_ref, cb_ref, lg_ref, lb_ref, w_ref, gc_ref,
                 o_ref, ext_ref, y_ref):
    i = pl.program_id(1)
    ts = u_ref.shape[1]
    prev = up_ref[0, ts - HALO:, :]
    ext_ref[0, 0:HALO, :] = jnp.where(i > 0, prev, 0.0)
    ext_ref[0, HALO:, :] = u_ref[0]
    n_sh = ts + HALO - SUBLANES
    for s in range(1, SUBLANES):
        for r0 in range(0, n_sh, CONV_ROWS):
            n = min(CONV_ROWS, n_sh - r0)
            ext_ref[s, r0:r0 + n, :] = ext_ref[0, r0 + s:r0 + s + n, :]
    base = HALO - (CONV_W - 1)
    for r0 in range(0, ts, CONV_ROWS):
        acc = jnp.zeros((CONV_ROWS, D_CONV), F32) + cb_ref[...]
        for w in range(CONV_W):
            s = (base + w) % SUBLANES
            a = r0 + base + w - s
            acc = acc + ext_ref[s, a:a + CONV_ROWS, :] * dw_ref[w:w + 1, :]
        y_ref[r0:r0 + CONV_ROWS, :] = acc
    y = y_ref[...]
    mu = jnp.mean(y, axis=-1, keepdims=True)
    yc = y - mu
    var = jnp.mean(yc * yc, axis=-1, keepdims=True)
    z = yc * lax.rsqrt(var + EPS) * lg_ref[...] + lb_ref[...]
    z = z * jax.nn.sigmoid(z)
    o = jnp.dot(z.astype(BF16), w_ref[...], preferred_element_type=F32)
    o_ref[0] = gc_ref[0] * o


def _conv(u, dw, cb, lg, lb, w_co, gc):
    bsz, s, dc = u.shape
    d = w_co.shape[1]
    ts = TS_CONV
    vec = lambda w: pl.BlockSpec((1, w), lambda b, i: (0, 0))
    return pl.pallas_call(
        _conv_kernel,
        grid=(bsz, s // ts),
        in_specs=[pl.BlockSpec((1, ts, dc), lambda b, i: (b, i, 0)),
                  pl.BlockSpec((1, ts, dc), lambda b, i: (b, jnp.maximum(i - 1, 0), 0)),
                  pl.BlockSpec((CONV_W, dc), lambda b, i: (0, 0)),
                  vec(dc), vec(dc), vec(dc),
                  pl.BlockSpec((dc, d), lambda b, i: (0, 0)),
                  pl.BlockSpec((1, ts, d), lambda b, i: (b, i, 0))],
        out_specs=pl.BlockSpec((1, ts, d), lambda b, i: (b, i, 0)),
        out_shape=jax.ShapeDtypeStruct((bsz, s, d), F32),
        scratch_shapes=[pltpu.VMEM((SUBLANES, ts + HALO, dc), F32), pltpu.VMEM((ts, dc), F32)],
        compiler_params=_cparams(("arbitrary", "arbitrary")),
        name="conv",
    )(u, u, dw, cb, lg, lb, w_co, gc)


def _attn_kernel(q_ref, k0_ref, k1_ref, k2_ref, v0_ref, v1_ref, v2_ref, bias_ref, o_ref):
    i = pl.program_id(1)
    k_refs = (k0_ref, k1_ref, k2_ref)
    v_refs = (v0_ref, v1_ref, v2_ref)
    low = lax.broadcasted_iota(jnp.int32, (TQ, LANES), 1) < HEAD_DIM

    def body(pens):
        for hp in range(N_HEADS * HEAD_DIM // LANES):
            lo = hp * LANES
            q2 = q_ref[0, :, lo:lo + LANES]
            outs = []
            for half in range(LANES // HEAD_DIM):
                h = hp * (LANES // HEAD_DIM) + half
                qh = jnp.where(low if half == 0 else jnp.logical_not(low), q2, jnp.zeros_like(q2))
                ss = []
                for j in range(NKB):
                    s = lax.dot_general(qh, k_refs[j][0, :, lo:lo + LANES],
                                        (((1,), (1,)), ((), ())), preferred_element_type=F32)
                    s = s + bias_ref[h, :, j * TQ:(j + 1) * TQ]
                    ss.append(s if pens is None else s + pens[j])
                m = jnp.maximum(jnp.maximum(jnp.max(ss[0], axis=-1, keepdims=True),
                                            jnp.max(ss[1], axis=-1, keepdims=True)),
                                jnp.max(ss[2], axis=-1, keepdims=True))
                l = jnp.zeros_like(m)
                acc = jnp.zeros((TQ, LANES), F32)
                for j in range(NKB):
                    p = jnp.exp2(ss[j] - m)
                    l = l + jnp.sum(p, axis=-1, keepdims=True)
                    acc = acc + jnp.dot(p.astype(BF16), v_refs[j][0, :, lo:lo + LANES],
                                        preferred_element_type=F32)
                outs.append(acc / l)
            o_ref[0, :, lo:lo + LANES] = jnp.where(low, outs[0], outs[1]).astype(BF16)

    @pl.when(i >= NKB - 1)
    def _():
        body(None)

    @pl.when(i < NKB - 1)
    def _():
        body([jnp.where(i - (NKB - 1) + j >= 0, 0.0, NEG_INF).astype(F32) for j in range(NKB)])


def _attn(q, k, v, bias):
    bsz, s, da = q.shape
    kspec = lambda j: pl.BlockSpec(
        (1, TQ, da), lambda b, i: (b, jnp.maximum(i - (NKB - 1) + j, 0), 0))
    return pl.pallas_call(
        _attn_kernel,
        grid=(bsz, s // TQ),
        in_specs=[pl.BlockSpec((1, TQ, da), lambda b, i: (b, i, 0)),
                  kspec(0), kspec(1), kspec(2), kspec(0), kspec(1), kspec(2),
                  pl.BlockSpec((N_HEADS, TQ, NKB * TQ), lambda b, i: (0, 0, 0))],
        out_specs=pl.BlockSpec((1, TQ, da), lambda b, i: (b, i, 0)),
        out_shape=jax.ShapeDtypeStruct((bsz, s, da), BF16),
        compiler_params=_cparams(("arbitrary", "arbitrary")),
        name="attn",
    )(q, k, k, k, v, v, v, bias)


def _bias_table(rel_bias):
    nk = NKB * TQ
    lw = TQ + nk - 1
    n_lo = (TQ - 1) - REL_CLIP
    n_hi = (nk - 1) - REL_CLIP
    w = jnp.concatenate([jnp.repeat(rel_bias[:, :1], n_lo, axis=1), rel_bias,
                         jnp.repeat(rel_bias[:, -1:], n_hi, axis=1)], axis=1).astype(F32)
    w2 = jnp.roll(w[:, ::-1], -(TQ - 1), axis=1)
    flat = jnp.tile(w2, (1, TQ))[:, :TQ * (lw - 1)]
    tab = flat.reshape(-1, TQ, lw - 1)[:, :, :nk]
    qi = np.arange(TQ)[:, None]
    kj = np.arange(nk)[None, :]
    qc = qi // CHUNK + (NKB - 1) * TQ // CHUNK
    kc = kj // CHUNK
    band = (kc >= qc - LEFT_CHUNKS) & (kc <= qc)
    return jnp.where(band[None], tab * LOG2E, NEG_INF)


def _merge_kernel(o_ref, mc_ref, ga_ref, x_ref, wa_ref, wo_ref, ga1_ref, g2_ref, sc_ref, sh_ref,
                  x1_ref, h2t_ref):
    ya = jnp.dot(o_ref[0], wa_ref[...], preferred_element_type=F32)
    merged = mc_ref[0] + ga_ref[0] * ya
    y = jnp.dot(merged.astype(BF16), wo_ref[...], preferred_element_type=F32)
    x1 = x_ref[0] + ga1_ref[0] * y
    x1_ref[0] = x1
    ms = jnp.mean(x1 * x1, axis=-1, keepdims=True)
    h2 = (x1 * lax.rsqrt(ms + EPS)) * g2_ref[...]
    h2 = h2 * (1.0 + sc_ref[0]) + sh_ref[0]
    h2t_ref[...] = h2.T.astype(BF16)


def _merge(o, mc, ga, x, wa, wo, ga1, g2, sc2, sh2):
    bsz, s, d = x.shape
    tm = TM_MERGE
    nb = s // tm
    row = lambda w: pl.BlockSpec((1, tm, w), lambda b, i: (b, i, 0))
    mod = pl.BlockSpec((1, 1, d), lambda b, i: (b, 0, 0))
    return pl.pallas_call(
        _merge_kernel,
        grid=(bsz, nb),
        in_specs=[row(D_ATTN), row(d), row(d), row(d),
                  pl.BlockSpec((D_ATTN, d), lambda b, i: (0, 0)),
                  pl.BlockSpec((d, d), lambda b, i: (0, 0)),
                  mod, pl.BlockSpec((1, d), lambda b, i: (0, 0)), mod, mod],
        out_specs=[row(d), pl.BlockSpec((d, tm), lambda b, i: (0, b * nb + i))],
        out_shape=[jax.ShapeDtypeStruct((bsz, s, d), F32),
                   jax.ShapeDtypeStruct((d, bsz * s), BF16)],
        compiler_params=_cparams(("arbitrary", "arbitrary")),
        name="merge",
    )(o, mc, ga, x, wa, wo, ga1, g2, sc2, sh2)


_CAND = [(a, b) for a in range(PEER_TOPK) for b in range(PEER_TOPK)
         if (a + 1) * (b + 1) <= PEER_TOPK]


def _sort16_network():
    n, pairs, p = PEER_TOPK, [], 1
    while p < n:
        k = p
        while k >= 1:
            for j in range(k % p, n - k, 2 * k):
                for i in range(min(k, n - j - k)):
                    if (i + j) // (2 * p) == (i + j + k) // (2 * p):
                        pairs.append((i + j, i + j + k))
            k //= 2
        p *= 2
    return pairs


_SORT16 = _sort16_network()


def _top16_sorted(problems, store_row):
    sub = 8
    lvs = [[s[g * sub:(g + 1) * sub, :] for g in range(PEER_NKEYS // sub)] for s in problems]
    for a, b in _SORT16:
        for lv in lvs:
            lv[a], lv[b] = jnp.maximum(lv[a], lv[b]), jnp.minimum(lv[a], lv[b])
    for r in range(PEER_TOPK):
        for p, lv in enumerate(lvs):
            m = jnp.max(lv[0], axis=0, keepdims=True)
            store_row(p, r, m)
            eq = lv[0] == m
            for k in range(PEER_TOPK - 1 - r):
                lv[k] = jnp.where(eq, lv[k + 1], lv[k])


def _route_kernel(h2t_ref, wqt_ref, keys_ref, s1_ref, theta_ref, e0_ref, e1_ref,
                  s_scr, v_scr, tau_scr, zi_scr):
    tb = h2t_ref.shape[1]
    ncol = tb // LANES
    n_hp = 2 * PEER_HEADS
    qt = jnp.dot(wqt_ref[...], h2t_ref[...], preferred_element_type=F32).astype(BF16)
    for hp in range(n_hp):
        s_scr[hp] = jnp.dot(keys_ref[hp], qt[hp * PEER_DH:(hp + 1) * PEER_DH, :],
                            preferred_element_type=F32)

    def stage1(it, carry):
        h = it // ncol
        col = it % ncol
        cs = pl.ds(pl.multiple_of(col * LANES, LANES), LANES)
        rw = pl.ds(h * ncol + col, 1)

        def store_row(p, r, m):
            v_scr[p, r, rw, :] = m

        _top16_sorted([s_scr[2 * h + p, :, cs] for p in range(2)], store_row)
        return carry

    lax.fori_loop(0, PEER_HEADS * ncol, stage1, 0)

    v0 = [v_scr[0, a] for a in range(PEER_TOPK)]
    v1 = [v_scr[1, b] for b in range(PEER_TOPK)]
    cand = [v0[a] + v1[b] for (a, b) in _CAND]
    top = cand[0]
    work = list(cand)
    for r in range(PEER_TOPK):
        c16 = functools.reduce(jnp.maximum, work)
        work = [jnp.where(w == c16, -jnp.inf, w) for w in work]
    c17 = functools.reduce(jnp.maximum, work)
    z = jnp.zeros_like(top)
    for c in cand:
        z = z + jnp.where(c >= c16, jnp.exp(c - top), 0.0)
    tau_scr[...] = 0.5 * (c16 + c17)
    zi_scr[...] = 0.5 / z

    def stage3(it, carry):
        h = it // ncol
        col = it % ncol
        cs = pl.ds(pl.multiple_of(col * LANES, LANES), LANES)
        rw = pl.ds(h * ncol + col, 1)
        s0 = s_scr[2 * h, :, cs]
        s1 = s_scr[2 * h + 1, :, cs]
        theta = jnp.where(s0 >= v_scr[0, PEER_TOPK - 1, rw, :], tau_scr[rw, :] - s0, jnp.inf)
        e0 = jnp.exp(s0 - v_scr[0, 0, rw, :])
        for g in range(PEER_NKEYS // ROWS_PER_STEP):
            rows = slice(g * ROWS_PER_STEP, (g + 1) * ROWS_PER_STEP)
            theta_ref[h, g, :, cs] = theta[rows]
            e0_ref[h, g, :, cs] = e0[rows]
        s1_ref[h, :, cs] = jnp.where(s1 >= v_scr[1, PEER_TOPK - 1, rw, :], s1, -jnp.inf)
        e1_ref[h, :, cs] = jnp.exp(s1 - v_scr[1, 0, rw, :]) * zi_scr[rw, :]
        return carry

    lax.fori_loop(0, PEER_HEADS * ncol, stage3, 0)


def _route(h2t, wqt, keys):
    d, t = h2t.shape
    tb = TB_ROUTE
    ncol = tb // LANES
    n_hp = 2 * PEER_HEADS
    n_grp = PEER_NKEYS // ROWS_PER_STEP
    tab = pl.BlockSpec((PEER_HEADS, PEER_NKEYS, tb), lambda i: (0, 0, i))
    rowtab = pl.BlockSpec((PEER_HEADS, n_grp, ROWS_PER_STEP, tb), lambda i: (0, 0, 0, i))
    tab_shape = jax.ShapeDtypeStruct((PEER_HEADS, PEER_NKEYS, t), F32)
    rowtab_shape = jax.ShapeDtypeStruct((PEER_HEADS, n_grp, ROWS_PER_STEP, t), F32)
    return pl.pallas_call(
        _route_kernel,
        grid=(t // tb,),
        in_specs=[pl.BlockSpec((d, tb), lambda i: (0, i)),
                  pl.BlockSpec(wqt.shape, lambda i: (0, 0)),
                  pl.BlockSpec(keys.shape, lambda i: (0, 0, 0))],
        out_specs=[tab, rowtab, rowtab, tab],
        out_shape=[tab_shape, rowtab_shape, rowtab_shape, tab_shape],
        scratch_shapes=[pltpu.VMEM((n_hp, PEER_NKEYS, tb), F32),
                        pltpu.VMEM((2, PEER_TOPK, PEER_HEADS * ncol, LANES), F32),
                        pltpu.VMEM((PEER_HEADS * ncol, LANES), F32),
                        pltpu.VMEM((PEER_HEADS * ncol, LANES), F32)],
        compiler_params=_cparams(("arbitrary",)),
        name="route",
    )(h2t, wqt, keys)


_INV_SQRT2 = float(1.0 / np.sqrt(2.0))
K_PIECE = 256
ROW_BLOCK = 4
JG_BLOCK = 4


def _experts_kernel(h2t_ref, u_ref, vt_ref, s1_ref, theta_ref, e0_ref, e1_ref,
                    x1_ref, ga2_ref, o_ref, a0_scr, a1_scr, p0_scr, p1_scr, acc_ref):
    c = pl.program_id(1)
    tb = h2t_ref.shape[1]

    @pl.when(c == 0)
    def _():
        acc_ref[...] = jnp.zeros_like(acc_ref)
        a1_scr[...] = jnp.zeros_like(a1_scr)
        p0_scr[...] = jnp.zeros_like(p0_scr)

    sub = (8, LANES)
    n_jg = PEER_NKEYS // sub[0]
    zero = jnp.zeros(sub, F32)

    def pair_body(k, refs):
        a_new, a_old, p_new, p_old = refs
        base = k * MXU_N
        ps = pl.ds(base, MXU_N)

        def mm_a(kc):
            ks = slice(kc * K_PIECE, (kc + 1) * K_PIECE)
            part = jnp.dot(u_ref[:, ks], h2t_ref[ks, ps], preferred_element_type=F32)
            if kc == 0:
                a_new[:, ps] = part
            else:
                a_new[:, ps] += part

        def mm_acc(kc, mh):
            ks = slice(kc * K_PIECE, (kc + 1) * K_PIECE)
            ms = slice(mh * EC, (mh + 1) * EC)
            acc_ref[ms, ps] += jnp.dot(vt_ref[ms, ks], p_old[ks, ps],
                                       preferred_element_type=F32)

        mm_pieces = ([functools.partial(mm_a, kc) for kc in range(D_MODEL // K_PIECE)]
                     + [functools.partial(mm_acc, kc, mh) for kc in range(EC // K_PIECE)
                        for mh in range(D_MODEL // EC)])
        blocks = [(half, ip, j0) for half in range(MXU_N // LANES)
                  for ip in range(ROWS_PER_STEP // ROW_BLOCK)
                  for j0 in range(0, n_jg, JG_BLOCK)]
        every = len(blocks) // len(mm_pieces)
        for bi, (half, ip, j0) in enumerate(blocks):
            if bi % every == 0:
                mm_pieces[bi // every]()
            cs = pl.ds(base + half * LANES, LANES)
            if True:
                rows = tuple(range(ROW_BLOCK * ip, ROW_BLOCK * (ip + 1)))
                g = [[zero] * JG_BLOCK for _ in rows]
                for h in range(PEER_HEADS):
                    th = [jnp.broadcast_to(theta_ref[h, 0, r:r + 1, cs], sub) for r in rows]
                    e0 = [jnp.broadcast_to(e0_ref[h, 0, r:r + 1, cs], sub) for r in rows]
                    for jg in range(JG_BLOCK):
                        js = slice((j0 + jg) * sub[0], (j0 + jg + 1) * sub[0])
                        s1 = s1_ref[h, js, cs]
                        e1 = e1_ref[h, js, cs]
                        for q in range(ROW_BLOCK):
                            g[q][jg] = g[q][jg] + jnp.where(s1 >= th[q], e1, zero) * e0[q]
                for q, r in enumerate(rows):
                    for jg in range(0, JG_BLOCK, 2):
                        lo = r * PEER_NKEYS + (j0 + jg) * sub[0]
                        a = a_old[lo:lo + 2 * sub[0], cs]
                        act = a + a * lax.erf(a * _INV_SQRT2)
                        gg = jnp.concatenate([g[q][jg], g[q][jg + 1]], axis=0)
                        p_new[lo:lo + 2 * sub[0], cs] = (act * gg).astype(BF16)

    even = (a0_scr, a1_scr, p1_scr, p0_scr)
    odd = (a1_scr, a0_scr, p0_scr, p1_scr)
    for parity, refs in ((0, even), (1, odd)):
        @pl.when(c % 2 == parity)
        def _(refs=refs):
            for k in range(tb // MXU_N):
                pair_body(k, refs)

    @pl.when(c == pl.num_programs(1) - 1)
    def _():
        o_ref[...] = x1_ref[...] + ga2_ref[0] * acc_ref[...].T


def _experts(h2t, u, vt, s1, theta, e0, e1, x1, ga2, seq):
    d, t = h2t.shape
    tb = TB_EXP
    n_chunks = u.shape[0] // EC
    last = n_chunks - 1
    per_batch = seq // tb
    once = dict(pipeline_mode=pl.Buffered(1))
    tab = pl.BlockSpec((PEER_HEADS, PEER_NKEYS, tb), lambda i, c: (0, 0, i), **once)
    rowtab = pl.BlockSpec((PEER_HEADS, 1, ROWS_PER_STEP, tb),
                          lambda i, c: (0, jnp.clip(c - 1, 0, last), 0, i))
    return pl.pallas_call(
        _experts_kernel,
        grid=(t // tb, n_chunks + 2),
        in_specs=[pl.BlockSpec((d, tb), lambda i, c: (0, i)),
                  pl.BlockSpec((EC, d), lambda i, c: (jnp.minimum(c, last), 0)),
                  pl.BlockSpec((d, EC), lambda i, c: (0, jnp.clip(c - 2, 0, last))),
                  tab, rowtab, rowtab, tab,
                  pl.BlockSpec((tb, d), lambda i, c: (i, 0), **once),
                  pl.BlockSpec((1, 1, d), lambda i, c: (i // per_batch, 0, 0))],
        out_specs=pl.BlockSpec((tb, d), lambda i, c: (i, 0)),
        out_shape=jax.ShapeDtypeStruct((t, d), F32),
        scratch_shapes=[pltpu.VMEM((EC, tb), F32), pltpu.VMEM((EC, tb), F32),
                        pltpu.VMEM((EC, tb), BF16), pltpu.VMEM((EC, tb), BF16),
                        pltpu.VMEM((d, tb), F32)],
        compiler_params=_cparams(("arbitrary", "arbitrary")),
        name="experts",
    )(h2t, u, vt, s1, theta, e0, e1, x1, ga2)


def kernel(x, c, w_ada, b_ada, norm1_g, norm2_g, w_in, conv_dw, conv_b, conv_ln_g, conv_ln_b,
           w_conv_out, q_norm_g, k_norm_g, rel_bias, w_attn_out, w_out, peer_wq, peer_keys,
           peer_u, peer_v):
    bsz, s, d = x.shape
    depth = w_ada.shape[0]
    bd = jnp.asarray(np.kron(np.eye(N_HEADS), np.full((HEAD_DIM, HEAD_DIM), 1.0 / HEAD_DIM)), BF16)
    for l in range(depth):
        mod = _ada(c, w_ada[l], b_ada[l])
        sh1, sc1, ga1, sh2, sc2, ga2 = [m.reshape(bsz, 1, d) for m in jnp.split(mod, 6, axis=-1)]
        u, q, k, v, gc, ga = _inproj(
            x, norm1_g[l].reshape(1, d), sc1, sh1, w_in[l].astype(BF16), bd,
            jnp.tile(q_norm_g[l], N_HEADS).reshape(1, D_ATTN),
            jnp.tile(k_norm_g[l], N_HEADS).reshape(1, D_ATTN))
        mc = _conv(u, conv_dw[l], conv_b[l].reshape(1, D_CONV), conv_ln_g[l].reshape(1, D_CONV),
                   conv_ln_b[l].reshape(1, D_CONV), w_conv_out[l].astype(BF16), gc)
        o = _attn(q, k, v, _bias_table(rel_bias[l]))
        x1, h2t = _merge(o, mc, ga, x, w_attn_out[l].astype(BF16), w_out[l].astype(BF16),
                         ga1, norm2_g[l].reshape(1, d), sc2, sh2)
        wqt = peer_wq[l].T.astype(BF16)
        keys = peer_keys[l].reshape(2 * PEER_HEADS, PEER_NKEYS, PEER_DH).astype(BF16)
        s1, theta, e0, e1 = _route(h2t, wqt, keys)
        out = _experts(h2t, peer_u[l].astype(BF16), peer_v[l].T.astype(BF16),
                       s1, theta, e0, e1, x1.reshape(bsz * s, d), ga2, s)
        x = out.reshape(bsz, s, d)
    return x
```

```python
import functools

import jax
import jax.numpy as jnp
import numpy as np
from jax import lax
from jax.experimental import pallas as pl
from jax.experimental.pallas import tpu as pltpu

F32 = jnp.float32
BF16 = jnp.bfloat16

D_MODEL = 1024
CHUNK = 64
N_HEADS = 8
HEAD_DIM = 64
D_ATTN = N_HEADS * HEAD_DIM
LEFT_CHUNKS = 8
REL_CLIP = 128
D_CONV = D_MODEL // 2
CONV_W = 31
PEER_HEADS = 8
PEER_NKEYS = 128
PEER_N = PEER_NKEYS * PEER_NKEYS
PEER_DH = 128
PEER_TOPK = 16
EPS = 1e-6
NEG_INF = -1e30
LOG2E = float(np.log2(np.e))

LANES = 128
SUBLANES = 8
MXU_N = 256
VMEM_LIMIT = 56 * 1024 * 1024

TM_IN = 512
TS_CONV = 512
HALO = 32
TQ = 256
NKB = 3
TM_MERGE = 512
TB_ROUTE = 512
TB_EXP = 1024
EC = 512
ROWS_PER_STEP = EC // PEER_NKEYS

def _cparams(sem):
    return pltpu.CompilerParams(dimension_semantics=sem, vmem_limit_bytes=VMEM_LIMIT)


def _ada_kernel(c_ref, w_ref, b_ref, o_ref):
    c = c_ref[...]
    cond = c * jax.nn.sigmoid(c)
    o_ref[...] = jnp.dot(cond.astype(BF16), w_ref[...].astype(BF16),
                         preferred_element_type=F32) + b_ref[...]


def _ada(c, w, b):
    bsz, d = c.shape
    n = w.shape[1]
    tn = 1024
    return pl.pallas_call(
        _ada_kernel,
        grid=(n // tn,),
        in_specs=[pl.BlockSpec((bsz, d), lambda j: (0, 0)),
                  pl.BlockSpec((d, tn), lambda j: (0, j)),
                  pl.BlockSpec((1, tn), lambda j: (0, j))],
        out_specs=pl.BlockSpec((bsz, tn), lambda j: (0, j)),
        out_shape=jax.ShapeDtypeStruct((bsz, n), F32),
        compiler_params=_cparams(("arbitrary",)),
        name="ada",
    )(c, w, b.reshape(1, n))


def _head_rms(t, bd_ref, gain):
    t2 = t * t
    hi = t2.astype(BF16)
    lo = (t2 - hi.astype(F32)).astype(BF16)
    ms = (jnp.dot(hi, bd_ref[...], preferred_element_type=F32)
          + jnp.dot(lo, bd_ref[...], preferred_element_type=F32))
    return t * lax.rsqrt(ms + EPS) * gain


def _inproj_kernel(x_ref, g_ref, sc_ref, sh_ref, w_ref, bd_ref, qg_ref, kg_ref,
                   u_ref, q_ref, k_ref, v_ref, gc_ref, ga_ref):
    x = x_ref[0]
    ms = jnp.mean(x * x, axis=-1, keepdims=True)
    h = (x * lax.rsqrt(ms + EPS)) * g_ref[...]
    h = h * (1.0 + sc_ref[0]) + sh_ref[0]
    hb = h.astype(BF16)

    def seg(lo, hi):
        return jnp.dot(hb, w_ref[:, lo:hi], preferred_element_type=F32)

    o = 0
    a = seg(o, o + D_CONV); o += D_CONV
    b = seg(o, o + D_CONV); o += D_CONV
    u_ref[0] = a * jax.nn.sigmoid(b)
    q = seg(o, o + D_ATTN); o += D_ATTN
    q_ref[0] = (_head_rms(q, bd_ref, qg_ref[...]) * (HEAD_DIM ** -0.5 * LOG2E)).astype(BF16)
    k = seg(o, o + D_ATTN); o += D_ATTN
    k_ref[0] = _head_rms(k, bd_ref, kg_ref[...]).astype(BF16)
    v_ref[0] = seg(o, o + D_ATTN).astype(BF16); o += D_ATTN
    gc_ref[0] = jax.nn.sigmoid(seg(o, o + D_MODEL)); o += D_MODEL
    ga_ref[0] = jax.nn.sigmoid(seg(o, o + D_MODEL))


def _inproj(x, g1, sc1, sh1, w_in, bd, qg, kg):
    bsz, s, d = x.shape
    tm = TM_IN
    n_in = w_in.shape[1]
    row = lambda w: pl.BlockSpec((1, tm, w), lambda b, i: (b, i, 0))
    vec = lambda w: pl.BlockSpec((1, w), lambda b, i: (0, 0))
    mod = pl.BlockSpec((1, 1, d), lambda b, i: (b, 0, 0))
    return pl.pallas_call(
        _inproj_kernel,
        grid=(bsz, s // tm),
        in_specs=[row(d), vec(d), mod, mod,
                  pl.BlockSpec((d, n_in), lambda b, i: (0, 0), pipeline_mode=pl.Buffered(1)),
                  pl.BlockSpec((D_ATTN, D_ATTN), lambda b, i: (0, 0)),
                  vec(D_ATTN), vec(D_ATTN)],
        out_specs=[row(D_CONV), row(D_ATTN), row(D_ATTN), row(D_ATTN), row(d), row(d)],
        out_shape=[jax.ShapeDtypeStruct((bsz, s, D_CONV), F32),
                   jax.ShapeDtypeStruct((bsz, s, D_ATTN), BF16),
                   jax.ShapeDtypeStruct((bsz, s, D_ATTN), BF16),
                   jax.ShapeDtypeStruct((bsz, s, D_ATTN), BF16),
                   jax.ShapeDtypeStruct((bsz, s, d), F32),
                   jax.ShapeDtypeStruct((bsz, s, d), F32)],
        compiler_params=_cparams(("arbitrary", "arbitrary")),
        name="inproj",
    )(x, g1, sc1, sh1, w_in, bd, qg, kg)


CONV_ROWS = 64


def _conv_kernel(u_ref, up_ref, dw_ref, cb_ref, lg_ref, lb_ref, w_ref, gc_ref,
                 o_ref, ext_ref, y_ref):
    i = pl.program_id(1)
    ts = u_ref.shape[1]
    prev = up_ref[0, ts - HALO:, :]
    ext_ref[0, 0:HALO, :] = jnp.where(i > 0, prev, 0.0)
    ext_ref[0, HALO:, :] = u_ref[0]
    n_sh = ts + HALO - SUBLANES
    for s in range(1, SUBLANES):
        for r0 in range(0, n_sh, CONV_ROWS):
            n = min(CONV_ROWS, n_sh - r0)
            ext_ref[s, r0:r0 + n, :] = ext_ref[0, r0 + s:r0 + s + n, :]
    base = HALO - (CONV_W - 1)
    for r0 in range(0, ts, CONV_ROWS):
        acc = jnp.zeros((CONV_ROWS, D_CONV), F32) + cb_ref[...]
        for w in range(CONV_W):
            s = (base + w) % SUBLANES
            a = r0 + base + w - s
            acc = acc + ext_ref[s, a:a + CONV_ROWS, :] * dw_ref[w:w + 1, :]
        y_ref[r0:r0 + CONV_ROWS, :] = acc
    y = y_ref[...]
    mu = jnp.mean(y, axis=-1, keepdims=True)
    yc = y - mu
    var = jnp.mean(yc * yc, axis=-1, keepdims=True)
    z = yc * lax.rsqrt(var + EPS) * lg_ref[...] + lb_ref[...]
    z = z * jax.nn.sigmoid(z)
    o = jnp.dot(z.astype(BF16), w_ref[...], preferred_element_type=F32)
    o_ref[0] = gc_ref[0] * o


def _conv(u, dw, cb, lg, lb, w_co, gc):
    bsz, s, dc = u.shape
    d = w_co.shape[1]
    ts = TS_CONV
    vec = lambda w: pl.BlockSpec((1, w), lambda b, i: (0, 0))
    return pl.pallas_call(
        _conv_kernel,
        grid=(bsz, s // ts),
        in_specs=[pl.BlockSpec((1, ts, dc), lambda b, i: (b, i, 0)),
                  pl.BlockSpec((1, ts, dc), lambda b, i: (b, jnp.maximum(i - 1, 0), 0)),
                  pl.BlockSpec((CONV_W, dc), lambda b, i: (0, 0)),
                  vec(dc), vec(dc), vec(dc),
                  pl.BlockSpec((dc, d), lambda b, i: (0, 0)),
                  pl.BlockSpec((1, ts, d), lambda b, i: (b, i, 0))],
        out_specs=pl.BlockSpec((1, ts, d), lambda b, i: (b, i, 0)),
        out_shape=jax.ShapeDtypeStruct((bsz, s, d), F32),
        scratch_shapes=[pltpu.VMEM((SUBLANES, ts + HALO, dc), F32), pltpu.VMEM((ts, dc), F32)],
        compiler_params=_cparams(("arbitrary", "arbitrary")),
        name="conv",
    )(u, u, dw, cb, lg, lb, w_co, gc)


def _attn_kernel(q_ref, k0_ref, k1_ref, k2_ref, v0_ref, v1_ref, v2_ref, bias_ref, o_ref):
    i = pl.program_id(1)
    k_refs = (k0_ref, k1_ref, k2_ref)
    v_refs = (v0_ref, v1_ref, v2_ref)
    low = lax.broadcasted_iota(jnp.int32, (TQ, LANES), 1) < HEAD_DIM

    def body(pens):
        for hp in range(N_HEADS * HEAD_DIM // LANES):
            lo = hp * LANES
            q2 = q_ref[0, :, lo:lo + LANES]
            outs = []
            for half in range(LANES // HEAD_DIM):
                h = hp * (LANES // HEAD_DIM) + half
                qh = jnp.where(low if half == 0 else jnp.logical_not(low), q2, jnp.zeros_like(q2))
                ss = []
                for j in range(NKB):
                    s = lax.dot_general(qh, k_refs[j][0, :, lo:lo + LANES],
                                        (((1,), (1,)), ((), ())), preferred_element_type=F32)
                    s = s + bias_ref[h, :, j * TQ:(j + 1) * TQ]
                    ss.append(s if pens is None else s + pens[j])
                m = jnp.maximum(jnp.maximum(jnp.max(ss[0], axis=-1, keepdims=True),
                                            jnp.max(ss[1], axis=-1, keepdims=True)),
                                jnp.max(ss[2], axis=-1, keepdims=True))
                l = jnp.zeros_like(m)
                acc = jnp.zeros((TQ, LANES), F32)
                for j in range(NKB):
                    p = jnp.exp2(ss[j] - m)
                    l = l + jnp.sum(p, axis=-1, keepdims=True)
                    acc = acc + jnp.dot(p.astype(BF16), v_refs[j][0, :, lo:lo + LANES],
                                        preferred_element_type=F32)
                outs.append(acc / l)
            o_ref[0, :, lo:lo + LANES] = jnp.where(low, outs[0], outs[1]).astype(BF16)

    @pl.when(i >= NKB - 1)
    def _():
        body(None)

    @pl.when(i < NKB - 1)
    def _():
        body([jnp.where(i - (NKB - 1) + j >= 0, 0.0, NEG_INF).astype(F32) for j in range(NKB)])


def _attn(q, k, v, bias):
    bsz, s, da = q.shape
    kspec = lambda j: pl.BlockSpec(
        (1, TQ, da), lambda b, i: (b, jnp.maximum(i - (NKB - 1) + j, 0), 0))
    return pl.pallas_call(
        _attn_kernel,
        grid=(bsz, s // TQ),
        in_specs=[pl.BlockSpec((1, TQ, da), lambda b, i: (b, i, 0)),
                  kspec(0), kspec(1), kspec(2), kspec(0), kspec(1), kspec(2),
                  pl.BlockSpec((N_HEADS, TQ, NKB * TQ), lambda b, i: (0, 0, 0))],
        out_specs=pl.BlockSpec((1, TQ, da), lambda b, i: (b, i, 0)),
        out_shape=jax.ShapeDtypeStruct((bsz, s, da), BF16),
        compiler_params=_cparams(("arbitrary", "arbitrary")),
        name="attn",
    )(q, k, k, k, v, v, v, bias)


def _bias_table(rel_bias):
    nk = NKB * TQ
    lw = TQ + nk - 1
    n_lo = (TQ - 1) - REL_CLIP
    n_hi = (nk - 1) - REL_CLIP
    w = jnp.concatenate([jnp.repeat(rel_bias[:, :1], n_lo, axis=1), rel_bias,
                         jnp.repeat(rel_bias[:, -1:], n_hi, axis=1)], axis=1).astype(F32)
    w2 = jnp.roll(w[:, ::-1], -(TQ - 1), axis=1)
    flat = jnp.tile(w2, (1, TQ))[:, :TQ * (lw - 1)]
    tab = flat.reshape(-1, TQ, lw - 1)[:, :, :nk]
    qi = np.arange(TQ)[:, None]
    kj = np.arange(nk)[None, :]
    qc = qi // CHUNK + (NKB - 1) * TQ // CHUNK
    kc = kj // CHUNK
    band = (kc >= qc - LEFT_CHUNKS) & (kc <= qc)
    return jnp.where(band[None], tab * LOG2E, NEG_INF)


def _merge_kernel(o_ref, mc_ref, ga_ref, x_ref, wa_ref, wo_ref, ga1_ref, g2_ref, sc_ref, sh_ref,
                  x1_ref, h2t_ref):
    ya = jnp.dot(o_ref[0], wa_ref[...], preferred_element_type=F32)
    merged = mc_ref[0] + ga_ref[0] * ya
    y = jnp.dot(merged.astype(BF16), wo_ref[...], preferred_element_type=F32)
    x1 = x_ref[0] + ga1_ref[0] * y
    x1_ref[0] = x1
    ms = jnp.mean(x1 * x1, axis=-1, keepdims=True)
    h2 = (x1 * lax.rsqrt(ms + EPS)) * g2_ref[...]
    h2 = h2 * (1.0 + sc_ref[0]) + sh_ref[0]
    h2t_ref[...] = h2.T.astype(BF16)


def _merge(o, mc, ga, x, wa, wo, ga1, g2, sc2, sh2):
    bsz, s, d = x.shape
    tm = TM_MERGE
    nb = s // tm
    row = lambda w: pl.BlockSpec((1, tm, w), lambda b, i: (b, i, 0))
    mod = pl.BlockSpec((1, 1, d), lambda b, i: (b, 0, 0))
    return pl.pallas_call(
        _merge_kernel,
        grid=(bsz, nb),
        in_specs=[row(D_ATTN), row(d), row(d), row(d),
                  pl.BlockSpec((D_ATTN, d), lambda b, i: (0, 0)),
                  pl.BlockSpec((d, d), lambda b, i: (0, 0)),
                  mod, pl.BlockSpec((1, d), lambda b, i: (0, 0)), mod, mod],
        out_specs=[row(d), pl.BlockSpec((d, tm), lambda b, i: (0, b * nb + i))],
        out_shape=[jax.ShapeDtypeStruct((bsz, s, d), F32),
                   jax.ShapeDtypeStruct((d, bsz * s), BF16)],
        compiler_params=_cparams(("arbitrary", "arbitrary")),
        name="merge",
    )(o, mc, ga, x, wa, wo, ga1, g2, sc2, sh2)


_CAND = [(a, b) for a in range(PEER_TOPK) for b in range(PEER_TOPK)
         if (a + 1) * (b + 1) <= PEER_TOPK]


def _sort16_network():
    n, pairs, p = PEER_TOPK, [], 1
    while p < n:
        k = p
        while k >= 1:
            for j in range(k % p, n - k, 2 * k):
                for i in range(min(k, n - j - k)):
                    if (i + j) // (2 * p) == (i + j + k) // (2 * p):
                        pairs.append((i + j, i + j + k))
            k //= 2
        p *= 2
    return pairs


_SORT16 = _sort16_network()


def _top16_sorted(problems, store_row):
    sub = 8
    lvs = [[s[g * sub:(g + 1) * sub, :] for g in range(PEER_NKEYS // sub)] for s in problems]
    for a, b in _SORT16:
        for lv in lvs:
            lv[a], lv[b] = jnp.maximum(lv[a], lv[b]), jnp.minimum(lv[a], lv[b])
    for r in range(PEER_TOPK):
        for p, lv in enumerate(lvs):
            m = jnp.max(lv[0], axis=0, keepdims=True)
            store_row(p, r, m)
            eq = lv[0] == m
            for k in range(PEER_TOPK - 1 - r):
                lv[k] = jnp.where(eq, lv[k + 1], lv[k])


def _route_kernel(h2t_ref, wqt_ref, keys_ref, s1_ref, theta_ref, e0_ref, e1_ref,
                  s_scr, v_scr, tau_scr, zi_scr):
    tb = h2t_ref.shape[1]
    ncol = tb // LANES
    n_hp = 2 * PEER_HEADS
    qt = jnp.dot(wqt_ref[...], h2t_ref[...], preferred_element_type=F32).astype(BF16)
    for hp in range(n_hp):
        s_scr[hp] = jnp.dot(keys_ref[hp], qt[hp * PEER_DH:(hp + 1) * PEER_DH, :],
                            preferred_element_type=F32)

    def stage1(it, carry):
        h = it // ncol
        col = it % ncol
        cs = pl.ds(pl.multiple_of(col * LANES, LANES), LANES)
        rw = pl.ds(h * ncol + col, 1)

        def store_row(p, r, m):
            v_scr[p, r, rw, :] = m

        _top16_sorted([s_scr[2 * h + p, :, cs] for p in range(2)], store_row)
        return carry

    lax.fori_loop(0, PEER_HEADS * ncol, stage1, 0)

    v0 = [v_scr[0, a] for a in range(PEER_TOPK)]
    v1 = [v_scr[1, b] for b in range(PEER_TOPK)]
    cand = [v0[a] + v1[b] for (a, b) in _CAND]
    top = cand[0]
    work = list(cand)
    for r in range(PEER_TOPK):
        c16 = functools.reduce(jnp.maximum, work)
        work = [jnp.where(w == c16, -jnp.inf, w) for w in work]
    c17 = functools.reduce(jnp.maximum, work)
    z = jnp.zeros_like(top)
    for c in cand:
        z = z + jnp.where(c >= c16, jnp.exp(c - top), 0.0)
    tau_scr[...] = 0.5 * (c16 + c17)
    zi_scr[...] = 0.5 / z

    def stage3(it, carry):
        h = it // ncol
        col = it % ncol
        cs = pl.ds(pl.multiple_of(col * LANES, LANES), LANES)
        rw = pl.ds(h * ncol + col, 1)
        s0 = s_scr[2 * h, :, cs]
        s1 = s_scr[2 * h + 1, :, cs]
        theta = jnp.where(s0 >= v_scr[0, PEER_TOPK - 1, rw, :], tau_scr[rw, :] - s0, jnp.inf)
        e0 = jnp.exp(s0 - v_scr[0, 0, rw, :])
        for g in range(PEER_NKEYS // ROWS_PER_STEP):
            rows = slice(g * ROWS_PER_STEP, (g + 1) * ROWS_PER_STEP)
            theta_ref[h, g, :, cs] = theta[rows]
            e0_ref[h, g, :, cs] = e0[rows]
        s1_ref[h, :, cs] = jnp.where(s1 >= v_scr[1, PEER_TOPK - 1, rw, :], s1, -jnp.inf)
        e1_ref[h, :, cs] = jnp.exp(s1 - v_scr[1, 0, rw, :]) * zi_scr[rw, :]
        return carry

    lax.fori_loop(0, PEER_HEADS * ncol, stage3, 0)


def _route(h2t, wqt, keys):
    d, t = h2t.shape
    tb = TB_ROUTE
    ncol = tb // LANES
    n_hp = 2 * PEER_HEADS
    n_grp = PEER_NKEYS // ROWS_PER_STEP
    tab = pl.BlockSpec((PEER_HEADS, PEER_NKEYS, tb), lambda i: (0, 0, i))
    rowtab = pl.BlockSpec((PEER_HEADS, n_grp, ROWS_PER_STEP, tb), lambda i: (0, 0, 0, i))
    tab_shape = jax.ShapeDtypeStruct((PEER_HEADS, PEER_NKEYS, t), F32)
    rowtab_shape = jax.ShapeDtypeStruct((PEER_HEADS, n_grp, ROWS_PER_STEP, t), F32)
    return pl.pallas_call(
        _route_kernel,
        grid=(t // tb,),
        in_specs=[pl.BlockSpec((d, tb), lambda i: (0, i)),
                  pl.BlockSpec(wqt.shape, lambda i: (0, 0)),
                  pl.BlockSpec(keys.shape, lambda i: (0, 0, 0))],
        out_specs=[tab, rowtab, rowtab, tab],
        out_shape=[tab_shape, rowtab_shape, rowtab_shape, tab_shape],
        scratch_shapes=[pltpu.VMEM((n_hp, PEER_NKEYS, tb), F32),
                        pltpu.VMEM((2, PEER_TOPK, PEER_HEADS * ncol, LANES), F32),
                        pltpu.VMEM((PEER_HEADS * ncol, LANES), F32),
                        pltpu.VMEM((PEER_HEADS * ncol, LANES), F32)],
        compiler_params=_cparams(("arbitrary",)),
        name="route",
    )(h2t, wqt, keys)


_INV_SQRT2 = float(1.0 / np.sqrt(2.0))
K_PIECE = 256
ROW_BLOCK = 4
JG_BLOCK = 4


def _experts_kernel(h2t_ref, u_ref, vt_ref, s1_ref, theta_ref, e0_ref, e1_ref,
                    x1_ref, ga2_ref, o_ref, a0_scr, a1_scr, p0_scr, p1_scr, acc_ref):
    c = pl.program_id(1)
    tb = h2t_ref.shape[1]

    @pl.when(c == 0)
    def _():
        acc_ref[...] = jnp.zeros_like(acc_ref)
        p1_scr[...] = jnp.zeros_like(p1_scr)

    sub = (8, LANES)
    n_jg = PEER_NKEYS // sub[0]
    zero = jnp.zeros(sub, F32)

    def pair_body(k, refs, stages):
        a_new, a_old, p_new, p_old = refs
        base = k * MXU_N
        ps = pl.ds(base, MXU_N)

        def mm_a(kc):
            ks = slice(kc * K_PIECE, (kc + 1) * K_PIECE)
            part = jnp.dot(u_ref[:, ks], h2t_ref[ks, ps], preferred_element_type=F32)
            if kc == 0:
                a_new[:, ps] = part
            else:
                a_new[:, ps] += part

        def mm_acc(kc, mh):
            ks = slice(kc * K_PIECE, (kc + 1) * K_PIECE)
            ms = slice(mh * EC, (mh + 1) * EC)
            acc_ref[ms, ps] += jnp.dot(vt_ref[ms, ks], p_old[ks, ps],
                                       preferred_element_type=F32)

        a_pieces = [functools.partial(mm_a, kc) for kc in range(D_MODEL // K_PIECE)]
        acc_pieces = [functools.partial(mm_acc, kc, mh) for kc in range(EC // K_PIECE)
                      for mh in range(D_MODEL // EC)]
        if stages == "fill":
            for piece in a_pieces:
                piece()
            return
        if stages == "drain":
            for piece in acc_pieces:
                piece()
            return
        mm_pieces = a_pieces + acc_pieces
        blocks = [(half, ip, j0) for half in range(MXU_N // LANES)
                  for ip in range(ROWS_PER_STEP // ROW_BLOCK)
                  for j0 in range(0, n_jg, JG_BLOCK)]
        every = len(blocks) // len(mm_pieces)
        for bi, (half, ip, j0) in enumerate(blocks):
            if bi % every == 0:
                mm_pieces[bi // every]()
            cs = pl.ds(base + half * LANES, LANES)
            if True:
                rows = tuple(range(ROW_BLOCK * ip, ROW_BLOCK * (ip + 1)))
                g = [[zero] * JG_BLOCK for _ in rows]
                for h in range(PEER_HEADS):
                    th = [jnp.broadcast_to(theta_ref[h, 0, r:r + 1, cs], sub) for r in rows]
                    e0 = [jnp.broadcast_to(e0_ref[h, 0, r:r + 1, cs], sub) for r in rows]
                    for jg in range(JG_BLOCK):
                        js = slice((j0 + jg) * sub[0], (j0 + jg + 1) * sub[0])
                        s1 = s1_ref[h, js, cs]
                        e1 = e1_ref[h, js, cs]
                        for q in range(ROW_BLOCK):
                            g[q][jg] = g[q][jg] + jnp.where(s1 >= th[q], e1, zero) * e0[q]
                for q, r in enumerate(rows):
                    for jg in range(0, JG_BLOCK, 2):
                        lo = r * PEER_NKEYS + (j0 + jg) * sub[0]
                        a = a_old[lo:lo + 2 * sub[0], cs]
                        act = a + a * lax.erf(a * _INV_SQRT2)
                        gg = jnp.concatenate([g[q][jg], g[q][jg + 1]], axis=0)
                        p_new[lo:lo + 2 * sub[0], cs] = (act * gg).astype(BF16)

    even = (a0_scr, a1_scr, p1_scr, p0_scr)
    odd = (a1_scr, a0_scr, p0_scr, p1_scr)
    last = pl.num_programs(1) - 1

    def run(refs, stages):
        for k in range(tb // MXU_N):
            pair_body(k, refs, stages)

    @pl.when(c == 0)
    def _():
        run(even, "fill")

    @pl.when(jnp.logical_and(c > 0, c % 2 == 0))
    def _():
        run(even, "all")

    @pl.when(jnp.logical_and(c < last, c % 2 == 1))
    def _():
        run(odd, "all")

    @pl.when(c == last)
    def _():
        run(odd, "drain")
        o_ref[...] = x1_ref[...] + ga2_ref[0] * acc_ref[...].T


def _experts(h2t, u, vt, s1, theta, e0, e1, x1, ga2, seq):
    d, t = h2t.shape
    tb = TB_EXP
    n_chunks = u.shape[0] // EC
    last = n_chunks - 1
    per_batch = seq // tb
    once = dict(pipeline_mode=pl.Buffered(1))
    tab = pl.BlockSpec((PEER_HEADS, PEER_NKEYS, tb), lambda i, c: (0, 0, i), **once)
    rowtab = pl.BlockSpec((PEER_HEADS, 1, ROWS_PER_STEP, tb),
                          lambda i, c: (0, jnp.clip(c - 1, 0, last), 0, i))
    return pl.pallas_call(
        _experts_kernel,
        grid=(t // tb, n_chunks + 2),
        in_specs=[pl.BlockSpec((d, tb), lambda i, c: (0, i)),
                  pl.BlockSpec((EC, d), lambda i, c: (jnp.minimum(c, last), 0)),
                  pl.BlockSpec((d, EC), lambda i, c: (0, jnp.clip(c - 2, 0, last))),
                  tab, rowtab, rowtab, tab,
                  pl.BlockSpec((tb, d), lambda i, c: (i, 0), **once),
                  pl.BlockSpec((1, 1, d), lambda i, c: (i // per_batch, 0, 0))],
        out_specs=pl.BlockSpec((tb, d), lambda i, c: (i, 0)),
        out_shape=jax.ShapeDtypeStruct((t, d), F32),
        scratch_shapes=[pltpu.VMEM((EC, tb), F32), pltpu.VMEM((EC, tb), F32),
                        pltpu.VMEM((EC, tb), BF16), pltpu.VMEM((EC, tb), BF16),
                        pltpu.VMEM((d, tb), F32)],
        compiler_params=_cparams(("arbitrary", "arbitrary")),
        name="experts",
    )(h2t, u, vt, s1, theta, e0, e1, x1, ga2)


def kernel(x, c, w_ada, b_ada, norm1_g, norm2_g, w_in, conv_dw, conv_b, conv_ln_g, conv_ln_b,
           w_conv_out, q_norm_g, k_norm_g, rel_bias, w_attn_out, w_out, peer_wq, peer_keys,
           peer_u, peer_v):
    bsz, s, d = x.shape
    depth = w_ada.shape[0]
    bd = jnp.asarray(np.kron(np.eye(N_HEADS), np.full((HEAD_DIM, HEAD_DIM), 1.0 / HEAD_DIM)), BF16)
    for l in range(depth):
        mod = _ada(c, w_ada[l], b_ada[l])
        sh1, sc1, ga1, sh2, sc2, ga2 = [m.reshape(bsz, 1, d) for m in jnp.split(mod, 6, axis=-1)]
        u, q, k, v, gc, ga = _inproj(
            x, norm1_g[l].reshape(1, d), sc1, sh1, w_in[l].astype(BF16), bd,
            jnp.tile(q_norm_g[l], N_HEADS).reshape(1, D_ATTN),
            jnp.tile(k_norm_g[l], N_HEADS).reshape(1, D_ATTN))
        mc = _conv(u, conv_dw[l], conv_b[l].reshape(1, D_CONV), conv_ln_g[l].reshape(1, D_CONV),
                   conv_ln_b[l].reshape(1, D_CONV), w_conv_out[l].astype(BF16), gc)
        o = _attn(q, k, v, _bias_table(rel_bias[l]))
        x1, h2t = _merge(o, mc, ga, x, w_attn_out[l].astype(BF16), w_out[l].astype(BF16),
                         ga1, norm2_g[l].reshape(1, d), sc2, sh2)
        wqt = peer_wq[l].T.astype(BF16)
        keys = peer_keys[l].reshape(2 * PEER_HEADS, PEER_NKEYS, PEER_DH).astype(BF16)
        s1, theta, e0, e1 = _route(h2t, wqt, keys)
        out = _experts(h2t, peer_u[l].astype(BF16), peer_v[l].T.astype(BF16),
                       s1, theta, e0, e1, x1.reshape(bsz * s, d), ga2, s)
        x = out.reshape(bsz, s, d)
    return x
```

```python
import functools

import jax
import jax.numpy as jnp
import numpy as np
from jax import lax
from jax.experimental import pallas as pl
from jax.experimental.pallas import tpu as pltpu

F32 = jnp.float32
BF16 = jnp.bfloat16

D_MODEL = 1024
CHUNK = 64
N_HEADS = 8
HEAD_DIM = 64
D_ATTN = N_HEADS * HEAD_DIM
LEFT_CHUNKS = 8
REL_CLIP = 128
D_CONV = D_MODEL // 2
CONV_W = 31
PEER_HEADS = 8
PEER_NKEYS = 128
PEER_N = PEER_NKEYS * PEER_NKEYS
PEER_DH = 128
PEER_TOPK = 16
EPS = 1e-6
NEG_INF = -1e30
LOG2E = float(np.log2(np.e))

LANES = 128
SUBLANES = 8
MXU_N = 256
VMEM_LIMIT = 56 * 1024 * 1024

TM_IN = 512
TS_CONV = 512
HALO = 32
TQ = 256
NKB = 3
TM_MERGE = 512
TB_ROUTE = 512
TB_EXP = 1024
EC = 512
ROWS_PER_STEP = EC // PEER_NKEYS

def _cparams(sem):
    return pltpu.CompilerParams(dimension_semantics=sem, vmem_limit_bytes=VMEM_LIMIT)


def _ada_kernel(c_ref, w_ref, b_ref, o_ref):
    c = c_ref[...]
    cond = c * jax.nn.sigmoid(c)
    o_ref[...] = jnp.dot(cond.astype(BF16), w_ref[...].astype(BF16),
                         preferred_element_type=F32) + b_ref[...]


def _ada(c, w, b):
    bsz, d = c.shape
    n = w.shape[1]
    tn = 1024
    return pl.pallas_call(
        _ada_kernel,
        grid=(n // tn,),
        in_specs=[pl.BlockSpec((bsz, d), lambda j: (0, 0)),
                  pl.BlockSpec((d, tn), lambda j: (0, j)),
                  pl.BlockSpec((1, tn), lambda j: (0, j))],
        out_specs=pl.BlockSpec((bsz, tn), lambda j: (0, j)),
        out_shape=jax.ShapeDtypeStruct((bsz, n), F32),
        compiler_params=_cparams(("arbitrary",)),
        name="ada",
    )(c, w, b.reshape(1, n))


def _head_rms(t, bd_ref, gain):
    t2 = t * t
    hi = t2.astype(BF16)
    lo = (t2 - hi.astype(F32)).astype(BF16)
    ms = (jnp.dot(hi, bd_ref[...], preferred_element_type=F32)
          + jnp.dot(lo, bd_ref[...], preferred_element_type=F32))
    return t * lax.rsqrt(ms + EPS) * gain


def _inproj_kernel(x_ref, g_ref, sc_ref, sh_ref, w_ref, bd_ref, qg_ref, kg_ref,
                   u_ref, q_ref, k_ref, v_ref, gc_ref, ga_ref):
    x = x_ref[0]
    ms = jnp.mean(x * x, axis=-1, keepdims=True)
    h = (x * lax.rsqrt(ms + EPS)) * g_ref[...]
    h = h * (1.0 + sc_ref[0]) + sh_ref[0]
    hb = h.astype(BF16)

    def seg(lo, hi):
        return jnp.dot(hb, w_ref[:, lo:hi], preferred_element_type=F32)

    bounds = np.cumsum([0, D_CONV, D_CONV, D_ATTN, D_ATTN, D_ATTN, D_MODEL, D_MODEL])
    lo, hi = bounds[:-1], bounds[1:]
    a = seg(lo[0], hi[0])
    b = seg(lo[1], hi[1])
    q = seg(lo[2], hi[2])
    u_ref[0] = a * jax.nn.sigmoid(b)
    k = seg(lo[3], hi[3])
    q_ref[0] = (_head_rms(q, bd_ref, qg_ref[...]) * (HEAD_DIM ** -0.5 * LOG2E)).astype(BF16)
    v = seg(lo[4], hi[4])
    k_ref[0] = _head_rms(k, bd_ref, kg_ref[...]).astype(BF16)
    gc = seg(lo[5], hi[5])
    v_ref[0] = v.astype(BF16)
    ga = seg(lo[6], hi[6])
    gc_ref[0] = jax.nn.sigmoid(gc)
    ga_ref[0] = jax.nn.sigmoid(ga)


def _inproj(x, g1, sc1, sh1, w_in, bd, qg, kg):
    bsz, s, d = x.shape
    tm = TM_IN
    n_in = w_in.shape[1]
    row = lambda w: pl.BlockSpec((1, tm, w), lambda b, i: (b, i, 0))
    vec = lambda w: pl.BlockSpec((1, w), lambda b, i: (0, 0))
    mod = pl.BlockSpec((1, 1, d), lambda b, i: (b, 0, 0))
    return pl.pallas_call(
        _inproj_kernel,
        grid=(bsz, s // tm),
        in_specs=[row(d), vec(d), mod, mod,
                  pl.BlockSpec((d, n_in), lambda b, i: (0, 0), pipeline_mode=pl.Buffered(1)),
                  pl.BlockSpec((D_ATTN, D_ATTN), lambda b, i: (0, 0)),
                  vec(D_ATTN), vec(D_ATTN)],
        out_specs=[row(D_CONV), row(D_ATTN), row(D_ATTN), row(D_ATTN), row(d), row(d)],
        out_shape=[jax.ShapeDtypeStruct((bsz, s, D_CONV), F32),
                   jax.ShapeDtypeStruct((bsz, s, D_ATTN), BF16),
                   jax.ShapeDtypeStruct((bsz, s, D_ATTN), BF16),
                   jax.ShapeDtypeStruct((bsz, s, D_ATTN), BF16),
                   jax.ShapeDtypeStruct((bsz, s, d), F32),
                   jax.ShapeDtypeStruct((bsz, s, d), F32)],
        compiler_params=_cparams(("arbitrary", "arbitrary")),
        name="inproj",
    )(x, g1, sc1, sh1, w_in, bd, qg, kg)


CONV_ROWS = 64


def _conv_kernel(u_ref, up_ref, dw_ref, cb_ref, lg_ref, lb_ref, w_ref, gc_ref,
                 o_ref, ext_ref, y_ref):
    i = pl.program_id(1)
    ts = u_ref.shape[1]
    prev = up_ref[0, ts - HALO:, :]
    ext_ref[0, 0:HALO, :] = jnp.where(i > 0, prev, 0.0)
    ext_ref[0, HALO:, :] = u_ref[0]
    n_sh = ts + HALO - SUBLANES
    for s in range(1, SUBLANES):
        for r0 in range(0, n_sh, CONV_ROWS):
            n = min(CONV_ROWS, n_sh - r0)
            ext_ref[s, r0:r0 + n, :] = ext_ref[0, r0 + s:r0 + s + n, :]
    base = HALO - (CONV_W - 1)
    for r0 in range(0, ts, CONV_ROWS):
        acc = jnp.zeros((CONV_ROWS, D_CONV), F32) + cb_ref[...]
        for w in range(CONV_W):
            s = (base + w) % SUBLANES
            a = r0 + base + w - s
            acc = acc + ext_ref[s, a:a + CONV_ROWS, :] * dw_ref[w:w + 1, :]
        y_ref[r0:r0 + CONV_ROWS, :] = acc
    y = y_ref[...]
    mu = jnp.mean(y, axis=-1, keepdims=True)
    yc = y - mu
    var = jnp.mean(yc * yc, axis=-1, keepdims=True)
    z = yc * lax.rsqrt(var + EPS) * lg_ref[...] + lb_ref[...]
    z = z * jax.nn.sigmoid(z)
    o = jnp.dot(z.astype(BF16), w_ref[...], preferred_element_type=F32)
    o_ref[0] = gc_ref[0] * o


def _conv(u, dw, cb, lg, lb, w_co, gc):
    bsz, s, dc = u.shape
    d = w_co.shape[1]
    ts = TS_CONV
    vec = lambda w: pl.BlockSpec((1, w), lambda b, i: (0, 0))
    return pl.pallas_call(
        _conv_kernel,
        grid=(bsz, s // ts),
        in_specs=[pl.BlockSpec((1, ts, dc), lambda b, i: (b, i, 0)),
                  pl.BlockSpec((1, ts, dc), lambda b, i: (b, jnp.maximum(i - 1, 0), 0)),
                  pl.BlockSpec((CONV_W, dc), lambda b, i: (0, 0)),
                  vec(dc), vec(dc), vec(dc),
                  pl.BlockSpec((dc, d), lambda b, i: (0, 0)),
                  pl.BlockSpec((1, ts, d), lambda b, i: (b, i, 0))],
        out_specs=pl.BlockSpec((1, ts, d), lambda b, i: (b, i, 0)),
        out_shape=jax.ShapeDtypeStruct((bsz, s, d), F32),
        scratch_shapes=[pltpu.VMEM((SUBLANES, ts + HALO, dc), F32), pltpu.VMEM((ts, dc), F32)],
        compiler_params=_cparams(("arbitrary", "arbitrary")),
        name="conv",
    )(u, u, dw, cb, lg, lb, w_co, gc)


def _attn_kernel(q_ref, k0_ref, k1_ref, k2_ref, v0_ref, v1_ref, v2_ref, bias_ref, o_ref):
    i = pl.program_id(1)
    k_refs = (k0_ref, k1_ref, k2_ref)
    v_refs = (v0_ref, v1_ref, v2_ref)
    low = lax.broadcasted_iota(jnp.int32, (TQ, LANES), 1) < HEAD_DIM

    def body(pens):
        for hp in range(N_HEADS * HEAD_DIM // LANES):
            lo = hp * LANES
            q2 = q_ref[0, :, lo:lo + LANES]
            outs = []
            for half in range(LANES // HEAD_DIM):
                h = hp * (LANES // HEAD_DIM) + half
                qh = jnp.where(low if half == 0 else jnp.logical_not(low), q2, jnp.zeros_like(q2))
                ss = []
                for j in range(NKB):
                    s = lax.dot_general(qh, k_refs[j][0, :, lo:lo + LANES],
                                        (((1,), (1,)), ((), ())), preferred_element_type=F32)
                    s = s + bias_ref[h, :, j * TQ:(j + 1) * TQ]
                    ss.append(s if pens is None else s + pens[j])
                m = jnp.maximum(jnp.maximum(jnp.max(ss[0], axis=-1, keepdims=True),
                                            jnp.max(ss[1], axis=-1, keepdims=True)),
                                jnp.max(ss[2], axis=-1, keepdims=True))
                l = jnp.zeros_like(m)
                acc = jnp.zeros((TQ, LANES), F32)
                for j in range(NKB):
                    p = jnp.exp2(ss[j] - m)
                    l = l + jnp.sum(p, axis=-1, keepdims=True)
                    acc = acc + jnp.dot(p.astype(BF16), v_refs[j][0, :, lo:lo + LANES],
                                        preferred_element_type=F32)
                outs.append(acc / l)
            o_ref[0, :, lo:lo + LANES] = jnp.where(low, outs[0], outs[1]).astype(BF16)

    @pl.when(i >= NKB - 1)
    def _():
        body(None)

    @pl.when(i < NKB - 1)
    def _():
        body([jnp.where(i - (NKB - 1) + j >= 0, 0.0, NEG_INF).astype(F32) for j in range(NKB)])


def _attn(q, k, v, bias):
    bsz, s, da = q.shape
    kspec = lambda j: pl.BlockSpec(
        (1, TQ, da), lambda b, i: (b, jnp.maximum(i - (NKB - 1) + j, 0), 0))
    return pl.pallas_call(
        _attn_kernel,
        grid=(bsz, s // TQ),
        in_specs=[pl.BlockSpec((1, TQ, da), lambda b, i: (b, i, 0)),
                  kspec(0), kspec(1), kspec(2), kspec(0), kspec(1), kspec(2),
                  pl.BlockSpec((N_HEADS, TQ, NKB * TQ), lambda b, i: (0, 0, 0))],
        out_specs=pl.BlockSpec((1, TQ, da), lambda b, i: (b, i, 0)),
        out_shape=jax.ShapeDtypeStruct((bsz, s, da), BF16),
        compiler_params=_cparams(("arbitrary", "arbitrary")),
        name="attn",
    )(q, k, k, k, v, v, v, bias)


def _bias_table(rel_bias):
    nk = NKB * TQ
    lw = TQ + nk - 1
    n_lo = (TQ - 1) - REL_CLIP
    n_hi = (nk - 1) - REL_CLIP
    w = jnp.concatenate([jnp.repeat(rel_bias[:, :1], n_lo, axis=1), rel_bias,
                         jnp.repeat(rel_bias[:, -1:], n_hi, axis=1)], axis=1).astype(F32)
    row = 1024
    assert row + 1 >= lw
    wp = jnp.concatenate([w[:, nk - 1::-1], jnp.zeros((w.shape[0], row + 1 - lw), F32),
                          w[:, lw - 1:nk - 1:-1]], axis=1)
    flat = jnp.tile(wp, (1, TQ))[:, :TQ * row]
    tab = flat.reshape(-1, TQ, row)[:, :, :nk]
    qi = np.arange(TQ)[:, None]
    kj = np.arange(nk)[None, :]
    qc = qi // CHUNK + (NKB - 1) * TQ // CHUNK
    kc = kj // CHUNK
    band = (kc >= qc - LEFT_CHUNKS) & (kc <= qc)
    return jnp.where(band[None], tab * LOG2E, NEG_INF)


def _merge_kernel(o_ref, mc_ref, ga_ref, x_ref, wa_ref, wo_ref, ga1_ref, g2_ref, sc_ref, sh_ref,
                  x1_ref, h2t_ref):
    ya = jnp.dot(o_ref[0], wa_ref[...], preferred_element_type=F32)
    merged = mc_ref[0] + ga_ref[0] * ya
    y = jnp.dot(merged.astype(BF16), wo_ref[...], preferred_element_type=F32)
    x1 = x_ref[0] + ga1_ref[0] * y
    x1_ref[0] = x1
    ms = jnp.mean(x1 * x1, axis=-1, keepdims=True)
    h2 = (x1 * lax.rsqrt(ms + EPS)) * g2_ref[...]
    h2 = h2 * (1.0 + sc_ref[0]) + sh_ref[0]
    h2t_ref[...] = h2.T.astype(BF16)


def _merge(o, mc, ga, x, wa, wo, ga1, g2, sc2, sh2):
    bsz, s, d = x.shape
    tm = TM_MERGE
    nb = s // tm
    row = lambda w: pl.BlockSpec((1, tm, w), lambda b, i: (b, i, 0))
    mod = pl.BlockSpec((1, 1, d), lambda b, i: (b, 0, 0))
    return pl.pallas_call(
        _merge_kernel,
        grid=(bsz, nb),
        in_specs=[row(D_ATTN), row(d), row(d), row(d),
                  pl.BlockSpec((D_ATTN, d), lambda b, i: (0, 0)),
                  pl.BlockSpec((d, d), lambda b, i: (0, 0)),
                  mod, pl.BlockSpec((1, d), lambda b, i: (0, 0)), mod, mod],
        out_specs=[row(d), pl.BlockSpec((d, tm), lambda b, i: (0, b * nb + i))],
        out_shape=[jax.ShapeDtypeStruct((bsz, s, d), F32),
                   jax.ShapeDtypeStruct((d, bsz * s), BF16)],
        compiler_params=_cparams(("arbitrary", "arbitrary")),
        name="merge",
    )(o, mc, ga, x, wa, wo, ga1, g2, sc2, sh2)


_CAND = [(a, b) for a in range(PEER_TOPK) for b in range(PEER_TOPK)
         if (a + 1) * (b + 1) <= PEER_TOPK]


def _sort16_network():
    n, pairs, p = PEER_TOPK, [], 1
    while p < n:
        k = p
        while k >= 1:
            for j in range(k % p, n - k, 2 * k):
                for i in range(min(k, n - j - k)):
                    if (i + j) // (2 * p) == (i + j + k) // (2 * p):
                        pairs.append((i + j, i + j + k))
            k //= 2
        p *= 2
    return pairs


_SORT16 = _sort16_network()


def _top16_sorted(problems, store_row):
    sub = 8
    lvs = [[s[g * sub:(g + 1) * sub, :] for g in range(PEER_NKEYS // sub)] for s in problems]
    for a, b in _SORT16:
        for lv in lvs:
            lv[a], lv[b] = jnp.maximum(lv[a], lv[b]), jnp.minimum(lv[a], lv[b])
    for r in range(PEER_TOPK):
        for p, lv in enumerate(lvs):
            m = jnp.max(lv[0], axis=0, keepdims=True)
            store_row(p, r, m)
            eq = lv[0] == m
            for k in range(PEER_TOPK - 1 - r):
                lv[k] = jnp.where(eq, lv[k + 1], lv[k])


def _route_kernel(h2t_ref, wqt_ref, keys_ref, s1_ref, theta_ref, e0_ref, e1_ref,
                  s_scr, v_scr, tau_scr, zi_scr):
    tb = h2t_ref.shape[1]
    ncol = tb // LANES
    n_hp = 2 * PEER_HEADS
    qt = jnp.dot(wqt_ref[...], h2t_ref[...], preferred_element_type=F32).astype(BF16)
    for hp in range(n_hp):
        s_scr[hp] = jnp.dot(keys_ref[hp], qt[hp * PEER_DH:(hp + 1) * PEER_DH, :],
                            preferred_element_type=F32)

    def stage1(it, carry):
        h = it // ncol
        col = it % ncol
        cs = pl.ds(pl.multiple_of(col * LANES, LANES), LANES)
        rw = pl.ds(h * ncol + col, 1)

        def store_row(p, r, m):
            v_scr[p, r, rw, :] = m

        _top16_sorted([s_scr[2 * h + p, :, cs] for p in range(2)], store_row)
        return carry

    lax.fori_loop(0, PEER_HEADS * ncol, stage1, 0)

    v0 = [v_scr[0, a] for a in range(PEER_TOPK)]
    v1 = [v_scr[1, b] for b in range(PEER_TOPK)]
    cand = [v0[a] + v1[b] for (a, b) in _CAND]
    top = cand[0]
    work = list(cand)
    for r in range(PEER_TOPK):
        c16 = functools.reduce(jnp.maximum, work)
        work = [jnp.where(w == c16, -jnp.inf, w) for w in work]
    c17 = functools.reduce(jnp.maximum, work)
    z = jnp.zeros_like(top)
    for c in cand:
        z = z + jnp.where(c >= c16, jnp.exp(c - top), 0.0)
    tau_scr[...] = 0.5 * (c16 + c17)
    zi_scr[...] = 0.5 / z

    def stage3(it, carry):
        h = it // ncol
        col = it % ncol
        cs = pl.ds(pl.multiple_of(col * LANES, LANES), LANES)
        rw = pl.ds(h * ncol + col, 1)
        s0 = s_scr[2 * h, :, cs]
        s1 = s_scr[2 * h + 1, :, cs]
        theta = jnp.where(s0 >= v_scr[0, PEER_TOPK - 1, rw, :], tau_scr[rw, :] - s0, jnp.inf)
        e0 = jnp.exp(s0 - v_scr[0, 0, rw, :])
        for g in range(PEER_NKEYS // ROWS_PER_STEP):
            rows = slice(g * ROWS_PER_STEP, (g + 1) * ROWS_PER_STEP)
            theta_ref[h, g, :, cs] = theta[rows]
            e0_ref[h, g, :, cs] = e0[rows]
        s1_ref[h, :, cs] = jnp.where(s1 >= v_scr[1, PEER_TOPK - 1, rw, :], s1, -jnp.inf)
        e1_ref[h, :, cs] = jnp.exp(s1 - v_scr[1, 0, rw, :]) * zi_scr[rw, :]
        return carry

    lax.fori_loop(0, PEER_HEADS * ncol, stage3, 0)


def _route(h2t, wqt, keys):
    d, t = h2t.shape
    tb = TB_ROUTE
    ncol = tb // LANES
    n_hp = 2 * PEER_HEADS
    n_grp = PEER_NKEYS // ROWS_PER_STEP
    tab = pl.BlockSpec((PEER_HEADS, PEER_NKEYS, tb), lambda i: (0, 0, i))
    rowtab = pl.BlockSpec((PEER_HEADS, n_grp, ROWS_PER_STEP, tb), lambda i: (0, 0, 0, i))
    tab_shape = jax.ShapeDtypeStruct((PEER_HEADS, PEER_NKEYS, t), F32)
    rowtab_shape = jax.ShapeDtypeStruct((PEER_HEADS, n_grp, ROWS_PER_STEP, t), F32)
    return pl.pallas_call(
        _route_kernel,
        grid=(t // tb,),
        in_specs=[pl.BlockSpec((d, tb), lambda i: (0, i)),
                  pl.BlockSpec(wqt.shape, lambda i: (0, 0)),
                  pl.BlockSpec(keys.shape, lambda i: (0, 0, 0))],
        out_specs=[tab, rowtab, rowtab, tab],
        out_shape=[tab_shape, rowtab_shape, rowtab_shape, tab_shape],
        scratch_shapes=[pltpu.VMEM((n_hp, PEER_NKEYS, tb), F32),
                        pltpu.VMEM((2, PEER_TOPK, PEER_HEADS * ncol, LANES), F32),
                        pltpu.VMEM((PEER_HEADS * ncol, LANES), F32),
                        pltpu.VMEM((PEER_HEADS * ncol, LANES), F32)],
        compiler_params=_cparams(("arbitrary",)),
        name="route",
    )(h2t, wqt, keys)


_INV_SQRT2 = float(1.0 / np.sqrt(2.0))
K_PIECE = 256
ROW_BLOCK = 4
JG_BLOCK = 4


def _experts_kernel(h2t_ref, u_ref, vt_ref, s1_ref, theta_ref, e0_ref, e1_ref,
                    x1_ref, ga2_ref, o_ref, a0_scr, a1_scr, p0_scr, p1_scr, acc_ref):
    c = pl.program_id(1)
    tb = h2t_ref.shape[1]

    @pl.when(c == 0)
    def _():
        acc_ref[...] = jnp.zeros_like(acc_ref)
        p1_scr[...] = jnp.zeros_like(p1_scr)

    sub = (8, LANES)
    n_jg = PEER_NKEYS // sub[0]
    zero = jnp.zeros(sub, F32)

    def pair_body(k, refs, stages):
        a_new, a_old, p_new, p_old = refs
        base = k * MXU_N
        ps = pl.ds(base, MXU_N)

        def mm_a(kc):
            ks = slice(kc * K_PIECE, (kc + 1) * K_PIECE)
            part = jnp.dot(u_ref[:, ks], h2t_ref[ks, ps], preferred_element_type=F32)
            if kc == 0:
                a_new[:, ps] = part
            else:
                a_new[:, ps] += part

        def mm_acc(kc, mh):
            ks = slice(kc * K_PIECE, (kc + 1) * K_PIECE)
            ms = slice(mh * EC, (mh + 1) * EC)
            acc_ref[ms, ps] += jnp.dot(vt_ref[ms, ks], p_old[ks, ps],
                                       preferred_element_type=F32)

        a_pieces = [functools.partial(mm_a, kc) for kc in range(D_MODEL // K_PIECE)]
        acc_pieces = [functools.partial(mm_acc, kc, mh) for kc in range(EC // K_PIECE)
                      for mh in range(D_MODEL // EC)]
        if stages == "fill":
            for piece in a_pieces:
                piece()
            return
        if stages == "drain":
            for piece in acc_pieces:
                piece()
            return
        mm_pieces = a_pieces + acc_pieces
        blocks = [(half, ip, j0) for half in range(MXU_N // LANES)
                  for ip in range(ROWS_PER_STEP // ROW_BLOCK)
                  for j0 in range(0, n_jg, JG_BLOCK)]
        every = len(blocks) // len(mm_pieces)
        for bi, (half, ip, j0) in enumerate(blocks):
            if bi % every == 0:
                mm_pieces[bi // every]()
            cs = pl.ds(base + half * LANES, LANES)
            if True:
                rows = tuple(range(ROW_BLOCK * ip, ROW_BLOCK * (ip + 1)))
                g = [[zero] * JG_BLOCK for _ in rows]
                for h in range(PEER_HEADS):
                    th = [jnp.broadcast_to(theta_ref[h, 0, r:r + 1, cs], sub) for r in rows]
                    e0 = [jnp.broadcast_to(e0_ref[h, 0, r:r + 1, cs], sub) for r in rows]
                    for jg in range(JG_BLOCK):
                        js = slice((j0 + jg) * sub[0], (j0 + jg + 1) * sub[0])
                        s1 = s1_ref[h, js, cs]
                        e1 = e1_ref[h, js, cs]
                        for q in range(ROW_BLOCK):
                            g[q][jg] = g[q][jg] + jnp.where(s1 >= th[q], e1, zero) * e0[q]
                for q, r in enumerate(rows):
                    for jg in range(0, JG_BLOCK, 2):
                        lo = r * PEER_NKEYS + (j0 + jg) * sub[0]
                        a = a_old[lo:lo + 2 * sub[0], cs]
                        act = a + a * lax.erf(a * _INV_SQRT2)
                        gg = jnp.concatenate([g[q][jg], g[q][jg + 1]], axis=0)
                        p_new[lo:lo + 2 * sub[0], cs] = (act * gg).astype(BF16)

    even = (a0_scr, a1_scr, p1_scr, p0_scr)
    odd = (a1_scr, a0_scr, p0_scr, p1_scr)
    last = pl.num_programs(1) - 1

    def run(refs, stages):
        for k in range(tb // MXU_N):
            pair_body(k, refs, stages)

    @pl.when(c == 0)
    def _():
        run(even, "fill")

    @pl.when(jnp.logical_and(c > 0, c % 2 == 0))
    def _():
        run(even, "all")

    @pl.when(jnp.logical_and(c < last, c % 2 == 1))
    def _():
        run(odd, "all")

    @pl.when(c == last)
    def _():
        run(odd, "drain")
        o_ref[...] = x1_ref[...] + ga2_ref[0] * acc_ref[...].T


def _experts(h2t, u, vt, s1, theta, e0, e1, x1, ga2, seq):
    d, t = h2t.shape
    tb = TB_EXP
    n_chunks = u.shape[0] // EC
    last = n_chunks - 1
    per_batch = seq // tb
    once = dict(pipeline_mode=pl.Buffered(1))
    tab = pl.BlockSpec((PEER_HEADS, PEER_NKEYS, tb), lambda i, c: (0, 0, i))
    rowtab = pl.BlockSpec((PEER_HEADS, 1, ROWS_PER_STEP, tb),
                          lambda i, c: (0, jnp.clip(c - 1, 0, last), 0, i))
    return pl.pallas_call(
        _experts_kernel,
        grid=(t // tb, n_chunks + 2),
        in_specs=[pl.BlockSpec((d, tb), lambda i, c: (0, i)),
                  pl.BlockSpec((EC, d), lambda i, c: (jnp.minimum(c, last), 0)),
                  pl.BlockSpec((d, EC), lambda i, c: (0, jnp.clip(c - 2, 0, last))),
                  tab, rowtab, rowtab, tab,
                  pl.BlockSpec((tb, d), lambda i, c: (i, 0), **once),
                  pl.BlockSpec((1, 1, d), lambda i, c: (i // per_batch, 0, 0))],
        out_specs=pl.BlockSpec((tb, d), lambda i, c: (i, 0)),
        out_shape=jax.ShapeDtypeStruct((t, d), F32),
        scratch_shapes=[pltpu.VMEM((EC, tb), F32), pltpu.VMEM((EC, tb), F32),
                        pltpu.VMEM((EC, tb), BF16), pltpu.VMEM((EC, tb), BF16),
                        pltpu.VMEM((d, tb), F32)],
        compiler_params=_cparams(("arbitrary", "arbitrary")),
        name="experts",
    )(h2t, u, vt, s1, theta, e0, e1, x1, ga2)


def kernel(x, c, w_ada, b_ada, norm1_g, norm2_g, w_in, conv_dw, conv_b, conv_ln_g, conv_ln_b,
           w_conv_out, q_norm_g, k_norm_g, rel_bias, w_attn_out, w_out, peer_wq, peer_keys,
           peer_u, peer_v):
    bsz, s, d = x.shape
    depth = w_ada.shape[0]
    bd = jnp.asarray(np.kron(np.eye(N_HEADS), np.full((HEAD_DIM, HEAD_DIM), 1.0 / HEAD_DIM)), BF16)
    for l in range(depth):
        mod = _ada(c, w_ada[l], b_ada[l])
        sh1, sc1, ga1, sh2, sc2, ga2 = [m.reshape(bsz, 1, d) for m in jnp.split(mod, 6, axis=-1)]
        u, q, k, v, gc, ga = _inproj(
            x, norm1_g[l].reshape(1, d), sc1, sh1, w_in[l].astype(BF16), bd,
            jnp.tile(q_norm_g[l], N_HEADS).reshape(1, D_ATTN),
            jnp.tile(k_norm_g[l], N_HEADS).reshape(1, D_ATTN))
        mc = _conv(u, conv_dw[l], conv_b[l].reshape(1, D_CONV), conv_ln_g[l].reshape(1, D_CONV),
                   conv_ln_b[l].reshape(1, D_CONV), w_conv_out[l].astype(BF16), gc)
        o = _attn(q, k, v, _bias_table(rel_bias[l]))
        x1, h2t = _merge(o, mc, ga, x, w_attn_out[l].astype(BF16), w_out[l].astype(BF16),
                         ga1, norm2_g[l].reshape(1, d), sc2, sh2)
        wqt = peer_wq[l].T.astype(BF16)
        keys = peer_keys[l].reshape(2 * PEER_HEADS, PEER_NKEYS, PEER_DH).astype(BF16)
        s1, theta, e0, e1 = _route(h2t, wqt, keys)
        out = _experts(h2t, peer_u[l].astype(BF16), peer_v[l].T.astype(BF16),
                       s1, theta, e0, e1, x1.reshape(bsz * s, d), ga2, s)
        x = out.reshape(bsz, s, d)
    return x
```

```python
import functools

import jax
import jax.numpy as jnp
import numpy as np
from jax import lax
from jax.experimental import pallas as pl
from jax.experimental.pallas import tpu as pltpu

F32 = jnp.float32
BF16 = jnp.bfloat16

D_MODEL = 1024
CHUNK = 64
N_HEADS = 8
HEAD_DIM = 64
D_ATTN = N_HEADS * HEAD_DIM
LEFT_CHUNKS = 8
REL_CLIP = 128
D_CONV = D_MODEL // 2
CONV_W = 31
PEER_HEADS = 8
PEER_NKEYS = 128
PEER_N = PEER_NKEYS * PEER_NKEYS
PEER_DH = 128
PEER_TOPK = 16
EPS = 1e-6
NEG_INF = -1e30
LOG2E = float(np.log2(np.e))

LANES = 128
SUBLANES = 8
MXU_N = 256
VMEM_LIMIT = 56 * 1024 * 1024

TM_IN = 512
TS_CONV = 512
HALO = 32
TQ = 256
NKB = 3
TM_MERGE = 512
TB_ROUTE = 512
TB_EXP = 1024
EC = 512
ROWS_PER_STEP = EC // PEER_NKEYS

def _cparams(sem):
    return pltpu.CompilerParams(dimension_semantics=sem, vmem_limit_bytes=VMEM_LIMIT)


def _ada_kernel(c_ref, w_ref, b_ref, o_ref):
    c = c_ref[...]
    cond = c * jax.nn.sigmoid(c)
    o_ref[...] = jnp.dot(cond.astype(BF16), w_ref[...].astype(BF16),
                         preferred_element_type=F32) + b_ref[...]


def _ada(c, w, b):
    bsz, d = c.shape
    n = w.shape[1]
    tn = 1024
    return pl.pallas_call(
        _ada_kernel,
        grid=(n // tn,),
        in_specs=[pl.BlockSpec((bsz, d), lambda j: (0, 0)),
                  pl.BlockSpec((d, tn), lambda j: (0, j)),
                  pl.BlockSpec((1, tn), lambda j: (0, j))],
        out_specs=pl.BlockSpec((bsz, tn), lambda j: (0, j)),
        out_shape=jax.ShapeDtypeStruct((bsz, n), F32),
        compiler_params=_cparams(("arbitrary",)),
        name="ada",
    )(c, w, b.reshape(1, n))


def _head_rms(t, bd_ref, gain):
    t2 = t * t
    hi = t2.astype(BF16)
    lo = (t2 - hi.astype(F32)).astype(BF16)
    ms = (jnp.dot(hi, bd_ref[...], preferred_element_type=F32)
          + jnp.dot(lo, bd_ref[...], preferred_element_type=F32))
    return t * lax.rsqrt(ms + EPS) * gain


def _inproj_kernel(x_ref, g_ref, sc_ref, sh_ref, w_ref, bd_ref, qg_ref, kg_ref,
                   u_ref, q_ref, k_ref, v_ref, gc_ref, ga_ref):
    x = x_ref[0]
    ms = jnp.mean(x * x, axis=-1, keepdims=True)
    h = (x * lax.rsqrt(ms + EPS)) * g_ref[...]
    h = h * (1.0 + sc_ref[0]) + sh_ref[0]
    hb = h.astype(BF16)

    def seg(lo, hi):
        return jnp.dot(hb, w_ref[:, lo:hi], preferred_element_type=F32)

    bounds = np.cumsum([0, D_CONV, D_CONV, D_ATTN, D_ATTN, D_ATTN, D_MODEL, D_MODEL])
    lo, hi = bounds[:-1], bounds[1:]
    a = seg(lo[0], hi[0])
    b = seg(lo[1], hi[1])
    q = seg(lo[2], hi[2])
    u_ref[0] = a * jax.nn.sigmoid(b)
    k = seg(lo[3], hi[3])
    q_ref[0] = (_head_rms(q, bd_ref, qg_ref[...]) * (HEAD_DIM ** -0.5 * LOG2E)).astype(BF16)
    v = seg(lo[4], hi[4])
    k_ref[0] = _head_rms(k, bd_ref, kg_ref[...]).astype(BF16)
    gc = seg(lo[5], hi[5])
    v_ref[0] = v.astype(BF16)
    ga = seg(lo[6], hi[6])
    gc_ref[0] = jax.nn.sigmoid(gc)
    ga_ref[0] = jax.nn.sigmoid(ga)


def _inproj(x, g1, sc1, sh1, w_in, bd, qg, kg):
    bsz, s, d = x.shape
    tm = TM_IN
    n_in = w_in.shape[1]
    row = lambda w: pl.BlockSpec((1, tm, w), lambda b, i: (b, i, 0))
    vec = lambda w: pl.BlockSpec((1, w), lambda b, i: (0, 0))
    mod = pl.BlockSpec((1, 1, d), lambda b, i: (b, 0, 0))
    return pl.pallas_call(
        _inproj_kernel,
        grid=(bsz, s // tm),
        in_specs=[row(d), vec(d), mod, mod,
                  pl.BlockSpec((d, n_in), lambda b, i: (0, 0), pipeline_mode=pl.Buffered(1)),
                  pl.BlockSpec((D_ATTN, D_ATTN), lambda b, i: (0, 0)),
                  vec(D_ATTN), vec(D_ATTN)],
        out_specs=[row(D_CONV), row(D_ATTN), row(D_ATTN), row(D_ATTN), row(d), row(d)],
        out_shape=[jax.ShapeDtypeStruct((bsz, s, D_CONV), F32),
                   jax.ShapeDtypeStruct((bsz, s, D_ATTN), BF16),
                   jax.ShapeDtypeStruct((bsz, s, D_ATTN), BF16),
                   jax.ShapeDtypeStruct((bsz, s, D_ATTN), BF16),
                   jax.ShapeDtypeStruct((bsz, s, d), F32),
                   jax.ShapeDtypeStruct((bsz, s, d), F32)],
        compiler_params=_cparams(("arbitrary", "arbitrary")),
        name="inproj",
    )(x, g1, sc1, sh1, w_in, bd, qg, kg)


CONV_ROWS = 64


def _conv_kernel(u_ref, up_ref, dw_ref, cb_ref, lg_ref, lb_ref, w_ref, gc_ref,
                 o_ref, ext_ref, y_ref):
    i = pl.program_id(1)
    ts = u_ref.shape[1]
    prev = up_ref[0, ts - HALO:, :]
    ext_ref[0, 0:HALO, :] = jnp.where(i > 0, prev, 0.0)
    ext_ref[0, HALO:, :] = u_ref[0]
    n_sh = ts + HALO - SUBLANES
    for s in range(1, SUBLANES):
        for r0 in range(0, n_sh, CONV_ROWS):
            n = min(CONV_ROWS, n_sh - r0)
            ext_ref[s, r0:r0 + n, :] = ext_ref[0, r0 + s:r0 + s + n, :]
    base = HALO - (CONV_W - 1)
    for r0 in range(0, ts, CONV_ROWS):
        acc = jnp.zeros((CONV_ROWS, D_CONV), F32) + cb_ref[...]
        for w in range(CONV_W):
            s = (base + w) % SUBLANES
            a = r0 + base + w - s
            acc = acc + ext_ref[s, a:a + CONV_ROWS, :] * dw_ref[w:w + 1, :]
        y_ref[r0:r0 + CONV_ROWS, :] = acc
    y = y_ref[...]
    mu = jnp.mean(y, axis=-1, keepdims=True)
    yc = y - mu
    var = jnp.mean(yc * yc, axis=-1, keepdims=True)
    z = yc * lax.rsqrt(var + EPS) * lg_ref[...] + lb_ref[...]
    z = z * jax.nn.sigmoid(z)
    o = jnp.dot(z.astype(BF16), w_ref[...], preferred_element_type=F32)
    o_ref[0] = gc_ref[0] * o


def _conv(u, dw, cb, lg, lb, w_co, gc):
    bsz, s, dc = u.shape
    d = w_co.shape[1]
    ts = TS_CONV
    vec = lambda w: pl.BlockSpec((1, w), lambda b, i: (0, 0))
    return pl.pallas_call(
        _conv_kernel,
        grid=(bsz, s // ts),
        in_specs=[pl.BlockSpec((1, ts, dc), lambda b, i: (b, i, 0)),
                  pl.BlockSpec((1, ts, dc), lambda b, i: (b, jnp.maximum(i - 1, 0), 0)),
                  pl.BlockSpec((CONV_W, dc), lambda b, i: (0, 0)),
                  vec(dc), vec(dc), vec(dc),
                  pl.BlockSpec((dc, d), lambda b, i: (0, 0)),
                  pl.BlockSpec((1, ts, d), lambda b, i: (b, i, 0))],
        out_specs=pl.BlockSpec((1, ts, d), lambda b, i: (b, i, 0)),
        out_shape=jax.ShapeDtypeStruct((bsz, s, d), F32),
        scratch_shapes=[pltpu.VMEM((SUBLANES, ts + HALO, dc), F32), pltpu.VMEM((ts, dc), F32)],
        compiler_params=_cparams(("arbitrary", "arbitrary")),
        name="conv",
    )(u, u, dw, cb, lg, lb, w_co, gc)


def _attn_kernel(q_ref, k0_ref, k1_ref, k2_ref, v0_ref, v1_ref, v2_ref, bias_ref, o_ref):
    i = pl.program_id(1)
    k_refs = (k0_ref, k1_ref, k2_ref)
    v_refs = (v0_ref, v1_ref, v2_ref)
    low = lax.broadcasted_iota(jnp.int32, (TQ, LANES), 1) < HEAD_DIM

    def body(pens):
        for hp in range(N_HEADS * HEAD_DIM // LANES):
            lo = hp * LANES
            q2 = q_ref[0, :, lo:lo + LANES]
            outs = []
            for half in range(LANES // HEAD_DIM):
                h = hp * (LANES // HEAD_DIM) + half
                qh = jnp.where(low if half == 0 else jnp.logical_not(low), q2, jnp.zeros_like(q2))
                ss = []
                for j in range(NKB):
                    s = lax.dot_general(qh, k_refs[j][0, :, lo:lo + LANES],
                                        (((1,), (1,)), ((), ())), preferred_element_type=F32)
                    s = s + bias_ref[h, :, j * TQ:(j + 1) * TQ]
                    ss.append(s if pens is None else s + pens[j])
                m = jnp.maximum(jnp.maximum(jnp.max(ss[0], axis=-1, keepdims=True),
                                            jnp.max(ss[1], axis=-1, keepdims=True)),
                                jnp.max(ss[2], axis=-1, keepdims=True))
                l = jnp.zeros_like(m)
                acc = jnp.zeros((TQ, LANES), F32)
                for j in range(NKB):
                    p = jnp.exp2(ss[j] - m)
                    l = l + jnp.sum(p, axis=-1, keepdims=True)
                    acc = acc + jnp.dot(p.astype(BF16), v_refs[j][0, :, lo:lo + LANES],
                                        preferred_element_type=F32)
                outs.append(acc / l)
            o_ref[0, :, lo:lo + LANES] = jnp.where(low, outs[0], outs[1]).astype(BF16)

    @pl.when(i >= NKB - 1)
    def _():
        body(None)

    @pl.when(i < NKB - 1)
    def _():
        body([jnp.where(i - (NKB - 1) + j >= 0, 0.0, NEG_INF).astype(F32) for j in range(NKB)])


def _attn(q, k, v, bias):
    bsz, s, da = q.shape
    kspec = lambda j: pl.BlockSpec(
        (1, TQ, da), lambda b, i: (b, jnp.maximum(i - (NKB - 1) + j, 0), 0))
    return pl.pallas_call(
        _attn_kernel,
        grid=(bsz, s // TQ),
        in_specs=[pl.BlockSpec((1, TQ, da), lambda b, i: (b, i, 0)),
                  kspec(0), kspec(1), kspec(2), kspec(0), kspec(1), kspec(2),
                  pl.BlockSpec((N_HEADS, TQ, NKB * TQ), lambda b, i: (0, 0, 0))],
        out_specs=pl.BlockSpec((1, TQ, da), lambda b, i: (b, i, 0)),
        out_shape=jax.ShapeDtypeStruct((bsz, s, da), BF16),
        compiler_params=_cparams(("arbitrary", "arbitrary")),
        name="attn",
    )(q, k, k, k, v, v, v, bias)


def _bias_table(rel_bias):
    nk = NKB * TQ
    lw = TQ + nk - 1
    n_lo = (TQ - 1) - REL_CLIP
    n_hi = (nk - 1) - REL_CLIP
    w = jnp.concatenate([jnp.repeat(rel_bias[:, :1], n_lo, axis=1), rel_bias,
                         jnp.repeat(rel_bias[:, -1:], n_hi, axis=1)], axis=1).astype(F32)
    row = 1024
    assert row + 1 >= lw
    wp = jnp.concatenate([w[:, nk - 1::-1], jnp.zeros((w.shape[0], row + 1 - lw), F32),
                          w[:, lw - 1:nk - 1:-1]], axis=1)
    flat = jnp.tile(wp, (1, TQ))[:, :TQ * row]
    tab = flat.reshape(-1, TQ, row)[:, :, :nk]
    qi = np.arange(TQ)[:, None]
    kj = np.arange(nk)[None, :]
    qc = qi // CHUNK + (NKB - 1) * TQ // CHUNK
    kc = kj // CHUNK
    band = (kc >= qc - LEFT_CHUNKS) & (kc <= qc)
    return jnp.where(band[None], tab * LOG2E, NEG_INF)


def _merge_kernel(o_ref, mc_ref, ga_ref, x_ref, wa_ref, wo_ref, ga1_ref, g2_ref, sc_ref, sh_ref,
                  x1_ref, h2t_ref):
    ya = jnp.dot(o_ref[0], wa_ref[...], preferred_element_type=F32)
    merged = mc_ref[0] + ga_ref[0] * ya
    y = jnp.dot(merged.astype(BF16), wo_ref[...], preferred_element_type=F32)
    x1 = x_ref[0] + ga1_ref[0] * y
    x1_ref[0] = x1
    ms = jnp.mean(x1 * x1, axis=-1, keepdims=True)
    h2 = (x1 * lax.rsqrt(ms + EPS)) * g2_ref[...]
    h2 = h2 * (1.0 + sc_ref[0]) + sh_ref[0]
    h2t_ref[...] = h2.T.astype(BF16)


def _merge(o, mc, ga, x, wa, wo, ga1, g2, sc2, sh2):
    bsz, s, d = x.shape
    tm = TM_MERGE
    nb = s // tm
    row = lambda w: pl.BlockSpec((1, tm, w), lambda b, i: (b, i, 0))
    mod = pl.BlockSpec((1, 1, d), lambda b, i: (b, 0, 0))
    return pl.pallas_call(
        _merge_kernel,
        grid=(bsz, nb),
        in_specs=[row(D_ATTN), row(d), row(d), row(d),
                  pl.BlockSpec((D_ATTN, d), lambda b, i: (0, 0)),
                  pl.BlockSpec((d, d), lambda b, i: (0, 0)),
                  mod, pl.BlockSpec((1, d), lambda b, i: (0, 0)), mod, mod],
        out_specs=[row(d), pl.BlockSpec((d, tm), lambda b, i: (0, b * nb + i))],
        out_shape=[jax.ShapeDtypeStruct((bsz, s, d), F32),
                   jax.ShapeDtypeStruct((d, bsz * s), BF16)],
        compiler_params=_cparams(("arbitrary", "arbitrary")),
        name="merge",
    )(o, mc, ga, x, wa, wo, ga1, g2, sc2, sh2)


_CAND = [(a, b) for a in range(PEER_TOPK) for b in range(PEER_TOPK)
         if (a + 1) * (b + 1) <= PEER_TOPK]


def _sort16_network():
    n, pairs, p = PEER_TOPK, [], 1
    while p < n:
        k = p
        while k >= 1:
            for j in range(k % p, n - k, 2 * k):
                for i in range(min(k, n - j - k)):
                    if (i + j) // (2 * p) == (i + j + k) // (2 * p):
                        pairs.append((i + j, i + j + k))
            k //= 2
        p *= 2
    return pairs


_SORT16 = _sort16_network()


def _top16_sorted(problems, store_row):
    sub = 8
    lvs = [[s[g * sub:(g + 1) * sub, :] for g in range(PEER_NKEYS // sub)] for s in problems]
    for a, b in _SORT16:
        for lv in lvs:
            lv[a], lv[b] = jnp.maximum(lv[a], lv[b]), jnp.minimum(lv[a], lv[b])
    for r in range(PEER_TOPK):
        for p, lv in enumerate(lvs):
            m = jnp.max(lv[0], axis=0, keepdims=True)
            store_row(p, r, m)
            eq = lv[0] == m
            for k in range(PEER_TOPK - 1 - r):
                lv[k] = jnp.where(eq, lv[k + 1], lv[k])


def _route_kernel(h2t_ref, wqt_ref, keys_ref, s1_ref, theta_ref, e0_ref, e1_ref,
                  s_scr, v_scr, tau_scr, zi_scr):
    tb = h2t_ref.shape[1]
    ncol = tb // LANES
    n_hp = 2 * PEER_HEADS
    qt = jnp.dot(wqt_ref[...], h2t_ref[...], preferred_element_type=F32).astype(BF16)
    for hp in range(n_hp):
        s_scr[hp] = jnp.dot(keys_ref[hp], qt[hp * PEER_DH:(hp + 1) * PEER_DH, :],
                            preferred_element_type=F32)

    def stage1(it, carry):
        h = it // ncol
        col = it % ncol
        cs = pl.ds(pl.multiple_of(col * LANES, LANES), LANES)
        rw = pl.ds(h * ncol + col, 1)

        def store_row(p, r, m):
            v_scr[p, r, rw, :] = m

        _top16_sorted([s_scr[2 * h + p, :, cs] for p in range(2)], store_row)
        return carry

    lax.fori_loop(0, PEER_HEADS * ncol, stage1, 0)

    v0 = [v_scr[0, a] for a in range(PEER_TOPK)]
    v1 = [v_scr[1, b] for b in range(PEER_TOPK)]
    cand = [v0[a] + v1[b] for (a, b) in _CAND]
    top = cand[0]
    work = list(cand)
    for r in range(PEER_TOPK):
        c16 = functools.reduce(jnp.maximum, work)
        work = [jnp.where(w == c16, -jnp.inf, w) for w in work]
    c17 = functools.reduce(jnp.maximum, work)
    z = jnp.zeros_like(top)
    for c in cand:
        z = z + jnp.where(c >= c16, jnp.exp(c - top), 0.0)
    tau_scr[...] = 0.5 * (c16 + c17)
    zi_scr[...] = 0.5 / z

    def stage3(it, carry):
        h = it // ncol
        col = it % ncol
        cs = pl.ds(pl.multiple_of(col * LANES, LANES), LANES)
        rw = pl.ds(h * ncol + col, 1)
        s0 = s_scr[2 * h, :, cs]
        s1 = s_scr[2 * h + 1, :, cs]
        theta = jnp.where(s0 >= v_scr[0, PEER_TOPK - 1, rw, :], tau_scr[rw, :] - s0, jnp.inf)
        e0 = jnp.exp(s0 - v_scr[0, 0, rw, :])
        for g in range(PEER_NKEYS // ROWS_PER_STEP):
            rows = slice(g * ROWS_PER_STEP, (g + 1) * ROWS_PER_STEP)
            theta_ref[h, g, :, cs] = theta[rows]
            e0_ref[h, g, :, cs] = e0[rows]
        s1_ref[h, :, cs] = jnp.where(s1 >= v_scr[1, PEER_TOPK - 1, rw, :], s1, -jnp.inf)
        e1_ref[h, :, cs] = jnp.exp(s1 - v_scr[1, 0, rw, :]) * zi_scr[rw, :]
        return carry

    lax.fori_loop(0, PEER_HEADS * ncol, stage3, 0)


def _route(h2t, wqt, keys):
    d, t = h2t.shape
    tb = TB_ROUTE
    ncol = tb // LANES
    n_hp = 2 * PEER_HEADS
    n_grp = PEER_NKEYS // ROWS_PER_STEP
    tab = pl.BlockSpec((PEER_HEADS, PEER_NKEYS, tb), lambda i: (0, 0, i))
    rowtab = pl.BlockSpec((PEER_HEADS, n_grp, ROWS_PER_STEP, tb), lambda i: (0, 0, 0, i))
    tab_shape = jax.ShapeDtypeStruct((PEER_HEADS, PEER_NKEYS, t), F32)
    rowtab_shape = jax.ShapeDtypeStruct((PEER_HEADS, n_grp, ROWS_PER_STEP, t), F32)
    return pl.pallas_call(
        _route_kernel,
        grid=(t // tb,),
        in_specs=[pl.BlockSpec((d, tb), lambda i: (0, i)),
                  pl.BlockSpec(wqt.shape, lambda i: (0, 0)),
                  pl.BlockSpec(keys.shape, lambda i: (0, 0, 0))],
        out_specs=[tab, rowtab, rowtab, tab],
        out_shape=[tab_shape, rowtab_shape, rowtab_shape, tab_shape],
        scratch_shapes=[pltpu.VMEM((n_hp, PEER_NKEYS, tb), F32),
                        pltpu.VMEM((2, PEER_TOPK, PEER_HEADS * ncol, LANES), F32),
                        pltpu.VMEM((PEER_HEADS * ncol, LANES), F32),
                        pltpu.VMEM((PEER_HEADS * ncol, LANES), F32)],
        compiler_params=_cparams(("arbitrary",)),
        name="route",
    )(h2t, wqt, keys)


_INV_SQRT2 = float(1.0 / np.sqrt(2.0))
K_PIECE = 256
K_PIECE_A = 512
ROW_BLOCK = 4
JG_BLOCK = 4


def _experts_kernel(h2t_ref, u_ref, vt_ref, s1_ref, theta_ref, e0_ref, e1_ref,
                    x1_ref, ga2_ref, o_ref, a0_scr, a1_scr, p0_scr, p1_scr, acc_ref):
    c = pl.program_id(1)
    tb = h2t_ref.shape[1]

    @pl.when(c == 0)
    def _():
        acc_ref[...] = jnp.zeros_like(acc_ref)
        p1_scr[...] = jnp.zeros_like(p1_scr)

    sub = (8, LANES)
    n_jg = PEER_NKEYS // sub[0]
    zero = jnp.zeros(sub, F32)

    def pair_body(k, refs, stages):
        a_new, a_old, p_new, p_old = refs
        base = k * MXU_N
        ps = pl.ds(base, MXU_N)

        def mm_a(kc):
            ks = slice(kc * K_PIECE_A, (kc + 1) * K_PIECE_A)
            part = jnp.dot(u_ref[:, ks], h2t_ref[ks, ps], preferred_element_type=F32)
            if kc == 0:
                a_new[:, ps] = part
            else:
                a_new[:, ps] += part

        def mm_acc(kc, mh):
            ks = slice(kc * K_PIECE, (kc + 1) * K_PIECE)
            ms = slice(mh * EC, (mh + 1) * EC)
            acc_ref[ms, ps] += jnp.dot(vt_ref[ms, ks], p_old[ks, ps],
                                       preferred_element_type=F32)

        a_pieces = [functools.partial(mm_a, kc) for kc in range(D_MODEL // K_PIECE_A)]
        acc_pieces = [functools.partial(mm_acc, kc, mh) for kc in range(EC // K_PIECE)
                      for mh in range(D_MODEL // EC)]
        if stages == "fill":
            for piece in a_pieces:
                piece()
            return
        if stages == "drain":
            for piece in acc_pieces:
                piece()
            return
        mm_pieces = a_pieces + acc_pieces
        blocks = [(half, ip, j0) for half in range(MXU_N // LANES)
                  for ip in range(ROWS_PER_STEP // ROW_BLOCK)
                  for j0 in range(0, n_jg, JG_BLOCK)]
        every = len(blocks) // len(mm_pieces)
        for bi, (half, ip, j0) in enumerate(blocks):
            if bi % every == 0 and bi // every < len(mm_pieces):
                mm_pieces[bi // every]()
            cs = pl.ds(base + half * LANES, LANES)
            if True:
                rows = tuple(range(ROW_BLOCK * ip, ROW_BLOCK * (ip + 1)))
                g = [[zero] * JG_BLOCK for _ in rows]
                for h in range(PEER_HEADS):
                    th = [jnp.broadcast_to(theta_ref[h, 0, r:r + 1, cs], sub) for r in rows]
                    e0 = [jnp.broadcast_to(e0_ref[h, 0, r:r + 1, cs], sub) for r in rows]
                    for jg in range(JG_BLOCK):
                        js = slice((j0 + jg) * sub[0], (j0 + jg + 1) * sub[0])
                        s1 = s1_ref[h, js, cs]
                        e1 = e1_ref[h, js, cs]
                        for q in range(ROW_BLOCK):
                            g[q][jg] = g[q][jg] + jnp.where(s1 >= th[q], e1, zero) * e0[q]
                for q, r in enumerate(rows):
                    for jg in range(0, JG_BLOCK, 2):
                        lo = r * PEER_NKEYS + (j0 + jg) * sub[0]
                        a = a_old[lo:lo + 2 * sub[0], cs]
                        act = a + a * lax.erf(a * _INV_SQRT2)
                        gg = jnp.concatenate([g[q][jg], g[q][jg + 1]], axis=0)
                        p_new[lo:lo + 2 * sub[0], cs] = (act * gg).astype(BF16)

    even = (a0_scr, a1_scr, p1_scr, p0_scr)
    odd = (a1_scr, a0_scr, p0_scr, p1_scr)
    last = pl.num_programs(1) - 1

    def run(refs, stages):
        for k in range(tb // MXU_N):
            pair_body(k, refs, stages)

    @pl.when(c == 0)
    def _():
        run(even, "fill")

    @pl.when(jnp.logical_and(c > 0, c % 2 == 0))
    def _():
        run(even, "all")

    @pl.when(jnp.logical_and(c < last, c % 2 == 1))
    def _():
        run(odd, "all")

    @pl.when(c == last)
    def _():
        run(odd, "drain")
        o_ref[...] = x1_ref[...] + ga2_ref[0] * acc_ref[...].T


def _experts(h2t, u, vt, s1, theta, e0, e1, x1, ga2, seq):
    d, t = h2t.shape
    tb = TB_EXP
    n_chunks = u.shape[0] // EC
    last = n_chunks - 1
    per_batch = seq // tb
    once = dict(pipeline_mode=pl.Buffered(1))
    tab = pl.BlockSpec((PEER_HEADS, PEER_NKEYS, tb), lambda i, c: (0, 0, i), **once)
    rowtab = pl.BlockSpec((PEER_HEADS, 1, ROWS_PER_STEP, tb),
                          lambda i, c: (0, jnp.clip(c - 1, 0, last), 0, i))
    return pl.pallas_call(
        _experts_kernel,
        grid=(t // tb, n_chunks + 2),
        in_specs=[pl.BlockSpec((d, tb), lambda i, c: (0, i)),
                  pl.BlockSpec((EC, d), lambda i, c: (jnp.minimum(c, last), 0)),
                  pl.BlockSpec((d, EC), lambda i, c: (0, jnp.clip(c - 2, 0, last))),
                  tab, rowtab, rowtab, tab,
                  pl.BlockSpec((tb, d), lambda i, c: (i, 0), **once),
                  pl.BlockSpec((1, 1, d), lambda i, c: (i // per_batch, 0, 0))],
        out_specs=pl.BlockSpec((tb, d), lambda i, c: (i, 0)),
        out_shape=jax.ShapeDtypeStruct((t, d), F32),
        scratch_shapes=[pltpu.VMEM((EC, tb), F32), pltpu.VMEM((EC, tb), F32),
                        pltpu.VMEM((EC, tb), BF16), pltpu.VMEM((EC, tb), BF16),
                        pltpu.VMEM((d, tb), F32)],
        compiler_params=_cparams(("arbitrary", "arbitrary")),
        name="experts",
    )(h2t, u, vt, s1, theta, e0, e1, x1, ga2)


def kernel(x, c, w_ada, b_ada, norm1_g, norm2_g, w_in, conv_dw, conv_b, conv_ln_g, conv_ln_b,
           w_conv_out, q_norm_g, k_norm_g, rel_bias, w_attn_out, w_out, peer_wq, peer_keys,
           peer_u, peer_v):
    bsz, s, d = x.shape
    depth = w_ada.shape[0]
    bd = jnp.asarray(np.kron(np.eye(N_HEADS), np.full((HEAD_DIM, HEAD_DIM), 1.0 / HEAD_DIM)), BF16)
    for l in range(depth):
        mod = _ada(c, w_ada[l], b_ada[l])
        sh1, sc1, ga1, sh2, sc2, ga2 = [m.reshape(bsz, 1, d) for m in jnp.split(mod, 6, axis=-1)]
        u, q, k, v, gc, ga = _inproj(
            x, norm1_g[l].reshape(1, d), sc1, sh1, w_in[l].astype(BF16), bd,
            jnp.tile(q_norm_g[l], N_HEADS).reshape(1, D_ATTN),
            jnp.tile(k_norm_g[l], N_HEADS).reshape(1, D_ATTN))
        mc = _conv(u, conv_dw[l], conv_b[l].reshape(1, D_CONV), conv_ln_g[l].reshape(1, D_CONV),
                   conv_ln_b[l].reshape(1, D_CONV), w_conv_out[l].astype(BF16), gc)
        o = _attn(q, k, v, _bias_table(rel_bias[l]))
        x1, h2t = _merge(o, mc, ga, x, w_attn_out[l].astype(BF16), w_out[l].astype(BF16),
                         ga1, norm2_g[l].reshape(1, d), sc2, sh2)
        wqt = peer_wq[l].T.astype(BF16)
        keys = peer_keys[l].reshape(2 * PEER_HEADS, PEER_NKEYS, PEER_DH).astype(BF16)
        s1, theta, e0, e1 = _route(h2t, wqt, keys)
        out = _experts(h2t, peer_u[l].astype(BF16), peer_v[l].T.astype(BF16),
                       s1, theta, e0, e1, x1.reshape(bsz * s, d), ga2, s)
        x = out.reshape(bsz, s, d)
    return x
```

```python
import functools

import jax
import jax.numpy as jnp
import numpy as np
from jax import lax
from jax.experimental import pallas as pl
from jax.experimental.pallas import tpu as pltpu

F32 = jnp.float32
BF16 = jnp.bfloat16

D_MODEL = 1024
CHUNK = 64
N_HEADS = 8
HEAD_DIM = 64
D_ATTN = N_HEADS * HEAD_DIM
LEFT_CHUNKS = 8
REL_CLIP = 128
D_CONV = D_MODEL // 2
CONV_W = 31
PEER_HEADS = 8
PEER_NKEYS = 128
PEER_N = PEER_NKEYS * PEER_NKEYS
PEER_DH = 128
PEER_TOPK = 16
EPS = 1e-6
NEG_INF = -1e30
LOG2E = float(np.log2(np.e))

LANES = 128
SUBLANES = 8
MXU_N = 256
VMEM_LIMIT = 56 * 1024 * 1024

TM_IN = 512
TS_CONV = 512
HALO = 32
TQ = 256
NKB = 3
TM_MERGE = 512
TB_ROUTE = 512
TB_EXP = 1024
EC = 512
ROWS_PER_STEP = EC // PEER_NKEYS

def _cparams(sem):
    return pltpu.CompilerParams(dimension_semantics=sem, vmem_limit_bytes=VMEM_LIMIT)


def _ada_kernel(c_ref, w_ref, b_ref, o_ref):
    c = c_ref[...]
    cond = c * jax.nn.sigmoid(c)
    o_ref[...] = jnp.dot(cond.astype(BF16), w_ref[...].astype(BF16),
                         preferred_element_type=F32) + b_ref[...]


def _ada(c, w, b):
    bsz, d = c.shape
    n = w.shape[1]
    tn = 1024
    return pl.pallas_call(
        _ada_kernel,
        grid=(n // tn,),
        in_specs=[pl.BlockSpec((bsz, d), lambda j: (0, 0)),
                  pl.BlockSpec((d, tn), lambda j: (0, j)),
                  pl.BlockSpec((1, tn), lambda j: (0, j))],
        out_specs=pl.BlockSpec((bsz, tn), lambda j: (0, j)),
        out_shape=jax.ShapeDtypeStruct((bsz, n), F32),
        compiler_params=_cparams(("arbitrary",)),
        name="ada",
    )(c, w, b.reshape(1, n))


def _head_rms(t, bd_ref, gain):
    t2 = t * t
    hi = t2.astype(BF16)
    lo = (t2 - hi.astype(F32)).astype(BF16)
    ms = (jnp.dot(hi, bd_ref[...], preferred_element_type=F32)
          + jnp.dot(lo, bd_ref[...], preferred_element_type=F32))
    return t * lax.rsqrt(ms + EPS) * gain


def _inproj_kernel(x_ref, g_ref, sc_ref, sh_ref, w_ref, bd_ref, qg_ref, kg_ref,
                   u_ref, q_ref, k_ref, v_ref, gc_ref, ga_ref):
    x = x_ref[0]
    ms = jnp.mean(x * x, axis=-1, keepdims=True)
    h = (x * lax.rsqrt(ms + EPS)) * g_ref[...]
    h = h * (1.0 + sc_ref[0]) + sh_ref[0]
    hb = h.astype(BF16)

    def seg(lo, hi):
        return jnp.dot(hb, w_ref[:, lo:hi], preferred_element_type=F32)

    bounds = np.cumsum([0, D_CONV, D_CONV, D_ATTN, D_ATTN, D_ATTN, D_MODEL, D_MODEL])
    lo, hi = bounds[:-1], bounds[1:]
    a = seg(lo[0], hi[0])
    b = seg(lo[1], hi[1])
    q = seg(lo[2], hi[2])
    u_ref[0] = a * jax.nn.sigmoid(b)
    k = seg(lo[3], hi[3])
    q_ref[0] = (_head_rms(q, bd_ref, qg_ref[...]) * (HEAD_DIM ** -0.5 * LOG2E)).astype(BF16)
    v = seg(lo[4], hi[4])
    k_ref[0] = _head_rms(k, bd_ref, kg_ref[...]).astype(BF16)
    gc = seg(lo[5], hi[5])
    v_ref[0] = v.astype(BF16)
    ga = seg(lo[6], hi[6])
    gc_ref[0] = jax.nn.sigmoid(gc)
    ga_ref[0] = jax.nn.sigmoid(ga)


def _inproj(x, g1, sc1, sh1, w_in, bd, qg, kg):
    bsz, s, d = x.shape
    tm = TM_IN
    n_in = w_in.shape[1]
    row = lambda w: pl.BlockSpec((1, tm, w), lambda b, i: (b, i, 0))
    vec = lambda w: pl.BlockSpec((1, w), lambda b, i: (0, 0))
    mod = pl.BlockSpec((1, 1, d), lambda b, i: (b, 0, 0))
    return pl.pallas_call(
        _inproj_kernel,
        grid=(bsz, s // tm),
        in_specs=[row(d), vec(d), mod, mod,
                  pl.BlockSpec((d, n_in), lambda b, i: (0, 0), pipeline_mode=pl.Buffered(1)),
                  pl.BlockSpec((D_ATTN, D_ATTN), lambda b, i: (0, 0)),
                  vec(D_ATTN), vec(D_ATTN)],
        out_specs=[row(D_CONV), row(D_ATTN), row(D_ATTN), row(D_ATTN), row(d), row(d)],
        out_shape=[jax.ShapeDtypeStruct((bsz, s, D_CONV), F32),
                   jax.ShapeDtypeStruct((bsz, s, D_ATTN), BF16),
                   jax.ShapeDtypeStruct((bsz, s, D_ATTN), BF16),
                   jax.ShapeDtypeStruct((bsz, s, D_ATTN), BF16),
                   jax.ShapeDtypeStruct((bsz, s, d), F32),
                   jax.ShapeDtypeStruct((bsz, s, d), F32)],
        compiler_params=_cparams(("arbitrary", "arbitrary")),
        name="inproj",
    )(x, g1, sc1, sh1, w_in, bd, qg, kg)


CONV_ROWS = 64


def _conv_kernel(u_ref, up_ref, dw_ref, cb_ref, lg_ref, lb_ref, w_ref, gc_ref,
                 o_ref, ext_ref, y_ref):
    i = pl.program_id(1)
    ts = u_ref.shape[1]
    prev = up_ref[0, ts - HALO:, :]
    ext_ref[0, 0:HALO, :] = jnp.where(i > 0, prev, 0.0)
    ext_ref[0, HALO:, :] = u_ref[0]
    n_sh = ts + HALO - SUBLANES
    for s in range(1, SUBLANES):
        for r0 in range(0, n_sh, CONV_ROWS):
            n = min(CONV_ROWS, n_sh - r0)
            ext_ref[s, r0:r0 + n, :] = ext_ref[0, r0 + s:r0 + s + n, :]
    base = HALO - (CONV_W - 1)
    for r0 in range(0, ts, CONV_ROWS):
        acc = jnp.zeros((CONV_ROWS, D_CONV), F32) + cb_ref[...]
        for w in range(CONV_W):
            s = (base + w) % SUBLANES
            a = r0 + base + w - s
            acc = acc + ext_ref[s, a:a + CONV_ROWS, :] * dw_ref[w:w + 1, :]
        y_ref[r0:r0 + CONV_ROWS, :] = acc
    y = y_ref[...]
    mu = jnp.mean(y, axis=-1, keepdims=True)
    yc = y - mu
    var = jnp.mean(yc * yc, axis=-1, keepdims=True)
    z = yc * lax.rsqrt(var + EPS) * lg_ref[...] + lb_ref[...]
    z = z * jax.nn.sigmoid(z)
    o = jnp.dot(z.astype(BF16), w_ref[...], preferred_element_type=F32)
    o_ref[0] = gc_ref[0] * o


def _conv(u, dw, cb, lg, lb, w_co, gc):
    bsz, s, dc = u.shape
    d = w_co.shape[1]
    ts = TS_CONV
    vec = lambda w: pl.BlockSpec((1, w), lambda b, i: (0, 0))
    return pl.pallas_call(
        _conv_kernel,
        grid=(bsz, s // ts),
        in_specs=[pl.BlockSpec((1, ts, dc), lambda b, i: (b, i, 0)),
                  pl.BlockSpec((1, ts, dc), lambda b, i: (b, jnp.maximum(i - 1, 0), 0)),
                  pl.BlockSpec((CONV_W, dc), lambda b, i: (0, 0)),
                  vec(dc), vec(dc), vec(dc),
                  pl.BlockSpec((dc, d), lambda b, i: (0, 0)),
                  pl.BlockSpec((1, ts, d), lambda b, i: (b, i, 0))],
        out_specs=pl.BlockSpec((1, ts, d), lambda b, i: (b, i, 0)),
        out_shape=jax.ShapeDtypeStruct((bsz, s, d), F32),
        scratch_shapes=[pltpu.VMEM((SUBLANES, ts + HALO, dc), F32), pltpu.VMEM((ts, dc), F32)],
        compiler_params=_cparams(("arbitrary", "arbitrary")),
        name="conv",
    )(u, u, dw, cb, lg, lb, w_co, gc)


def _attn_kernel(q_ref, k0_ref, k1_ref, k2_ref, v0_ref, v1_ref, v2_ref, bias_ref, o_ref):
    i = pl.program_id(1)
    k_refs = (k0_ref, k1_ref, k2_ref)
    v_refs = (v0_ref, v1_ref, v2_ref)
    low = lax.broadcasted_iota(jnp.int32, (TQ, LANES), 1) < HEAD_DIM

    n_heads_per_slab = LANES // HEAD_DIM

    def body(pens):
        def scores(h):
            lo = (h // n_heads_per_slab) * LANES
            q2 = q_ref[0, :, lo:lo + LANES]
            keep = low if h % n_heads_per_slab == 0 else jnp.logical_not(low)
            qh = jnp.where(keep, q2, jnp.zeros_like(q2))
            ss = []
            for j in range(NKB):
                s = lax.dot_general(qh, k_refs[j][0, :, lo:lo + LANES],
                                    (((1,), (1,)), ((), ())), preferred_element_type=F32)
                s = s + bias_ref[h, :, j * TQ:(j + 1) * TQ]
                ss.append(s if pens is None else s + pens[j])
            return ss

        def softmax_pv(h, ss):
            lo = (h // n_heads_per_slab) * LANES
            m = jnp.maximum(jnp.maximum(jnp.max(ss[0], axis=-1, keepdims=True),
                                        jnp.max(ss[1], axis=-1, keepdims=True)),
                            jnp.max(ss[2], axis=-1, keepdims=True))
            ps = [jnp.exp2(ss[j] - m) for j in range(NKB)]
            l = jnp.sum(ps[0] + ps[1] + ps[2], axis=-1, keepdims=True)
            p = jnp.concatenate([pj.astype(BF16) for pj in ps], axis=1)
            v = jnp.concatenate([v_refs[j][0, :, lo:lo + LANES] for j in range(NKB)], axis=0)
            return jnp.dot(p, v, preferred_element_type=F32) / l

        outs = []
        ss_next = scores(0)
        for h in range(N_HEADS):
            ss = ss_next
            if h + 1 < N_HEADS:
                ss_next = scores(h + 1)
            outs.append(softmax_pv(h, ss))
            if h % n_heads_per_slab == n_heads_per_slab - 1:
                lo = (h // n_heads_per_slab) * LANES
                o_ref[0, :, lo:lo + LANES] = jnp.where(low, outs[0], outs[1]).astype(BF16)
                outs = []

    @pl.when(i >= NKB - 1)
    def _():
        body(None)

    @pl.when(i < NKB - 1)
    def _():
        body([jnp.where(i - (NKB - 1) + j >= 0, 0.0, NEG_INF).astype(F32) for j in range(NKB)])


def _attn(q, k, v, bias):
    bsz, s, da = q.shape
    kspec = lambda j: pl.BlockSpec(
        (1, TQ, da), lambda b, i: (b, jnp.maximum(i - (NKB - 1) + j, 0), 0))
    return pl.pallas_call(
        _attn_kernel,
        grid=(bsz, s // TQ),
        in_specs=[pl.BlockSpec((1, TQ, da), lambda b, i: (b, i, 0)),
                  kspec(0), kspec(1), kspec(2), kspec(0), kspec(1), kspec(2),
                  pl.BlockSpec((N_HEADS, TQ, NKB * TQ), lambda b, i: (0, 0, 0))],
        out_specs=pl.BlockSpec((1, TQ, da), lambda b, i: (b, i, 0)),
        out_shape=jax.ShapeDtypeStruct((bsz, s, da), BF16),
        compiler_params=_cparams(("arbitrary", "arbitrary")),
        name="attn",
    )(q, k, k, k, v, v, v, bias)


def _bias_table(rel_bias):
    nk = NKB * TQ
    lw = TQ + nk - 1
    n_lo = (TQ - 1) - REL_CLIP
    n_hi = (nk - 1) - REL_CLIP
    w = jnp.concatenate([jnp.repeat(rel_bias[:, :1], n_lo, axis=1), rel_bias,
                         jnp.repeat(rel_bias[:, -1:], n_hi, axis=1)], axis=1).astype(F32)
    row = 1024
    assert row + 1 >= lw
    wp = jnp.concatenate([w[:, nk - 1::-1], jnp.zeros((w.shape[0], row + 1 - lw), F32),
                          w[:, lw - 1:nk - 1:-1]], axis=1)
    flat = jnp.tile(wp, (1, TQ))[:, :TQ * row]
    tab = flat.reshape(-1, TQ, row)[:, :, :nk]
    qi = np.arange(TQ)[:, None]
    kj = np.arange(nk)[None, :]
    qc = qi // CHUNK + (NKB - 1) * TQ // CHUNK
    kc = kj // CHUNK
    band = (kc >= qc - LEFT_CHUNKS) & (kc <= qc)
    return jnp.where(band[None], tab * LOG2E, NEG_INF)


def _merge_kernel(o_ref, mc_ref, ga_ref, x_ref, wa_ref, wo_ref, ga1_ref, g2_ref, sc_ref, sh_ref,
                  x1_ref, h2t_ref):
    ya = jnp.dot(o_ref[0], wa_ref[...], preferred_element_type=F32)
    merged = mc_ref[0] + ga_ref[0] * ya
    y = jnp.dot(merged.astype(BF16), wo_ref[...], preferred_element_type=F32)
    x1 = x_ref[0] + ga1_ref[0] * y
    x1_ref[0] = x1
    ms = jnp.mean(x1 * x1, axis=-1, keepdims=True)
    h2 = (x1 * lax.rsqrt(ms + EPS)) * g2_ref[...]
    h2 = h2 * (1.0 + sc_ref[0]) + sh_ref[0]
    h2t_ref[...] = h2.T.astype(BF16)


def _merge(o, mc, ga, x, wa, wo, ga1, g2, sc2, sh2):
    bsz, s, d = x.shape
    tm = TM_MERGE
    nb = s // tm
    row = lambda w: pl.BlockSpec((1, tm, w), lambda b, i: (b, i, 0))
    mod = pl.BlockSpec((1, 1, d), lambda b, i: (b, 0, 0))
    return pl.pallas_call(
        _merge_kernel,
        grid=(bsz, nb),
        in_specs=[row(D_ATTN), row(d), row(d), row(d),
                  pl.BlockSpec((D_ATTN, d), lambda b, i: (0, 0)),
                  pl.BlockSpec((d, d), lambda b, i: (0, 0)),
                  mod, pl.BlockSpec((1, d), lambda b, i: (0, 0)), mod, mod],
        out_specs=[row(d), pl.BlockSpec((d, tm), lambda b, i: (0, b * nb + i))],
        out_shape=[jax.ShapeDtypeStruct((bsz, s, d), F32),
                   jax.ShapeDtypeStruct((d, bsz * s), BF16)],
        compiler_params=_cparams(("arbitrary", "arbitrary")),
        name="merge",
    )(o, mc, ga, x, wa, wo, ga1, g2, sc2, sh2)


_CAND = [(a, b) for a in range(PEER_TOPK) for b in range(PEER_TOPK)
         if (a + 1) * (b + 1) <= PEER_TOPK]


def _sort16_network():
    n, pairs, p = PEER_TOPK, [], 1
    while p < n:
        k = p
        while k >= 1:
            for j in range(k % p, n - k, 2 * k):
                for i in range(min(k, n - j - k)):
                    if (i + j) // (2 * p) == (i + j + k) // (2 * p):
                        pairs.append((i + j, i + j + k))
            k //= 2
        p *= 2
    return pairs


_SORT16 = _sort16_network()


def _top16_sorted(problems, store_row):
    sub = 8
    lvs = [[s[g * sub:(g + 1) * sub, :] for g in range(PEER_NKEYS // sub)] for s in problems]
    for a, b in _SORT16:
        for lv in lvs:
            lv[a], lv[b] = jnp.maximum(lv[a], lv[b]), jnp.minimum(lv[a], lv[b])
    for r in range(PEER_TOPK):
        for p, lv in enumerate(lvs):
            m = jnp.max(lv[0], axis=0, keepdims=True)
            store_row(p, r, m)
            eq = lv[0] == m
            for k in range(PEER_TOPK - 1 - r):
                lv[k] = jnp.where(eq, lv[k + 1], lv[k])


def _route_kernel(h2t_ref, wqt_ref, keys_ref, s1_ref, theta_ref, e0_ref, e1_ref,
                  s_scr, v_scr, tau_scr, zi_scr):
    tb = h2t_ref.shape[1]
    ncol = tb // LANES
    n_hp = 2 * PEER_HEADS
    qt = jnp.dot(wqt_ref[...], h2t_ref[...], preferred_element_type=F32).astype(BF16)
    for hp in range(n_hp):
        s_scr[hp] = jnp.dot(keys_ref[hp], qt[hp * PEER_DH:(hp + 1) * PEER_DH, :],
                            preferred_element_type=F32)

    def stage1(it, carry):
        h = it // ncol
        col = it % ncol
        cs = pl.ds(pl.multiple_of(col * LANES, LANES), LANES)
        rw = pl.ds(h * ncol + col, 1)

        def store_row(p, r, m):
            v_scr[p, r, rw, :] = m

        _top16_sorted([s_scr[2 * h + p, :, cs] for p in range(2)], store_row)
        return carry

    lax.fori_loop(0, PEER_HEADS * ncol, stage1, 0)

    v0 = [v_scr[0, a] for a in range(PEER_TOPK)]
    v1 = [v_scr[1, b] for b in range(PEER_TOPK)]
    cand = [v0[a] + v1[b] for (a, b) in _CAND]
    top = cand[0]
    work = list(cand)
    for r in range(PEER_TOPK):
        c16 = functools.reduce(jnp.maximum, work)
        work = [jnp.where(w == c16, -jnp.inf, w) for w in work]
    c17 = functools.reduce(jnp.maximum, work)
    z = jnp.zeros_like(top)
    for c in cand:
        z = z + jnp.where(c >= c16, jnp.exp(c - top), 0.0)
    tau_scr[...] = 0.5 * (c16 + c17)
    zi_scr[...] = 0.5 / z

    def stage3(it, carry):
        h = it // ncol
        col = it % ncol
        cs = pl.ds(pl.multiple_of(col * LANES, LANES), LANES)
        rw = pl.ds(h * ncol + col, 1)
        s0 = s_scr[2 * h, :, cs]
        s1 = s_scr[2 * h + 1, :, cs]
        theta = jnp.where(s0 >= v_scr[0, PEER_TOPK - 1, rw, :], tau_scr[rw, :] - s0, jnp.inf)
        e0 = jnp.exp(s0 - v_scr[0, 0, rw, :])
        for g in range(PEER_NKEYS // ROWS_PER_STEP):
            rows = slice(g * ROWS_PER_STEP, (g + 1) * ROWS_PER_STEP)
            theta_ref[h, g, :, cs] = theta[rows]
            e0_ref[h, g, :, cs] = e0[rows]
        s1_ref[h, :, cs] = jnp.where(s1 >= v_scr[1, PEER_TOPK - 1, rw, :], s1, -jnp.inf)
        e1_ref[h, :, cs] = jnp.exp(s1 - v_scr[1, 0, rw, :]) * zi_scr[rw, :]
        return carry

    lax.fori_loop(0, PEER_HEADS * ncol, stage3, 0)


def _route(h2t, wqt, keys):
    d, t = h2t.shape
    tb = TB_ROUTE
    ncol = tb // LANES
    n_hp = 2 * PEER_HEADS
    n_grp = PEER_NKEYS // ROWS_PER_STEP
    tab = pl.BlockSpec((PEER_HEADS, PEER_NKEYS, tb), lambda i: (0, 0, i))
    rowtab = pl.BlockSpec((PEER_HEADS, n_grp, ROWS_PER_STEP, tb), lambda i: (0, 0, 0, i))
    tab_shape = jax.ShapeDtypeStruct((PEER_HEADS, PEER_NKEYS, t), F32)
    rowtab_shape = jax.ShapeDtypeStruct((PEER_HEADS, n_grp, ROWS_PER_STEP, t), F32)
    return pl.pallas_call(
        _route_kernel,
        grid=(t // tb,),
        in_specs=[pl.BlockSpec((d, tb), lambda i: (0, i)),
                  pl.BlockSpec(wqt.shape, lambda i: (0, 0)),
                  pl.BlockSpec(keys.shape, lambda i: (0, 0, 0))],
        out_specs=[tab, rowtab, rowtab, tab],
        out_shape=[tab_shape, rowtab_shape, rowtab_shape, tab_shape],
        scratch_shapes=[pltpu.VMEM((n_hp, PEER_NKEYS, tb), F32),
                        pltpu.VMEM((2, PEER_TOPK, PEER_HEADS * ncol, LANES), F32),
                        pltpu.VMEM((PEER_HEADS * ncol, LANES), F32),
                        pltpu.VMEM((PEER_HEADS * ncol, LANES), F32)],
        compiler_params=_cparams(("arbitrary",)),
        name="route",
    )(h2t, wqt, keys)


_INV_SQRT2 = float(1.0 / np.sqrt(2.0))
K_PIECE = 256
K_PIECE_A = 256
ROW_BLOCK = 4
JG_BLOCK = 4


def _experts_kernel(h2t_ref, u_ref, vt_ref, s1_ref, theta_ref, e0_ref, e1_ref,
                    x1_ref, ga2_ref, o_ref, a0_scr, a1_scr, p0_scr, p1_scr, acc_ref):
    c = pl.program_id(1)
    tb = h2t_ref.shape[1]

    @pl.when(c == 0)
    def _():
        acc_ref[...] = jnp.zeros_like(acc_ref)
        p1_scr[...] = jnp.zeros_like(p1_scr)

    sub = (8, LANES)
    n_jg = PEER_NKEYS // sub[0]
    zero = jnp.zeros(sub, F32)

    def pair_body(k, refs, stages):
        a_new, a_old, p_new, p_old = refs
        base = k * MXU_N
        ps = pl.ds(base, MXU_N)

        def mm_a(kc):
            ks = slice(kc * K_PIECE_A, (kc + 1) * K_PIECE_A)
            part = jnp.dot(u_ref[:, ks], h2t_ref[ks, ps], preferred_element_type=F32)
            if kc == 0:
                a_new[:, ps] = part
            else:
                a_new[:, ps] += part

        def mm_acc(kc, mh):
            ks = slice(kc * K_PIECE, (kc + 1) * K_PIECE)
            ms = slice(mh * EC, (mh + 1) * EC)
            acc_ref[ms, ps] += jnp.dot(vt_ref[ms, ks], p_old[ks, ps],
                                       preferred_element_type=F32)

        a_pieces = [functools.partial(mm_a, kc) for kc in range(D_MODEL // K_PIECE_A)]
        acc_pieces = [functools.partial(mm_acc, kc, mh) for kc in range(EC // K_PIECE)
                      for mh in range(D_MODEL // EC)]
        if stages == "fill":
            for piece in a_pieces:
                piece()
            return
        if stages == "drain":
            for piece in acc_pieces:
                piece()
            return
        mm_pieces = a_pieces + acc_pieces
        blocks = [(half, ip, j0) for half in range(MXU_N // LANES)
                  for ip in range(ROWS_PER_STEP // ROW_BLOCK)
                  for j0 in range(0, n_jg, JG_BLOCK)]
        every = len(blocks) // len(mm_pieces)
        for bi, (half, ip, j0) in enumerate(blocks):
            if bi % every == 0 and bi // every < len(mm_pieces):
                mm_pieces[bi // every]()
            cs = pl.ds(base + half * LANES, LANES)
            if True:
                rows = tuple(range(ROW_BLOCK * ip, ROW_BLOCK * (ip + 1)))
                g = [[zero] * JG_BLOCK for _ in rows]
                for h in range(PEER_HEADS):
                    th = [jnp.broadcast_to(theta_ref[h, 0, r:r + 1, cs], sub) for r in rows]
                    e0 = [jnp.broadcast_to(e0_ref[h, 0, r:r + 1, cs], sub) for r in rows]
                    for jg in range(JG_BLOCK):
                        js = slice((j0 + jg) * sub[0], (j0 + jg + 1) * sub[0])
                        s1 = s1_ref[h, js, cs]
                        e1 = e1_ref[h, js, cs]
                        for q in range(ROW_BLOCK):
                            g[q][jg] = g[q][jg] + jnp.where(s1 >= th[q], e1, zero) * e0[q]
                for q, r in enumerate(rows):
                    for jg in range(0, JG_BLOCK, 2):
                        lo = r * PEER_NKEYS + (j0 + jg) * sub[0]
                        a = a_old[lo:lo + 2 * sub[0], cs]
                        act = a + a * lax.erf(a * _INV_SQRT2)
                        gg = jnp.concatenate([g[q][jg], g[q][jg + 1]], axis=0)
                        p_new[lo:lo + 2 * sub[0], cs] = (act * gg).astype(BF16)

    even = (a0_scr, a1_scr, p1_scr, p0_scr)
    odd = (a1_scr, a0_scr, p0_scr, p1_scr)
    last = pl.num_programs(1) - 1

    def run(refs, stages):
        for k in range(tb // MXU_N):
            pair_body(k, refs, stages)

    @pl.when(c == 0)
    def _():
        run(even, "fill")

    @pl.when(jnp.logical_and(c > 0, c % 2 == 0))
    def _():
        run(even, "all")

    @pl.when(jnp.logical_and(c < last, c % 2 == 1))
    def _():
        run(odd, "all")

    @pl.when(c == last)
    def _():
        run(odd, "drain")
        o_ref[...] = x1_ref[...] + ga2_ref[0] * acc_ref[...].T


def _experts(h2t, u, vt, s1, theta, e0, e1, x1, ga2, seq):
    d, t = h2t.shape
    tb = TB_EXP
    n_chunks = u.shape[0] // EC
    last = n_chunks - 1
    per_batch = seq // tb
    once = dict(pipeline_mode=pl.Buffered(1))
    tab = pl.BlockSpec((PEER_HEADS, PEER_NKEYS, tb), lambda i, c: (0, 0, i), **once)
    rowtab = pl.BlockSpec((PEER_HEADS, 1, ROWS_PER_STEP, tb),
                          lambda i, c: (0, jnp.clip(c - 1, 0, last), 0, i))
    return pl.pallas_call(
        _experts_kernel,
        grid=(t // tb, n_chunks + 2),
        in_specs=[pl.BlockSpec((d, tb), lambda i, c: (0, i)),
                  pl.BlockSpec((EC, d), lambda i, c: (jnp.minimum(c, last), 0)),
                  pl.BlockSpec((d, EC), lambda i, c: (0, jnp.clip(c - 2, 0, last))),
                  tab, rowtab, rowtab, tab,
                  pl.BlockSpec((tb, d), lambda i, c: (i, 0), **once),
                  pl.BlockSpec((1, 1, d), lambda i, c: (i // per_batch, 0, 0))],
        out_specs=pl.BlockSpec((tb, d), lambda i, c: (i, 0)),
        out_shape=jax.ShapeDtypeStruct((t, d), F32),
        scratch_shapes=[pltpu.VMEM((EC, tb), F32), pltpu.VMEM((EC, tb), F32),
                        pltpu.VMEM((EC, tb), BF16), pltpu.VMEM((EC, tb), BF16),
                        pltpu.VMEM((d, tb), F32)],
        compiler_params=_cparams(("arbitrary", "arbitrary")),
        name="experts",
    )(h2t, u, vt, s1, theta, e0, e1, x1, ga2)


def kernel(x, c, w_ada, b_ada, norm1_g, norm2_g, w_in, conv_dw, conv_b, conv_ln_g, conv_ln_b,
           w_conv_out, q_norm_g, k_norm_g, rel_bias, w_attn_out, w_out, peer_wq, peer_keys,
           peer_u, peer_v):
    bsz, s, d = x.shape
    depth = w_ada.shape[0]
    bd = jnp.asarray(np.kron(np.eye(N_HEADS), np.full((HEAD_DIM, HEAD_DIM), 1.0 / HEAD_DIM)), BF16)
    for l in range(depth):
        mod = _ada(c, w_ada[l], b_ada[l])
        sh1, sc1, ga1, sh2, sc2, ga2 = [m.reshape(bsz, 1, d) for m in jnp.split(mod, 6, axis=-1)]
        u, q, k, v, gc, ga = _inproj(
            x, norm1_g[l].reshape(1, d), sc1, sh1, w_in[l].astype(BF16), bd,
            jnp.tile(q_norm_g[l], N_HEADS).reshape(1, D_ATTN),
            jnp.tile(k_norm_g[l], N_HEADS).reshape(1, D_ATTN))
        mc = _conv(u, conv_dw[l], conv_b[l].reshape(1, D_CONV), conv_ln_g[l].reshape(1, D_CONV),
                   conv_ln_b[l].reshape(1, D_CONV), w_conv_out[l].astype(BF16), gc)
        o = _attn(q, k, v, _bias_table(rel_bias[l]))
        x1, h2t = _merge(o, mc, ga, x, w_attn_out[l].astype(BF16), w_out[l].astype(BF16),
                         ga1, norm2_g[l].reshape(1, d), sc2, sh2)
        wqt = peer_wq[l].T.astype(BF16)
        keys = peer_keys[l].reshape(2 * PEER_HEADS, PEER_NKEYS, PEER_DH).astype(BF16)
        s1, theta, e0, e1 = _route(h2t, wqt, keys)
        out = _experts(h2t, peer_u[l].astype(BF16), peer_v[l].T.astype(BF16),
                       s1, theta, e0, e1, x1.reshape(bsz * s, d), ga2, s)
        x = out.reshape(bsz, s, d)
    return x
```

```python
import functools

import jax
import jax.numpy as jnp
import numpy as np
from jax import lax
from jax.experimental import pallas as pl
from jax.experimental.pallas import tpu as pltpu

F32 = jnp.float32
BF16 = jnp.bfloat16

D_MODEL = 1024
CHUNK = 64
N_HEADS = 8
HEAD_DIM = 64
D_ATTN = N_HEADS * HEAD_DIM
LEFT_CHUNKS = 8
REL_CLIP = 128
D_CONV = D_MODEL // 2
CONV_W = 31
PEER_HEADS = 8
PEER_NKEYS = 128
PEER_N = PEER_NKEYS * PEER_NKEYS
PEER_DH = 128
PEER_TOPK = 16
EPS = 1e-6
NEG_INF = -1e30
LOG2E = float(np.log2(np.e))

LANES = 128
SUBLANES = 8
MXU_N = 256
VMEM_LIMIT = 56 * 1024 * 1024

TM_IN = 512
TS_CONV = 512
HALO = 32
TQ = 256
NKB = 3
TM_MERGE = 512
TB_ROUTE = 512
TB_EXP = 1024
EC = 512
ROWS_PER_STEP = EC // PEER_NKEYS

def _cparams(sem):
    return pltpu.CompilerParams(dimension_semantics=sem, vmem_limit_bytes=VMEM_LIMIT)


def _ada_kernel(c_ref, w_ref, b_ref, o_ref):
    c = c_ref[...]
    cond = c * jax.nn.sigmoid(c)
    o_ref[...] = jnp.dot(cond.astype(BF16), w_ref[...].astype(BF16),
                         preferred_element_type=F32) + b_ref[...]


def _ada(c, w, b):
    bsz, d = c.shape
    n = w.shape[1]
    tn = 1024
    return pl.pallas_call(
        _ada_kernel,
        grid=(n // tn,),
        in_specs=[pl.BlockSpec((bsz, d), lambda j: (0, 0)),
                  pl.BlockSpec((d, tn), lambda j: (0, j)),
                  pl.BlockSpec((1, tn), lambda j: (0, j))],
        out_specs=pl.BlockSpec((bsz, tn), lambda j: (0, j)),
        out_shape=jax.ShapeDtypeStruct((bsz, n), F32),
        compiler_params=_cparams(("arbitrary",)),
        name="ada",
    )(c, w, b.reshape(1, n))


def _head_rms(t, bd_ref, gain):
    t2 = t * t
    hi = t2.astype(BF16)
    lo = (t2 - hi.astype(F32)).astype(BF16)
    ms = (jnp.dot(hi, bd_ref[...], preferred_element_type=F32)
          + jnp.dot(lo, bd_ref[...], preferred_element_type=F32))
    return t * lax.rsqrt(ms + EPS) * gain


def _inproj_kernel(x_ref, g_ref, sc_ref, sh_ref, w_ref, bd_ref, qg_ref, kg_ref,
                   u_ref, q_ref, k_ref, v_ref, gc_ref, ga_ref):
    x = x_ref[0]
    ms = jnp.mean(x * x, axis=-1, keepdims=True)
    h = (x * lax.rsqrt(ms + EPS)) * g_ref[...]
    h = h * (1.0 + sc_ref[0]) + sh_ref[0]
    hb = h.astype(BF16)

    def seg(lo, hi):
        return jnp.dot(hb, w_ref[:, lo:hi], preferred_element_type=F32)

    bounds = np.cumsum([0, D_CONV, D_CONV, D_ATTN, D_ATTN, D_ATTN, D_MODEL, D_MODEL])
    lo, hi = bounds[:-1], bounds[1:]
    a = seg(lo[0], hi[0])
    b = seg(lo[1], hi[1])
    q = seg(lo[2], hi[2])
    u_ref[0] = a * jax.nn.sigmoid(b)
    k = seg(lo[3], hi[3])
    q_ref[0] = (_head_rms(q, bd_ref, qg_ref[...]) * (HEAD_DIM ** -0.5 * LOG2E)).astype(BF16)
    v = seg(lo[4], hi[4])
    k_ref[0] = _head_rms(k, bd_ref, kg_ref[...]).astype(BF16)
    gc = seg(lo[5], hi[5])
    v_ref[0] = v.astype(BF16)
    ga = seg(lo[6], hi[6])
    gc_ref[0] = jax.nn.sigmoid(gc)
    ga_ref[0] = jax.nn.sigmoid(ga)


def _inproj(x, g1, sc1, sh1, w_in, bd, qg, kg):
    bsz, s, d = x.shape
    tm = TM_IN
    n_in = w_in.shape[1]
    row = lambda w: pl.BlockSpec((1, tm, w), lambda b, i: (b, i, 0))
    vec = lambda w: pl.BlockSpec((1, w), lambda b, i: (0, 0))
    mod = pl.BlockSpec((1, 1, d), lambda b, i: (b, 0, 0))
    return pl.pallas_call(
        _inproj_kernel,
        grid=(bsz, s // tm),
        in_specs=[row(d), vec(d), mod, mod,
                  pl.BlockSpec((d, n_in), lambda b, i: (0, 0), pipeline_mode=pl.Buffered(1)),
                  pl.BlockSpec((D_ATTN, D_ATTN), lambda b, i: (0, 0)),
                  vec(D_ATTN), vec(D_ATTN)],
        out_specs=[row(D_CONV), row(D_ATTN), row(D_ATTN), row(D_ATTN), row(d), row(d)],
        out_shape=[jax.ShapeDtypeStruct((bsz, s, D_CONV), F32),
                   jax.ShapeDtypeStruct((bsz, s, D_ATTN), BF16),
                   jax.ShapeDtypeStruct((bsz, s, D_ATTN), BF16),
                   jax.ShapeDtypeStruct((bsz, s, D_ATTN), BF16),
                   jax.ShapeDtypeStruct((bsz, s, d), F32),
                   jax.ShapeDtypeStruct((bsz, s, d), F32)],
        compiler_params=_cparams(("arbitrary", "arbitrary")),
        name="inproj",
    )(x, g1, sc1, sh1, w_in, bd, qg, kg)


CONV_ROWS = 64


def _conv_kernel(u_ref, up_ref, dw_ref, cb_ref, lg_ref, lb_ref, w_ref, gc_ref,
                 o_ref, ext_ref, y_ref):
    i = pl.program_id(1)
    ts = u_ref.shape[1]
    prev = up_ref[0, ts - HALO:, :]
    ext_ref[0, 0:HALO, :] = jnp.where(i > 0, prev, 0.0)
    ext_ref[0, HALO:, :] = u_ref[0]
    n_sh = ts + HALO - SUBLANES
    for s in range(1, SUBLANES):
        for r0 in range(0, n_sh, CONV_ROWS):
            n = min(CONV_ROWS, n_sh - r0)
            ext_ref[s, r0:r0 + n, :] = ext_ref[0, r0 + s:r0 + s + n, :]
    base = HALO - (CONV_W - 1)
    for r0 in range(0, ts, CONV_ROWS):
        acc = jnp.zeros((CONV_ROWS, D_CONV), F32) + cb_ref[...]
        for w in range(CONV_W):
            s = (base + w) % SUBLANES
            a = r0 + base + w - s
            acc = acc + ext_ref[s, a:a + CONV_ROWS, :] * dw_ref[w:w + 1, :]
        y_ref[r0:r0 + CONV_ROWS, :] = acc
    y = y_ref[...]
    mu = jnp.mean(y, axis=-1, keepdims=True)
    yc = y - mu
    var = jnp.mean(yc * yc, axis=-1, keepdims=True)
    z = yc * lax.rsqrt(var + EPS) * lg_ref[...] + lb_ref[...]
    z = z * jax.nn.sigmoid(z)
    o = jnp.dot(z.astype(BF16), w_ref[...], preferred_element_type=F32)
    o_ref[0] = gc_ref[0] * o


def _conv(u, dw, cb, lg, lb, w_co, gc):
    bsz, s, dc = u.shape
    d = w_co.shape[1]
    ts = TS_CONV
    vec = lambda w: pl.BlockSpec((1, w), lambda b, i: (0, 0))
    return pl.pallas_call(
        _conv_kernel,
        grid=(bsz, s // ts),
        in_specs=[pl.BlockSpec((1, ts, dc), lambda b, i: (b, i, 0)),
                  pl.BlockSpec((1, ts, dc), lambda b, i: (b, jnp.maximum(i - 1, 0), 0)),
                  pl.BlockSpec((CONV_W, dc), lambda b, i: (0, 0)),
                  vec(dc), vec(dc), vec(dc),
                  pl.BlockSpec((dc, d), lambda b, i: (0, 0)),
                  pl.BlockSpec((1, ts, d), lambda b, i: (b, i, 0))],
        out_specs=pl.BlockSpec((1, ts, d), lambda b, i: (b, i, 0)),
        out_shape=jax.ShapeDtypeStruct((bsz, s, d), F32),
        scratch_shapes=[pltpu.VMEM((SUBLANES, ts + HALO, dc), F32), pltpu.VMEM((ts, dc), F32)],
        compiler_params=_cparams(("arbitrary", "arbitrary")),
        name="conv",
    )(u, u, dw, cb, lg, lb, w_co, gc)


def _attn_kernel(q_ref, k0_ref, k1_ref, k2_ref, v0_ref, v1_ref, v2_ref, bias_ref, o_ref):
    i = pl.program_id(1)
    k_refs = (k0_ref, k1_ref, k2_ref)
    v_refs = (v0_ref, v1_ref, v2_ref)
    low = lax.broadcasted_iota(jnp.int32, (TQ, LANES), 1) < HEAD_DIM

    n_heads_per_slab = LANES // HEAD_DIM

    def body(pens):
        def scores(h):
            lo = (h // n_heads_per_slab) * LANES
            q2 = q_ref[0, :, lo:lo + LANES]
            keep = low if h % n_heads_per_slab == 0 else jnp.logical_not(low)
            qh = jnp.where(keep, q2, jnp.zeros_like(q2))
            ss = []
            for j in range(NKB):
                s = lax.dot_general(qh, k_refs[j][0, :, lo:lo + LANES],
                                    (((1,), (1,)), ((), ())), preferred_element_type=F32)
                s = s + bias_ref[h, :, j * TQ:(j + 1) * TQ]
                ss.append(s if pens is None else s + pens[j])
            return ss

        def softmax_pv(h, ss):
            lo = (h // n_heads_per_slab) * LANES
            m = jnp.maximum(jnp.maximum(jnp.max(ss[0], axis=-1, keepdims=True),
                                        jnp.max(ss[1], axis=-1, keepdims=True)),
                            jnp.max(ss[2], axis=-1, keepdims=True))
            ps = [jnp.exp2(ss[j] - m) for j in range(NKB)]
            l = jnp.sum(ps[0] + ps[1] + ps[2], axis=-1, keepdims=True)
            p = jnp.concatenate([pj.astype(BF16) for pj in ps], axis=1)
            v = jnp.concatenate([v_refs[j][0, :, lo:lo + LANES] for j in range(NKB)], axis=0)
            return jnp.dot(p, v, preferred_element_type=F32) / l

        outs = []
        ss_next = scores(0)
        for h in range(N_HEADS):
            ss = ss_next
            if h + 1 < N_HEADS:
                ss_next = scores(h + 1)
            outs.append(softmax_pv(h, ss))
            if h % n_heads_per_slab == n_heads_per_slab - 1:
                lo = (h // n_heads_per_slab) * LANES
                o_ref[0, :, lo:lo + LANES] = jnp.where(low, outs[0], outs[1]).astype(BF16)
                outs = []

    @pl.when(i >= NKB - 1)
    def _():
        body(None)

    @pl.when(i < NKB - 1)
    def _():
        body([jnp.where(i - (NKB - 1) + j >= 0, 0.0, NEG_INF).astype(F32) for j in range(NKB)])


def _attn(q, k, v, bias):
    bsz, s, da = q.shape
    kspec = lambda j: pl.BlockSpec(
        (1, TQ, da), lambda b, i: (b, jnp.maximum(i - (NKB - 1) + j, 0), 0))
    return pl.pallas_call(
        _attn_kernel,
        grid=(bsz, s // TQ),
        in_specs=[pl.BlockSpec((1, TQ, da), lambda b, i: (b, i, 0)),
                  kspec(0), kspec(1), kspec(2), kspec(0), kspec(1), kspec(2),
                  pl.BlockSpec((N_HEADS, TQ, NKB * TQ), lambda b, i: (0, 0, 0))],
        out_specs=pl.BlockSpec((1, TQ, da), lambda b, i: (b, i, 0)),
        out_shape=jax.ShapeDtypeStruct((bsz, s, da), BF16),
        compiler_params=_cparams(("arbitrary", "arbitrary")),
        name="attn",
    )(q, k, k, k, v, v, v, bias)


def _bias_table(rel_bias):
    nk = NKB * TQ
    lw = TQ + nk - 1
    n_lo = (TQ - 1) - REL_CLIP
    n_hi = (nk - 1) - REL_CLIP
    w = jnp.concatenate([jnp.repeat(rel_bias[:, :1], n_lo, axis=1), rel_bias,
                         jnp.repeat(rel_bias[:, -1:], n_hi, axis=1)], axis=1).astype(F32)
    row = 1024
    assert row + 1 >= lw
    wp = jnp.concatenate([w[:, nk - 1::-1], jnp.zeros((w.shape[0], row + 1 - lw), F32),
                          w[:, lw - 1:nk - 1:-1]], axis=1)
    flat = jnp.tile(wp, (1, TQ))[:, :TQ * row]
    tab = flat.reshape(-1, TQ, row)[:, :, :nk]
    qi = np.arange(TQ)[:, None]
    kj = np.arange(nk)[None, :]
    qc = qi // CHUNK + (NKB - 1) * TQ // CHUNK
    kc = kj // CHUNK
    band = (kc >= qc - LEFT_CHUNKS) & (kc <= qc)
    return jnp.where(band[None], tab * LOG2E, NEG_INF)


def _merge_kernel(o_ref, mc_ref, ga_ref, x_ref, wa_ref, wo_ref, ga1_ref, g2_ref, sc_ref, sh_ref,
                  x1_ref, h2t_ref):
    ya = jnp.dot(o_ref[0], wa_ref[...], preferred_element_type=F32)
    merged = mc_ref[0] + ga_ref[0] * ya
    y = jnp.dot(merged.astype(BF16), wo_ref[...], preferred_element_type=F32)
    x1 = x_ref[0] + ga1_ref[0] * y
    x1_ref[0] = x1
    ms = jnp.mean(x1 * x1, axis=-1, keepdims=True)
    h2 = (x1 * lax.rsqrt(ms + EPS)) * g2_ref[...]
    h2 = h2 * (1.0 + sc_ref[0]) + sh_ref[0]
    h2t_ref[...] = h2.T.astype(BF16)


def _merge(o, mc, ga, x, wa, wo, ga1, g2, sc2, sh2):
    bsz, s, d = x.shape
    tm = TM_MERGE
    nb = s // tm
    row = lambda w: pl.BlockSpec((1, tm, w), lambda b, i: (b, i, 0))
    mod = pl.BlockSpec((1, 1, d), lambda b, i: (b, 0, 0))
    return pl.pallas_call(
        _merge_kernel,
        grid=(bsz, nb),
        in_specs=[row(D_ATTN), row(d), row(d), row(d),
                  pl.BlockSpec((D_ATTN, d), lambda b, i: (0, 0)),
                  pl.BlockSpec((d, d), lambda b, i: (0, 0)),
                  mod, pl.BlockSpec((1, d), lambda b, i: (0, 0)), mod, mod],
        out_specs=[row(d), pl.BlockSpec((d, tm), lambda b, i: (0, b * nb + i))],
        out_shape=[jax.ShapeDtypeStruct((bsz, s, d), F32),
                   jax.ShapeDtypeStruct((d, bsz * s), BF16)],
        compiler_params=_cparams(("arbitrary", "arbitrary")),
        name="merge",
    )(o, mc, ga, x, wa, wo, ga1, g2, sc2, sh2)


_CAND = [(a, b) for a in range(PEER_TOPK) for b in range(PEER_TOPK)
         if (a + 1) * (b + 1) <= PEER_TOPK]


def _sort16_network():
    n, pairs, p = PEER_TOPK, [], 1
    while p < n:
        k = p
        while k >= 1:
            for j in range(k % p, n - k, 2 * k):
                for i in range(min(k, n - j - k)):
                    if (i + j) // (2 * p) == (i + j + k) // (2 * p):
                        pairs.append((i + j, i + j + k))
            k //= 2
        p *= 2
    return pairs


_SORT16 = _sort16_network()


def _top16_sorted(problems, store_row):
    sub = 8
    lvs = [[s[g * sub:(g + 1) * sub, :] for g in range(PEER_NKEYS // sub)] for s in problems]
    for a, b in _SORT16:
        for lv in lvs:
            lv[a], lv[b] = jnp.maximum(lv[a], lv[b]), jnp.minimum(lv[a], lv[b])
    for r in range(PEER_TOPK):
        for p, lv in enumerate(lvs):
            m = jnp.max(lv[0], axis=0, keepdims=True)
            store_row(p, r, m)
            eq = lv[0] == m
            for k in range(PEER_TOPK - 1 - r):
                lv[k] = jnp.where(eq, lv[k + 1], lv[k])


def _route_kernel(h2t_ref, wqt_ref, keys_ref, s1_ref, theta_ref, e0_ref, e1_ref,
                  s_scr, v_scr, tau_scr, zi_scr):
    tb = h2t_ref.shape[1]
    ncol = tb // LANES
    n_hp = 2 * PEER_HEADS
    def scores(h):
        rows = slice(2 * h * PEER_DH, 2 * (h + 1) * PEER_DH)
        qt = jnp.dot(wqt_ref[rows, :], h2t_ref[...], preferred_element_type=F32).astype(BF16)
        for p in range(2):
            s_scr[2 * h + p] = jnp.dot(keys_ref[2 * h + p], qt[p * PEER_DH:(p + 1) * PEER_DH, :],
                                       preferred_element_type=F32)

    def top16(h):
        for col in range(ncol):
            cs = slice(col * LANES, (col + 1) * LANES)
            rw = slice(h * ncol + col, h * ncol + col + 1)

            def store_row(p, r, m, rw=rw):
                v_scr[p, r, rw, :] = m

            _top16_sorted([s_scr[2 * h + p, :, cs] for p in range(2)], store_row)

    scores(0)
    for h in range(PEER_HEADS):
        if h + 1 < PEER_HEADS:
            scores(h + 1)
        top16(h)

    v0 = [v_scr[0, a] for a in range(PEER_TOPK)]
    v1 = [v_scr[1, b] for b in range(PEER_TOPK)]
    cand = [v0[a] + v1[b] for (a, b) in _CAND]
    top = cand[0]
    work = list(cand)
    for r in range(PEER_TOPK):
        c16 = functools.reduce(jnp.maximum, work)
        work = [jnp.where(w == c16, -jnp.inf, w) for w in work]
    c17 = functools.reduce(jnp.maximum, work)
    z = jnp.zeros_like(top)
    for c in cand:
        z = z + jnp.where(c >= c16, jnp.exp(c - top), 0.0)
    tau_scr[...] = 0.5 * (c16 + c17)
    zi_scr[...] = 0.5 / z

    def stage3(it, carry):
        h = it // ncol
        col = it % ncol
        cs = pl.ds(pl.multiple_of(col * LANES, LANES), LANES)
        rw = pl.ds(h * ncol + col, 1)
        s0 = s_scr[2 * h, :, cs]
        s1 = s_scr[2 * h + 1, :, cs]
        theta = jnp.where(s0 >= v_scr[0, PEER_TOPK - 1, rw, :], tau_scr[rw, :] - s0, jnp.inf)
        e0 = jnp.exp(s0 - v_scr[0, 0, rw, :])
        for g in range(PEER_NKEYS // ROWS_PER_STEP):
            rows = slice(g * ROWS_PER_STEP, (g + 1) * ROWS_PER_STEP)
            theta_ref[h, g, :, cs] = theta[rows]
            e0_ref[h, g, :, cs] = e0[rows]
        s1_ref[h, :, cs] = jnp.where(s1 >= v_scr[1, PEER_TOPK - 1, rw, :], s1, -jnp.inf)
        e1_ref[h, :, cs] = jnp.exp(s1 - v_scr[1, 0, rw, :]) * zi_scr[rw, :]
        return carry

    lax.fori_loop(0, PEER_HEADS * ncol, stage3, 0)


def _route(h2t, wqt, keys):
    d, t = h2t.shape
    tb = TB_ROUTE
    ncol = tb // LANES
    n_hp = 2 * PEER_HEADS
    n_grp = PEER_NKEYS // ROWS_PER_STEP
    tab = pl.BlockSpec((PEER_HEADS, PEER_NKEYS, tb), lambda i: (0, 0, i))
    rowtab = pl.BlockSpec((PEER_HEADS, n_grp, ROWS_PER_STEP, tb), lambda i: (0, 0, 0, i))
    tab_shape = jax.ShapeDtypeStruct((PEER_HEADS, PEER_NKEYS, t), F32)
    rowtab_shape = jax.ShapeDtypeStruct((PEER_HEADS, n_grp, ROWS_PER_STEP, t), F32)
    return pl.pallas_call(
        _route_kernel,
        grid=(t // tb,),
        in_specs=[pl.BlockSpec((d, tb), lambda i: (0, i)),
                  pl.BlockSpec(wqt.shape, lambda i: (0, 0)),
                  pl.BlockSpec(keys.shape, lambda i: (0, 0, 0))],
        out_specs=[tab, rowtab, rowtab, tab],
        out_shape=[tab_shape, rowtab_shape, rowtab_shape, tab_shape],
        scratch_shapes=[pltpu.VMEM((n_hp, PEER_NKEYS, tb), F32),
                        pltpu.VMEM((2, PEER_TOPK, PEER_HEADS * ncol, LANES), F32),
                        pltpu.VMEM((PEER_HEADS * ncol, LANES), F32),
                        pltpu.VMEM((PEER_HEADS * ncol, LANES), F32)],
        compiler_params=_cparams(("arbitrary",)),
        name="route",
    )(h2t, wqt, keys)


_INV_SQRT2 = float(1.0 / np.sqrt(2.0))
K_PIECE = 256
K_PIECE_A = 256
ROW_BLOCK = 4
JG_BLOCK = 4


def _experts_kernel(h2t_ref, u_ref, vt_ref, s1_ref, theta_ref, e0_ref, e1_ref,
                    x1_ref, ga2_ref, o_ref, a0_scr, a1_scr, p0_scr, p1_scr, acc_ref):
    c = pl.program_id(1)
    tb = h2t_ref.shape[1]

    @pl.when(c == 0)
    def _():
        acc_ref[...] = jnp.zeros_like(acc_ref)
        p1_scr[...] = jnp.zeros_like(p1_scr)

    sub = (8, LANES)
    n_jg = PEER_NKEYS // sub[0]
    zero = jnp.zeros(sub, F32)

    def pair_body(k, refs, stages):
        a_new, a_old, p_new, p_old = refs
        base = k * MXU_N
        ps = pl.ds(base, MXU_N)

        def mm_a(kc):
            ks = slice(kc * K_PIECE_A, (kc + 1) * K_PIECE_A)
            part = jnp.dot(u_ref[:, ks], h2t_ref[ks, ps], preferred_element_type=F32)
            if kc == 0:
                a_new[:, ps] = part
            else:
                a_new[:, ps] += part

        def mm_acc(kc, mh):
            ks = slice(kc * K_PIECE, (kc + 1) * K_PIECE)
            ms = slice(mh * EC, (mh + 1) * EC)
            acc_ref[ms, ps] += jnp.dot(vt_ref[ms, ks], p_old[ks, ps],
                                       preferred_element_type=F32)

        a_pieces = [functools.partial(mm_a, kc) for kc in range(D_MODEL // K_PIECE_A)]
        acc_pieces = [functools.partial(mm_acc, kc, mh) for kc in range(EC // K_PIECE)
                      for mh in range(D_MODEL // EC)]
        if stages == "fill":
            for piece in a_pieces:
                piece()
            return
        if stages == "drain":
            for piece in acc_pieces:
                piece()
            return
        mm_pieces = a_pieces + acc_pieces
        blocks = [(half, ip, j0) for half in range(MXU_N // LANES)
                  for ip in range(ROWS_PER_STEP // ROW_BLOCK)
                  for j0 in range(0, n_jg, JG_BLOCK)]
        every = len(blocks) // len(mm_pieces)
        for bi, (half, ip, j0) in enumerate(blocks):
            if bi % every == 0 and bi // every < len(mm_pieces):
                mm_pieces[bi // every]()
            cs = pl.ds(base + half * LANES, LANES)
            if True:
                rows = tuple(range(ROW_BLOCK * ip, ROW_BLOCK * (ip + 1)))
                g = [[zero] * JG_BLOCK for _ in rows]
                for h in range(PEER_HEADS):
                    th = [jnp.broadcast_to(theta_ref[h, 0, r:r + 1, cs], sub) for r in rows]
                    e0 = [jnp.broadcast_to(e0_ref[h, 0, r:r + 1, cs], sub) for r in rows]
                    for jg in range(JG_BLOCK):
                        js = slice((j0 + jg) * sub[0], (j0 + jg + 1) * sub[0])
                        s1 = s1_ref[h, js, cs]
                        e1 = e1_ref[h, js, cs]
                        for q in range(ROW_BLOCK):
                            g[q][jg] = g[q][jg] + jnp.where(s1 >= th[q], e1, zero) * e0[q]
                for q, r in enumerate(rows):
                    for jg in range(0, JG_BLOCK, 2):
                        lo = r * PEER_NKEYS + (j0 + jg) * sub[0]
                        a = a_old[lo:lo + 2 * sub[0], cs]
                        act = a + a * lax.erf(a * _INV_SQRT2)
                        gg = jnp.concatenate([g[q][jg], g[q][jg + 1]], axis=0)
                        p_new[lo:lo + 2 * sub[0], cs] = (act * gg).astype(BF16)

    even = (a0_scr, a1_scr, p1_scr, p0_scr)
    odd = (a1_scr, a0_scr, p0_scr, p1_scr)
    last = pl.num_programs(1) - 1

    def run(refs, stages):
        for k in range(tb // MXU_N):
            pair_body(k, refs, stages)

    @pl.when(c == 0)
    def _():
        run(even, "fill")

    @pl.when(jnp.logical_and(c > 0, c % 2 == 0))
    def _():
        run(even, "all")

    @pl.when(jnp.logical_and(c < last, c % 2 == 1))
    def _():
        run(odd, "all")

    @pl.when(c == last)
    def _():
        run(odd, "drain")
        o_ref[...] = x1_ref[...] + ga2_ref[0] * acc_ref[...].T


def _experts(h2t, u, vt, s1, theta, e0, e1, x1, ga2, seq):
    d, t = h2t.shape
    tb = TB_EXP
    n_chunks = u.shape[0] // EC
    last = n_chunks - 1
    per_batch = seq // tb
    once = dict(pipeline_mode=pl.Buffered(1))
    tab = pl.BlockSpec((PEER_HEADS, PEER_NKEYS, tb), lambda i, c: (0, 0, i), **once)
    rowtab = pl.BlockSpec((PEER_HEADS, 1, ROWS_PER_STEP, tb),
                          lambda i, c: (0, jnp.clip(c - 1, 0, last), 0, i))
    return pl.pallas_call(
        _experts_kernel,
        grid=(t // tb, n_chunks + 2),
        in_specs=[pl.BlockSpec((d, tb), lambda i, c: (0, i)),
                  pl.BlockSpec((EC, d), lambda i, c: (jnp.minimum(c, last), 0)),
                  pl.BlockSpec((d, EC), lambda i, c: (0, jnp.clip(c - 2, 0, last))),
                  tab, rowtab, rowtab, tab,
                  pl.BlockSpec((tb, d), lambda i, c: (i, 0), **once),
                  pl.BlockSpec((1, 1, d), lambda i, c: (i // per_batch, 0, 0))],
        out_specs=pl.BlockSpec((tb, d), lambda i, c: (i, 0)),
        out_shape=jax.ShapeDtypeStruct((t, d), F32),
        scratch_shapes=[pltpu.VMEM((EC, tb), F32), pltpu.VMEM((EC, tb), F32),
                        pltpu.VMEM((EC, tb), BF16), pltpu.VMEM((EC, tb), BF16),
                        pltpu.VMEM((d, tb), F32)],
        compiler_params=_cparams(("arbitrary", "arbitrary")),
        name="experts",
    )(h2t, u, vt, s1, theta, e0, e1, x1, ga2)


def kernel(x, c, w_ada, b_ada, norm1_g, norm2_g, w_in, conv_dw, conv_b, conv_ln_g, conv_ln_b,
           w_conv_out, q_norm_g, k_norm_g, rel_bias, w_attn_out, w_out, peer_wq, peer_keys,
           peer_u, peer_v):
    bsz, s, d = x.shape
    depth = w_ada.shape[0]
    bd = jnp.asarray(np.kron(np.eye(N_HEADS), np.full((HEAD_DIM, HEAD_DIM), 1.0 / HEAD_DIM)), BF16)
    for l in range(depth):
        mod = _ada(c, w_ada[l], b_ada[l])
        sh1, sc1, ga1, sh2, sc2, ga2 = [m.reshape(bsz, 1, d) for m in jnp.split(mod, 6, axis=-1)]
        u, q, k, v, gc, ga = _inproj(
            x, norm1_g[l].reshape(1, d), sc1, sh1, w_in[l].astype(BF16), bd,
            jnp.tile(q_norm_g[l], N_HEADS).reshape(1, D_ATTN),
            jnp.tile(k_norm_g[l], N_HEADS).reshape(1, D_ATTN))
        mc = _conv(u, conv_dw[l], conv_b[l].reshape(1, D_CONV), conv_ln_g[l].reshape(1, D_CONV),
                   conv_ln_b[l].reshape(1, D_CONV), w_conv_out[l].astype(BF16), gc)
        o = _attn(q, k, v, _bias_table(rel_bias[l]))
        x1, h2t = _merge(o, mc, ga, x, w_attn_out[l].astype(BF16), w_out[l].astype(BF16),
                         ga1, norm2_g[l].reshape(1, d), sc2, sh2)
        wqt = peer_wq[l].T.astype(BF16)
        keys = peer_keys[l].reshape(2 * PEER_HEADS, PEER_NKEYS, PEER_DH).astype(BF16)
        s1, theta, e0, e1 = _route(h2t, wqt, keys)
        out = _experts(h2t, peer_u[l].astype(BF16), peer_v[l].T.astype(BF16),
                       s1, theta, e0, e1, x1.reshape(bsz * s, d), ga2, s)
        x = out.reshape(bsz, s, d)
    return x
```

```python
import functools

import jax
import jax.numpy as jnp
import numpy as np
from jax import lax
from jax.experimental import pallas as pl
from jax.experimental.pallas import tpu as pltpu

F32 = jnp.float32
BF16 = jnp.bfloat16

D_MODEL = 1024
CHUNK = 64
N_HEADS = 8
HEAD_DIM = 64
D_ATTN = N_HEADS * HEAD_DIM
LEFT_CHUNKS = 8
REL_CLIP = 128
D_CONV = D_MODEL // 2
CONV_W = 31
PEER_HEADS = 8
PEER_NKEYS = 128
PEER_N = PEER_NKEYS * PEER_NKEYS
PEER_DH = 128
PEER_TOPK = 16
EPS = 1e-6
NEG_INF = -1e30
LOG2E = float(np.log2(np.e))

LANES = 128
SUBLANES = 8
MXU_N = 256
VMEM_LIMIT = 56 * 1024 * 1024

TM_IN = 512
TS_CONV = 512
HALO = 32
TQ = 256
NKB = 3
TM_MERGE = 512
TB_ROUTE = 512
TB_EXP = 1024
EC = 512
ROWS_PER_STEP = EC // PEER_NKEYS

def _cparams(sem):
    return pltpu.CompilerParams(dimension_semantics=sem, vmem_limit_bytes=VMEM_LIMIT)


def _ada_kernel(c_ref, w_ref, b_ref, o_ref):
    c = c_ref[...]
    cond = c * jax.nn.sigmoid(c)
    o_ref[...] = jnp.dot(cond.astype(BF16), w_ref[...].astype(BF16),
                         preferred_element_type=F32) + b_ref[...]


def _ada(c, w, b):
    bsz, d = c.shape
    n = w.shape[1]
    tn = 1024
    return pl.pallas_call(
        _ada_kernel,
        grid=(n // tn,),
        in_specs=[pl.BlockSpec((bsz, d), lambda j: (0, 0)),
                  pl.BlockSpec((d, tn), lambda j: (0, j)),
                  pl.BlockSpec((1, tn), lambda j: (0, j))],
        out_specs=pl.BlockSpec((bsz, tn), lambda j: (0, j)),
        out_shape=jax.ShapeDtypeStruct((bsz, n), F32),
        compiler_params=_cparams(("arbitrary",)),
        name="ada",
    )(c, w, b.reshape(1, n))


def _head_rms(t, bd_ref, gain):
    t2 = t * t
    hi = t2.astype(BF16)
    lo = (t2 - hi.astype(F32)).astype(BF16)
    ms = (jnp.dot(hi, bd_ref[...], preferred_element_type=F32)
          + jnp.dot(lo, bd_ref[...], preferred_element_type=F32))
    return t * lax.rsqrt(ms + EPS) * gain


def _inproj_kernel(x_ref, g_ref, sc_ref, sh_ref, w_ref, bd_ref, qg_ref, kg_ref, pu_ref, pv_ref,
                   u_ref, q_ref, k_ref, v_ref, gc_ref, ga_ref, pub_ref, pvt_ref):
    x = x_ref[0]
    ms = jnp.mean(x * x, axis=-1, keepdims=True)
    h = (x * lax.rsqrt(ms + EPS)) * g_ref[...]
    h = h * (1.0 + sc_ref[0]) + sh_ref[0]
    hb = h.astype(BF16)

    def seg(lo, hi):
        return jnp.dot(hb, w_ref[:, lo:hi], preferred_element_type=F32)

    bounds = np.cumsum([0, D_CONV, D_CONV, D_ATTN, D_ATTN, D_ATTN, D_MODEL, D_MODEL])
    lo, hi = bounds[:-1], bounds[1:]
    a = seg(lo[0], hi[0])
    b = seg(lo[1], hi[1])
    q = seg(lo[2], hi[2])
    u_ref[0] = a * jax.nn.sigmoid(b)
    k = seg(lo[3], hi[3])
    q_ref[0] = (_head_rms(q, bd_ref, qg_ref[...]) * (HEAD_DIM ** -0.5 * LOG2E)).astype(BF16)
    v = seg(lo[4], hi[4])
    k_ref[0] = _head_rms(k, bd_ref, kg_ref[...]).astype(BF16)
    gc = seg(lo[5], hi[5])
    v_ref[0] = v.astype(BF16)
    pub_ref[...] = pu_ref[...].astype(BF16)
    ga = seg(lo[6], hi[6])
    pvt_ref[...] = pv_ref[...].T.astype(BF16)
    gc_ref[0] = jax.nn.sigmoid(gc)
    ga_ref[0] = jax.nn.sigmoid(ga)


def _inproj(x, g1, sc1, sh1, w_in, bd, qg, kg, pu, pv):
    bsz, s, d = x.shape
    tm = TM_IN
    n_in = w_in.shape[1]
    nb = s // tm
    n_e = pu.shape[0]
    er = n_e // (bsz * nb)
    row = lambda w: pl.BlockSpec((1, tm, w), lambda b, i: (b, i, 0))
    vec = lambda w: pl.BlockSpec((1, w), lambda b, i: (0, 0))
    mod = pl.BlockSpec((1, 1, d), lambda b, i: (b, 0, 0))
    erow = pl.BlockSpec((er, d), lambda b, i: (b * nb + i, 0))
    return pl.pallas_call(
        _inproj_kernel,
        grid=(bsz, nb),
        in_specs=[row(d), vec(d), mod, mod,
                  pl.BlockSpec((d, n_in), lambda b, i: (0, 0), pipeline_mode=pl.Buffered(1)),
                  pl.BlockSpec((D_ATTN, D_ATTN), lambda b, i: (0, 0)),
                  vec(D_ATTN), vec(D_ATTN), erow, erow],
        out_specs=[row(D_CONV), row(D_ATTN), row(D_ATTN), row(D_ATTN), row(d), row(d),
                   erow, pl.BlockSpec((d, er), lambda b, i: (0, b * nb + i))],
        out_shape=[jax.ShapeDtypeStruct((bsz, s, D_CONV), F32),
                   jax.ShapeDtypeStruct((bsz, s, D_ATTN), BF16),
                   jax.ShapeDtypeStruct((bsz, s, D_ATTN), BF16),
                   jax.ShapeDtypeStruct((bsz, s, D_ATTN), BF16),
                   jax.ShapeDtypeStruct((bsz, s, d), F32),
                   jax.ShapeDtypeStruct((bsz, s, d), F32),
                   jax.ShapeDtypeStruct((n_e, d), BF16),
                   jax.ShapeDtypeStruct((d, n_e), BF16)],
        compiler_params=_cparams(("arbitrary", "arbitrary")),
        name="inproj",
    )(x, g1, sc1, sh1, w_in, bd, qg, kg, pu, pv)


CONV_ROWS = 64


def _conv_kernel(u_ref, up_ref, dw_ref, cb_ref, lg_ref, lb_ref, w_ref, gc_ref,
                 o_ref, ext_ref, y_ref):
    i = pl.program_id(1)
    ts = u_ref.shape[1]
    prev = up_ref[0, ts - HALO:, :]
    ext_ref[0, 0:HALO, :] = jnp.where(i > 0, prev, 0.0)
    ext_ref[0, HALO:, :] = u_ref[0]
    n_sh = ts + HALO - SUBLANES
    for s in range(1, SUBLANES):
        for r0 in range(0, n_sh, CONV_ROWS):
            n = min(CONV_ROWS, n_sh - r0)
            ext_ref[s, r0:r0 + n, :] = ext_ref[0, r0 + s:r0 + s + n, :]
    base = HALO - (CONV_W - 1)
    for r0 in range(0, ts, CONV_ROWS):
        acc = jnp.zeros((CONV_ROWS, D_CONV), F32) + cb_ref[...]
        for w in range(CONV_W):
            s = (base + w) % SUBLANES
            a = r0 + base + w - s
            acc = acc + ext_ref[s, a:a + CONV_ROWS, :] * dw_ref[w:w + 1, :]
        y_ref[r0:r0 + CONV_ROWS, :] = acc
    y = y_ref[...]
    mu = jnp.mean(y, axis=-1, keepdims=True)
    yc = y - mu
    var = jnp.mean(yc * yc, axis=-1, keepdims=True)
    z = yc * lax.rsqrt(var + EPS) * lg_ref[...] + lb_ref[...]
    z = z * jax.nn.sigmoid(z)
    o = jnp.dot(z.astype(BF16), w_ref[...], preferred_element_type=F32)
    o_ref[0] = gc_ref[0] * o


def _conv(u, dw, cb, lg, lb, w_co, gc):
    bsz, s, dc = u.shape
    d = w_co.shape[1]
    ts = TS_CONV
    vec = lambda w: pl.BlockSpec((1, w), lambda b, i: (0, 0))
    return pl.pallas_call(
        _conv_kernel,
        grid=(bsz, s // ts),
        in_specs=[pl.BlockSpec((1, ts, dc), lambda b, i: (b, i, 0)),
                  pl.BlockSpec((1, ts, dc), lambda b, i: (b, jnp.maximum(i - 1, 0), 0)),
                  pl.BlockSpec((CONV_W, dc), lambda b, i: (0, 0)),
                  vec(dc), vec(dc), vec(dc),
                  pl.BlockSpec((dc, d), lambda b, i: (0, 0)),
                  pl.BlockSpec((1, ts, d), lambda b, i: (b, i, 0))],
        out_specs=pl.BlockSpec((1, ts, d), lambda b, i: (b, i, 0)),
        out_shape=jax.ShapeDtypeStruct((bsz, s, d), F32),
        scratch_shapes=[pltpu.VMEM((SUBLANES, ts + HALO, dc), F32), pltpu.VMEM((ts, dc), F32)],
        compiler_params=_cparams(("arbitrary", "arbitrary")),
        name="conv",
    )(u, u, dw, cb, lg, lb, w_co, gc)


def _attn_kernel(q_ref, k0_ref, k1_ref, k2_ref, v0_ref, v1_ref, v2_ref, bias_ref, o_ref):
    i = pl.program_id(1)
    k_refs = (k0_ref, k1_ref, k2_ref)
    v_refs = (v0_ref, v1_ref, v2_ref)
    low = lax.broadcasted_iota(jnp.int32, (TQ, LANES), 1) < HEAD_DIM

    n_heads_per_slab = LANES // HEAD_DIM

    def body(pens):
        def scores(h):
            lo = (h // n_heads_per_slab) * LANES
            q2 = q_ref[0, :, lo:lo + LANES]
            keep = low if h % n_heads_per_slab == 0 else jnp.logical_not(low)
            qh = jnp.where(keep, q2, jnp.zeros_like(q2))
            ss = []
            for j in range(NKB):
                s = lax.dot_general(qh, k_refs[j][0, :, lo:lo + LANES],
                                    (((1,), (1,)), ((), ())), preferred_element_type=F32)
                s = s + bias_ref[h, :, j * TQ:(j + 1) * TQ]
                ss.append(s if pens is None else s + pens[j])
            return ss

        def softmax_pv(h, ss):
            lo = (h // n_heads_per_slab) * LANES
            m = jnp.maximum(jnp.maximum(jnp.max(ss[0], axis=-1, keepdims=True),
                                        jnp.max(ss[1], axis=-1, keepdims=True)),
                            jnp.max(ss[2], axis=-1, keepdims=True))
            ps = [jnp.exp2(ss[j] - m) for j in range(NKB)]
            l = jnp.sum(ps[0] + ps[1] + ps[2], axis=-1, keepdims=True)
            p = jnp.concatenate([pj.astype(BF16) for pj in ps], axis=1)
            v = jnp.concatenate([v_refs[j][0, :, lo:lo + LANES] for j in range(NKB)], axis=0)
            return jnp.dot(p, v, preferred_element_type=F32) / l

        outs = []
        ss_next = scores(0)
        for h in range(N_HEADS):
            ss = ss_next
            if h + 1 < N_HEADS:
                ss_next = scores(h + 1)
            outs.append(softmax_pv(h, ss))
            if h % n_heads_per_slab == n_heads_per_slab - 1:
                lo = (h // n_heads_per_slab) * LANES
                o_ref[0, :, lo:lo + LANES] = jnp.where(low, outs[0], outs[1]).astype(BF16)
                outs = []

    @pl.when(i >= NKB - 1)
    def _():
        body(None)

    @pl.when(i < NKB - 1)
    def _():
        body([jnp.where(i - (NKB - 1) + j >= 0, 0.0, NEG_INF).astype(F32) for j in range(NKB)])


def _attn(q, k, v, bias):
    bsz, s, da = q.shape
    kspec = lambda j: pl.BlockSpec(
        (1, TQ, da), lambda b, i: (b, jnp.maximum(i - (NKB - 1) + j, 0), 0))
    return pl.pallas_call(
        _attn_kernel,
        grid=(bsz, s // TQ),
        in_specs=[pl.BlockSpec((1, TQ, da), lambda b, i: (b, i, 0)),
                  kspec(0), kspec(1), kspec(2), kspec(0), kspec(1), kspec(2),
                  pl.BlockSpec((N_HEADS, TQ, NKB * TQ), lambda b, i: (0, 0, 0))],
        out_specs=pl.BlockSpec((1, TQ, da), lambda b, i: (b, i, 0)),
        out_shape=jax.ShapeDtypeStruct((bsz, s, da), BF16),
        compiler_params=_cparams(("arbitrary", "arbitrary")),
        name="attn",
    )(q, k, k, k, v, v, v, bias)


def _bias_table(rel_bias):
    nk = NKB * TQ
    lw = TQ + nk - 1
    n_lo = (TQ - 1) - REL_CLIP
    n_hi = (nk - 1) - REL_CLIP
    w = jnp.concatenate([jnp.repeat(rel_bias[:, :1], n_lo, axis=1), rel_bias,
                         jnp.repeat(rel_bias[:, -1:], n_hi, axis=1)], axis=1).astype(F32)
    row = 1024
    assert row + 1 >= lw
    wp = jnp.concatenate([w[:, nk - 1::-1], jnp.zeros((w.shape[0], row + 1 - lw), F32),
                          w[:, lw - 1:nk - 1:-1]], axis=1)
    flat = jnp.tile(wp, (1, TQ))[:, :TQ * row]
    tab = flat.reshape(-1, TQ, row)[:, :, :nk]
    qi = np.arange(TQ)[:, None]
    kj = np.arange(nk)[None, :]
    qc = qi // CHUNK + (NKB - 1) * TQ // CHUNK
    kc = kj // CHUNK
    band = (kc >= qc - LEFT_CHUNKS) & (kc <= qc)
    return jnp.where(band[None], tab * LOG2E, NEG_INF)


def _merge_kernel(o_ref, mc_ref, ga_ref, x_ref, wa_ref, wo_ref, ga1_ref, g2_ref, sc_ref, sh_ref,
                  x1_ref, h2t_ref):
    ya = jnp.dot(o_ref[0], wa_ref[...], preferred_element_type=F32)
    merged = mc_ref[0] + ga_ref[0] * ya
    y = jnp.dot(merged.astype(BF16), wo_ref[...], preferred_element_type=F32)
    x1 = x_ref[0] + ga1_ref[0] * y
    x1_ref[0] = x1
    ms = jnp.mean(x1 * x1, axis=-1, keepdims=True)
    h2 = (x1 * lax.rsqrt(ms + EPS)) * g2_ref[...]
    h2 = h2 * (1.0 + sc_ref[0]) + sh_ref[0]
    h2t_ref[...] = h2.T.astype(BF16)


def _merge(o, mc, ga, x, wa, wo, ga1, g2, sc2, sh2):
    bsz, s, d = x.shape
    tm = TM_MERGE
    nb = s // tm
    row = lambda w: pl.BlockSpec((1, tm, w), lambda b, i: (b, i, 0))
    mod = pl.BlockSpec((1, 1, d), lambda b, i: (b, 0, 0))
    return pl.pallas_call(
        _merge_kernel,
        grid=(bsz, nb),
        in_specs=[row(D_ATTN), row(d), row(d), row(d),
                  pl.BlockSpec((D_ATTN, d), lambda b, i: (0, 0)),
                  pl.BlockSpec((d, d), lambda b, i: (0, 0)),
                  mod, pl.BlockSpec((1, d), lambda b, i: (0, 0)), mod, mod],
        out_specs=[row(d), pl.BlockSpec((d, tm), lambda b, i: (0, b * nb + i))],
        out_shape=[jax.ShapeDtypeStruct((bsz, s, d), F32),
                   jax.ShapeDtypeStruct((d, bsz * s), BF16)],
        compiler_params=_cparams(("arbitrary", "arbitrary")),
        name="merge",
    )(o, mc, ga, x, wa, wo, ga1, g2, sc2, sh2)


_CAND = [(a, b) for a in range(PEER_TOPK) for b in range(PEER_TOPK)
         if (a + 1) * (b + 1) <= PEER_TOPK]


def _sort16_network():
    n, pairs, p = PEER_TOPK, [], 1
    while p < n:
        k = p
        while k >= 1:
            for j in range(k % p, n - k, 2 * k):
                for i in range(min(k, n - j - k)):
                    if (i + j) // (2 * p) == (i + j + k) // (2 * p):
                        pairs.append((i + j, i + j + k))
            k //= 2
        p *= 2
    return pairs


_SORT16 = _sort16_network()


def _top16_sorted(problems, store_row):
    sub = 8
    lvs = [[s[g * sub:(g + 1) * sub, :] for g in range(PEER_NKEYS // sub)] for s in problems]
    for a, b in _SORT16:
        for lv in lvs:
            lv[a], lv[b] = jnp.maximum(lv[a], lv[b]), jnp.minimum(lv[a], lv[b])
    for r in range(PEER_TOPK):
        for p, lv in enumerate(lvs):
            m = jnp.max(lv[0], axis=0, keepdims=True)
            store_row(p, r, m)
            eq = lv[0] == m
            for k in range(PEER_TOPK - 1 - r):
                lv[k] = jnp.where(eq, lv[k + 1], lv[k])


def _route_kernel(h2t_ref, wqt_ref, keys_ref, s1_ref, theta_ref, e0_ref, e1_ref,
                  s_scr, v_scr, tau_scr, zi_scr):
    tb = h2t_ref.shape[1]
    ncol = tb // LANES
    n_hp = 2 * PEER_HEADS
    def scores(h):
        rows = slice(2 * h * PEER_DH, 2 * (h + 1) * PEER_DH)
        qt = jnp.dot(wqt_ref[rows, :], h2t_ref[...], preferred_element_type=F32).astype(BF16)
        for p in range(2):
            s_scr[2 * h + p] = jnp.dot(keys_ref[2 * h + p], qt[p * PEER_DH:(p + 1) * PEER_DH, :],
                                       preferred_element_type=F32)

    def top16(h):
        for col in range(ncol):
            cs = slice(col * LANES, (col + 1) * LANES)
            rw = slice(h * ncol + col, h * ncol + col + 1)

            def store_row(p, r, m, rw=rw):
                v_scr[p, r, rw, :] = m

            _top16_sorted([s_scr[2 * h + p, :, cs] for p in range(2)], store_row)

    scores(0)
    for h in range(PEER_HEADS):
        if h + 1 < PEER_HEADS:
            scores(h + 1)
        top16(h)

    v0 = [v_scr[0, a] for a in range(PEER_TOPK)]
    v1 = [v_scr[1, b] for b in range(PEER_TOPK)]
    cand = [v0[a] + v1[b] for (a, b) in _CAND]
    top = cand[0]
    work = list(cand)
    for r in range(PEER_TOPK):
        c16 = functools.reduce(jnp.maximum, work)
        work = [jnp.where(w == c16, -jnp.inf, w) for w in work]
    c17 = functools.reduce(jnp.maximum, work)
    z = jnp.zeros_like(top)
    for c in cand:
        z = z + jnp.where(c >= c16, jnp.exp(c - top), 0.0)
    tau_scr[...] = 0.5 * (c16 + c17)
    zi_scr[...] = 0.5 / z

    def stage3(it, carry):
        h = it // ncol
        col = it % ncol
        cs = pl.ds(pl.multiple_of(col * LANES, LANES), LANES)
        rw = pl.ds(h * ncol + col, 1)
        s0 = s_scr[2 * h, :, cs]
        s1 = s_scr[2 * h + 1, :, cs]
        theta = jnp.where(s0 >= v_scr[0, PEER_TOPK - 1, rw, :], tau_scr[rw, :] - s0, jnp.inf)
        e0 = jnp.exp(s0 - v_scr[0, 0, rw, :])
        for g in range(PEER_NKEYS // ROWS_PER_STEP):
            rows = slice(g * ROWS_PER_STEP, (g + 1) * ROWS_PER_STEP)
            theta_ref[h, g, :, cs] = theta[rows]
            e0_ref[h, g, :, cs] = e0[rows]
        s1_ref[h, :, cs] = jnp.where(s1 >= v_scr[1, PEER_TOPK - 1, rw, :], s1, -jnp.inf)
        e1_ref[h, :, cs] = jnp.exp(s1 - v_scr[1, 0, rw, :]) * zi_scr[rw, :]
        return carry

    lax.fori_loop(0, PEER_HEADS * ncol, stage3, 0)


def _route(h2t, wqt, keys):
    d, t = h2t.shape
    tb = TB_ROUTE
    ncol = tb // LANES
    n_hp = 2 * PEER_HEADS
    n_grp = PEER_NKEYS // ROWS_PER_STEP
    tab = pl.BlockSpec((PEER_HEADS, PEER_NKEYS, tb), lambda i: (0, 0, i))
    rowtab = pl.BlockSpec((PEER_HEADS, n_grp, ROWS_PER_STEP, tb), lambda i: (0, 0, 0, i))
    tab_shape = jax.ShapeDtypeStruct((PEER_HEADS, PEER_NKEYS, t), F32)
    rowtab_shape = jax.ShapeDtypeStruct((PEER_HEADS, n_grp, ROWS_PER_STEP, t), F32)
    return pl.pallas_call(
        _route_kernel,
        grid=(t // tb,),
        in_specs=[pl.BlockSpec((d, tb), lambda i: (0, i)),
                  pl.BlockSpec(wqt.shape, lambda i: (0, 0)),
                  pl.BlockSpec(keys.shape, lambda i: (0, 0, 0))],
        out_specs=[tab, rowtab, rowtab, tab],
        out_shape=[tab_shape, rowtab_shape, rowtab_shape, tab_shape],
        scratch_shapes=[pltpu.VMEM((n_hp, PEER_NKEYS, tb), F32),
                        pltpu.VMEM((2, PEER_TOPK, PEER_HEADS * ncol, LANES), F32),
                        pltpu.VMEM((PEER_HEADS * ncol, LANES), F32),
                        pltpu.VMEM((PEER_HEADS * ncol, LANES), F32)],
        compiler_params=_cparams(("arbitrary",)),
        name="route",
    )(h2t, wqt, keys)


_INV_SQRT2 = float(1.0 / np.sqrt(2.0))
K_PIECE = 256
K_PIECE_A = 256
ROW_BLOCK = 4
JG_BLOCK = 4


def _experts_kernel(h2t_ref, u_ref, vt_ref, s1_ref, theta_ref, e0_ref, e1_ref,
                    x1_ref, ga2_ref, o_ref, a0_scr, a1_scr, p0_scr, p1_scr, acc_ref):
    c = pl.program_id(1)
    tb = h2t_ref.shape[1]

    @pl.when(c == 0)
    def _():
        acc_ref[...] = jnp.zeros_like(acc_ref)
        p1_scr[...] = jnp.zeros_like(p1_scr)

    sub = (8, LANES)
    n_jg = PEER_NKEYS // sub[0]
    zero = jnp.zeros(sub, F32)

    def pair_body(k, refs, stages):
        a_new, a_old, p_new, p_old = refs
        base = k * MXU_N
        ps = pl.ds(base, MXU_N)

        def mm_a(kc):
            ks = slice(kc * K_PIECE_A, (kc + 1) * K_PIECE_A)
            part = jnp.dot(u_ref[:, ks], h2t_ref[ks, ps], preferred_element_type=F32)
            if kc == 0:
                a_new[:, ps] = part
            else:
                a_new[:, ps] += part

        def mm_acc(kc, mh):
            ks = slice(kc * K_PIECE, (kc + 1) * K_PIECE)
            ms = slice(mh * EC, (mh + 1) * EC)
            acc_ref[ms, ps] += jnp.dot(vt_ref[ms, ks], p_old[ks, ps],
                                       preferred_element_type=F32)

        a_pieces = [functools.partial(mm_a, kc) for kc in range(D_MODEL // K_PIECE_A)]
        acc_pieces = [functools.partial(mm_acc, kc, mh) for kc in range(EC // K_PIECE)
                      for mh in range(D_MODEL // EC)]
        if stages == "fill":
            for piece in a_pieces:
                piece()
            return
        if stages == "drain":
            for piece in acc_pieces:
                piece()
            return
        mm_pieces = a_pieces + acc_pieces
        blocks = [(half, ip, j0) for half in range(MXU_N // LANES)
                  for ip in range(ROWS_PER_STEP // ROW_BLOCK)
                  for j0 in range(0, n_jg, JG_BLOCK)]
        every = len(blocks) // len(mm_pieces)
        for bi, (half, ip, j0) in enumerate(blocks):
            if bi % every == 0 and bi // every < len(mm_pieces):
                mm_pieces[bi // every]()
            cs = pl.ds(base + half * LANES, LANES)
            if True:
                rows = tuple(range(ROW_BLOCK * ip, ROW_BLOCK * (ip + 1)))
                g = [[zero] * JG_BLOCK for _ in rows]
                for h in range(PEER_HEADS):
                    th = [jnp.broadcast_to(theta_ref[h, 0, r:r + 1, cs], sub) for r in rows]
                    e0 = [jnp.broadcast_to(e0_ref[h, 0, r:r + 1, cs], sub) for r in rows]
                    for jg in range(JG_BLOCK):
                        js = slice((j0 + jg) * sub[0], (j0 + jg + 1) * sub[0])
                        s1 = s1_ref[h, js, cs]
                        e1 = e1_ref[h, js, cs]
                        for q in range(ROW_BLOCK):
                            g[q][jg] = g[q][jg] + jnp.where(s1 >= th[q], e1, zero) * e0[q]
                for q, r in enumerate(rows):
                    for jg in range(0, JG_BLOCK, 2):
                        lo = r * PEER_NKEYS + (j0 + jg) * sub[0]
                        a = a_old[lo:lo + 2 * sub[0], cs]
                        act = a + a * lax.erf(a * _INV_SQRT2)
                        gg = jnp.concatenate([g[q][jg], g[q][jg + 1]], axis=0)
                        p_new[lo:lo + 2 * sub[0], cs] = (act * gg).astype(BF16)

    even = (a0_scr, a1_scr, p1_scr, p0_scr)
    odd = (a1_scr, a0_scr, p0_scr, p1_scr)
    last = pl.num_programs(1) - 1

    def run(refs, stages):
        for k in range(tb // MXU_N):
            pair_body(k, refs, stages)

    @pl.when(c == 0)
    def _():
        run(even, "fill")

    @pl.when(jnp.logical_and(c > 0, c % 2 == 0))
    def _():
        run(even, "all")

    @pl.when(jnp.logical_and(c < last, c % 2 == 1))
    def _():
        run(odd, "all")

    @pl.when(c == last)
    def _():
        run(odd, "drain")
        o_ref[...] = x1_ref[...] + ga2_ref[0] * acc_ref[...].T


def _experts(h2t, u, vt, s1, theta, e0, e1, x1, ga2, seq):
    d, t = h2t.shape
    tb = TB_EXP
    n_chunks = u.shape[0] // EC
    last = n_chunks - 1
    per_batch = seq // tb
    once = dict(pipeline_mode=pl.Buffered(1))
    tab = pl.BlockSpec((PEER_HEADS, PEER_NKEYS, tb), lambda i, c: (0, 0, i), **once)
    rowtab = pl.BlockSpec((PEER_HEADS, 1, ROWS_PER_STEP, tb),
                          lambda i, c: (0, jnp.clip(c - 1, 0, last), 0, i))
    return pl.pallas_call(
        _experts_kernel,
        grid=(t // tb, n_chunks + 2),
        in_specs=[pl.BlockSpec((d, tb), lambda i, c: (0, i)),
                  pl.BlockSpec((EC, d), lambda i, c: (jnp.minimum(c, last), 0)),
                  pl.BlockSpec((d, EC), lambda i, c: (0, jnp.clip(c - 2, 0, last))),
                  tab, rowtab, rowtab, tab,
                  pl.BlockSpec((tb, d), lambda i, c: (i, 0), **once),
                  pl.BlockSpec((1, 1, d), lambda i, c: (i // per_batch, 0, 0))],
        out_specs=pl.BlockSpec((tb, d), lambda i, c: (i, 0)),
        out_shape=jax.ShapeDtypeStruct((t, d), F32),
        scratch_shapes=[pltpu.VMEM((EC, tb), F32), pltpu.VMEM((EC, tb), F32),
                        pltpu.VMEM((EC, tb), BF16), pltpu.VMEM((EC, tb), BF16),
                        pltpu.VMEM((d, tb), F32)],
        compiler_params=_cparams(("arbitrary", "arbitrary")),
        name="experts",
    )(h2t, u, vt, s1, theta, e0, e1, x1, ga2)


def kernel(x, c, w_ada, b_ada, norm1_g, norm2_g, w_in, conv_dw, conv_b, conv_ln_g, conv_ln_b,
           w_conv_out, q_norm_g, k_norm_g, rel_bias, w_attn_out, w_out, peer_wq, peer_keys,
           peer_u, peer_v):
    bsz, s, d = x.shape
    depth = w_ada.shape[0]
    bd = jnp.asarray(np.kron(np.eye(N_HEADS), np.full((HEAD_DIM, HEAD_DIM), 1.0 / HEAD_DIM)), BF16)
    for l in range(depth):
        mod = _ada(c, w_ada[l], b_ada[l])
        sh1, sc1, ga1, sh2, sc2, ga2 = [m.reshape(bsz, 1, d) for m in jnp.split(mod, 6, axis=-1)]
        u, q, k, v, gc, ga, pu_bf, pvt_bf = _inproj(
            x, norm1_g[l].reshape(1, d), sc1, sh1, w_in[l].astype(BF16), bd,
            jnp.tile(q_norm_g[l], N_HEADS).reshape(1, D_ATTN),
            jnp.tile(k_norm_g[l], N_HEADS).reshape(1, D_ATTN), peer_u[l], peer_v[l])
        mc = _conv(u, conv_dw[l], conv_b[l].reshape(1, D_CONV), conv_ln_g[l].reshape(1, D_CONV),
                   conv_ln_b[l].reshape(1, D_CONV), w_conv_out[l].astype(BF16), gc)
        o = _attn(q, k, v, _bias_table(rel_bias[l]))
        x1, h2t = _merge(o, mc, ga, x, w_attn_out[l].astype(BF16), w_out[l].astype(BF16),
                         ga1, norm2_g[l].reshape(1, d), sc2, sh2)
        wqt = peer_wq[l].T.astype(BF16)
        keys = peer_keys[l].reshape(2 * PEER_HEADS, PEER_NKEYS, PEER_DH).astype(BF16)
        s1, theta, e0, e1 = _route(h2t, wqt, keys)
        out = _experts(h2t, pu_bf, pvt_bf, s1, theta, e0, e1, x1.reshape(bsz * s, d), ga2, s)
        x = out.reshape(bsz, s, d)
    return x
```

```python
import functools

import jax
import jax.numpy as jnp
import numpy as np
from jax import lax
from jax.experimental import pallas as pl
from jax.experimental.pallas import tpu as pltpu

F32 = jnp.float32
BF16 = jnp.bfloat16

D_MODEL = 1024
CHUNK = 64
N_HEADS = 8
HEAD_DIM = 64
D_ATTN = N_HEADS * HEAD_DIM
LEFT_CHUNKS = 8
REL_CLIP = 128
D_CONV = D_MODEL // 2
CONV_W = 31
PEER_HEADS = 8
PEER_NKEYS = 128
PEER_N = PEER_NKEYS * PEER_NKEYS
PEER_DH = 128
PEER_TOPK = 16
EPS = 1e-6
NEG_INF = -1e30
LOG2E = float(np.log2(np.e))

LANES = 128
SUBLANES = 8
MXU_N = 256
VMEM_LIMIT = 56 * 1024 * 1024

TM_IN = 512
TS_CONV = 512
HALO = 32
TQ = 256
NKB = 3
TM_MERGE = 512
TB_ROUTE = 512
TB_EXP = 1024
EC = 512
ROWS_PER_STEP = EC // PEER_NKEYS

def _cparams(sem):
    return pltpu.CompilerParams(dimension_semantics=sem, vmem_limit_bytes=VMEM_LIMIT)


def _ada_kernel(c_ref, w_ref, b_ref, o_ref):
    c = c_ref[...]
    cond = c * jax.nn.sigmoid(c)
    o_ref[...] = jnp.dot(cond.astype(BF16), w_ref[...].astype(BF16),
                         preferred_element_type=F32) + b_ref[...]


def _ada(c, w, b):
    bsz, d = c.shape
    n = w.shape[1]
    tn = 1024
    return pl.pallas_call(
        _ada_kernel,
        grid=(n // tn,),
        in_specs=[pl.BlockSpec((bsz, d), lambda j: (0, 0)),
                  pl.BlockSpec((d, tn), lambda j: (0, j)),
                  pl.BlockSpec((1, tn), lambda j: (0, j))],
        out_specs=pl.BlockSpec((bsz, tn), lambda j: (0, j)),
        out_shape=jax.ShapeDtypeStruct((bsz, n), F32),
        compiler_params=_cparams(("arbitrary",)),
        name="ada",
    )(c, w, b.reshape(1, n))


def _head_rms(t, bd_ref, gain):
    t2 = t * t
    hi = t2.astype(BF16)
    lo = (t2 - hi.astype(F32)).astype(BF16)
    ms = (jnp.dot(hi, bd_ref[...], preferred_element_type=F32)
          + jnp.dot(lo, bd_ref[...], preferred_element_type=F32))
    return t * lax.rsqrt(ms + EPS) * gain


def _inproj_kernel(x_ref, g_ref, sc_ref, sh_ref, w_ref, bd_ref, qg_ref, kg_ref, pu_ref, pv_ref,
                   u_ref, q_ref, k_ref, v_ref, gc_ref, ga_ref, pub_ref, pvt_ref):
    x = x_ref[0]
    ms = jnp.mean(x * x, axis=-1, keepdims=True)
    h = (x * lax.rsqrt(ms + EPS)) * g_ref[...]
    h = h * (1.0 + sc_ref[0]) + sh_ref[0]
    hb = h.astype(BF16)

    def seg(lo, hi):
        return jnp.dot(hb, w_ref[:, lo:hi], preferred_element_type=F32)

    bounds = np.cumsum([0, D_CONV, D_CONV, D_ATTN, D_ATTN, D_ATTN, D_MODEL, D_MODEL])
    lo, hi = bounds[:-1], bounds[1:]
    a = seg(lo[0], hi[0])
    b = seg(lo[1], hi[1])
    q = seg(lo[2], hi[2])
    u_ref[0] = a * jax.nn.sigmoid(b)
    k = seg(lo[3], hi[3])
    q_ref[0] = (_head_rms(q, bd_ref, qg_ref[...]) * (HEAD_DIM ** -0.5 * LOG2E)).astype(BF16)
    v = seg(lo[4], hi[4])
    k_ref[0] = _head_rms(k, bd_ref, kg_ref[...]).astype(BF16)
    gc = seg(lo[5], hi[5])
    v_ref[0] = v.astype(BF16)
    pub_ref[...] = (pu_ref[...] * _INV_SQRT2).astype(BF16)
    ga = seg(lo[6], hi[6])
    pvt_ref[...] = pv_ref[...].T.astype(BF16)
    gc_ref[0] = jax.nn.sigmoid(gc)
    ga_ref[0] = jax.nn.sigmoid(ga)


def _inproj(x, g1, sc1, sh1, w_in, bd, qg, kg, pu, pv):
    bsz, s, d = x.shape
    tm = TM_IN
    n_in = w_in.shape[1]
    nb = s // tm
    n_e = pu.shape[0]
    er = n_e // (bsz * nb)
    row = lambda w: pl.BlockSpec((1, tm, w), lambda b, i: (b, i, 0))
    vec = lambda w: pl.BlockSpec((1, w), lambda b, i: (0, 0))
    mod = pl.BlockSpec((1, 1, d), lambda b, i: (b, 0, 0))
    erow = pl.BlockSpec((er, d), lambda b, i: (b * nb + i, 0))
    return pl.pallas_call(
        _inproj_kernel,
        grid=(bsz, nb),
        in_specs=[row(d), vec(d), mod, mod,
                  pl.BlockSpec((d, n_in), lambda b, i: (0, 0), pipeline_mode=pl.Buffered(1)),
                  pl.BlockSpec((D_ATTN, D_ATTN), lambda b, i: (0, 0)),
                  vec(D_ATTN), vec(D_ATTN), erow, erow],
        out_specs=[row(D_CONV), row(D_ATTN), row(D_ATTN), row(D_ATTN), row(d), row(d),
                   erow, pl.BlockSpec((d, er), lambda b, i: (0, b * nb + i))],
        out_shape=[jax.ShapeDtypeStruct((bsz, s, D_CONV), F32),
                   jax.ShapeDtypeStruct((bsz, s, D_ATTN), BF16),
                   jax.ShapeDtypeStruct((bsz, s, D_ATTN), BF16),
                   jax.ShapeDtypeStruct((bsz, s, D_ATTN), BF16),
                   jax.ShapeDtypeStruct((bsz, s, d), F32),
                   jax.ShapeDtypeStruct((bsz, s, d), F32),
                   jax.ShapeDtypeStruct((n_e, d), BF16),
                   jax.ShapeDtypeStruct((d, n_e), BF16)],
        compiler_params=_cparams(("arbitrary", "arbitrary")),
        name="inproj",
    )(x, g1, sc1, sh1, w_in, bd, qg, kg, pu, pv)


CONV_ROWS = 64


def _conv_kernel(u_ref, up_ref, dw_ref, cb_ref, lg_ref, lb_ref, w_ref, gc_ref,
                 o_ref, ext_ref, y_ref):
    i = pl.program_id(1)
    ts = u_ref.shape[1]
    prev = up_ref[0, ts - HALO:, :]
    ext_ref[0, 0:HALO, :] = jnp.where(i > 0, prev, 0.0)
    ext_ref[0, HALO:, :] = u_ref[0]
    n_sh = ts + HALO - SUBLANES
    for s in range(1, SUBLANES):
        for r0 in range(0, n_sh, CONV_ROWS):
            n = min(CONV_ROWS, n_sh - r0)
            ext_ref[s, r0:r0 + n, :] = ext_ref[0, r0 + s:r0 + s + n, :]
    base = HALO - (CONV_W - 1)
    for r0 in range(0, ts, CONV_ROWS):
        acc = jnp.zeros((CONV_ROWS, D_CONV), F32) + cb_ref[...]
        for w in range(CONV_W):
            s = (base + w) % SUBLANES
            a = r0 + base + w - s
            acc = acc + ext_ref[s, a:a + CONV_ROWS, :] * dw_ref[w:w + 1, :]
        y_ref[r0:r0 + CONV_ROWS, :] = acc
    y = y_ref[...]
    mu = jnp.mean(y, axis=-1, keepdims=True)
    yc = y - mu
    var = jnp.mean(yc * yc, axis=-1, keepdims=True)
    z = yc * lax.rsqrt(var + EPS) * lg_ref[...] + lb_ref[...]
    z = z * jax.nn.sigmoid(z)
    o = jnp.dot(z.astype(BF16), w_ref[...], preferred_element_type=F32)
    o_ref[0] = gc_ref[0] * o


def _conv(u, dw, cb, lg, lb, w_co, gc):
    bsz, s, dc = u.shape
    d = w_co.shape[1]
    ts = TS_CONV
    vec = lambda w: pl.BlockSpec((1, w), lambda b, i: (0, 0))
    return pl.pallas_call(
        _conv_kernel,
        grid=(bsz, s // ts),
        in_specs=[pl.BlockSpec((1, ts, dc), lambda b, i: (b, i, 0)),
                  pl.BlockSpec((1, ts, dc), lambda b, i: (b, jnp.maximum(i - 1, 0), 0)),
                  pl.BlockSpec((CONV_W, dc), lambda b, i: (0, 0)),
                  vec(dc), vec(dc), vec(dc),
                  pl.BlockSpec((dc, d), lambda b, i: (0, 0)),
                  pl.BlockSpec((1, ts, d), lambda b, i: (b, i, 0))],
        out_specs=pl.BlockSpec((1, ts, d), lambda b, i: (b, i, 0)),
        out_shape=jax.ShapeDtypeStruct((bsz, s, d), F32),
        scratch_shapes=[pltpu.VMEM((SUBLANES, ts + HALO, dc), F32), pltpu.VMEM((ts, dc), F32)],
        compiler_params=_cparams(("arbitrary", "arbitrary")),
        name="conv",
    )(u, u, dw, cb, lg, lb, w_co, gc)


def _attn_kernel(q_ref, k0_ref, k1_ref, k2_ref, v0_ref, v1_ref, v2_ref, bias_ref, o_ref):
    i = pl.program_id(1)
    k_refs = (k0_ref, k1_ref, k2_ref)
    v_refs = (v0_ref, v1_ref, v2_ref)
    low = lax.broadcasted_iota(jnp.int32, (TQ, LANES), 1) < HEAD_DIM

    n_heads_per_slab = LANES // HEAD_DIM

    def body(pens):
        def scores(h):
            lo = (h // n_heads_per_slab) * LANES
            q2 = q_ref[0, :, lo:lo + LANES]
            keep = low if h % n_heads_per_slab == 0 else jnp.logical_not(low)
            qh = jnp.where(keep, q2, jnp.zeros_like(q2))
            ss = []
            for j in range(NKB):
                s = lax.dot_general(qh, k_refs[j][0, :, lo:lo + LANES],
                                    (((1,), (1,)), ((), ())), preferred_element_type=F32)
                s = s + bias_ref[h, :, j * TQ:(j + 1) * TQ]
                ss.append(s if pens is None else s + pens[j])
            return ss

        def softmax_pv(h, ss):
            lo = (h // n_heads_per_slab) * LANES
            m = jnp.maximum(jnp.maximum(jnp.max(ss[0], axis=-1, keepdims=True),
                                        jnp.max(ss[1], axis=-1, keepdims=True)),
                            jnp.max(ss[2], axis=-1, keepdims=True))
            ps = [jnp.exp2(ss[j] - m) for j in range(NKB)]
            l = jnp.sum(ps[0] + ps[1] + ps[2], axis=-1, keepdims=True)
            p = jnp.concatenate([pj.astype(BF16) for pj in ps], axis=1)
            v = jnp.concatenate([v_refs[j][0, :, lo:lo + LANES] for j in range(NKB)], axis=0)
            return jnp.dot(p, v, preferred_element_type=F32) / l

        outs = []
        ss_next = scores(0)
        for h in range(N_HEADS):
            ss = ss_next
            if h + 1 < N_HEADS:
                ss_next = scores(h + 1)
            outs.append(softmax_pv(h, ss))
            if h % n_heads_per_slab == n_heads_per_slab - 1:
                lo = (h // n_heads_per_slab) * LANES
                o_ref[0, :, lo:lo + LANES] = jnp.where(low, outs[0], outs[1]).astype(BF16)
                outs = []

    @pl.when(i >= NKB - 1)
    def _():
        body(None)

    @pl.when(i < NKB - 1)
    def _():
        body([jnp.where(i - (NKB - 1) + j >= 0, 0.0, NEG_INF).astype(F32) for j in range(NKB)])


def _attn(q, k, v, bias):
    bsz, s, da = q.shape
    kspec = lambda j: pl.BlockSpec(
        (1, TQ, da), lambda b, i: (b, jnp.maximum(i - (NKB - 1) + j, 0), 0))
    return pl.pallas_call(
        _attn_kernel,
        grid=(bsz, s // TQ),
        in_specs=[pl.BlockSpec((1, TQ, da), lambda b, i: (b, i, 0)),
                  kspec(0), kspec(1), kspec(2), kspec(0), kspec(1), kspec(2),
                  pl.BlockSpec((N_HEADS, TQ, NKB * TQ), lambda b, i: (0, 0, 0))],
        out_specs=pl.BlockSpec((1, TQ, da), lambda b, i: (b, i, 0)),
        out_shape=jax.ShapeDtypeStruct((bsz, s, da), BF16),
        compiler_params=_cparams(("arbitrary", "arbitrary")),
        name="attn",
    )(q, k, k, k, v, v, v, bias)


def _bias_table(rel_bias):
    nk = NKB * TQ
    lw = TQ + nk - 1
    n_lo = (TQ - 1) - REL_CLIP
    n_hi = (nk - 1) - REL_CLIP
    w = jnp.concatenate([jnp.repeat(rel_bias[:, :1], n_lo, axis=1), rel_bias,
                         jnp.repeat(rel_bias[:, -1:], n_hi, axis=1)], axis=1).astype(F32)
    row = 1024
    assert row + 1 >= lw
    wp = jnp.concatenate([w[:, nk - 1::-1], jnp.zeros((w.shape[0], row + 1 - lw), F32),
                          w[:, lw - 1:nk - 1:-1]], axis=1)
    flat = jnp.tile(wp, (1, TQ))[:, :TQ * row]
    tab = flat.reshape(-1, TQ, row)[:, :, :nk]
    qi = np.arange(TQ)[:, None]
    kj = np.arange(nk)[None, :]
    qc = qi // CHUNK + (NKB - 1) * TQ // CHUNK
    kc = kj // CHUNK
    band = (kc >= qc - LEFT_CHUNKS) & (kc <= qc)
    return jnp.where(band[None], tab * LOG2E, NEG_INF)


def _merge_kernel(o_ref, mc_ref, ga_ref, x_ref, wa_ref, wo_ref, ga1_ref, g2_ref, sc_ref, sh_ref,
                  x1_ref, h2t_ref):
    ya = jnp.dot(o_ref[0], wa_ref[...], preferred_element_type=F32)
    merged = mc_ref[0] + ga_ref[0] * ya
    y = jnp.dot(merged.astype(BF16), wo_ref[...], preferred_element_type=F32)
    x1 = x_ref[0] + ga1_ref[0] * y
    x1_ref[0] = x1
    ms = jnp.mean(x1 * x1, axis=-1, keepdims=True)
    h2 = (x1 * lax.rsqrt(ms + EPS)) * g2_ref[...]
    h2 = h2 * (1.0 + sc_ref[0]) + sh_ref[0]
    h2t_ref[...] = h2.T.astype(BF16)


def _merge(o, mc, ga, x, wa, wo, ga1, g2, sc2, sh2):
    bsz, s, d = x.shape
    tm = TM_MERGE
    nb = s // tm
    row = lambda w: pl.BlockSpec((1, tm, w), lambda b, i: (b, i, 0))
    mod = pl.BlockSpec((1, 1, d), lambda b, i: (b, 0, 0))
    return pl.pallas_call(
        _merge_kernel,
        grid=(bsz, nb),
        in_specs=[row(D_ATTN), row(d), row(d), row(d),
                  pl.BlockSpec((D_ATTN, d), lambda b, i: (0, 0)),
                  pl.BlockSpec((d, d), lambda b, i: (0, 0)),
                  mod, pl.BlockSpec((1, d), lambda b, i: (0, 0)), mod, mod],
        out_specs=[row(d), pl.BlockSpec((d, tm), lambda b, i: (0, b * nb + i))],
        out_shape=[jax.ShapeDtypeStruct((bsz, s, d), F32),
                   jax.ShapeDtypeStruct((d, bsz * s), BF16)],
        compiler_params=_cparams(("arbitrary", "arbitrary")),
        name="merge",
    )(o, mc, ga, x, wa, wo, ga1, g2, sc2, sh2)


_CAND = [(a, b) for a in range(PEER_TOPK) for b in range(PEER_TOPK)
         if (a + 1) * (b + 1) <= PEER_TOPK]


def _sort16_network():
    n, pairs, p = PEER_TOPK, [], 1
    while p < n:
        k = p
        while k >= 1:
            for j in range(k % p, n - k, 2 * k):
                for i in range(min(k, n - j - k)):
                    if (i + j) // (2 * p) == (i + j + k) // (2 * p):
                        pairs.append((i + j, i + j + k))
            k //= 2
        p *= 2
    return pairs


_SORT16 = _sort16_network()


def _top16_sorted(problems, store_row):
    sub = 8
    lvs = [[s[g * sub:(g + 1) * sub, :] for g in range(PEER_NKEYS // sub)] for s in problems]
    for a, b in _SORT16:
        for lv in lvs:
            lv[a], lv[b] = jnp.maximum(lv[a], lv[b]), jnp.minimum(lv[a], lv[b])
    for r in range(PEER_TOPK):
        for p, lv in enumerate(lvs):
            m = jnp.max(lv[0], axis=0, keepdims=True)
            store_row(p, r, m)
            eq = lv[0] == m
            for k in range(PEER_TOPK - 1 - r):
                lv[k] = jnp.where(eq, lv[k + 1], lv[k])


def _route_kernel(h2t_ref, wqt_ref, keys_ref, s1_ref, theta_ref, e0_ref, e1_ref,
                  s_scr, v_scr, tau_scr, zi_scr):
    tb = h2t_ref.shape[1]
    ncol = tb // LANES
    n_hp = 2 * PEER_HEADS
    def scores(h):
        rows = slice(2 * h * PEER_DH, 2 * (h + 1) * PEER_DH)
        qt = jnp.dot(wqt_ref[rows, :], h2t_ref[...], preferred_element_type=F32).astype(BF16)
        for p in range(2):
            s_scr[2 * h + p] = jnp.dot(keys_ref[2 * h + p], qt[p * PEER_DH:(p + 1) * PEER_DH, :],
                                       preferred_element_type=F32)

    def top16(h):
        for col in range(ncol):
            cs = slice(col * LANES, (col + 1) * LANES)
            rw = slice(h * ncol + col, h * ncol + col + 1)

            def store_row(p, r, m, rw=rw):
                v_scr[p, r, rw, :] = m

            _top16_sorted([s_scr[2 * h + p, :, cs] for p in range(2)], store_row)

    scores(0)
    for h in range(PEER_HEADS):
        if h + 1 < PEER_HEADS:
            scores(h + 1)
        top16(h)

    v0 = [v_scr[0, a] for a in range(PEER_TOPK)]
    v1 = [v_scr[1, b] for b in range(PEER_TOPK)]
    cand = [v0[a] + v1[b] for (a, b) in _CAND]
    top = cand[0]
    work = list(cand)
    for r in range(PEER_TOPK):
        c16 = functools.reduce(jnp.maximum, work)
        work = [jnp.where(w == c16, -jnp.inf, w) for w in work]
    c17 = functools.reduce(jnp.maximum, work)
    z = jnp.zeros_like(top)
    for c in cand:
        z = z + jnp.where(c >= c16, jnp.exp(c - top), 0.0)
    tau_scr[...] = 0.5 * (c16 + c17)
    zi_scr[...] = _INV_SQRT2 / z

    def stage3(it, carry):
        h = it // ncol
        col = it % ncol
        cs = pl.ds(pl.multiple_of(col * LANES, LANES), LANES)
        rw = pl.ds(h * ncol + col, 1)
        s0 = s_scr[2 * h, :, cs]
        s1 = s_scr[2 * h + 1, :, cs]
        theta = jnp.where(s0 >= v_scr[0, PEER_TOPK - 1, rw, :], tau_scr[rw, :] - s0, jnp.inf)
        e0 = jnp.exp(s0 - v_scr[0, 0, rw, :])
        for g in range(PEER_NKEYS // ROWS_PER_STEP):
            rows = slice(g * ROWS_PER_STEP, (g + 1) * ROWS_PER_STEP)
            theta_ref[h, g, :, cs] = theta[rows]
            e0_ref[h, g, :, cs] = e0[rows]
        s1_ref[h, :, cs] = jnp.where(s1 >= v_scr[1, PEER_TOPK - 1, rw, :], s1, -jnp.inf)
        e1_ref[h, :, cs] = jnp.exp(s1 - v_scr[1, 0, rw, :]) * zi_scr[rw, :]
        return carry

    lax.fori_loop(0, PEER_HEADS * ncol, stage3, 0)


def _route(h2t, wqt, keys):
    d, t = h2t.shape
    tb = TB_ROUTE
    ncol = tb // LANES
    n_hp = 2 * PEER_HEADS
    n_grp = PEER_NKEYS // ROWS_PER_STEP
    tab = pl.BlockSpec((PEER_HEADS, PEER_NKEYS, tb), lambda i: (0, 0, i))
    rowtab = pl.BlockSpec((PEER_HEADS, n_grp, ROWS_PER_STEP, tb), lambda i: (0, 0, 0, i))
    tab_shape = jax.ShapeDtypeStruct((PEER_HEADS, PEER_NKEYS, t), F32)
    rowtab_shape = jax.ShapeDtypeStruct((PEER_HEADS, n_grp, ROWS_PER_STEP, t), F32)
    return pl.pallas_call(
        _route_kernel,
        grid=(t // tb,),
        in_specs=[pl.BlockSpec((d, tb), lambda i: (0, i)),
                  pl.BlockSpec(wqt.shape, lambda i: (0, 0)),
                  pl.BlockSpec(keys.shape, lambda i: (0, 0, 0))],
        out_specs=[tab, rowtab, rowtab, tab],
        out_shape=[tab_shape, rowtab_shape, rowtab_shape, tab_shape],
        scratch_shapes=[pltpu.VMEM((n_hp, PEER_NKEYS, tb), F32),
                        pltpu.VMEM((2, PEER_TOPK, PEER_HEADS * ncol, LANES), F32),
                        pltpu.VMEM((PEER_HEADS * ncol, LANES), F32),
                        pltpu.VMEM((PEER_HEADS * ncol, LANES), F32)],
        compiler_params=_cparams(("arbitrary",)),
        name="route",
    )(h2t, wqt, keys)


_INV_SQRT2 = float(1.0 / np.sqrt(2.0))
K_PIECE = 256
K_PIECE_A = 256
ROW_BLOCK = 4
JG_BLOCK = 4


def _experts_kernel(h2t_ref, u_ref, vt_ref, s1_ref, theta_ref, e0_ref, e1_ref,
                    x1_ref, ga2_ref, o_ref, a0_scr, a1_scr, p0_scr, p1_scr, acc_ref):
    c = pl.program_id(1)
    tb = h2t_ref.shape[1]

    @pl.when(c == 0)
    def _():
        acc_ref[...] = jnp.zeros_like(acc_ref)
        p1_scr[...] = jnp.zeros_like(p1_scr)

    sub = (8, LANES)
    n_jg = PEER_NKEYS // sub[0]
    zero = jnp.zeros(sub, F32)

    def pair_body(k, refs, stages):
        a_new, a_old, p_new, p_old = refs
        base = k * MXU_N
        ps = pl.ds(base, MXU_N)

        def mm_a(kc):
            ks = slice(kc * K_PIECE_A, (kc + 1) * K_PIECE_A)
            part = jnp.dot(u_ref[:, ks], h2t_ref[ks, ps], preferred_element_type=F32)
            if kc == 0:
                a_new[:, ps] = part
            else:
                a_new[:, ps] += part

        def mm_acc(kc, mh):
            ks = slice(kc * K_PIECE, (kc + 1) * K_PIECE)
            ms = slice(mh * EC, (mh + 1) * EC)
            acc_ref[ms, ps] += jnp.dot(vt_ref[ms, ks], p_old[ks, ps],
                                       preferred_element_type=F32)

        a_pieces = [functools.partial(mm_a, kc) for kc in range(D_MODEL // K_PIECE_A)]
        acc_pieces = [functools.partial(mm_acc, kc, mh) for kc in range(EC // K_PIECE)
                      for mh in range(D_MODEL // EC)]
        if stages == "fill":
            for piece in a_pieces:
                piece()
            return
        if stages == "drain":
            for piece in acc_pieces:
                piece()
            return
        mm_pieces = a_pieces + acc_pieces
        blocks = [(half, ip, j0) for half in range(MXU_N // LANES)
                  for ip in range(ROWS_PER_STEP // ROW_BLOCK)
                  for j0 in range(0, n_jg, JG_BLOCK)]
        every = len(blocks) // len(mm_pieces)
        for bi, (half, ip, j0) in enumerate(blocks):
            if bi % every == 0 and bi // every < len(mm_pieces):
                mm_pieces[bi // every]()
            cs = pl.ds(base + half * LANES, LANES)
            if True:
                rows = tuple(range(ROW_BLOCK * ip, ROW_BLOCK * (ip + 1)))
                g = [[zero] * JG_BLOCK for _ in rows]
                for h in range(PEER_HEADS):
                    th = [jnp.broadcast_to(theta_ref[h, 0, r:r + 1, cs], sub) for r in rows]
                    e0 = [jnp.broadcast_to(e0_ref[h, 0, r:r + 1, cs], sub) for r in rows]
                    for jg in range(JG_BLOCK):
                        js = slice((j0 + jg) * sub[0], (j0 + jg + 1) * sub[0])
                        s1 = s1_ref[h, js, cs]
                        e1 = e1_ref[h, js, cs]
                        for q in range(ROW_BLOCK):
                            g[q][jg] = g[q][jg] + jnp.where(s1 >= th[q], e1, zero) * e0[q]
                for q, r in enumerate(rows):
                    for jg in range(0, JG_BLOCK, 2):
                        lo = r * PEER_NKEYS + (j0 + jg) * sub[0]
                        a = a_old[lo:lo + 2 * sub[0], cs]
                        act = a + a * lax.erf(a)
                        gg = jnp.concatenate([g[q][jg], g[q][jg + 1]], axis=0)
                        p_new[lo:lo + 2 * sub[0], cs] = (act * gg).astype(BF16)

    even = (a0_scr, a1_scr, p1_scr, p0_scr)
    odd = (a1_scr, a0_scr, p0_scr, p1_scr)
    last = pl.num_programs(1) - 1

    def run(refs, stages):
        for k in range(tb // MXU_N):
            pair_body(k, refs, stages)

    @pl.when(c == 0)
    def _():
        run(even, "fill")

    @pl.when(jnp.logical_and(c > 0, c % 2 == 0))
    def _():
        run(even, "all")

    @pl.when(jnp.logical_and(c < last, c % 2 == 1))
    def _():
        run(odd, "all")

    @pl.when(c == last)
    def _():
        run(odd, "drain")
        o_ref[...] = x1_ref[...] + ga2_ref[0] * acc_ref[...].T


def _experts(h2t, u, vt, s1, theta, e0, e1, x1, ga2, seq):
    d, t = h2t.shape
    tb = TB_EXP
    n_chunks = u.shape[0] // EC
    last = n_chunks - 1
    per_batch = seq // tb
    once = dict(pipeline_mode=pl.Buffered(1))
    tab = pl.BlockSpec((PEER_HEADS, PEER_NKEYS, tb), lambda i, c: (0, 0, i), **once)
    rowtab = pl.BlockSpec((PEER_HEADS, 1, ROWS_PER_STEP, tb),
                          lambda i, c: (0, jnp.clip(c - 1, 0, last), 0, i))
    return pl.pallas_call(
        _experts_kernel,
        grid=(t // tb, n_chunks + 2),
        in_specs=[pl.BlockSpec((d, tb), lambda i, c: (0, i)),
                  pl.BlockSpec((EC, d), lambda i, c: (jnp.minimum(c, last), 0)),
                  pl.BlockSpec((d, EC), lambda i, c: (0, jnp.clip(c - 2, 0, last))),
                  tab, rowtab, rowtab, tab,
                  pl.BlockSpec((tb, d), lambda i, c: (i, 0), **once),
                  pl.BlockSpec((1, 1, d), lambda i, c: (i // per_batch, 0, 0))],
        out_specs=pl.BlockSpec((tb, d), lambda i, c: (i, 0)),
        out_shape=jax.ShapeDtypeStruct((t, d), F32),
        scratch_shapes=[pltpu.VMEM((EC, tb), F32), pltpu.VMEM((EC, tb), F32),
                        pltpu.VMEM((EC, tb), BF16), pltpu.VMEM((EC, tb), BF16),
                        pltpu.VMEM((d, tb), F32)],
        compiler_params=_cparams(("arbitrary", "arbitrary")),
        name="experts",
    )(h2t, u, vt, s1, theta, e0, e1, x1, ga2)


def kernel(x, c, w_ada, b_ada, norm1_g, norm2_g, w_in, conv_dw, conv_b, conv_ln_g, conv_ln_b,
           w_conv_out, q_norm_g, k_norm_g, rel_bias, w_attn_out, w_out, peer_wq, peer_keys,
           peer_u, peer_v):
    bsz, s, d = x.shape
    depth = w_ada.shape[0]
    bd = jnp.asarray(np.kron(np.eye(N_HEADS), np.full((HEAD_DIM, HEAD_DIM), 1.0 / HEAD_DIM)), BF16)
    for l in range(depth):
        mod = _ada(c, w_ada[l], b_ada[l])
        sh1, sc1, ga1, sh2, sc2, ga2 = [m.reshape(bsz, 1, d) for m in jnp.split(mod, 6, axis=-1)]
        u, q, k, v, gc, ga, pu_bf, pvt_bf = _inproj(
            x, norm1_g[l].reshape(1, d), sc1, sh1, w_in[l].astype(BF16), bd,
            jnp.tile(q_norm_g[l], N_HEADS).reshape(1, D_ATTN),
            jnp.tile(k_norm_g[l], N_HEADS).reshape(1, D_ATTN), peer_u[l], peer_v[l])
        mc = _conv(u, conv_dw[l], conv_b[l].reshape(1, D_CONV), conv_ln_g[l].reshape(1, D_CONV),
                   conv_ln_b[l].reshape(1, D_CONV), w_conv_out[l].astype(BF16), gc)
        o = _attn(q, k, v, _bias_table(rel_bias[l]))
        x1, h2t = _merge(o, mc, ga, x, w_attn_out[l].astype(BF16), w_out[l].astype(BF16),
                         ga1, norm2_g[l].reshape(1, d), sc2, sh2)
        wqt = peer_wq[l].T.astype(BF16)
        keys = peer_keys[l].reshape(2 * PEER_HEADS, PEER_NKEYS, PEER_DH).astype(BF16)
        s1, theta, e0, e1 = _route(h2t, wqt, keys)
        out = _experts(h2t, pu_bf, pvt_bf, s1, theta, e0, e1, x1.reshape(bsz * s, d), ga2, s)
        x = out.reshape(bsz, s, d)
    return x
```

```python
import functools

import jax
import jax.numpy as jnp
import numpy as np
from jax import lax
from jax.experimental import pallas as pl
from jax.experimental.pallas import tpu as pltpu

F32 = jnp.float32
BF16 = jnp.bfloat16

D_MODEL = 1024
CHUNK = 64
N_HEADS = 8
HEAD_DIM = 64
D_ATTN = N_HEADS * HEAD_DIM
LEFT_CHUNKS = 8
REL_CLIP = 128
D_CONV = D_MODEL // 2
CONV_W = 31
PEER_HEADS = 8
PEER_NKEYS = 128
PEER_N = PEER_NKEYS * PEER_NKEYS
PEER_DH = 128
PEER_TOPK = 16
EPS = 1e-6
NEG_INF = -1e30
LOG2E = float(np.log2(np.e))

LANES = 128
SUBLANES = 8
MXU_N = 256
VMEM_LIMIT = 56 * 1024 * 1024

TM_IN = 512
TS_CONV = 512
HALO = 32
TQ = 256
NKB = 3
TM_MERGE = 512
TB_ROUTE = 512
TB_EXP = 1024
EC = 512
ROWS_PER_STEP = EC // PEER_NKEYS

def _cparams(sem):
    return pltpu.CompilerParams(dimension_semantics=sem, vmem_limit_bytes=VMEM_LIMIT)


def _ada_kernel(c_ref, w_ref, b_ref, o_ref):
    c = c_ref[...]
    cond = c * jax.nn.sigmoid(c)
    o_ref[...] = jnp.dot(cond.astype(BF16), w_ref[...].astype(BF16),
                         preferred_element_type=F32) + b_ref[...]


def _ada(c, w, b):
    bsz, d = c.shape
    n = w.shape[1]
    tn = 1024
    return pl.pallas_call(
        _ada_kernel,
        grid=(n // tn,),
        in_specs=[pl.BlockSpec((bsz, d), lambda j: (0, 0)),
                  pl.BlockSpec((d, tn), lambda j: (0, j)),
                  pl.BlockSpec((1, tn), lambda j: (0, j))],
        out_specs=pl.BlockSpec((bsz, tn), lambda j: (0, j)),
        out_shape=jax.ShapeDtypeStruct((bsz, n), F32),
        compiler_params=_cparams(("arbitrary",)),
        name="ada",
    )(c, w, b.reshape(1, n))


def _head_rms(t, bd_ref, gain):
    t2 = t * t
    hi = t2.astype(BF16)
    lo = (t2 - hi.astype(F32)).astype(BF16)
    ms = (jnp.dot(hi, bd_ref[...], preferred_element_type=F32)
          + jnp.dot(lo, bd_ref[...], preferred_element_type=F32))
    return t * lax.rsqrt(ms + EPS) * gain


def _inproj_kernel(x_ref, g_ref, sc_ref, sh_ref, w_ref, bd_ref, qg_ref, kg_ref, pu_ref, pv_ref,
                   u_ref, q_ref, k_ref, v_ref, gc_ref, ga_ref, pub_ref, pvt_ref):
    x = x_ref[0]
    ms = jnp.mean(x * x, axis=-1, keepdims=True)
    h = (x * lax.rsqrt(ms + EPS)) * g_ref[...]
    h = h * (1.0 + sc_ref[0]) + sh_ref[0]
    hb = h.astype(BF16)

    def seg(lo, hi):
        return jnp.dot(hb, w_ref[:, lo:hi], preferred_element_type=F32)

    bounds = np.cumsum([0, D_CONV, D_CONV, D_ATTN, D_ATTN, D_ATTN, D_MODEL, D_MODEL])
    lo, hi = bounds[:-1], bounds[1:]
    a = seg(lo[0], hi[0])
    b = seg(lo[1], hi[1])
    q = seg(lo[2], hi[2])
    u_ref[0] = a * jax.nn.sigmoid(b)
    k = seg(lo[3], hi[3])
    q_ref[0] = (_head_rms(q, bd_ref, qg_ref[...]) * (HEAD_DIM ** -0.5 * LOG2E)).astype(BF16)
    v = seg(lo[4], hi[4])
    k_ref[0] = _head_rms(k, bd_ref, kg_ref[...]).astype(BF16)
    gc = seg(lo[5], hi[5])
    v_ref[0] = v.astype(BF16)
    pub_ref[...] = (pu_ref[...] * _INV_SQRT2).astype(BF16)
    ga = seg(lo[6], hi[6])
    pvt_ref[...] = pv_ref[...].T.astype(BF16)
    gc_ref[0] = jax.nn.sigmoid(gc)
    ga_ref[0] = jax.nn.sigmoid(ga)


def _inproj(x, g1, sc1, sh1, w_in, bd, qg, kg, pu, pv):
    bsz, s, d = x.shape
    tm = TM_IN
    n_in = w_in.shape[1]
    nb = s // tm
    n_e = pu.shape[0]
    er = n_e // (bsz * nb)
    row = lambda w: pl.BlockSpec((1, tm, w), lambda b, i: (b, i, 0))
    vec = lambda w: pl.BlockSpec((1, w), lambda b, i: (0, 0))
    mod = pl.BlockSpec((1, 1, d), lambda b, i: (b, 0, 0))
    erow = pl.BlockSpec((er, d), lambda b, i: (b * nb + i, 0))
    return pl.pallas_call(
        _inproj_kernel,
        grid=(bsz, nb),
        in_specs=[row(d), vec(d), mod, mod,
                  pl.BlockSpec((d, n_in), lambda b, i: (0, 0), pipeline_mode=pl.Buffered(1)),
                  pl.BlockSpec((D_ATTN, D_ATTN), lambda b, i: (0, 0)),
                  vec(D_ATTN), vec(D_ATTN), erow, erow],
        out_specs=[row(D_CONV), row(D_ATTN), row(D_ATTN), row(D_ATTN), row(d), row(d),
                   erow, pl.BlockSpec((d, er), lambda b, i: (0, b * nb + i))],
        out_shape=[jax.ShapeDtypeStruct((bsz, s, D_CONV), F32),
                   jax.ShapeDtypeStruct((bsz, s, D_ATTN), BF16),
                   jax.ShapeDtypeStruct((bsz, s, D_ATTN), BF16),
                   jax.ShapeDtypeStruct((bsz, s, D_ATTN), BF16),
                   jax.ShapeDtypeStruct((bsz, s, d), F32),
                   jax.ShapeDtypeStruct((bsz, s, d), F32),
                   jax.ShapeDtypeStruct((n_e, d), BF16),
                   jax.ShapeDtypeStruct((d, n_e), BF16)],
        compiler_params=_cparams(("arbitrary", "arbitrary")),
        name="inproj",
    )(x, g1, sc1, sh1, w_in, bd, qg, kg, pu, pv)


CONV_ROWS = 64


def _conv_kernel(u_ref, up_ref, dw_ref, cb_ref, lg_ref, lb_ref, w_ref, gc_ref,
                 o_ref, ext_ref, y_ref):
    i = pl.program_id(1)
    ts = u_ref.shape[1]
    prev = up_ref[0, ts - HALO:, :]
    ext_ref[0, 0:HALO, :] = jnp.where(i > 0, prev, 0.0)
    ext_ref[0, HALO:, :] = u_ref[0]
    n_sh = ts + HALO - SUBLANES
    for s in range(1, SUBLANES):
        for r0 in range(0, n_sh, CONV_ROWS):
            n = min(CONV_ROWS, n_sh - r0)
            ext_ref[s, r0:r0 + n, :] = ext_ref[0, r0 + s:r0 + s + n, :]
    base = HALO - (CONV_W - 1)
    for r0 in range(0, ts, CONV_ROWS):
        acc = jnp.zeros((CONV_ROWS, D_CONV), F32) + cb_ref[...]
        for w in range(CONV_W):
            s = (base + w) % SUBLANES
            a = r0 + base + w - s
            acc = acc + ext_ref[s, a:a + CONV_ROWS, :] * dw_ref[w:w + 1, :]
        y_ref[r0:r0 + CONV_ROWS, :] = acc
    y = y_ref[...]
    mu = jnp.mean(y, axis=-1, keepdims=True)
    yc = y - mu
    var = jnp.mean(yc * yc, axis=-1, keepdims=True)
    z = yc * lax.rsqrt(var + EPS) * lg_ref[...] + lb_ref[...]
    z = z * jax.nn.sigmoid(z)
    o = jnp.dot(z.astype(BF16), w_ref[...], preferred_element_type=F32)
    o_ref[0] = gc_ref[0] * o


def _conv(u, dw, cb, lg, lb, w_co, gc):
    bsz, s, dc = u.shape
    d = w_co.shape[1]
    ts = TS_CONV
    vec = lambda w: pl.BlockSpec((1, w), lambda b, i: (0, 0))
    return pl.pallas_call(
        _conv_kernel,
        grid=(bsz, s // ts),
        in_specs=[pl.BlockSpec((1, ts, dc), lambda b, i: (b, i, 0)),
                  pl.BlockSpec((1, ts, dc), lambda b, i: (b, jnp.maximum(i - 1, 0), 0)),
                  pl.BlockSpec((CONV_W, dc), lambda b, i: (0, 0)),
                  vec(dc), vec(dc), vec(dc),
                  pl.BlockSpec((dc, d), lambda b, i: (0, 0)),
                  pl.BlockSpec((1, ts, d), lambda b, i: (b, i, 0))],
        out_specs=pl.BlockSpec((1, ts, d), lambda b, i: (b, i, 0)),
        out_shape=jax.ShapeDtypeStruct((bsz, s, d), F32),
        scratch_shapes=[pltpu.VMEM((SUBLANES, ts + HALO, dc), F32), pltpu.VMEM((ts, dc), F32)],
        compiler_params=_cparams(("arbitrary", "arbitrary")),
        name="conv",
    )(u, u, dw, cb, lg, lb, w_co, gc)


def _attn_kernel(q_ref, k0_ref, k1_ref, k2_ref, v0_ref, v1_ref, v2_ref, bias_ref, o_ref):
    i = pl.program_id(1)
    k_refs = (k0_ref, k1_ref, k2_ref)
    v_refs = (v0_ref, v1_ref, v2_ref)
    low = lax.broadcasted_iota(jnp.int32, (TQ, LANES), 1) < HEAD_DIM

    n_heads_per_slab = LANES // HEAD_DIM

    def body(pens):
        def scores(h):
            lo = (h // n_heads_per_slab) * LANES
            q2 = q_ref[0, :, lo:lo + LANES]
            keep = low if h % n_heads_per_slab == 0 else jnp.logical_not(low)
            qh = jnp.where(keep, q2, jnp.zeros_like(q2))
            ss = []
            for j in range(NKB):
                s = lax.dot_general(qh, k_refs[j][0, :, lo:lo + LANES],
                                    (((1,), (1,)), ((), ())), preferred_element_type=F32)
                s = s + bias_ref[h, :, j * TQ:(j + 1) * TQ]
                ss.append(s if pens is None else s + pens[j])
            return ss

        def softmax_pv(h, ss):
            lo = (h // n_heads_per_slab) * LANES
            m = jnp.maximum(jnp.maximum(jnp.max(ss[0], axis=-1, keepdims=True),
                                        jnp.max(ss[1], axis=-1, keepdims=True)),
                            jnp.max(ss[2], axis=-1, keepdims=True))
            ps = [jnp.exp2(ss[j] - m) for j in range(NKB)]
            l = jnp.sum(ps[0] + ps[1] + ps[2], axis=-1, keepdims=True)
            p = jnp.concatenate([pj.astype(BF16) for pj in ps], axis=1)
            v = jnp.concatenate([v_refs[j][0, :, lo:lo + LANES] for j in range(NKB)], axis=0)
            return jnp.dot(p, v, preferred_element_type=F32) / l

        outs = []
        ss_next = scores(0)
        for h in range(N_HEADS):
            ss = ss_next
            if h + 1 < N_HEADS:
                ss_next = scores(h + 1)
            outs.append(softmax_pv(h, ss))
            if h % n_heads_per_slab == n_heads_per_slab - 1:
                lo = (h // n_heads_per_slab) * LANES
                o_ref[0, :, lo:lo + LANES] = jnp.where(low, outs[0], outs[1]).astype(BF16)
                outs = []

    @pl.when(i >= NKB - 1)
    def _():
        body(None)

    @pl.when(i < NKB - 1)
    def _():
        body([jnp.where(i - (NKB - 1) + j >= 0, 0.0, NEG_INF).astype(F32) for j in range(NKB)])


def _attn(q, k, v, bias):
    bsz, s, da = q.shape
    kspec = lambda j: pl.BlockSpec(
        (1, TQ, da), lambda b, i: (b, jnp.maximum(i - (NKB - 1) + j, 0), 0))
    return pl.pallas_call(
        _attn_kernel,
        grid=(bsz, s // TQ),
        in_specs=[pl.BlockSpec((1, TQ, da), lambda b, i: (b, i, 0)),
                  kspec(0), kspec(1), kspec(2), kspec(0), kspec(1), kspec(2),
                  pl.BlockSpec((N_HEADS, TQ, NKB * TQ), lambda b, i: (0, 0, 0))],
        out_specs=pl.BlockSpec((1, TQ, da), lambda b, i: (b, i, 0)),
        out_shape=jax.ShapeDtypeStruct((bsz, s, da), BF16),
        compiler_params=_cparams(("arbitrary", "arbitrary")),
        name="attn",
    )(q, k, k, k, v, v, v, bias)


def _bias_table(rel_bias):
    nk = NKB * TQ
    lw = TQ + nk - 1
    n_lo = (TQ - 1) - REL_CLIP
    n_hi = (nk - 1) - REL_CLIP
    w = jnp.concatenate([jnp.repeat(rel_bias[:, :1], n_lo, axis=1), rel_bias,
                         jnp.repeat(rel_bias[:, -1:], n_hi, axis=1)], axis=1).astype(F32)
    assert BIAS_ROW >= lw
    base = jnp.concatenate([w[:, nk - 1::-1], jnp.zeros((w.shape[0], BIAS_ROW - lw), F32),
                            w[:, lw - 1:nk - 1:-1]], axis=1) * LOG2E
    return pl.pallas_call(
        _bias_kernel,
        out_shape=jax.ShapeDtypeStruct((N_HEADS, TQ, nk), F32),
        compiler_params=pltpu.CompilerParams(vmem_limit_bytes=VMEM_LIMIT),
        name="bias",
    )(base)


BIAS_ROW = 1024


def _bias_kernel(base_ref, o_ref):
    nk = NKB * TQ
    qi = lax.broadcasted_iota(jnp.int32, (TQ, nk), 0)
    kj = lax.broadcasted_iota(jnp.int32, (TQ, nk), 1)
    qc = qi // CHUNK + (NKB - 1) * TQ // CHUNK
    kc = kj // CHUNK
    band = (kc >= qc - LEFT_CHUNKS) & (kc <= qc)
    for h in range(N_HEADS):
        x = jnp.broadcast_to(base_ref[h:h + 1, :], (TQ, BIAS_ROW))
        y = pltpu.roll(x, 0, 1, stride=1, stride_axis=0)
        o_ref[h] = jnp.where(band, y[:, :nk], NEG_INF)


def _merge_kernel(o_ref, mc_ref, ga_ref, x_ref, wa_ref, wo_ref, ga1_ref, g2_ref, sc_ref, sh_ref,
                  x1_ref, h2t_ref):
    ya = jnp.dot(o_ref[0], wa_ref[...], preferred_element_type=F32)
    merged = mc_ref[0] + ga_ref[0] * ya
    y = jnp.dot(merged.astype(BF16), wo_ref[...], preferred_element_type=F32)
    x1 = x_ref[0] + ga1_ref[0] * y
    x1_ref[0] = x1
    ms = jnp.mean(x1 * x1, axis=-1, keepdims=True)
    h2 = (x1 * lax.rsqrt(ms + EPS)) * g2_ref[...]
    h2 = h2 * (1.0 + sc_ref[0]) + sh_ref[0]
    h2t_ref[...] = h2.T.astype(BF16)


def _merge(o, mc, ga, x, wa, wo, ga1, g2, sc2, sh2):
    bsz, s, d = x.shape
    tm = TM_MERGE
    nb = s // tm
    row = lambda w: pl.BlockSpec((1, tm, w), lambda b, i: (b, i, 0))
    mod = pl.BlockSpec((1, 1, d), lambda b, i: (b, 0, 0))
    return pl.pallas_call(
        _merge_kernel,
        grid=(bsz, nb),
        in_specs=[row(D_ATTN), row(d), row(d), row(d),
                  pl.BlockSpec((D_ATTN, d), lambda b, i: (0, 0)),
                  pl.BlockSpec((d, d), lambda b, i: (0, 0)),
                  mod, pl.BlockSpec((1, d), lambda b, i: (0, 0)), mod, mod],
        out_specs=[row(d), pl.BlockSpec((d, tm), lambda b, i: (0, b * nb + i))],
        out_shape=[jax.ShapeDtypeStruct((bsz, s, d), F32),
                   jax.ShapeDtypeStruct((d, bsz * s), BF16)],
        compiler_params=_cparams(("arbitrary", "arbitrary")),
        name="merge",
    )(o, mc, ga, x, wa, wo, ga1, g2, sc2, sh2)


_CAND = [(a, b) for a in range(PEER_TOPK) for b in range(PEER_TOPK)
         if (a + 1) * (b + 1) <= PEER_TOPK]


def _sort16_network():
    n, pairs, p = PEER_TOPK, [], 1
    while p < n:
        k = p
        while k >= 1:
            for j in range(k % p, n - k, 2 * k):
                for i in range(min(k, n - j - k)):
                    if (i + j) // (2 * p) == (i + j + k) // (2 * p):
                        pairs.append((i + j, i + j + k))
            k //= 2
        p *= 2
    return pairs


_SORT16 = _sort16_network()


def _top16_sorted(problems, store_row):
    sub = 8
    lvs = [[s[g * sub:(g + 1) * sub, :] for g in range(PEER_NKEYS // sub)] for s in problems]
    for a, b in _SORT16:
        for lv in lvs:
            lv[a], lv[b] = jnp.maximum(lv[a], lv[b]), jnp.minimum(lv[a], lv[b])
    for r in range(PEER_TOPK):
        for p, lv in enumerate(lvs):
            m = jnp.max(lv[0], axis=0, keepdims=True)
            store_row(p, r, m)
            eq = lv[0] == m
            for k in range(PEER_TOPK - 1 - r):
                lv[k] = jnp.where(eq, lv[k + 1], lv[k])


def _route_kernel(h2t_ref, wqt_ref, keys_ref, s1_ref, theta_ref, e0_ref, e1_ref,
                  s_scr, v_scr, tau_scr, zi_scr):
    tb = h2t_ref.shape[1]
    ncol = tb // LANES
    n_hp = 2 * PEER_HEADS
    def scores(h):
        rows = slice(2 * h * PEER_DH, 2 * (h + 1) * PEER_DH)
        qt = jnp.dot(wqt_ref[rows, :], h2t_ref[...], preferred_element_type=F32).astype(BF16)
        for p in range(2):
            s_scr[2 * h + p] = jnp.dot(keys_ref[2 * h + p], qt[p * PEER_DH:(p + 1) * PEER_DH, :],
                                       preferred_element_type=F32)

    def top16(h):
        for col in range(ncol):
            cs = slice(col * LANES, (col + 1) * LANES)
            rw = slice(h * ncol + col, h * ncol + col + 1)

            def store_row(p, r, m, rw=rw):
                v_scr[p, r, rw, :] = m

            _top16_sorted([s_scr[2 * h + p, :, cs] for p in range(2)], store_row)

    scores(0)
    for h in range(PEER_HEADS):
        if h + 1 < PEER_HEADS:
            scores(h + 1)
        top16(h)

    v0 = [v_scr[0, a] for a in range(PEER_TOPK)]
    v1 = [v_scr[1, b] for b in range(PEER_TOPK)]
    cand = [v0[a] + v1[b] for (a, b) in _CAND]
    top = cand[0]
    work = list(cand)
    for r in range(PEER_TOPK):
        c16 = functools.reduce(jnp.maximum, work)
        work = [jnp.where(w == c16, -jnp.inf, w) for w in work]
    c17 = functools.reduce(jnp.maximum, work)
    z = jnp.zeros_like(top)
    for c in cand:
        z = z + jnp.where(c >= c16, jnp.exp(c - top), 0.0)
    tau_scr[...] = 0.5 * (c16 + c17)
    zi_scr[...] = _INV_SQRT2 / z

    def stage3(it, carry):
        h = it // ncol
        col = it % ncol
        cs = pl.ds(pl.multiple_of(col * LANES, LANES), LANES)
        rw = pl.ds(h * ncol + col, 1)
        s0 = s_scr[2 * h, :, cs]
        s1 = s_scr[2 * h + 1, :, cs]
        theta = jnp.where(s0 >= v_scr[0, PEER_TOPK - 1, rw, :], tau_scr[rw, :] - s0, jnp.inf)
        e0 = jnp.exp(s0 - v_scr[0, 0, rw, :])
        for g in range(PEER_NKEYS // ROWS_PER_STEP):
            rows = slice(g * ROWS_PER_STEP, (g + 1) * ROWS_PER_STEP)
            theta_ref[h, g, :, cs] = theta[rows]
            e0_ref[h, g, :, cs] = e0[rows]
        s1_ref[h, :, cs] = jnp.where(s1 >= v_scr[1, PEER_TOPK - 1, rw, :], s1, -jnp.inf)
        e1_ref[h, :, cs] = jnp.exp(s1 - v_scr[1, 0, rw, :]) * zi_scr[rw, :]
        return carry

    lax.fori_loop(0, PEER_HEADS * ncol, stage3, 0)


def _route(h2t, wqt, keys):
    d, t = h2t.shape
    tb = TB_ROUTE
    ncol = tb // LANES
    n_hp = 2 * PEER_HEADS
    n_grp = PEER_NKEYS // ROWS_PER_STEP
    tab = pl.BlockSpec((PEER_HEADS, PEER_NKEYS, tb), lambda i: (0, 0, i))
    rowtab = pl.BlockSpec((PEER_HEADS, n_grp, ROWS_PER_STEP, tb), lambda i: (0, 0, 0, i))
    tab_shape = jax.ShapeDtypeStruct((PEER_HEADS, PEER_NKEYS, t), F32)
    rowtab_shape = jax.ShapeDtypeStruct((PEER_HEADS, n_grp, ROWS_PER_STEP, t), F32)
    return pl.pallas_call(
        _route_kernel,
        grid=(t // tb,),
        in_specs=[pl.BlockSpec((d, tb), lambda i: (0, i)),
                  pl.BlockSpec(wqt.shape, lambda i: (0, 0)),
                  pl.BlockSpec(keys.shape, lambda i: (0, 0, 0))],
        out_specs=[tab, rowtab, rowtab, tab],
        out_shape=[tab_shape, rowtab_shape, rowtab_shape, tab_shape],
        scratch_shapes=[pltpu.VMEM((n_hp, PEER_NKEYS, tb), F32),
                        pltpu.VMEM((2, PEER_TOPK, PEER_HEADS * ncol, LANES), F32),
                        pltpu.VMEM((PEER_HEADS * ncol, LANES), F32),
                        pltpu.VMEM((PEER_HEADS * ncol, LANES), F32)],
        compiler_params=_cparams(("arbitrary",)),
        name="route",
    )(h2t, wqt, keys)


_INV_SQRT2 = float(1.0 / np.sqrt(2.0))
K_PIECE = 256
K_PIECE_A = 256
ROW_BLOCK = 4
JG_BLOCK = 4


def _experts_kernel(h2t_ref, u_ref, vt_ref, s1_ref, theta_ref, e0_ref, e1_ref,
                    x1_ref, ga2_ref, o_ref, a0_scr, a1_scr, p0_scr, p1_scr, acc_ref):
    c = pl.program_id(1)
    tb = h2t_ref.shape[1]

    @pl.when(c == 0)
    def _():
        acc_ref[...] = jnp.zeros_like(acc_ref)
        p1_scr[...] = jnp.zeros_like(p1_scr)

    sub = (8, LANES)
    n_jg = PEER_NKEYS // sub[0]
    zero = jnp.zeros(sub, F32)

    def pair_body(k, refs, stages):
        a_new, a_old, p_new, p_old = refs
        base = k * MXU_N
        ps = pl.ds(base, MXU_N)

        def mm_a(kc):
            ks = slice(kc * K_PIECE_A, (kc + 1) * K_PIECE_A)
            part = jnp.dot(u_ref[:, ks], h2t_ref[ks, ps], preferred_element_type=F32)
            if kc == 0:
                a_new[:, ps] = part
            else:
                a_new[:, ps] += part

        def mm_acc(kc, mh):
            ks = slice(kc * K_PIECE, (kc + 1) * K_PIECE)
            ms = slice(mh * EC, (mh + 1) * EC)
            acc_ref[ms, ps] += jnp.dot(vt_ref[ms, ks], p_old[ks, ps],
                                       preferred_element_type=F32)

        a_pieces = [functools.partial(mm_a, kc) for kc in range(D_MODEL // K_PIECE_A)]
        acc_pieces = [functools.partial(mm_acc, kc, mh) for kc in range(EC // K_PIECE)
                      for mh in range(D_MODEL // EC)]
        if stages == "fill":
            for piece in a_pieces:
                piece()
            return
        if stages == "drain":
            for piece in acc_pieces:
                piece()
            return
        mm_pieces = a_pieces + acc_pieces
        blocks = [(half, ip, j0) for half in range(MXU_N // LANES)
                  for ip in range(ROWS_PER_STEP // ROW_BLOCK)
                  for j0 in range(0, n_jg, JG_BLOCK)]
        every = len(blocks) // len(mm_pieces)
        for bi, (half, ip, j0) in enumerate(blocks):
            if bi % every == 0 and bi // every < len(mm_pieces):
                mm_pieces[bi // every]()
            cs = pl.ds(base + half * LANES, LANES)
            if True:
                rows = tuple(range(ROW_BLOCK * ip, ROW_BLOCK * (ip + 1)))
                g = [[zero] * JG_BLOCK for _ in rows]
                for h in range(PEER_HEADS):
                    th = [jnp.broadcast_to(theta_ref[h, 0, r:r + 1, cs], sub) for r in rows]
                    e0 = [jnp.broadcast_to(e0_ref[h, 0, r:r + 1, cs], sub) for r in rows]
                    for jg in range(JG_BLOCK):
                        js = slice((j0 + jg) * sub[0], (j0 + jg + 1) * sub[0])
                        s1 = s1_ref[h, js, cs]
                        e1 = e1_ref[h, js, cs]
                        for q in range(ROW_BLOCK):
                            g[q][jg] = g[q][jg] + jnp.where(s1 >= th[q], e1, zero) * e0[q]
                for q, r in enumerate(rows):
                    for jg in range(0, JG_BLOCK, 2):
                        lo = r * PEER_NKEYS + (j0 + jg) * sub[0]
                        a = a_old[lo:lo + 2 * sub[0], cs]
                        act = a + a * lax.erf(a)
                        gg = jnp.concatenate([g[q][jg], g[q][jg + 1]], axis=0)
                        p_new[lo:lo + 2 * sub[0], cs] = (act * gg).astype(BF16)

    even = (a0_scr, a1_scr, p1_scr, p0_scr)
    odd = (a1_scr, a0_scr, p0_scr, p1_scr)
    last = pl.num_programs(1) - 1

    def run(refs, stages):
        for k in range(tb // MXU_N):
            pair_body(k, refs, stages)

    @pl.when(c == 0)
    def _():
        run(even, "fill")

    @pl.when(jnp.logical_and(c > 0, c % 2 == 0))
    def _():
        run(even, "all")

    @pl.when(jnp.logical_and(c < last, c % 2 == 1))
    def _():
        run(odd, "all")

    @pl.when(c == last)
    def _():
        run(odd, "drain")
        o_ref[...] = x1_ref[...] + ga2_ref[0] * acc_ref[...].T


def _experts(h2t, u, vt, s1, theta, e0, e1, x1, ga2, seq):
    d, t = h2t.shape
    tb = TB_EXP
    n_chunks = u.shape[0] // EC
    last = n_chunks - 1
    per_batch = seq // tb
    once = dict(pipeline_mode=pl.Buffered(1))
    tab = pl.BlockSpec((PEER_HEADS, PEER_NKEYS, tb), lambda i, c: (0, 0, i), **once)
    rowtab = pl.BlockSpec((PEER_HEADS, 1, ROWS_PER_STEP, tb),
                          lambda i, c: (0, jnp.clip(c - 1, 0, last), 0, i))
    return pl.pallas_call(
        _experts_kernel,
        grid=(t // tb, n_chunks + 2),
        in_specs=[pl.BlockSpec((d, tb), lambda i, c: (0, i)),
                  pl.BlockSpec((EC, d), lambda i, c: (jnp.minimum(c, last), 0)),
                  pl.BlockSpec((d, EC), lambda i, c: (0, jnp.clip(c - 2, 0, last))),
                  tab, rowtab, rowtab, tab,
                  pl.BlockSpec((tb, d), lambda i, c: (i, 0), **once),
                  pl.BlockSpec((1, 1, d), lambda i, c: (i // per_batch, 0, 0))],
        out_specs=pl.BlockSpec((tb, d), lambda i, c: (i, 0)),
        out_shape=jax.ShapeDtypeStruct((t, d), F32),
        scratch_shapes=[pltpu.VMEM((EC, tb), F32), pltpu.VMEM((EC, tb), F32),
                        pltpu.VMEM((EC, tb), BF16), pltpu.VMEM((EC, tb), BF16),
                        pltpu.VMEM((d, tb), F32)],
        compiler_params=_cparams(("arbitrary", "arbitrary")),
        name="experts",
    )(h2t, u, vt, s1, theta, e0, e1, x1, ga2)


def kernel(x, c, w_ada, b_ada, norm1_g, norm2_g, w_in, conv_dw, conv_b, conv_ln_g, conv_ln_b,
           w_conv_out, q_norm_g, k_norm_g, rel_bias, w_attn_out, w_out, peer_wq, peer_keys,
           peer_u, peer_v):
    bsz, s, d = x.shape
    depth = w_ada.shape[0]
    bd = jnp.asarray(np.kron(np.eye(N_HEADS), np.full((HEAD_DIM, HEAD_DIM), 1.0 / HEAD_DIM)), BF16)
    for l in range(depth):
        mod = _ada(c, w_ada[l], b_ada[l])
        sh1, sc1, ga1, sh2, sc2, ga2 = [m.reshape(bsz, 1, d) for m in jnp.split(mod, 6, axis=-1)]
        u, q, k, v, gc, ga, pu_bf, pvt_bf = _inproj(
            x, norm1_g[l].reshape(1, d), sc1, sh1, w_in[l].astype(BF16), bd,
            jnp.tile(q_norm_g[l], N_HEADS).reshape(1, D_ATTN),
            jnp.tile(k_norm_g[l], N_HEADS).reshape(1, D_ATTN), peer_u[l], peer_v[l])
        mc = _conv(u, conv_dw[l], conv_b[l].reshape(1, D_CONV), conv_ln_g[l].reshape(1, D_CONV),
                   conv_ln_b[l].reshape(1, D_CONV), w_conv_out[l].astype(BF16), gc)
        o = _attn(q, k, v, _bias_table(rel_bias[l]))
        x1, h2t = _merge(o, mc, ga, x, w_attn_out[l].astype(BF16), w_out[l].astype(BF16),
                         ga1, norm2_g[l].reshape(1, d), sc2, sh2)
        wqt = peer_wq[l].T.astype(BF16)
        keys = peer_keys[l].reshape(2 * PEER_HEADS, PEER_NKEYS, PEER_DH).astype(BF16)
        s1, theta, e0, e1 = _route(h2t, wqt, keys)
        out = _experts(h2t, pu_bf, pvt_bf, s1, theta, e0, e1, x1.reshape(bsz * s, d), ga2, s)
        x = out.reshape(bsz, s, d)
    return x
```

```python
import functools

import jax
import jax.numpy as jnp
import numpy as np
from jax import lax
from jax.experimental import pallas as pl
from jax.experimental.pallas import tpu as pltpu

F32 = jnp.float32
BF16 = jnp.bfloat16

D_MODEL = 1024
CHUNK = 64
N_HEADS = 8
HEAD_DIM = 64
D_ATTN = N_HEADS * HEAD_DIM
LEFT_CHUNKS = 8
REL_CLIP = 128
D_CONV = D_MODEL // 2
CONV_W = 31
PEER_HEADS = 8
PEER_NKEYS = 128
PEER_N = PEER_NKEYS * PEER_NKEYS
PEER_DH = 128
PEER_TOPK = 16
EPS = 1e-6
NEG_INF = -1e30
LOG2E = float(np.log2(np.e))

LANES = 128
SUBLANES = 8
MXU_N = 256
VMEM_LIMIT = 56 * 1024 * 1024

TM_IN = 512
TS_CONV = 512
HALO = 32
TQ = 256
NKB = 3
TM_MERGE = 512
TB_ROUTE = 512
TB_EXP = 1024
EC = 512
ROWS_PER_STEP = EC // PEER_NKEYS

def _cparams(sem):
    return pltpu.CompilerParams(dimension_semantics=sem, vmem_limit_bytes=VMEM_LIMIT)


def _ada_kernel(c_ref, w_ref, b_ref, win_ref, o_ref, winb_ref):
    c = c_ref[...]
    cond = c * jax.nn.sigmoid(c)
    o_ref[...] = jnp.dot(cond.astype(BF16), w_ref[...].astype(BF16),
                         preferred_element_type=F32) + b_ref[...]
    winb_ref[...] = win_ref[...].astype(BF16)


def _ada(c, w, b, w_in):
    bsz, d = c.shape
    n = w.shape[1]
    tn = 1024
    steps = n // tn
    n_in = w_in.shape[1]
    tin = n_in // steps
    return pl.pallas_call(
        _ada_kernel,
        grid=(steps,),
        in_specs=[pl.BlockSpec((bsz, d), lambda j: (0, 0)),
                  pl.BlockSpec((d, tn), lambda j: (0, j)),
                  pl.BlockSpec((1, tn), lambda j: (0, j)),
                  pl.BlockSpec((d, tin), lambda j: (0, j))],
        out_specs=[pl.BlockSpec((bsz, tn), lambda j: (0, j)),
                   pl.BlockSpec((d, tin), lambda j: (0, j))],
        out_shape=[jax.ShapeDtypeStruct((bsz, n), F32),
                   jax.ShapeDtypeStruct(w_in.shape, BF16)],
        compiler_params=_cparams(("arbitrary",)),
        name="ada",
    )(c, w, b.reshape(1, n), w_in)


def _head_rms(t, bd_ref, gain):
    t2 = t * t
    hi = t2.astype(BF16)
    lo = (t2 - hi.astype(F32)).astype(BF16)
    ms = (jnp.dot(hi, bd_ref[...], preferred_element_type=F32)
          + jnp.dot(lo, bd_ref[...], preferred_element_type=F32))
    return t * lax.rsqrt(ms + EPS) * gain


def _inproj_kernel(x_ref, g_ref, sc_ref, sh_ref, w_ref, bd_ref, qg_ref, kg_ref, pu_ref, pv_ref,
                   u_ref, q_ref, k_ref, v_ref, gc_ref, ga_ref, pub_ref, pvt_ref):
    x = x_ref[0]
    ms = jnp.mean(x * x, axis=-1, keepdims=True)
    h = (x * lax.rsqrt(ms + EPS)) * g_ref[...]
    h = h * (1.0 + sc_ref[0]) + sh_ref[0]
    hb = h.astype(BF16)

    def seg(lo, hi):
        return jnp.dot(hb, w_ref[:, lo:hi], preferred_element_type=F32)

    bounds = np.cumsum([0, D_CONV, D_CONV, D_ATTN, D_ATTN, D_ATTN, D_MODEL, D_MODEL])
    lo, hi = bounds[:-1], bounds[1:]
    a = seg(lo[0], hi[0])
    b = seg(lo[1], hi[1])
    q = seg(lo[2], hi[2])
    u_ref[0] = a * jax.nn.sigmoid(b)
    k = seg(lo[3], hi[3])
    q_ref[0] = (_head_rms(q, bd_ref, qg_ref[...]) * (HEAD_DIM ** -0.5 * LOG2E)).astype(BF16)
    v = seg(lo[4], hi[4])
    k_ref[0] = _head_rms(k, bd_ref, kg_ref[...]).astype(BF16)
    gc = seg(lo[5], hi[5])
    v_ref[0] = v.astype(BF16)
    pub_ref[...] = (pu_ref[...] * _INV_SQRT2).astype(BF16)
    ga = seg(lo[6], hi[6])
    pvt_ref[...] = pv_ref[...].T.astype(BF16)
    gc_ref[0] = jax.nn.sigmoid(gc)
    ga_ref[0] = jax.nn.sigmoid(ga)


def _inproj(x, g1, sc1, sh1, w_in, bd, qg, kg, pu, pv):
    bsz, s, d = x.shape
    tm = TM_IN
    n_in = w_in.shape[1]
    nb = s // tm
    n_e = pu.shape[0]
    er = n_e // (bsz * nb)
    row = lambda w: pl.BlockSpec((1, tm, w), lambda b, i: (b, i, 0))
    vec = lambda w: pl.BlockSpec((1, w), lambda b, i: (0, 0))
    mod = pl.BlockSpec((1, 1, d), lambda b, i: (b, 0, 0))
    erow = pl.BlockSpec((er, d), lambda b, i: (b * nb + i, 0))
    return pl.pallas_call(
        _inproj_kernel,
        grid=(bsz, nb),
        in_specs=[row(d), vec(d), mod, mod,
                  pl.BlockSpec((d, n_in), lambda b, i: (0, 0), pipeline_mode=pl.Buffered(1)),
                  pl.BlockSpec((D_ATTN, D_ATTN), lambda b, i: (0, 0)),
                  vec(D_ATTN), vec(D_ATTN), erow, erow],
        out_specs=[row(D_CONV), row(D_ATTN), row(D_ATTN), row(D_ATTN), row(d), row(d),
                   erow, pl.BlockSpec((d, er), lambda b, i: (0, b * nb + i))],
        out_shape=[jax.ShapeDtypeStruct((bsz, s, D_CONV), F32),
                   jax.ShapeDtypeStruct((bsz, s, D_ATTN), BF16),
                   jax.ShapeDtypeStruct((bsz, s, D_ATTN), BF16),
                   jax.ShapeDtypeStruct((bsz, s, D_ATTN), BF16),
                   jax.ShapeDtypeStruct((bsz, s, d), F32),
                   jax.ShapeDtypeStruct((bsz, s, d), F32),
                   jax.ShapeDtypeStruct((n_e, d), BF16),
                   jax.ShapeDtypeStruct((d, n_e), BF16)],
        compiler_params=_cparams(("arbitrary", "arbitrary")),
        name="inproj",
    )(x, g1, sc1, sh1, w_in, bd, qg, kg, pu, pv)


CONV_ROWS = 64


def _conv_kernel(u_ref, up_ref, dw_ref, cb_ref, lg_ref, lb_ref, w_ref, gc_ref,
                 o_ref, ext_ref, y_ref):
    i = pl.program_id(1)
    ts = u_ref.shape[1]
    prev = up_ref[0, ts - HALO:, :]
    ext_ref[0, 0:HALO, :] = jnp.where(i > 0, prev, 0.0)
    ext_ref[0, HALO:, :] = u_ref[0]
    n_sh = ts + HALO - SUBLANES
    for s in range(1, SUBLANES):
        for r0 in range(0, n_sh, CONV_ROWS):
            n = min(CONV_ROWS, n_sh - r0)
            ext_ref[s, r0:r0 + n, :] = ext_ref[0, r0 + s:r0 + s + n, :]
    base = HALO - (CONV_W - 1)
    for r0 in range(0, ts, CONV_ROWS):
        acc = jnp.zeros((CONV_ROWS, D_CONV), F32) + cb_ref[...]
        for w in range(CONV_W):
            s = (base + w) % SUBLANES
            a = r0 + base + w - s
            acc = acc + ext_ref[s, a:a + CONV_ROWS, :] * dw_ref[w:w + 1, :]
        y_ref[r0:r0 + CONV_ROWS, :] = acc
    y = y_ref[...]
    mu = jnp.mean(y, axis=-1, keepdims=True)
    yc = y - mu
    var = jnp.mean(yc * yc, axis=-1, keepdims=True)
    z = yc * lax.rsqrt(var + EPS) * lg_ref[...] + lb_ref[...]
    z = z * jax.nn.sigmoid(z)
    o = jnp.dot(z.astype(BF16), w_ref[...], preferred_element_type=F32)
    o_ref[0] = gc_ref[0] * o


def _conv(u, dw, cb, lg, lb, w_co, gc):
    bsz, s, dc = u.shape
    d = w_co.shape[1]
    ts = TS_CONV
    vec = lambda w: pl.BlockSpec((1, w), lambda b, i: (0, 0))
    return pl.pallas_call(
        _conv_kernel,
        grid=(bsz, s // ts),
        in_specs=[pl.BlockSpec((1, ts, dc), lambda b, i: (b, i, 0)),
                  pl.BlockSpec((1, ts, dc), lambda b, i: (b, jnp.maximum(i - 1, 0), 0)),
                  pl.BlockSpec((CONV_W, dc), lambda b, i: (0, 0)),
                  vec(dc), vec(dc), vec(dc),
                  pl.BlockSpec((dc, d), lambda b, i: (0, 0)),
                  pl.BlockSpec((1, ts, d), lambda b, i: (b, i, 0))],
        out_specs=pl.BlockSpec((1, ts, d), lambda b, i: (b, i, 0)),
        out_shape=jax.ShapeDtypeStruct((bsz, s, d), F32),
        scratch_shapes=[pltpu.VMEM((SUBLANES, ts + HALO, dc), F32), pltpu.VMEM((ts, dc), F32)],
        compiler_params=_cparams(("arbitrary", "arbitrary")),
        name="conv",
    )(u, u, dw, cb, lg, lb, w_co, gc)


def _attn_kernel(q_ref, k0_ref, k1_ref, k2_ref, v0_ref, v1_ref, v2_ref, bias_ref, o_ref):
    i = pl.program_id(1)
    k_refs = (k0_ref, k1_ref, k2_ref)
    v_refs = (v0_ref, v1_ref, v2_ref)
    low = lax.broadcasted_iota(jnp.int32, (TQ, LANES), 1) < HEAD_DIM

    n_heads_per_slab = LANES // HEAD_DIM

    def body(pens):
        def scores(h):
            lo = (h // n_heads_per_slab) * LANES
            q2 = q_ref[0, :, lo:lo + LANES]
            keep = low if h % n_heads_per_slab == 0 else jnp.logical_not(low)
            qh = jnp.where(keep, q2, jnp.zeros_like(q2))
            ss = []
            for j in range(NKB):
                s = lax.dot_general(qh, k_refs[j][0, :, lo:lo + LANES],
                                    (((1,), (1,)), ((), ())), preferred_element_type=F32)
                s = s + bias_ref[h, :, j * TQ:(j + 1) * TQ]
                ss.append(s if pens is None else s + pens[j])
            return ss

        def softmax_pv(h, ss):
            lo = (h // n_heads_per_slab) * LANES
            m = jnp.maximum(jnp.maximum(jnp.max(ss[0], axis=-1, keepdims=True),
                                        jnp.max(ss[1], axis=-1, keepdims=True)),
                            jnp.max(ss[2], axis=-1, keepdims=True))
            ps = [jnp.exp2(ss[j] - m) for j in range(NKB)]
            l = jnp.sum(ps[0] + ps[1] + ps[2], axis=-1, keepdims=True)
            p = jnp.concatenate([pj.astype(BF16) for pj in ps], axis=1)
            v = jnp.concatenate([v_refs[j][0, :, lo:lo + LANES] for j in range(NKB)], axis=0)
            return jnp.dot(p, v, preferred_element_type=F32) / l

        outs = []
        ss_next = scores(0)
        for h in range(N_HEADS):
            ss = ss_next
            if h + 1 < N_HEADS:
                ss_next = scores(h + 1)
            outs.append(softmax_pv(h, ss))
            if h % n_heads_per_slab == n_heads_per_slab - 1:
                lo = (h // n_heads_per_slab) * LANES
                o_ref[0, :, lo:lo + LANES] = jnp.where(low, outs[0], outs[1]).astype(BF16)
                outs = []

    @pl.when(i >= NKB - 1)
    def _():
        body(None)

    @pl.when(i < NKB - 1)
    def _():
        body([jnp.where(i - (NKB - 1) + j >= 0, 0.0, NEG_INF).astype(F32) for j in range(NKB)])


def _attn(q, k, v, bias):
    bsz, s, da = q.shape
    kspec = lambda j: pl.BlockSpec(
        (1, TQ, da), lambda b, i: (b, jnp.maximum(i - (NKB - 1) + j, 0), 0))
    return pl.pallas_call(
        _attn_kernel,
        grid=(bsz, s // TQ),
        in_specs=[pl.BlockSpec((1, TQ, da), lambda b, i: (b, i, 0)),
                  kspec(0), kspec(1), kspec(2), kspec(0), kspec(1), kspec(2),
                  pl.BlockSpec((N_HEADS, TQ, NKB * TQ), lambda b, i: (0, 0, 0))],
        out_specs=pl.BlockSpec((1, TQ, da), lambda b, i: (b, i, 0)),
        out_shape=jax.ShapeDtypeStruct((bsz, s, da), BF16),
        compiler_params=_cparams(("arbitrary", "arbitrary")),
        name="attn",
    )(q, k, k, k, v, v, v, bias)


def _bias_table(rel_bias, peer_wq):
    nk = NKB * TQ
    lw = TQ + nk - 1
    n_lo = (TQ - 1) - REL_CLIP
    n_hi = (nk - 1) - REL_CLIP
    w = jnp.concatenate([jnp.repeat(rel_bias[:, :1], n_lo, axis=1), rel_bias,
                         jnp.repeat(rel_bias[:, -1:], n_hi, axis=1)], axis=1).astype(F32)
    assert BIAS_ROW >= lw
    base = jnp.concatenate([w[:, nk - 1::-1], jnp.zeros((w.shape[0], BIAS_ROW - lw), F32),
                            w[:, lw - 1:nk - 1:-1]], axis=1) * LOG2E
    return pl.pallas_call(
        _bias_kernel,
        out_shape=[jax.ShapeDtypeStruct((N_HEADS, TQ, nk), F32),
                   jax.ShapeDtypeStruct(peer_wq.shape[::-1], BF16)],
        compiler_params=pltpu.CompilerParams(vmem_limit_bytes=VMEM_LIMIT),
        name="bias",
    )(base, peer_wq)


BIAS_ROW = 1024


def _bias_kernel(base_ref, wq_ref, o_ref, wqt_ref):
    wqt_ref[...] = wq_ref[...].T.astype(BF16)
    nk = NKB * TQ
    qi = lax.broadcasted_iota(jnp.int32, (TQ, nk), 0)
    kj = lax.broadcasted_iota(jnp.int32, (TQ, nk), 1)
    qc = qi // CHUNK + (NKB - 1) * TQ // CHUNK
    kc = kj // CHUNK
    band = (kc >= qc - LEFT_CHUNKS) & (kc <= qc)
    for h in range(N_HEADS):
        x = jnp.broadcast_to(base_ref[h:h + 1, :], (TQ, BIAS_ROW))
        y = pltpu.roll(x, 0, 1, stride=1, stride_axis=0)
        o_ref[h] = jnp.where(band, y[:, :nk], NEG_INF)


def _merge_kernel(o_ref, mc_ref, ga_ref, x_ref, wa_ref, wo_ref, ga1_ref, g2_ref, sc_ref, sh_ref,
                  x1_ref, h2t_ref):
    ya = jnp.dot(o_ref[0], wa_ref[...], preferred_element_type=F32)
    merged = mc_ref[0] + ga_ref[0] * ya
    y = jnp.dot(merged.astype(BF16), wo_ref[...], preferred_element_type=F32)
    x1 = x_ref[0] + ga1_ref[0] * y
    x1_ref[0] = x1
    ms = jnp.mean(x1 * x1, axis=-1, keepdims=True)
    h2 = (x1 * lax.rsqrt(ms + EPS)) * g2_ref[...]
    h2 = h2 * (1.0 + sc_ref[0]) + sh_ref[0]
    h2t_ref[...] = h2.T.astype(BF16)


def _merge(o, mc, ga, x, wa, wo, ga1, g2, sc2, sh2):
    bsz, s, d = x.shape
    tm = TM_MERGE
    nb = s // tm
    row = lambda w: pl.BlockSpec((1, tm, w), lambda b, i: (b, i, 0))
    mod = pl.BlockSpec((1, 1, d), lambda b, i: (b, 0, 0))
    return pl.pallas_call(
        _merge_kernel,
        grid=(bsz, nb),
        in_specs=[row(D_ATTN), row(d), row(d), row(d),
                  pl.BlockSpec((D_ATTN, d), lambda b, i: (0, 0)),
                  pl.BlockSpec((d, d), lambda b, i: (0, 0)),
                  mod, pl.BlockSpec((1, d), lambda b, i: (0, 0)), mod, mod],
        out_specs=[row(d), pl.BlockSpec((d, tm), lambda b, i: (0, b * nb + i))],
        out_shape=[jax.ShapeDtypeStruct((bsz, s, d), F32),
                   jax.ShapeDtypeStruct((d, bsz * s), BF16)],
        compiler_params=_cparams(("arbitrary", "arbitrary")),
        name="merge",
    )(o, mc, ga, x, wa, wo, ga1, g2, sc2, sh2)


_CAND = [(a, b) for a in range(PEER_TOPK) for b in range(PEER_TOPK)
         if (a + 1) * (b + 1) <= PEER_TOPK]


def _sort16_network():
    n, pairs, p = PEER_TOPK, [], 1
    while p < n:
        k = p
        while k >= 1:
            for j in range(k % p, n - k, 2 * k):
                for i in range(min(k, n - j - k)):
                    if (i + j) // (2 * p) == (i + j + k) // (2 * p):
                        pairs.append((i + j, i + j + k))
            k //= 2
        p *= 2
    return pairs


_SORT16 = _sort16_network()


def _top16_sorted(problems, store_row):
    sub = 8
    lvs = [[s[g * sub:(g + 1) * sub, :] for g in range(PEER_NKEYS // sub)] for s in problems]
    for a, b in _SORT16:
        for lv in lvs:
            lv[a], lv[b] = jnp.maximum(lv[a], lv[b]), jnp.minimum(lv[a], lv[b])
    for r in range(PEER_TOPK):
        for p, lv in enumerate(lvs):
            m = jnp.max(lv[0], axis=0, keepdims=True)
            store_row(p, r, m)
            eq = lv[0] == m
            for k in range(PEER_TOPK - 1 - r):
                lv[k] = jnp.where(eq, lv[k + 1], lv[k])


def _route_kernel(h2t_ref, wqt_ref, keys_ref, s1_ref, theta_ref, e0_ref, e1_ref,
                  s_scr, v_scr, tau_scr, zi_scr):
    tb = h2t_ref.shape[1]
    ncol = tb // LANES
    n_hp = 2 * PEER_HEADS
    def scores(h):
        rows = slice(2 * h * PEER_DH, 2 * (h + 1) * PEER_DH)
        qt = jnp.dot(wqt_ref[rows, :], h2t_ref[...], preferred_element_type=F32).astype(BF16)
        for p in range(2):
            s_scr[2 * h + p] = jnp.dot(keys_ref[2 * h + p], qt[p * PEER_DH:(p + 1) * PEER_DH, :],
                                       preferred_element_type=F32)

    def top16(h):
        for col in range(ncol):
            cs = slice(col * LANES, (col + 1) * LANES)
            rw = slice(h * ncol + col, h * ncol + col + 1)

            def store_row(p, r, m, rw=rw):
                v_scr[p, r, rw, :] = m

            _top16_sorted([s_scr[2 * h + p, :, cs] for p in range(2)], store_row)

    scores(0)
    for h in range(PEER_HEADS):
        if h + 1 < PEER_HEADS:
            scores(h + 1)
        top16(h)

    v0 = [v_scr[0, a] for a in range(PEER_TOPK)]
    v1 = [v_scr[1, b] for b in range(PEER_TOPK)]
    cand = [v0[a] + v1[b] for (a, b) in _CAND]
    top = cand[0]
    work = list(cand)
    for r in range(PEER_TOPK):
        c16 = functools.reduce(jnp.maximum, work)
        work = [jnp.where(w == c16, -jnp.inf, w) for w in work]
    c17 = functools.reduce(jnp.maximum, work)
    z = jnp.zeros_like(top)
    for c in cand:
        z = z + jnp.where(c >= c16, jnp.exp(c - top), 0.0)
    tau_scr[...] = 0.5 * (c16 + c17)
    zi_scr[...] = _INV_SQRT2 / z

    def stage3(it, carry):
        h = it // ncol
        col = it % ncol
        cs = pl.ds(pl.multiple_of(col * LANES, LANES), LANES)
        rw = pl.ds(h * ncol + col, 1)
        s0 = s_scr[2 * h, :, cs]
        s1 = s_scr[2 * h + 1, :, cs]
        theta = jnp.where(s0 >= v_scr[0, PEER_TOPK - 1, rw, :], tau_scr[rw, :] - s0, jnp.inf)
        e0 = jnp.exp(s0 - v_scr[0, 0, rw, :])
        for g in range(PEER_NKEYS // ROWS_PER_STEP):
            rows = slice(g * ROWS_PER_STEP, (g + 1) * ROWS_PER_STEP)
            theta_ref[h, g, :, cs] = theta[rows]
            e0_ref[h, g, :, cs] = e0[rows]
        s1_ref[h, :, cs] = jnp.where(s1 >= v_scr[1, PEER_TOPK - 1, rw, :], s1, -jnp.inf)
        e1_ref[h, :, cs] = jnp.exp(s1 - v_scr[1, 0, rw, :]) * zi_scr[rw, :]
        return carry

    lax.fori_loop(0, PEER_HEADS * ncol, stage3, 0)


def _route(h2t, wqt, keys):
    d, t = h2t.shape
    tb = TB_ROUTE
    ncol = tb // LANES
    n_hp = 2 * PEER_HEADS
    n_grp = PEER_NKEYS // ROWS_PER_STEP
    tab = pl.BlockSpec((PEER_HEADS, PEER_NKEYS, tb), lambda i: (0, 0, i))
    rowtab = pl.BlockSpec((PEER_HEADS, n_grp, ROWS_PER_STEP, tb), lambda i: (0, 0, 0, i))
    tab_shape = jax.ShapeDtypeStruct((PEER_HEADS, PEER_NKEYS, t), F32)
    rowtab_shape = jax.ShapeDtypeStruct((PEER_HEADS, n_grp, ROWS_PER_STEP, t), F32)
    return pl.pallas_call(
        _route_kernel,
        grid=(t // tb,),
        in_specs=[pl.BlockSpec((d, tb), lambda i: (0, i)),
                  pl.BlockSpec(wqt.shape, lambda i: (0, 0)),
                  pl.BlockSpec(keys.shape, lambda i: (0, 0, 0))],
        out_specs=[tab, rowtab, rowtab, tab],
        out_shape=[tab_shape, rowtab_shape, rowtab_shape, tab_shape],
        scratch_shapes=[pltpu.VMEM((n_hp, PEER_NKEYS, tb), F32),
                        pltpu.VMEM((2, PEER_TOPK, PEER_HEADS * ncol, LANES), F32),
                        pltpu.VMEM((PEER_HEADS * ncol, LANES), F32),
                        pltpu.VMEM((PEER_HEADS * ncol, LANES), F32)],
        compiler_params=_cparams(("arbitrary",)),
        name="route",
    )(h2t, wqt, keys)


_INV_SQRT2 = float(1.0 / np.sqrt(2.0))
K_PIECE = 256
K_PIECE_A = 256
ROW_BLOCK = 4
JG_BLOCK = 4


def _experts_kernel(h2t_ref, u_ref, vt_ref, s1_ref, theta_ref, e0_ref, e1_ref,
                    x1_ref, ga2_ref, o_ref, a0_scr, a1_scr, p0_scr, p1_scr, acc_ref):
    c = pl.program_id(1)
    tb = h2t_ref.shape[1]

    @pl.when(c == 0)
    def _():
        acc_ref[...] = jnp.zeros_like(acc_ref)
        p1_scr[...] = jnp.zeros_like(p1_scr)

    sub = (8, LANES)
    n_jg = PEER_NKEYS // sub[0]
    zero = jnp.zeros(sub, F32)

    def pair_body(k, refs, stages):
        a_new, a_old, p_new, p_old = refs
        base = k * MXU_N
        ps = pl.ds(base, MXU_N)

        def mm_a(kc):
            ks = slice(kc * K_PIECE_A, (kc + 1) * K_PIECE_A)
            part = jnp.dot(u_ref[:, ks], h2t_ref[ks, ps], preferred_element_type=F32)
            if kc == 0:
                a_new[:, ps] = part
            else:
                a_new[:, ps] += part

        def mm_acc(kc, mh):
            ks = slice(kc * K_PIECE, (kc + 1) * K_PIECE)
            ms = slice(mh * EC, (mh + 1) * EC)
            acc_ref[ms, ps] += jnp.dot(vt_ref[ms, ks], p_old[ks, ps],
                                       preferred_element_type=F32)

        a_pieces = [functools.partial(mm_a, kc) for kc in range(D_MODEL // K_PIECE_A)]
        acc_pieces = [functools.partial(mm_acc, kc, mh) for kc in range(EC // K_PIECE)
                      for mh in range(D_MODEL // EC)]
        if stages == "fill":
            for piece in a_pieces:
                piece()
            return
        if stages == "drain":
            for piece in acc_pieces:
                piece()
            return
        mm_pieces = a_pieces + acc_pieces
        blocks = [(half, ip, j0) for half in range(MXU_N // LANES)
                  for ip in range(ROWS_PER_STEP // ROW_BLOCK)
                  for j0 in range(0, n_jg, JG_BLOCK)]
        every = len(blocks) // len(mm_pieces)
        for bi, (half, ip, j0) in enumerate(blocks):
            if bi % every == 0 and bi // every < len(mm_pieces):
                mm_pieces[bi // every]()
            cs = pl.ds(base + half * LANES, LANES)
            if True:
                rows = tuple(range(ROW_BLOCK * ip, ROW_BLOCK * (ip + 1)))
                g = [[zero] * JG_BLOCK for _ in rows]
                for h in range(PEER_HEADS):
                    th = [jnp.broadcast_to(theta_ref[h, 0, r:r + 1, cs], sub) for r in rows]
                    e0 = [jnp.broadcast_to(e0_ref[h, 0, r:r + 1, cs], sub) for r in rows]
                    for jg in range(JG_BLOCK):
                        js = slice((j0 + jg) * sub[0], (j0 + jg + 1) * sub[0])
                        s1 = s1_ref[h, js, cs]
                        e1 = e1_ref[h, js, cs]
                        for q in range(ROW_BLOCK):
                            g[q][jg] = g[q][jg] + jnp.where(s1 >= th[q], e1, zero) * e0[q]
                for q, r in enumerate(rows):
                    for jg in range(0, JG_BLOCK, 2):
                        lo = r * PEER_NKEYS + (j0 + jg) * sub[0]
                        a = a_old[lo:lo + 2 * sub[0], cs]
                        act = a + a * lax.erf(a)
                        gg = jnp.concatenate([g[q][jg], g[q][jg + 1]], axis=0)
                        p_new[lo:lo + 2 * sub[0], cs] = (act * gg).astype(BF16)

    even = (a0_scr, a1_scr, p1_scr, p0_scr)
    odd = (a1_scr, a0_scr, p0_scr, p1_scr)
    last = pl.num_programs(1) - 1

    def run(refs, stages):
        for k in range(tb // MXU_N):
            pair_body(k, refs, stages)

    @pl.when(c == 0)
    def _():
        run(even, "fill")

    @pl.when(jnp.logical_and(c > 0, c % 2 == 0))
    def _():
        run(even, "all")

    @pl.when(jnp.logical_and(c < last, c % 2 == 1))
    def _():
        run(odd, "all")

    @pl.when(c == last)
    def _():
        run(odd, "drain")
        o_ref[...] = x1_ref[...] + ga2_ref[0] * acc_ref[...].T


def _experts(h2t, u, vt, s1, theta, e0, e1, x1, ga2, seq):
    d, t = h2t.shape
    tb = TB_EXP
    n_chunks = u.shape[0] // EC
    last = n_chunks - 1
    per_batch = seq // tb
    once = dict(pipeline_mode=pl.Buffered(1))
    tab = pl.BlockSpec((PEER_HEADS, PEER_NKEYS, tb), lambda i, c: (0, 0, i), **once)
    rowtab = pl.BlockSpec((PEER_HEADS, 1, ROWS_PER_STEP, tb),
                          lambda i, c: (0, jnp.clip(c - 1, 0, last), 0, i))
    return pl.pallas_call(
        _experts_kernel,
        grid=(t // tb, n_chunks + 2),
        in_specs=[pl.BlockSpec((d, tb), lambda i, c: (0, i)),
                  pl.BlockSpec((EC, d), lambda i, c: (jnp.minimum(c, last), 0)),
                  pl.BlockSpec((d, EC), lambda i, c: (0, jnp.clip(c - 2, 0, last))),
                  tab, rowtab, rowtab, tab,
                  pl.BlockSpec((tb, d), lambda i, c: (i, 0), **once),
                  pl.BlockSpec((1, 1, d), lambda i, c: (i // per_batch, 0, 0))],
        out_specs=pl.BlockSpec((tb, d), lambda i, c: (i, 0)),
        out_shape=jax.ShapeDtypeStruct((t, d), F32),
        scratch_shapes=[pltpu.VMEM((EC, tb), F32), pltpu.VMEM((EC, tb), F32),
                        pltpu.VMEM((EC, tb), BF16), pltpu.VMEM((EC, tb), BF16),
                        pltpu.VMEM((d, tb), F32)],
        compiler_params=_cparams(("arbitrary", "arbitrary")),
        name="experts",
    )(h2t, u, vt, s1, theta, e0, e1, x1, ga2)


def kernel(x, c, w_ada, b_ada, norm1_g, norm2_g, w_in, conv_dw, conv_b, conv_ln_g, conv_ln_b,
           w_conv_out, q_norm_g, k_norm_g, rel_bias, w_attn_out, w_out, peer_wq, peer_keys,
           peer_u, peer_v):
    bsz, s, d = x.shape
    depth = w_ada.shape[0]
    bd = jnp.asarray(np.kron(np.eye(N_HEADS), np.full((HEAD_DIM, HEAD_DIM), 1.0 / HEAD_DIM)), BF16)
    for l in range(depth):
        mod, w_in_bf = _ada(c, w_ada[l], b_ada[l], w_in[l])
        sh1, sc1, ga1, sh2, sc2, ga2 = [m.reshape(bsz, 1, d) for m in jnp.split(mod, 6, axis=-1)]
        u, q, k, v, gc, ga, pu_bf, pvt_bf = _inproj(
            x, norm1_g[l].reshape(1, d), sc1, sh1, w_in_bf, bd,
            jnp.tile(q_norm_g[l], N_HEADS).reshape(1, D_ATTN),
            jnp.tile(k_norm_g[l], N_HEADS).reshape(1, D_ATTN), peer_u[l], peer_v[l])
        mc = _conv(u, conv_dw[l], conv_b[l].reshape(1, D_CONV), conv_ln_g[l].reshape(1, D_CONV),
                   conv_ln_b[l].reshape(1, D_CONV), w_conv_out[l].astype(BF16), gc)
        bias, wqt = _bias_table(rel_bias[l], peer_wq[l])
        o = _attn(q, k, v, bias)
        x1, h2t = _merge(o, mc, ga, x, w_attn_out[l].astype(BF16), w_out[l].astype(BF16),
                         ga1, norm2_g[l].reshape(1, d), sc2, sh2)
        keys = peer_keys[l].reshape(2 * PEER_HEADS, PEER_NKEYS, PEER_DH).astype(BF16)
        s1, theta, e0, e1 = _route(h2t, wqt, keys)
        out = _experts(h2t, pu_bf, pvt_bf, s1, theta, e0, e1, x1.reshape(bsz * s, d), ga2, s)
        x = out.reshape(bsz, s, d)
    return x
```

```python
import functools

import jax
import jax.numpy as jnp
import numpy as np
from jax import lax
from jax.experimental import pallas as pl
from jax.experimental.pallas import tpu as pltpu

F32 = jnp.float32
BF16 = jnp.bfloat16

D_MODEL = 1024
CHUNK = 64
N_HEADS = 8
HEAD_DIM = 64
D_ATTN = N_HEADS * HEAD_DIM
LEFT_CHUNKS = 8
REL_CLIP = 128
D_CONV = D_MODEL // 2
CONV_W = 31
PEER_HEADS = 8
PEER_NKEYS = 128
PEER_N = PEER_NKEYS * PEER_NKEYS
PEER_DH = 128
PEER_TOPK = 16
EPS = 1e-6
NEG_INF = -1e30
LOG2E = float(np.log2(np.e))

LANES = 128
SUBLANES = 8
MXU_N = 256
VMEM_LIMIT = 56 * 1024 * 1024

TM_IN = 512
TS_CONV = 512
HALO = 32
TQ = 256
NKB = 3
TM_MERGE = 512
TB_ROUTE = 512
TB_EXP = 1024
EC = 512
ROWS_PER_STEP = EC // PEER_NKEYS

def _cparams(sem):
    return pltpu.CompilerParams(dimension_semantics=sem, vmem_limit_bytes=VMEM_LIMIT)


def _ada_kernel(c_ref, w_ref, b_ref, win_ref, o_ref, winb_ref):
    c = c_ref[...]
    cond = c * jax.nn.sigmoid(c)
    o_ref[...] = jnp.dot(cond.astype(BF16), w_ref[...].astype(BF16),
                         preferred_element_type=F32) + b_ref[...]
    winb_ref[...] = win_ref[...].astype(BF16)


def _ada(c, w, b, w_in):
    bsz, d = c.shape
    n = w.shape[1]
    tn = 1024
    steps = n // tn
    n_in = w_in.shape[1]
    tin = n_in // steps
    return pl.pallas_call(
        _ada_kernel,
        grid=(steps,),
        in_specs=[pl.BlockSpec((bsz, d), lambda j: (0, 0)),
                  pl.BlockSpec((d, tn), lambda j: (0, j)),
                  pl.BlockSpec((1, tn), lambda j: (0, j)),
                  pl.BlockSpec((d, tin), lambda j: (0, j))],
        out_specs=[pl.BlockSpec((bsz, tn), lambda j: (0, j)),
                   pl.BlockSpec((d, tin), lambda j: (0, j))],
        out_shape=[jax.ShapeDtypeStruct((bsz, n), F32),
                   jax.ShapeDtypeStruct(w_in.shape, BF16)],
        compiler_params=_cparams(("arbitrary",)),
        name="ada",
    )(c, w, b.reshape(1, n), w_in)


def _head_rms(t, bd_ref, gain):
    t2 = t * t
    hi = t2.astype(BF16)
    lo = (t2 - hi.astype(F32)).astype(BF16)
    ms = (jnp.dot(hi, bd_ref[...], preferred_element_type=F32)
          + jnp.dot(lo, bd_ref[...], preferred_element_type=F32))
    return t * lax.rsqrt(ms + EPS) * gain


def _inproj_kernel(x_ref, g_ref, sc_ref, sh_ref, w_ref, bd_ref, qg_ref, kg_ref, pu_ref, pv_ref,
                   u_ref, q_ref, k_ref, v_ref, gc_ref, ga_ref, pub_ref, pvt_ref):
    x = x_ref[0]
    ms = jnp.mean(x * x, axis=-1, keepdims=True)
    h = (x * lax.rsqrt(ms + EPS)) * g_ref[...]
    h = h * (1.0 + sc_ref[0]) + sh_ref[0]
    hb = h.astype(BF16)

    def seg(lo, hi):
        return jnp.dot(hb, w_ref[:, lo:hi], preferred_element_type=F32)

    bounds = np.cumsum([0, D_CONV, D_CONV, D_ATTN, D_ATTN, D_ATTN, D_MODEL, D_MODEL])
    lo, hi = bounds[:-1], bounds[1:]
    a = seg(lo[0], hi[0])
    b = seg(lo[1], hi[1])
    q = seg(lo[2], hi[2])
    u_ref[0] = a * jax.nn.sigmoid(b)
    k = seg(lo[3], hi[3])
    q_ref[0] = (_head_rms(q, bd_ref, qg_ref[...]) * (HEAD_DIM ** -0.5 * LOG2E)).astype(BF16)
    v = seg(lo[4], hi[4])
    k_ref[0] = _head_rms(k, bd_ref, kg_ref[...]).astype(BF16)
    gc = seg(lo[5], hi[5])
    v_ref[0] = v.astype(BF16)
    pub_ref[...] = (pu_ref[...] * _INV_SQRT2).astype(BF16)
    ga = seg(lo[6], hi[6])
    pvt_ref[...] = pv_ref[...].T.astype(BF16)
    gc_ref[0] = jax.nn.sigmoid(gc)
    ga_ref[0] = jax.nn.sigmoid(ga)


def _inproj(x, g1, sc1, sh1, w_in, bd, qg, kg, pu, pv):
    bsz, s, d = x.shape
    tm = TM_IN
    n_in = w_in.shape[1]
    nb = s // tm
    n_e = pu.shape[0]
    er = n_e // (bsz * nb)
    row = lambda w: pl.BlockSpec((1, tm, w), lambda b, i: (b, i, 0))
    vec = lambda w: pl.BlockSpec((1, w), lambda b, i: (0, 0))
    mod = pl.BlockSpec((1, 1, d), lambda b, i: (b, 0, 0))
    erow = pl.BlockSpec((er, d), lambda b, i: (b * nb + i, 0))
    return pl.pallas_call(
        _inproj_kernel,
        grid=(bsz, nb),
        in_specs=[row(d), vec(d), mod, mod,
                  pl.BlockSpec((d, n_in), lambda b, i: (0, 0), pipeline_mode=pl.Buffered(1)),
                  pl.BlockSpec((D_ATTN, D_ATTN), lambda b, i: (0, 0)),
                  vec(D_ATTN), vec(D_ATTN), erow, erow],
        out_specs=[row(D_CONV), row(D_ATTN), row(D_ATTN), row(D_ATTN), row(d), row(d),
                   erow, pl.BlockSpec((d, er), lambda b, i: (0, b * nb + i))],
        out_shape=[jax.ShapeDtypeStruct((bsz, s, D_CONV), F32),
                   jax.ShapeDtypeStruct((bsz, s, D_ATTN), BF16),
                   jax.ShapeDtypeStruct((bsz, s, D_ATTN), BF16),
                   jax.ShapeDtypeStruct((bsz, s, D_ATTN), BF16),
                   jax.ShapeDtypeStruct((bsz, s, d), F32),
                   jax.ShapeDtypeStruct((bsz, s, d), F32),
                   jax.ShapeDtypeStruct((n_e, d), BF16),
                   jax.ShapeDtypeStruct((d, n_e), BF16)],
        compiler_params=_cparams(("arbitrary", "arbitrary")),
        name="inproj",
    )(x, g1, sc1, sh1, w_in, bd, qg, kg, pu, pv)


CONV_ROWS = 64


def _conv_kernel(u_ref, up_ref, dw_ref, cb_ref, lg_ref, lb_ref, w_ref, gc_ref,
                 o_ref, ext_ref, y_ref):
    i = pl.program_id(1)
    ts = u_ref.shape[1]
    prev = up_ref[0, ts - HALO:, :]
    ext_ref[0, 0:HALO, :] = jnp.where(i > 0, prev, 0.0)
    ext_ref[0, HALO:, :] = u_ref[0]
    n_sh = ts + HALO - SUBLANES
    for s in range(1, SUBLANES):
        for r0 in range(0, n_sh, CONV_ROWS):
            n = min(CONV_ROWS, n_sh - r0)
            ext_ref[s, r0:r0 + n, :] = ext_ref[0, r0 + s:r0 + s + n, :]
    base = HALO - (CONV_W - 1)
    for r0 in range(0, ts, CONV_ROWS):
        acc = jnp.zeros((CONV_ROWS, D_CONV), F32) + cb_ref[...]
        for w in range(CONV_W):
            s = (base + w) % SUBLANES
            a = r0 + base + w - s
            acc = acc + ext_ref[s, a:a + CONV_ROWS, :] * dw_ref[w:w + 1, :]
        y_ref[r0:r0 + CONV_ROWS, :] = acc
    y = y_ref[...]
    mu = jnp.mean(y, axis=-1, keepdims=True)
    yc = y - mu
    var = jnp.mean(yc * yc, axis=-1, keepdims=True)
    z = yc * lax.rsqrt(var + EPS) * lg_ref[...] + lb_ref[...]
    z = z * jax.nn.sigmoid(z)
    o = jnp.dot(z.astype(BF16), w_ref[...], preferred_element_type=F32)
    o_ref[0] = gc_ref[0] * o


def _conv(u, dw, cb, lg, lb, w_co, gc):
    bsz, s, dc = u.shape
    d = w_co.shape[1]
    ts = TS_CONV
    vec = lambda w: pl.BlockSpec((1, w), lambda b, i: (0, 0))
    return pl.pallas_call(
        _conv_kernel,
        grid=(bsz, s // ts),
        in_specs=[pl.BlockSpec((1, ts, dc), lambda b, i: (b, i, 0)),
                  pl.BlockSpec((1, ts, dc), lambda b, i: (b, jnp.maximum(i - 1, 0), 0)),
                  pl.BlockSpec((CONV_W, dc), lambda b, i: (0, 0)),
                  vec(dc), vec(dc), vec(dc),
                  pl.BlockSpec((dc, d), lambda b, i: (0, 0)),
                  pl.BlockSpec((1, ts, d), lambda b, i: (b, i, 0))],
        out_specs=pl.BlockSpec((1, ts, d), lambda b, i: (b, i, 0)),
        out_shape=jax.ShapeDtypeStruct((bsz, s, d), F32),
        scratch_shapes=[pltpu.VMEM((SUBLANES, ts + HALO, dc), F32), pltpu.VMEM((ts, dc), F32)],
        compiler_params=_cparams(("arbitrary", "arbitrary")),
        name="conv",
    )(u, u, dw, cb, lg, lb, w_co, gc)


def _attn_kernel(q_ref, k0_ref, k1_ref, k2_ref, v0_ref, v1_ref, v2_ref, bias_ref, o_ref):
    i = pl.program_id(1)
    k_refs = (k0_ref, k1_ref, k2_ref)
    v_refs = (v0_ref, v1_ref, v2_ref)
    low = lax.broadcasted_iota(jnp.int32, (TQ, LANES), 1) < HEAD_DIM

    n_heads_per_slab = LANES // HEAD_DIM

    def body(pens):
        def scores(h):
            lo = (h // n_heads_per_slab) * LANES
            q2 = q_ref[0, :, lo:lo + LANES]
            keep = low if h % n_heads_per_slab == 0 else jnp.logical_not(low)
            qh = jnp.where(keep, q2, jnp.zeros_like(q2))
            ss = []
            for j in range(NKB):
                s = lax.dot_general(qh, k_refs[j][0, :, lo:lo + LANES],
                                    (((1,), (1,)), ((), ())), preferred_element_type=F32)
                s = s + bias_ref[h, :, j * TQ:(j + 1) * TQ]
                ss.append(s if pens is None else s + pens[j])
            return ss

        def softmax_pv(h, ss):
            lo = (h // n_heads_per_slab) * LANES
            m = jnp.maximum(jnp.maximum(jnp.max(ss[0], axis=-1, keepdims=True),
                                        jnp.max(ss[1], axis=-1, keepdims=True)),
                            jnp.max(ss[2], axis=-1, keepdims=True))
            ps = [jnp.exp2(ss[j] - m) for j in range(NKB)]
            l = jnp.sum(ps[0] + ps[1] + ps[2], axis=-1, keepdims=True)
            p = jnp.concatenate([pj.astype(BF16) for pj in ps], axis=1)
            v = jnp.concatenate([v_refs[j][0, :, lo:lo + LANES] for j in range(NKB)], axis=0)
            return jnp.dot(p, v, preferred_element_type=F32) / l

        outs = []
        ss_next = scores(0)
        for h in range(N_HEADS):
            ss = ss_next
            if h + 1 < N_HEADS:
                ss_next = scores(h + 1)
            outs.append(softmax_pv(h, ss))
            if h % n_heads_per_slab == n_heads_per_slab - 1:
                lo = (h // n_heads_per_slab) * LANES
                o_ref[0, :, lo:lo + LANES] = jnp.where(low, outs[0], outs[1]).astype(BF16)
                outs = []

    @pl.when(i >= NKB - 1)
    def _():
        body(None)

    @pl.when(i < NKB - 1)
    def _():
        body([jnp.where(i - (NKB - 1) + j >= 0, 0.0, NEG_INF).astype(F32) for j in range(NKB)])


def _attn(q, k, v, bias):
    bsz, s, da = q.shape
    kspec = lambda j: pl.BlockSpec(
        (1, TQ, da), lambda b, i: (b, jnp.maximum(i - (NKB - 1) + j, 0), 0))
    return pl.pallas_call(
        _attn_kernel,
        grid=(bsz, s // TQ),
        in_specs=[pl.BlockSpec((1, TQ, da), lambda b, i: (b, i, 0)),
                  kspec(0), kspec(1), kspec(2), kspec(0), kspec(1), kspec(2),
                  pl.BlockSpec((N_HEADS, TQ, NKB * TQ), lambda b, i: (0, 0, 0))],
        out_specs=pl.BlockSpec((1, TQ, da), lambda b, i: (b, i, 0)),
        out_shape=jax.ShapeDtypeStruct((bsz, s, da), BF16),
        compiler_params=_cparams(("arbitrary", "arbitrary")),
        name="attn",
    )(q, k, k, k, v, v, v, bias)


def _bias_table(rel_bias, peer_wq):
    nk = NKB * TQ
    lw = TQ + nk - 1
    n_lo = (TQ - 1) - REL_CLIP
    n_hi = (nk - 1) - REL_CLIP
    w = jnp.concatenate([jnp.repeat(rel_bias[:, :1], n_lo, axis=1), rel_bias,
                         jnp.repeat(rel_bias[:, -1:], n_hi, axis=1)], axis=1).astype(F32)
    assert BIAS_ROW >= lw
    base = jnp.concatenate([w[:, nk - 1::-1], jnp.zeros((w.shape[0], BIAS_ROW - lw), F32),
                            w[:, lw - 1:nk - 1:-1]], axis=1) * LOG2E
    return pl.pallas_call(
        _bias_kernel,
        out_shape=[jax.ShapeDtypeStruct((N_HEADS, TQ, nk), F32),
                   jax.ShapeDtypeStruct(peer_wq.shape[::-1], BF16)],
        compiler_params=pltpu.CompilerParams(vmem_limit_bytes=VMEM_LIMIT),
        name="bias",
    )(base, peer_wq)


BIAS_ROW = 1024


def _bias_kernel(base_ref, wq_ref, o_ref, wqt_ref):
    wqt_ref[...] = wq_ref[...].T.astype(BF16)
    nk = NKB * TQ
    qi = lax.broadcasted_iota(jnp.int32, (TQ, nk), 0)
    kj = lax.broadcasted_iota(jnp.int32, (TQ, nk), 1)
    qc = qi // CHUNK + (NKB - 1) * TQ // CHUNK
    kc = kj // CHUNK
    band = (kc >= qc - LEFT_CHUNKS) & (kc <= qc)
    for h in range(N_HEADS):
        x = jnp.broadcast_to(base_ref[h:h + 1, :], (TQ, BIAS_ROW))
        y = pltpu.roll(x, 0, 1, stride=1, stride_axis=0)
        o_ref[h] = jnp.where(band, y[:, :nk], NEG_INF)


def _merge_kernel(o_ref, mc_ref, ga_ref, x_ref, wa_ref, wo_ref, ga1_ref, g2_ref, sc_ref, sh_ref,
                  x1_ref, h2t_ref):
    ya = jnp.dot(o_ref[0], wa_ref[...], preferred_element_type=F32)
    merged = mc_ref[0] + ga_ref[0] * ya
    y = jnp.dot(merged.astype(BF16), wo_ref[...], preferred_element_type=F32)
    x1 = x_ref[0] + ga1_ref[0] * y
    x1_ref[0] = x1
    ms = jnp.mean(x1 * x1, axis=-1, keepdims=True)
    h2 = (x1 * lax.rsqrt(ms + EPS)) * g2_ref[...]
    h2 = h2 * (1.0 + sc_ref[0]) + sh_ref[0]
    h2t_ref[...] = h2.T.astype(BF16)


def _merge(o, mc, ga, x, wa, wo, ga1, g2, sc2, sh2):
    bsz, s, d = x.shape
    tm = TM_MERGE
    nb = s // tm
    row = lambda w: pl.BlockSpec((1, tm, w), lambda b, i: (b, i, 0))
    mod = pl.BlockSpec((1, 1, d), lambda b, i: (b, 0, 0))
    return pl.pallas_call(
        _merge_kernel,
        grid=(bsz, nb),
        in_specs=[row(D_ATTN), row(d), row(d), row(d),
                  pl.BlockSpec((D_ATTN, d), lambda b, i: (0, 0)),
                  pl.BlockSpec((d, d), lambda b, i: (0, 0)),
                  mod, pl.BlockSpec((1, d), lambda b, i: (0, 0)), mod, mod],
        out_specs=[row(d), pl.BlockSpec((d, tm), lambda b, i: (0, b * nb + i))],
        out_shape=[jax.ShapeDtypeStruct((bsz, s, d), F32),
                   jax.ShapeDtypeStruct((d, bsz * s), BF16)],
        compiler_params=_cparams(("arbitrary", "arbitrary")),
        name="merge",
    )(o, mc, ga, x, wa, wo, ga1, g2, sc2, sh2)


_CAND = [(a, b) for a in range(PEER_TOPK) for b in range(PEER_TOPK)
         if (a + 1) * (b + 1) <= PEER_TOPK]


def _sort16_network():
    n, pairs, p = PEER_TOPK, [], 1
    while p < n:
        k = p
        while k >= 1:
            for j in range(k % p, n - k, 2 * k):
                for i in range(min(k, n - j - k)):
                    if (i + j) // (2 * p) == (i + j + k) // (2 * p):
                        pairs.append((i + j, i + j + k))
            k //= 2
        p *= 2
    return pairs


_SORT16 = _sort16_network()


def _top16_sorted(problems, store_row):
    sub = 8
    lvs = [[s[g * sub:(g + 1) * sub, :] for g in range(PEER_NKEYS // sub)] for s in problems]
    for a, b in _SORT16:
        for lv in lvs:
            lv[a], lv[b] = jnp.maximum(lv[a], lv[b]), jnp.minimum(lv[a], lv[b])
    for r in range(PEER_TOPK):
        for p, lv in enumerate(lvs):
            m = jnp.max(lv[0], axis=0, keepdims=True)
            store_row(p, r, m)
            eq = lv[0] == m
            for k in range(PEER_TOPK - 1 - r):
                lv[k] = jnp.where(eq, lv[k + 1], lv[k])


def _route_kernel(h2t_ref, wqt_ref, keys_ref, s1_ref, theta_ref, e0_ref, e1_ref,
                  s_scr, v_scr, tau_scr, zi_scr):
    tb = h2t_ref.shape[1]
    ncol = tb // LANES

    def scores(h):
        rows = slice(2 * h * PEER_DH, 2 * (h + 1) * PEER_DH)
        qt = jnp.dot(wqt_ref[rows, :], h2t_ref[...], preferred_element_type=F32).astype(BF16)
        for p in range(2):
            s_scr[2 * h + p] = jnp.dot(keys_ref[2 * h + p], qt[p * PEER_DH:(p + 1) * PEER_DH, :],
                                       preferred_element_type=F32)

    def top16(h):
        for col in range(ncol):
            cs = slice(col * LANES, (col + 1) * LANES)
            rw = slice(h * ncol + col, h * ncol + col + 1)

            def store_row(p, r, m, rw=rw):
                v_scr[p, r, rw, :] = m

            _top16_sorted([s_scr[2 * h + p, :, cs] for p in range(2)], store_row)

    scores(0)
    for h in range(PEER_HEADS):
        if h + 1 < PEER_HEADS:
            scores(h + 1)
        top16(h)

    v0 = [v_scr[0, a] for a in range(PEER_TOPK)]
    v1 = [v_scr[1, b] for b in range(PEER_TOPK)]
    cand = [v0[a] + v1[b] for (a, b) in _CAND]
    top = cand[0]
    work = list(cand)
    for r in range(PEER_TOPK):
        c16 = functools.reduce(jnp.maximum, work)
        work = [jnp.where(w == c16, -jnp.inf, w) for w in work]
    c17 = functools.reduce(jnp.maximum, work)
    z = jnp.zeros_like(top)
    for c in cand:
        z = z + jnp.where(c >= c16, jnp.exp(c - top), 0.0)
    tau_scr[...] = 0.5 * (c16 + c17)
    zi_scr[...] = _INV_SQRT2 / z

    def stage3(it, carry):
        h = it // ncol
        col = it % ncol
        cs = pl.ds(pl.multiple_of(col * LANES, LANES), LANES)
        rw = pl.ds(h * ncol + col, 1)
        s0 = s_scr[2 * h, :, cs]
        s1 = s_scr[2 * h + 1, :, cs]
        theta = jnp.where(s0 >= v_scr[0, PEER_TOPK - 1, rw, :], tau_scr[rw, :] - s0, jnp.inf)
        e0 = jnp.exp(s0 - v_scr[0, 0, rw, :])
        for g in range(PEER_NKEYS // ROWS_PER_STEP):
            rows = slice(g * ROWS_PER_STEP, (g + 1) * ROWS_PER_STEP)
            theta_ref[h, g, :, cs] = theta[rows]
            e0_ref[h, g, :, cs] = e0[rows]
        s1_ref[h, :, cs] = jnp.where(s1 >= v_scr[1, PEER_TOPK - 1, rw, :], s1, -jnp.inf)
        e1_ref[h, :, cs] = jnp.exp(s1 - v_scr[1, 0, rw, :]) * zi_scr[rw, :]
        return carry

    lax.fori_loop(0, PEER_HEADS * ncol, stage3, 0)


def _route(h2t, wqt, keys):
    d, t = h2t.shape
    tb = TB_ROUTE
    ncol = tb // LANES
    n_hp = 2 * PEER_HEADS
    n_grp = PEER_NKEYS // ROWS_PER_STEP
    tab = pl.BlockSpec((PEER_HEADS, PEER_NKEYS, tb), lambda i: (0, 0, i))
    rowtab = pl.BlockSpec((PEER_HEADS, n_grp, ROWS_PER_STEP, tb), lambda i: (0, 0, 0, i))
    tab_shape = jax.ShapeDtypeStruct((PEER_HEADS, PEER_NKEYS, t), F32)
    rowtab_shape = jax.ShapeDtypeStruct((PEER_HEADS, n_grp, ROWS_PER_STEP, t), F32)
    return pl.pallas_call(
        _route_kernel,
        grid=(t // tb,),
        in_specs=[pl.BlockSpec((d, tb), lambda i: (0, i)),
                  pl.BlockSpec(wqt.shape, lambda i: (0, 0)),
                  pl.BlockSpec(keys.shape, lambda i: (0, 0, 0))],
        out_specs=[tab, rowtab, rowtab, tab],
        out_shape=[tab_shape, rowtab_shape, rowtab_shape, tab_shape],
        scratch_shapes=[pltpu.VMEM((n_hp, PEER_NKEYS, tb), F32),
                        pltpu.VMEM((2, PEER_TOPK, PEER_HEADS * ncol, LANES), F32),
                        pltpu.VMEM((PEER_HEADS * ncol, LANES), F32),
                        pltpu.VMEM((PEER_HEADS * ncol, LANES), F32)],
        compiler_params=_cparams(("arbitrary",)),
        name="route",
    )(h2t, wqt, keys)


_INV_SQRT2 = float(1.0 / np.sqrt(2.0))
K_PIECE = 256
K_PIECE_A = 256
ROW_BLOCK = 4
JG_BLOCK = 4


def _experts_kernel(h2t_ref, u_ref, vt_ref, s1_ref, theta_ref, e0_ref, e1_ref,
                    x1_ref, ga2_ref, o_ref, a0_scr, a1_scr, p0_scr, p1_scr, acc_ref):
    c = pl.program_id(1)
    tb = h2t_ref.shape[1]

    @pl.when(c == 0)
    def _():
        acc_ref[...] = jnp.zeros_like(acc_ref)
        p1_scr[...] = jnp.zeros_like(p1_scr)

    sub = (8, LANES)
    n_jg = PEER_NKEYS // sub[0]
    zero = jnp.zeros(sub, F32)

    def pair_body(k, refs, stages):
        a_new, a_old, p_new, p_old = refs
        base = k * MXU_N
        ps = pl.ds(base, MXU_N)

        def mm_a(kc):
            ks = slice(kc * K_PIECE_A, (kc + 1) * K_PIECE_A)
            part = jnp.dot(u_ref[:, ks], h2t_ref[ks, ps], preferred_element_type=F32)
            if kc == 0:
                a_new[:, ps] = part
            else:
                a_new[:, ps] += part

        def mm_acc(kc, mh):
            ks = slice(kc * K_PIECE, (kc + 1) * K_PIECE)
            ms = slice(mh * EC, (mh + 1) * EC)
            acc_ref[ms, ps] += jnp.dot(vt_ref[ms, ks], p_old[ks, ps],
                                       preferred_element_type=F32)

        a_pieces = [functools.partial(mm_a, kc) for kc in range(D_MODEL // K_PIECE_A)]
        acc_pieces = [functools.partial(mm_acc, kc, mh) for kc in range(EC // K_PIECE)
                      for mh in range(D_MODEL // EC)]
        if stages == "fill":
            for piece in a_pieces:
                piece()
            return
        if stages == "drain":
            for piece in acc_pieces:
                piece()
            return
        mm_pieces = a_pieces + acc_pieces
        blocks = [(half, ip, j0) for half in range(MXU_N // LANES)
                  for ip in range(ROWS_PER_STEP // ROW_BLOCK)
                  for j0 in range(0, n_jg, JG_BLOCK)]
        every = len(blocks) // len(mm_pieces)
        for bi, (half, ip, j0) in enumerate(blocks):
            if bi % every == 0 and bi // every < len(mm_pieces):
                mm_pieces[bi // every]()
            cs = pl.ds(base + half * LANES, LANES)
            rows = tuple(range(ROW_BLOCK * ip, ROW_BLOCK * (ip + 1)))
            g = [[zero] * JG_BLOCK for _ in rows]
            for h in range(PEER_HEADS):
                th = [jnp.broadcast_to(theta_ref[h, 0, r:r + 1, cs], sub) for r in rows]
                e0 = [jnp.broadcast_to(e0_ref[h, 0, r:r + 1, cs], sub) for r in rows]
                for jg in range(JG_BLOCK):
                    js = slice((j0 + jg) * sub[0], (j0 + jg + 1) * sub[0])
                    s1 = s1_ref[h, js, cs]
                    e1 = e1_ref[h, js, cs]
                    for q in range(ROW_BLOCK):
                        g[q][jg] = g[q][jg] + jnp.where(s1 >= th[q], e1, zero) * e0[q]
            for q, r in enumerate(rows):
                for jg in range(0, JG_BLOCK, 2):
                    lo = r * PEER_NKEYS + (j0 + jg) * sub[0]
                    a = a_old[lo:lo + 2 * sub[0], cs]
                    act = a + a * lax.erf(a)
                    gg = jnp.concatenate([g[q][jg], g[q][jg + 1]], axis=0)
                    p_new[lo:lo + 2 * sub[0], cs] = (act * gg).astype(BF16)

    even = (a0_scr, a1_scr, p1_scr, p0_scr)
    odd = (a1_scr, a0_scr, p0_scr, p1_scr)
    last = pl.num_programs(1) - 1

    def run(refs, stages):
        for k in range(tb // MXU_N):
            pair_body(k, refs, stages)

    @pl.when(c == 0)
    def _():
        run(even, "fill")

    @pl.when(jnp.logical_and(c > 0, c % 2 == 0))
    def _():
        run(even, "all")

    @pl.when(jnp.logical_and(c < last, c % 2 == 1))
    def _():
        run(odd, "all")

    @pl.when(c == last)
    def _():
        run(odd, "drain")
        o_ref[...] = x1_ref[...] + ga2_ref[0] * acc_ref[...].T


def _experts(h2t, u, vt, s1, theta, e0, e1, x1, ga2, seq):
    d, t = h2t.shape
    tb = TB_EXP
    n_chunks = u.shape[0] // EC
    last = n_chunks - 1
    per_batch = seq // tb
    once = dict(pipeline_mode=pl.Buffered(1))
    tab = pl.BlockSpec((PEER_HEADS, PEER_NKEYS, tb), lambda i, c: (0, 0, i), **once)
    rowtab = pl.BlockSpec((PEER_HEADS, 1, ROWS_PER_STEP, tb),
                          lambda i, c: (0, jnp.clip(c - 1, 0, last), 0, i))
    return pl.pallas_call(
        _experts_kernel,
        grid=(t // tb, n_chunks + 2),
        in_specs=[pl.BlockSpec((d, tb), lambda i, c: (0, i)),
                  pl.BlockSpec((EC, d), lambda i, c: (jnp.minimum(c, last), 0)),
                  pl.BlockSpec((d, EC), lambda i, c: (0, jnp.clip(c - 2, 0, last))),
                  tab, rowtab, rowtab, tab,
                  pl.BlockSpec((tb, d), lambda i, c: (i, 0), **once),
                  pl.BlockSpec((1, 1, d), lambda i, c: (i // per_batch, 0, 0))],
        out_specs=pl.BlockSpec((tb, d), lambda i, c: (i, 0)),
        out_shape=jax.ShapeDtypeStruct((t, d), F32),
        scratch_shapes=[pltpu.VMEM((EC, tb), F32), pltpu.VMEM((EC, tb), F32),
                        pltpu.VMEM((EC, tb), BF16), pltpu.VMEM((EC, tb), BF16),
                        pltpu.VMEM((d, tb), F32)],
        compiler_params=_cparams(("arbitrary", "arbitrary")),
        name="experts",
    )(h2t, u, vt, s1, theta, e0, e1, x1, ga2)


def kernel(x, c, w_ada, b_ada, norm1_g, norm2_g, w_in, conv_dw, conv_b, conv_ln_g, conv_ln_b,
           w_conv_out, q_norm_g, k_norm_g, rel_bias, w_attn_out, w_out, peer_wq, peer_keys,
           peer_u, peer_v):
    bsz, s, d = x.shape
    depth = w_ada.shape[0]
    bd = jnp.asarray(np.kron(np.eye(N_HEADS), np.full((HEAD_DIM, HEAD_DIM), 1.0 / HEAD_DIM)), BF16)
    for l in range(depth):
        mod, w_in_bf = _ada(c, w_ada[l], b_ada[l], w_in[l])
        sh1, sc1, ga1, sh2, sc2, ga2 = [m.reshape(bsz, 1, d) for m in jnp.split(mod, 6, axis=-1)]
        u, q, k, v, gc, ga, pu_bf, pvt_bf = _inproj(
            x, norm1_g[l].reshape(1, d), sc1, sh1, w_in_bf, bd,
            jnp.tile(q_norm_g[l], N_HEADS).reshape(1, D_ATTN),
            jnp.tile(k_norm_g[l], N_HEADS).reshape(1, D_ATTN), peer_u[l], peer_v[l])
        mc = _conv(u, conv_dw[l], conv_b[l].reshape(1, D_CONV), conv_ln_g[l].reshape(1, D_CONV),
                   conv_ln_b[l].reshape(1, D_CONV), w_conv_out[l].astype(BF16), gc)
        bias, wqt = _bias_table(rel_bias[l], peer_wq[l])
        o = _attn(q, k, v, bias)
        x1, h2t = _merge(o, mc, ga, x, w_attn_out[l].astype(BF16), w_out[l].astype(BF16),
                         ga1, norm2_g[l].reshape(1, d), sc2, sh2)
        keys = peer_keys[l].reshape(2 * PEER_HEADS, PEER_NKEYS, PEER_DH).astype(BF16)
        s1, theta, e0, e1 = _route(h2t, wqt, keys)
        out = _experts(h2t, pu_bf, pvt_bf, s1, theta, e0, e1, x1.reshape(bsz * s, d), ga2, s)
        x = out.reshape(bsz, s, d)
    return x
```

```python
import functools

import jax
import jax.numpy as jnp
import numpy as np
from jax import lax
from jax.experimental import pallas as pl
from jax.experimental.pallas import tpu as pltpu

F32 = jnp.float32
BF16 = jnp.bfloat16

D_MODEL = 1024
CHUNK = 64
N_HEADS = 8
HEAD_DIM = 64
D_ATTN = N_HEADS * HEAD_DIM
LEFT_CHUNKS = 8
REL_CLIP = 128
D_CONV = D_MODEL // 2
CONV_W = 31
PEER_HEADS = 8
PEER_NKEYS = 128
PEER_N = PEER_NKEYS * PEER_NKEYS
PEER_DH = 128
PEER_TOPK = 16
EPS = 1e-6
NEG_INF = -1e30
LOG2E = float(np.log2(np.e))

LANES = 128
SUBLANES = 8
MXU_N = 256
VMEM_LIMIT = 56 * 1024 * 1024

TM_IN = 512
TS_CONV = 512
HALO = 32
TQ = 256
NKB = 3
TM_MERGE = 512
TB_ROUTE = 512
TB_EXP = 512
EC = 512
ROWS_PER_STEP = EC // PEER_NKEYS

def _cparams(sem):
    return pltpu.CompilerParams(dimension_semantics=sem, vmem_limit_bytes=VMEM_LIMIT)


def _ada_kernel(c_ref, w_ref, b_ref, win_ref, o_ref, winb_ref):
    c = c_ref[...]
    cond = c * jax.nn.sigmoid(c)
    o_ref[...] = jnp.dot(cond.astype(BF16), w_ref[...].astype(BF16),
                         preferred_element_type=F32) + b_ref[...]
    winb_ref[...] = win_ref[...].astype(BF16)


def _ada(c, w, b, w_in):
    bsz, d = c.shape
    n = w.shape[1]
    tn = 1024
    steps = n // tn
    n_in = w_in.shape[1]
    tin = n_in // steps
    return pl.pallas_call(
        _ada_kernel,
        grid=(steps,),
        in_specs=[pl.BlockSpec((bsz, d), lambda j: (0, 0)),
                  pl.BlockSpec((d, tn), lambda j: (0, j)),
                  pl.BlockSpec((1, tn), lambda j: (0, j)),
                  pl.BlockSpec((d, tin), lambda j: (0, j))],
        out_specs=[pl.BlockSpec((bsz, tn), lambda j: (0, j)),
                   pl.BlockSpec((d, tin), lambda j: (0, j))],
        out_shape=[jax.ShapeDtypeStruct((bsz, n), F32),
                   jax.ShapeDtypeStruct(w_in.shape, BF16)],
        compiler_params=_cparams(("arbitrary",)),
        name="ada",
    )(c, w, b.reshape(1, n), w_in)


def _head_rms(t, bd_ref, gain):
    t2 = t * t
    hi = t2.astype(BF16)
    lo = (t2 - hi.astype(F32)).astype(BF16)
    ms = (jnp.dot(hi, bd_ref[...], preferred_element_type=F32)
          + jnp.dot(lo, bd_ref[...], preferred_element_type=F32))
    return t * lax.rsqrt(ms + EPS) * gain


def _inproj_kernel(x_ref, g_ref, sc_ref, sh_ref, w_ref, bd_ref, qg_ref, kg_ref, pu_ref, pv_ref,
                   u_ref, q_ref, k_ref, v_ref, gc_ref, ga_ref, pub_ref, pvt_ref):
    x = x_ref[0]
    ms = jnp.mean(x * x, axis=-1, keepdims=True)
    h = (x * lax.rsqrt(ms + EPS)) * g_ref[...]
    h = h * (1.0 + sc_ref[0]) + sh_ref[0]
    hb = h.astype(BF16)

    def seg(lo, hi):
        return jnp.dot(hb, w_ref[:, lo:hi], preferred_element_type=F32)

    bounds = np.cumsum([0, D_CONV, D_CONV, D_ATTN, D_ATTN, D_ATTN, D_MODEL, D_MODEL])
    lo, hi = bounds[:-1], bounds[1:]
    a = seg(lo[0], hi[0])
    b = seg(lo[1], hi[1])
    q = seg(lo[2], hi[2])
    u_ref[0] = a * jax.nn.sigmoid(b)
    k = seg(lo[3], hi[3])
    q_ref[0] = (_head_rms(q, bd_ref, qg_ref[...]) * (HEAD_DIM ** -0.5 * LOG2E)).astype(BF16)
    v = seg(lo[4], hi[4])
    k_ref[0] = _head_rms(k, bd_ref, kg_ref[...]).astype(BF16)
    gc = seg(lo[5], hi[5])
    v_ref[0] = v.astype(BF16)
    pub_ref[...] = (pu_ref[...] * _INV_SQRT2).astype(BF16)
    ga = seg(lo[6], hi[6])
    pvt_ref[...] = pv_ref[...].T.astype(BF16)
    gc_ref[0] = jax.nn.sigmoid(gc)
    ga_ref[0] = jax.nn.sigmoid(ga)


def _inproj(x, g1, sc1, sh1, w_in, bd, qg, kg, pu, pv):
    bsz, s, d = x.shape
    tm = TM_IN
    n_in = w_in.shape[1]
    nb = s // tm
    n_e = pu.shape[0]
    er = n_e // (bsz * nb)
    row = lambda w: pl.BlockSpec((1, tm, w), lambda b, i: (b, i, 0))
    vec = lambda w: pl.BlockSpec((1, w), lambda b, i: (0, 0))
    mod = pl.BlockSpec((1, 1, d), lambda b, i: (b, 0, 0))
    erow = pl.BlockSpec((er, d), lambda b, i: (b * nb + i, 0))
    return pl.pallas_call(
        _inproj_kernel,
        grid=(bsz, nb),
        in_specs=[row(d), vec(d), mod, mod,
                  pl.BlockSpec((d, n_in), lambda b, i: (0, 0), pipeline_mode=pl.Buffered(1)),
                  pl.BlockSpec((D_ATTN, D_ATTN), lambda b, i: (0, 0)),
                  vec(D_ATTN), vec(D_ATTN), erow, erow],
        out_specs=[row(D_CONV), row(D_ATTN), row(D_ATTN), row(D_ATTN), row(d), row(d),
                   erow, pl.BlockSpec((d, er), lambda b, i: (0, b * nb + i))],
        out_shape=[jax.ShapeDtypeStruct((bsz, s, D_CONV), F32),
                   jax.ShapeDtypeStruct((bsz, s, D_ATTN), BF16),
                   jax.ShapeDtypeStruct((bsz, s, D_ATTN), BF16),
                   jax.ShapeDtypeStruct((bsz, s, D_ATTN), BF16),
                   jax.ShapeDtypeStruct((bsz, s, d), F32),
                   jax.ShapeDtypeStruct((bsz, s, d), F32),
                   jax.ShapeDtypeStruct((n_e, d), BF16),
                   jax.ShapeDtypeStruct((d, n_e), BF16)],
        compiler_params=_cparams(("arbitrary", "arbitrary")),
        name="inproj",
    )(x, g1, sc1, sh1, w_in, bd, qg, kg, pu, pv)


CONV_ROWS = 64


def _conv_kernel(u_ref, up_ref, dw_ref, cb_ref, lg_ref, lb_ref, w_ref, gc_ref,
                 o_ref, ext_ref, y_ref):
    i = pl.program_id(1)
    ts = u_ref.shape[1]
    prev = up_ref[0, ts - HALO:, :]
    ext_ref[0, 0:HALO, :] = jnp.where(i > 0, prev, 0.0)
    ext_ref[0, HALO:, :] = u_ref[0]
    n_sh = ts + HALO - SUBLANES
    for s in range(1, SUBLANES):
        for r0 in range(0, n_sh, CONV_ROWS):
            n = min(CONV_ROWS, n_sh - r0)
            ext_ref[s, r0:r0 + n, :] = ext_ref[0, r0 + s:r0 + s + n, :]
    base = HALO - (CONV_W - 1)
    for r0 in range(0, ts, CONV_ROWS):
        acc = jnp.zeros((CONV_ROWS, D_CONV), F32) + cb_ref[...]
        for w in range(CONV_W):
            s = (base + w) % SUBLANES
            a = r0 + base + w - s
            acc = acc + ext_ref[s, a:a + CONV_ROWS, :] * dw_ref[w:w + 1, :]
        y_ref[r0:r0 + CONV_ROWS, :] = acc
    y = y_ref[...]
    mu = jnp.mean(y, axis=-1, keepdims=True)
    yc = y - mu
    var = jnp.mean(yc * yc, axis=-1, keepdims=True)
    z = yc * lax.rsqrt(var + EPS) * lg_ref[...] + lb_ref[...]
    z = z * jax.nn.sigmoid(z)
    o = jnp.dot(z.astype(BF16), w_ref[...], preferred_element_type=F32)
    o_ref[0] = gc_ref[0] * o


def _conv(u, dw, cb, lg, lb, w_co, gc):
    bsz, s, dc = u.shape
    d = w_co.shape[1]
    ts = TS_CONV
    vec = lambda w: pl.BlockSpec((1, w), lambda b, i: (0, 0))
    return pl.pallas_call(
        _conv_kernel,
        grid=(bsz, s // ts),
        in_specs=[pl.BlockSpec((1, ts, dc), lambda b, i: (b, i, 0)),
                  pl.BlockSpec((1, ts, dc), lambda b, i: (b, jnp.maximum(i - 1, 0), 0)),
                  pl.BlockSpec((CONV_W, dc), lambda b, i: (0, 0)),
                  vec(dc), vec(dc), vec(dc),
                  pl.BlockSpec((dc, d), lambda b, i: (0, 0)),
                  pl.BlockSpec((1, ts, d), lambda b, i: (b, i, 0))],
        out_specs=pl.BlockSpec((1, ts, d), lambda b, i: (b, i, 0)),
        out_shape=jax.ShapeDtypeStruct((bsz, s, d), F32),
        scratch_shapes=[pltpu.VMEM((SUBLANES, ts + HALO, dc), F32), pltpu.VMEM((ts, dc), F32)],
        compiler_params=_cparams(("arbitrary", "arbitrary")),
        name="conv",
    )(u, u, dw, cb, lg, lb, w_co, gc)


def _attn_kernel(q_ref, k0_ref, k1_ref, k2_ref, v0_ref, v1_ref, v2_ref, bias_ref, o_ref):
    i = pl.program_id(1)
    k_refs = (k0_ref, k1_ref, k2_ref)
    v_refs = (v0_ref, v1_ref, v2_ref)
    low = lax.broadcasted_iota(jnp.int32, (TQ, LANES), 1) < HEAD_DIM

    n_heads_per_slab = LANES // HEAD_DIM

    def body(pens):
        def scores(h):
            lo = (h // n_heads_per_slab) * LANES
            q2 = q_ref[0, :, lo:lo + LANES]
            keep = low if h % n_heads_per_slab == 0 else jnp.logical_not(low)
            qh = jnp.where(keep, q2, jnp.zeros_like(q2))
            ss = []
            for j in range(NKB):
                s = lax.dot_general(qh, k_refs[j][0, :, lo:lo + LANES],
                                    (((1,), (1,)), ((), ())), preferred_element_type=F32)
                s = s + bias_ref[h, :, j * TQ:(j + 1) * TQ]
                ss.append(s if pens is None else s + pens[j])
            return ss

        def softmax_pv(h, ss):
            lo = (h // n_heads_per_slab) * LANES
            m = jnp.maximum(jnp.maximum(jnp.max(ss[0], axis=-1, keepdims=True),
                                        jnp.max(ss[1], axis=-1, keepdims=True)),
                            jnp.max(ss[2], axis=-1, keepdims=True))
            ps = [jnp.exp2(ss[j] - m) for j in range(NKB)]
            l = jnp.sum(ps[0] + ps[1] + ps[2], axis=-1, keepdims=True)
            p = jnp.concatenate([pj.astype(BF16) for pj in ps], axis=1)
            v = jnp.concatenate([v_refs[j][0, :, lo:lo + LANES] for j in range(NKB)], axis=0)
            return jnp.dot(p, v, preferred_element_type=F32) / l

        outs = []
        ss_next = scores(0)
        for h in range(N_HEADS):
            ss = ss_next
            if h + 1 < N_HEADS:
                ss_next = scores(h + 1)
            outs.append(softmax_pv(h, ss))
            if h % n_heads_per_slab == n_heads_per_slab - 1:
                lo = (h // n_heads_per_slab) * LANES
                o_ref[0, :, lo:lo + LANES] = jnp.where(low, outs[0], outs[1]).astype(BF16)
                outs = []

    @pl.when(i >= NKB - 1)
    def _():
        body(None)

    @pl.when(i < NKB - 1)
    def _():
        body([jnp.where(i - (NKB - 1) + j >= 0, 0.0, NEG_INF).astype(F32) for j in range(NKB)])


def _attn(q, k, v, bias):
    bsz, s, da = q.shape
    kspec = lambda j: pl.BlockSpec(
        (1, TQ, da), lambda b, i: (b, jnp.maximum(i - (NKB - 1) + j, 0), 0))
    return pl.pallas_call(
        _attn_kernel,
        grid=(bsz, s // TQ),
        in_specs=[pl.BlockSpec((1, TQ, da), lambda b, i: (b, i, 0)),
                  kspec(0), kspec(1), kspec(2), kspec(0), kspec(1), kspec(2),
                  pl.BlockSpec((N_HEADS, TQ, NKB * TQ), lambda b, i: (0, 0, 0))],
        out_specs=pl.BlockSpec((1, TQ, da), lambda b, i: (b, i, 0)),
        out_shape=jax.ShapeDtypeStruct((bsz, s, da), BF16),
        compiler_params=_cparams(("arbitrary", "arbitrary")),
        name="attn",
    )(q, k, k, k, v, v, v, bias)


def _bias_table(rel_bias, peer_wq):
    nk = NKB * TQ
    lw = TQ + nk - 1
    n_lo = (TQ - 1) - REL_CLIP
    n_hi = (nk - 1) - REL_CLIP
    w = jnp.concatenate([jnp.repeat(rel_bias[:, :1], n_lo, axis=1), rel_bias,
                         jnp.repeat(rel_bias[:, -1:], n_hi, axis=1)], axis=1).astype(F32)
    assert BIAS_ROW >= lw
    base = jnp.concatenate([w[:, nk - 1::-1], jnp.zeros((w.shape[0], BIAS_ROW - lw), F32),
                            w[:, lw - 1:nk - 1:-1]], axis=1) * LOG2E
    return pl.pallas_call(
        _bias_kernel,
        out_shape=[jax.ShapeDtypeStruct((N_HEADS, TQ, nk), F32),
                   jax.ShapeDtypeStruct(peer_wq.shape[::-1], BF16)],
        compiler_params=pltpu.CompilerParams(vmem_limit_bytes=VMEM_LIMIT),
        name="bias",
    )(base, peer_wq)


BIAS_ROW = 1024


def _bias_kernel(base_ref, wq_ref, o_ref, wqt_ref):
    wqt_ref[...] = wq_ref[...].T.astype(BF16)
    nk = NKB * TQ
    qi = lax.broadcasted_iota(jnp.int32, (TQ, nk), 0)
    kj = lax.broadcasted_iota(jnp.int32, (TQ, nk), 1)
    qc = qi // CHUNK + (NKB - 1) * TQ // CHUNK
    kc = kj // CHUNK
    band = (kc >= qc - LEFT_CHUNKS) & (kc <= qc)
    for h in range(N_HEADS):
        x = jnp.broadcast_to(base_ref[h:h + 1, :], (TQ, BIAS_ROW))
        y = pltpu.roll(x, 0, 1, stride=1, stride_axis=0)
        o_ref[h] = jnp.where(band, y[:, :nk], NEG_INF)


def _merge_kernel(o_ref, mc_ref, ga_ref, x_ref, wa_ref, wo_ref, ga1_ref, g2_ref, sc_ref, sh_ref,
                  x1_ref, h2t_ref):
    ya = jnp.dot(o_ref[0], wa_ref[...], preferred_element_type=F32)
    merged = mc_ref[0] + ga_ref[0] * ya
    y = jnp.dot(merged.astype(BF16), wo_ref[...], preferred_element_type=F32)
    x1 = x_ref[0] + ga1_ref[0] * y
    x1_ref[0] = x1
    ms = jnp.mean(x1 * x1, axis=-1, keepdims=True)
    h2 = (x1 * lax.rsqrt(ms + EPS)) * g2_ref[...]
    h2 = h2 * (1.0 + sc_ref[0]) + sh_ref[0]
    h2t_ref[...] = h2.T.astype(BF16)


def _merge(o, mc, ga, x, wa, wo, ga1, g2, sc2, sh2):
    bsz, s, d = x.shape
    tm = TM_MERGE
    nb = s // tm
    row = lambda w: pl.BlockSpec((1, tm, w), lambda b, i: (b, i, 0))
    mod = pl.BlockSpec((1, 1, d), lambda b, i: (b, 0, 0))
    return pl.pallas_call(
        _merge_kernel,
        grid=(bsz, nb),
        in_specs=[row(D_ATTN), row(d), row(d), row(d),
                  pl.BlockSpec((D_ATTN, d), lambda b, i: (0, 0)),
                  pl.BlockSpec((d, d), lambda b, i: (0, 0)),
                  mod, pl.BlockSpec((1, d), lambda b, i: (0, 0)), mod, mod],
        out_specs=[row(d), pl.BlockSpec((d, tm), lambda b, i: (0, b * nb + i))],
        out_shape=[jax.ShapeDtypeStruct((bsz, s, d), F32),
                   jax.ShapeDtypeStruct((d, bsz * s), BF16)],
        compiler_params=_cparams(("arbitrary", "arbitrary")),
        name="merge",
    )(o, mc, ga, x, wa, wo, ga1, g2, sc2, sh2)


_CAND = [(a, b) for a in range(PEER_TOPK) for b in range(PEER_TOPK)
         if (a + 1) * (b + 1) <= PEER_TOPK]


def _sort16_network():
    n, pairs, p = PEER_TOPK, [], 1
    while p < n:
        k = p
        while k >= 1:
            for j in range(k % p, n - k, 2 * k):
                for i in range(min(k, n - j - k)):
                    if (i + j) // (2 * p) == (i + j + k) // (2 * p):
                        pairs.append((i + j, i + j + k))
            k //= 2
        p *= 2
    return pairs


_SORT16 = _sort16_network()


def _top16_sorted(problems, store_row):
    sub = 8
    lvs = [[s[g * sub:(g + 1) * sub, :] for g in range(PEER_NKEYS // sub)] for s in problems]
    for a, b in _SORT16:
        for lv in lvs:
            lv[a], lv[b] = jnp.maximum(lv[a], lv[b]), jnp.minimum(lv[a], lv[b])
    for r in range(PEER_TOPK):
        for p, lv in enumerate(lvs):
            m = jnp.max(lv[0], axis=0, keepdims=True)
            store_row(p, r, m)
            eq = lv[0] == m
            for k in range(PEER_TOPK - 1 - r):
                lv[k] = jnp.where(eq, lv[k + 1], lv[k])


def _route_kernel(h2t_ref, wqt_ref, keys_ref, s1_ref, theta_ref, e0_ref, e1_ref,
                  s_scr, v_scr, tau_scr, zi_scr):
    tb = h2t_ref.shape[1]
    ncol = tb // LANES

    def scores(h):
        rows = slice(2 * h * PEER_DH, 2 * (h + 1) * PEER_DH)
        qt = jnp.dot(wqt_ref[rows, :], h2t_ref[...], preferred_element_type=F32).astype(BF16)
        for p in range(2):
            s_scr[2 * h + p] = jnp.dot(keys_ref[2 * h + p], qt[p * PEER_DH:(p + 1) * PEER_DH, :],
                                       preferred_element_type=F32)

    def top16(h):
        for col in range(ncol):
            cs = slice(col * LANES, (col + 1) * LANES)
            rw = slice(h * ncol + col, h * ncol + col + 1)

            def store_row(p, r, m, rw=rw):
                v_scr[p, r, rw, :] = m

            _top16_sorted([s_scr[2 * h + p, :, cs] for p in range(2)], store_row)

    scores(0)
    for h in range(PEER_HEADS):
        if h + 1 < PEER_HEADS:
            scores(h + 1)
        top16(h)

    v0 = [v_scr[0, a] for a in range(PEER_TOPK)]
    v1 = [v_scr[1, b] for b in range(PEER_TOPK)]
    cand = [v0[a] + v1[b] for (a, b) in _CAND]
    top = cand[0]
    work = list(cand)
    for r in range(PEER_TOPK):
        c16 = functools.reduce(jnp.maximum, work)
        work = [jnp.where(w == c16, -jnp.inf, w) for w in work]
    c17 = functools.reduce(jnp.maximum, work)
    z = jnp.zeros_like(top)
    for c in cand:
        z = z + jnp.where(c >= c16, jnp.exp(c - top), 0.0)
    tau_scr[...] = 0.5 * (c16 + c17)
    zi_scr[...] = _INV_SQRT2 / z

    def stage3(it, carry):
        h = it // ncol
        col = it % ncol
        cs = pl.ds(pl.multiple_of(col * LANES, LANES), LANES)
        rw = pl.ds(h * ncol + col, 1)
        s0 = s_scr[2 * h, :, cs]
        s1 = s_scr[2 * h + 1, :, cs]
        theta = jnp.where(s0 >= v_scr[0, PEER_TOPK - 1, rw, :], tau_scr[rw, :] - s0, jnp.inf)
        e0 = jnp.exp(s0 - v_scr[0, 0, rw, :])
        for g in range(PEER_NKEYS // ROWS_PER_STEP):
            rows = slice(g * ROWS_PER_STEP, (g + 1) * ROWS_PER_STEP)
            theta_ref[h, g, :, cs] = theta[rows]
            e0_ref[h, g, :, cs] = e0[rows]
        s1_ref[h, :, cs] = jnp.where(s1 >= v_scr[1, PEER_TOPK - 1, rw, :], s1, -jnp.inf)
        e1_ref[h, :, cs] = jnp.exp(s1 - v_scr[1, 0, rw, :]) * zi_scr[rw, :]
        return carry

    lax.fori_loop(0, PEER_HEADS * ncol, stage3, 0)


def _route(h2t, wqt, keys):
    d, t = h2t.shape
    tb = TB_ROUTE
    ncol = tb // LANES
    n_hp = 2 * PEER_HEADS
    n_grp = PEER_NKEYS // ROWS_PER_STEP
    tab = pl.BlockSpec((PEER_HEADS, PEER_NKEYS, tb), lambda i: (0, 0, i))
    rowtab = pl.BlockSpec((PEER_HEADS, n_grp, ROWS_PER_STEP, tb), lambda i: (0, 0, 0, i))
    tab_shape = jax.ShapeDtypeStruct((PEER_HEADS, PEER_NKEYS, t), F32)
    rowtab_shape = jax.ShapeDtypeStruct((PEER_HEADS, n_grp, ROWS_PER_STEP, t), F32)
    return pl.pallas_call(
        _route_kernel,
        grid=(t // tb,),
        in_specs=[pl.BlockSpec((d, tb), lambda i: (0, i)),
                  pl.BlockSpec(wqt.shape, lambda i: (0, 0)),
                  pl.BlockSpec(keys.shape, lambda i: (0, 0, 0))],
        out_specs=[tab, rowtab, rowtab, tab],
        out_shape=[tab_shape, rowtab_shape, rowtab_shape, tab_shape],
        scratch_shapes=[pltpu.VMEM((n_hp, PEER_NKEYS, tb), F32),
                        pltpu.VMEM((2, PEER_TOPK, PEER_HEADS * ncol, LANES), F32),
                        pltpu.VMEM((PEER_HEADS * ncol, LANES), F32),
                        pltpu.VMEM((PEER_HEADS * ncol, LANES), F32)],
        compiler_params=_cparams(("arbitrary",)),
        name="route",
    )(h2t, wqt, keys)


_INV_SQRT2 = float(1.0 / np.sqrt(2.0))
K_PIECE = 256
K_PIECE_A = 256
ROW_BLOCK = 4
JG_BLOCK = 4


def _experts_kernel(h2t_ref, u_ref, vt_ref, s1_ref, theta_ref, e0_ref, e1_ref,
                    x1_ref, ga2_ref, o_ref, a0_scr, a1_scr, p0_scr, p1_scr, acc_ref):
    c = pl.program_id(1)
    tb = h2t_ref.shape[1]

    @pl.when(c == 0)
    def _():
        acc_ref[...] = jnp.zeros_like(acc_ref)
        p1_scr[...] = jnp.zeros_like(p1_scr)

    sub = (8, LANES)
    n_jg = PEER_NKEYS // sub[0]
    zero = jnp.zeros(sub, F32)

    def pair_body(k, refs, stages):
        a_new, a_old, p_new, p_old = refs
        base = k * MXU_N
        ps = pl.ds(base, MXU_N)

        def mm_a(kc):
            ks = slice(kc * K_PIECE_A, (kc + 1) * K_PIECE_A)
            part = jnp.dot(u_ref[:, ks], h2t_ref[ks, ps], preferred_element_type=F32)
            if kc == 0:
                a_new[:, ps] = part
            else:
                a_new[:, ps] += part

        def mm_acc(kc, mh):
            ks = slice(kc * K_PIECE, (kc + 1) * K_PIECE)
            ms = slice(mh * EC, (mh + 1) * EC)
            acc_ref[ms, ps] += jnp.dot(vt_ref[ms, ks], p_old[ks, ps],
                                       preferred_element_type=F32)

        a_pieces = [functools.partial(mm_a, kc) for kc in range(D_MODEL // K_PIECE_A)]
        acc_pieces = [functools.partial(mm_acc, kc, mh) for kc in range(EC // K_PIECE)
                      for mh in range(D_MODEL // EC)]
        if stages == "fill":
            for piece in a_pieces:
                piece()
            return
        if stages == "drain":
            for piece in acc_pieces:
                piece()
            return
        mm_pieces = a_pieces + acc_pieces
        blocks = [(half, ip, j0) for half in range(MXU_N // LANES)
                  for ip in range(ROWS_PER_STEP // ROW_BLOCK)
                  for j0 in range(0, n_jg, JG_BLOCK)]
        every = len(blocks) // len(mm_pieces)
        for bi, (half, ip, j0) in enumerate(blocks):
            if bi % every == 0 and bi // every < len(mm_pieces):
                mm_pieces[bi // every]()
            cs = pl.ds(base + half * LANES, LANES)
            rows = tuple(range(ROW_BLOCK * ip, ROW_BLOCK * (ip + 1)))
            g = [[zero] * JG_BLOCK for _ in rows]
            for h in range(PEER_HEADS):
                th = [jnp.broadcast_to(theta_ref[h, 0, r:r + 1, cs], sub) for r in rows]
                e0 = [jnp.broadcast_to(e0_ref[h, 0, r:r + 1, cs], sub) for r in rows]
                for jg in range(JG_BLOCK):
                    js = slice((j0 + jg) * sub[0], (j0 + jg + 1) * sub[0])
                    s1 = s1_ref[h, js, cs]
                    e1 = e1_ref[h, js, cs]
                    for q in range(ROW_BLOCK):
                        g[q][jg] = g[q][jg] + jnp.where(s1 >= th[q], e1, zero) * e0[q]
            for q, r in enumerate(rows):
                for jg in range(0, JG_BLOCK, 2):
                    lo = r * PEER_NKEYS + (j0 + jg) * sub[0]
                    a = a_old[lo:lo + 2 * sub[0], cs]
                    act = a + a * lax.erf(a)
                    gg = jnp.concatenate([g[q][jg], g[q][jg + 1]], axis=0)
                    p_new[lo:lo + 2 * sub[0], cs] = (act * gg).astype(BF16)

    even = (a0_scr, a1_scr, p1_scr, p0_scr)
    odd = (a1_scr, a0_scr, p0_scr, p1_scr)
    last = pl.num_programs(1) - 1

    def run(refs, stages):
        for k in range(tb // MXU_N):
            pair_body(k, refs, stages)

    @pl.when(c == 0)
    def _():
        run(even, "fill")

    @pl.when(jnp.logical_and(c > 0, c % 2 == 0))
    def _():
        run(even, "all")

    @pl.when(jnp.logical_and(c < last, c % 2 == 1))
    def _():
        run(odd, "all")

    @pl.when(c == last)
    def _():
        run(odd, "drain")
        o_ref[...] = x1_ref[...] + ga2_ref[0] * acc_ref[...].T


def _experts(h2t, u, vt, s1, theta, e0, e1, x1, ga2, seq):
    d, t = h2t.shape
    tb = TB_EXP
    n_chunks = u.shape[0] // EC
    last = n_chunks - 1
    per_batch = seq // tb
    once = dict(pipeline_mode=pl.Buffered(1))
    tab = pl.BlockSpec((PEER_HEADS, PEER_NKEYS, tb), lambda i, c: (0, 0, i), **once)
    rowtab = pl.BlockSpec((PEER_HEADS, 1, ROWS_PER_STEP, tb),
                          lambda i, c: (0, jnp.clip(c - 1, 0, last), 0, i))
    return pl.pallas_call(
        _experts_kernel,
        grid=(t // tb, n_chunks + 2),
        in_specs=[pl.BlockSpec((d, tb), lambda i, c: (0, i)),
                  pl.BlockSpec((EC, d), lambda i, c: (jnp.minimum(c, last), 0)),
                  pl.BlockSpec((d, EC), lambda i, c: (0, jnp.clip(c - 2, 0, last))),
                  tab, rowtab, rowtab, tab,
                  pl.BlockSpec((tb, d), lambda i, c: (i, 0), **once),
                  pl.BlockSpec((1, 1, d), lambda i, c: (i // per_batch, 0, 0))],
        out_specs=pl.BlockSpec((tb, d), lambda i, c: (i, 0)),
        out_shape=jax.ShapeDtypeStruct((t, d), F32),
        scratch_shapes=[pltpu.VMEM((EC, tb), F32), pltpu.VMEM((EC, tb), F32),
                        pltpu.VMEM((EC, tb), BF16), pltpu.VMEM((EC, tb), BF16),
                        pltpu.VMEM((d, tb), F32)],
        compiler_params=_cparams(("arbitrary", "arbitrary")),
        name="experts",
    )(h2t, u, vt, s1, theta, e0, e1, x1, ga2)


def kernel(x, c, w_ada, b_ada, norm1_g, norm2_g, w_in, conv_dw, conv_b, conv_ln_g, conv_ln_b,
           w_conv_out, q_norm_g, k_norm_g, rel_bias, w_attn_out, w_out, peer_wq, peer_keys,
           peer_u, peer_v):
    bsz, s, d = x.shape
    depth = w_ada.shape[0]
    bd = jnp.asarray(np.kron(np.eye(N_HEADS), np.full((HEAD_DIM, HEAD_DIM), 1.0 / HEAD_DIM)), BF16)
    for l in range(depth):
        mod, w_in_bf = _ada(c, w_ada[l], b_ada[l], w_in[l])
        sh1, sc1, ga1, sh2, sc2, ga2 = [m.reshape(bsz, 1, d) for m in jnp.split(mod, 6, axis=-1)]
        u, q, k, v, gc, ga, pu_bf, pvt_bf = _inproj(
            x, norm1_g[l].reshape(1, d), sc1, sh1, w_in_bf, bd,
            jnp.tile(q_norm_g[l], N_HEADS).reshape(1, D_ATTN),
            jnp.tile(k_norm_g[l], N_HEADS).reshape(1, D_ATTN), peer_u[l], peer_v[l])
        mc = _conv(u, conv_dw[l], conv_b[l].reshape(1, D_CONV), conv_ln_g[l].reshape(1, D_CONV),
                   conv_ln_b[l].reshape(1, D_CONV), w_conv_out[l].astype(BF16), gc)
        bias, wqt = _bias_table(rel_bias[l], peer_wq[l])
        o = _attn(q, k, v, bias)
        x1, h2t = _merge(o, mc, ga, x, w_attn_out[l].astype(BF16), w_out[l].astype(BF16),
                         ga1, norm2_g[l].reshape(1, d), sc2, sh2)
        keys = peer_keys[l].reshape(2 * PEER_HEADS, PEER_NKEYS, PEER_DH).astype(BF16)
        s1, theta, e0, e1 = _route(h2t, wqt, keys)
        out = _experts(h2t, pu_bf, pvt_bf, s1, theta, e0, e1, x1.reshape(bsz * s, d), ga2, s)
        x = out.reshape(bsz, s, d)
    return x
```

```python
import functools

import jax
import jax.numpy as jnp
import numpy as np
from jax import lax
from jax.experimental import pallas as pl
from jax.experimental.pallas import tpu as pltpu

F32 = jnp.float32
BF16 = jnp.bfloat16

D_MODEL = 1024
CHUNK = 64
N_HEADS = 8
HEAD_DIM = 64
D_ATTN = N_HEADS * HEAD_DIM
LEFT_CHUNKS = 8
REL_CLIP = 128
D_CONV = D_MODEL // 2
CONV_W = 31
PEER_HEADS = 8
PEER_NKEYS = 128
PEER_N = PEER_NKEYS * PEER_NKEYS
PEER_DH = 128
PEER_TOPK = 16
EPS = 1e-6
NEG_INF = -1e30
LOG2E = float(np.log2(np.e))

LANES = 128
SUBLANES = 8
MXU_N = 256
VMEM_LIMIT = 56 * 1024 * 1024

TM_IN = 512
TS_CONV = 512
HALO = 32
TQ = 256
NKB = 3
TM_MERGE = 512
TB_ROUTE = 512
TB_EXP = 1024
EC = 512
ROWS_PER_STEP = EC // PEER_NKEYS

def _cparams(sem):
    return pltpu.CompilerParams(dimension_semantics=sem, vmem_limit_bytes=VMEM_LIMIT)


def _ada_kernel(c_ref, w_ref, b_ref, win_ref, o_ref, winb_ref):
    c = c_ref[...]
    cond = c * jax.nn.sigmoid(c)
    o_ref[...] = jnp.dot(cond.astype(BF16), w_ref[...].astype(BF16),
                         preferred_element_type=F32) + b_ref[...]
    winb_ref[...] = win_ref[...].astype(BF16)


def _ada(c, w, b, w_in):
    bsz, d = c.shape
    n = w.shape[1]
    tn = 1024
    steps = n // tn
    n_in = w_in.shape[1]
    tin = n_in // steps
    return pl.pallas_call(
        _ada_kernel,
        grid=(steps,),
        in_specs=[pl.BlockSpec((bsz, d), lambda j: (0, 0)),
                  pl.BlockSpec((d, tn), lambda j: (0, j)),
                  pl.BlockSpec((1, tn), lambda j: (0, j)),
                  pl.BlockSpec((d, tin), lambda j: (0, j))],
        out_specs=[pl.BlockSpec((bsz, tn), lambda j: (0, j)),
                   pl.BlockSpec((d, tin), lambda j: (0, j))],
        out_shape=[jax.ShapeDtypeStruct((bsz, n), F32),
                   jax.ShapeDtypeStruct(w_in.shape, BF16)],
        compiler_params=_cparams(("arbitrary",)),
        name="ada",
    )(c, w, b.reshape(1, n), w_in)


def _head_rms(t, bd_ref, gain):
    t2 = t * t
    hi = t2.astype(BF16)
    lo = (t2 - hi.astype(F32)).astype(BF16)
    ms = (jnp.dot(hi, bd_ref[...], preferred_element_type=F32)
          + jnp.dot(lo, bd_ref[...], preferred_element_type=F32))
    return t * lax.rsqrt(ms + EPS) * gain


def _inproj_kernel(x_ref, g_ref, sc_ref, sh_ref, w_ref, bd_ref, qg_ref, kg_ref, pu_ref, pv_ref,
                   u_ref, q_ref, k_ref, v_ref, gc_ref, ga_ref, pub_ref, pvt_ref):
    x = x_ref[0]
    ms = jnp.mean(x * x, axis=-1, keepdims=True)
    h = (x * lax.rsqrt(ms + EPS)) * g_ref[...]
    h = h * (1.0 + sc_ref[0]) + sh_ref[0]
    hb = h.astype(BF16)

    def seg(lo, hi):
        return jnp.dot(hb, w_ref[:, lo:hi], preferred_element_type=F32)

    bounds = np.cumsum([0, D_CONV, D_CONV, D_ATTN, D_ATTN, D_ATTN, D_MODEL, D_MODEL])
    lo, hi = bounds[:-1], bounds[1:]
    a = seg(lo[0], hi[0])
    b = seg(lo[1], hi[1])
    q = seg(lo[2], hi[2])
    u_ref[0] = a * jax.nn.sigmoid(b)
    k = seg(lo[3], hi[3])
    q_ref[0] = (_head_rms(q, bd_ref, qg_ref[...]) * (HEAD_DIM ** -0.5 * LOG2E)).astype(BF16)
    v = seg(lo[4], hi[4])
    k_ref[0] = _head_rms(k, bd_ref, kg_ref[...]).astype(BF16)
    gc = seg(lo[5], hi[5])
    v_ref[0] = v.astype(BF16)
    pub_ref[...] = (pu_ref[...] * _INV_SQRT2).astype(BF16)
    ga = seg(lo[6], hi[6])
    pvt_ref[...] = pv_ref[...].T.astype(BF16)
    gc_ref[0] = jax.nn.sigmoid(gc)
    ga_ref[0] = jax.nn.sigmoid(ga)


def _inproj(x, g1, sc1, sh1, w_in, bd, qg, kg, pu, pv):
    bsz, s, d = x.shape
    tm = TM_IN
    n_in = w_in.shape[1]
    nb = s // tm
    n_e = pu.shape[0]
    er = n_e // (bsz * nb)
    row = lambda w: pl.BlockSpec((1, tm, w), lambda b, i: (b, i, 0))
    vec = lambda w: pl.BlockSpec((1, w), lambda b, i: (0, 0))
    mod = pl.BlockSpec((1, 1, d), lambda b, i: (b, 0, 0))
    erow = pl.BlockSpec((er, d), lambda b, i: (b * nb + i, 0))
    return pl.pallas_call(
        _inproj_kernel,
        grid=(bsz, nb),
        in_specs=[row(d), vec(d), mod, mod,
                  pl.BlockSpec((d, n_in), lambda b, i: (0, 0), pipeline_mode=pl.Buffered(1)),
                  pl.BlockSpec((D_ATTN, D_ATTN), lambda b, i: (0, 0)),
                  vec(D_ATTN), vec(D_ATTN), erow, erow],
        out_specs=[row(D_CONV), row(D_ATTN), row(D_ATTN), row(D_ATTN), row(d), row(d),
                   erow, pl.BlockSpec((d, er), lambda b, i: (0, b * nb + i))],
        out_shape=[jax.ShapeDtypeStruct((bsz, s, D_CONV), F32),
                   jax.ShapeDtypeStruct((bsz, s, D_ATTN), BF16),
                   jax.ShapeDtypeStruct((bsz, s, D_ATTN), BF16),
                   jax.ShapeDtypeStruct((bsz, s, D_ATTN), BF16),
                   jax.ShapeDtypeStruct((bsz, s, d), F32),
                   jax.ShapeDtypeStruct((bsz, s, d), F32),
                   jax.ShapeDtypeStruct((n_e, d), BF16),
                   jax.ShapeDtypeStruct((d, n_e), BF16)],
        compiler_params=_cparams(("arbitrary", "arbitrary")),
        name="inproj",
    )(x, g1, sc1, sh1, w_in, bd, qg, kg, pu, pv)


CONV_ROWS = 64


def _conv_kernel(u_ref, up_ref, dw_ref, cb_ref, lg_ref, lb_ref, w_ref, gc_ref,
                 o_ref, ext_ref, y_ref):
    i = pl.program_id(1)
    ts = u_ref.shape[1]
    prev = up_ref[0, ts - HALO:, :]
    ext_ref[0, 0:HALO, :] = jnp.where(i > 0, prev, 0.0)
    ext_ref[0, HALO:, :] = u_ref[0]
    n_sh = ts + HALO - SUBLANES
    for s in range(1, SUBLANES):
        for r0 in range(0, n_sh, CONV_ROWS):
            n = min(CONV_ROWS, n_sh - r0)
            ext_ref[s, r0:r0 + n, :] = ext_ref[0, r0 + s:r0 + s + n, :]
    base = HALO - (CONV_W - 1)
    for r0 in range(0, ts, CONV_ROWS):
        acc = jnp.zeros((CONV_ROWS, D_CONV), F32) + cb_ref[...]
        for w in range(CONV_W):
            s = (base + w) % SUBLANES
            a = r0 + base + w - s
            acc = acc + ext_ref[s, a:a + CONV_ROWS, :] * dw_ref[w:w + 1, :]
        y_ref[r0:r0 + CONV_ROWS, :] = acc
    y = y_ref[...]
    mu = jnp.mean(y, axis=-1, keepdims=True)
    yc = y - mu
    var = jnp.mean(yc * yc, axis=-1, keepdims=True)
    z = yc * lax.rsqrt(var + EPS) * lg_ref[...] + lb_ref[...]
    z = z * jax.nn.sigmoid(z)
    o = jnp.dot(z.astype(BF16), w_ref[...], preferred_element_type=F32)
    o_ref[0] = gc_ref[0] * o


def _conv(u, dw, cb, lg, lb, w_co, gc):
    bsz, s, dc = u.shape
    d = w_co.shape[1]
    ts = TS_CONV
    vec = lambda w: pl.BlockSpec((1, w), lambda b, i: (0, 0))
    return pl.pallas_call(
        _conv_kernel,
        grid=(bsz, s // ts),
        in_specs=[pl.BlockSpec((1, ts, dc), lambda b, i: (b, i, 0)),
                  pl.BlockSpec((1, ts, dc), lambda b, i: (b, jnp.maximum(i - 1, 0), 0)),
                  pl.BlockSpec((CONV_W, dc), lambda b, i: (0, 0)),
                  vec(dc), vec(dc), vec(dc),
                  pl.BlockSpec((dc, d), lambda b, i: (0, 0)),
                  pl.BlockSpec((1, ts, d), lambda b, i: (b, i, 0))],
        out_specs=pl.BlockSpec((1, ts, d), lambda b, i: (b, i, 0)),
        out_shape=jax.ShapeDtypeStruct((bsz, s, d), F32),
        scratch_shapes=[pltpu.VMEM((SUBLANES, ts + HALO, dc), F32), pltpu.VMEM((ts, dc), F32)],
        compiler_params=_cparams(("arbitrary", "arbitrary")),
        name="conv",
    )(u, u, dw, cb, lg, lb, w_co, gc)


def _attn_kernel(q_ref, k0_ref, k1_ref, k2_ref, v0_ref, v1_ref, v2_ref, bias_ref, o_ref):
    i = pl.program_id(1)
    k_refs = (k0_ref, k1_ref, k2_ref)
    v_refs = (v0_ref, v1_ref, v2_ref)
    low = lax.broadcasted_iota(jnp.int32, (TQ, LANES), 1) < HEAD_DIM

    n_heads_per_slab = LANES // HEAD_DIM

    def body(pens):
        def scores(h):
            lo = (h // n_heads_per_slab) * LANES
            q2 = q_ref[0, :, lo:lo + LANES]
            keep = low if h % n_heads_per_slab == 0 else jnp.logical_not(low)
            qh = jnp.where(keep, q2, jnp.zeros_like(q2))
            ss = []
            for j in range(NKB):
                s = lax.dot_general(qh, k_refs[j][0, :, lo:lo + LANES],
                                    (((1,), (1,)), ((), ())), preferred_element_type=F32)
                s = s + bias_ref[h, :, j * TQ:(j + 1) * TQ]
                ss.append(s if pens is None else s + pens[j])
            return ss

        def softmax_pv(h, ss):
            lo = (h // n_heads_per_slab) * LANES
            m = jnp.maximum(jnp.maximum(jnp.max(ss[0], axis=-1, keepdims=True),
                                        jnp.max(ss[1], axis=-1, keepdims=True)),
                            jnp.max(ss[2], axis=-1, keepdims=True))
            ps = [jnp.exp2(ss[j] - m) for j in range(NKB)]
            l = jnp.sum(ps[0] + ps[1] + ps[2], axis=-1, keepdims=True)
            p = jnp.concatenate([pj.astype(BF16) for pj in ps], axis=1)
            v = jnp.concatenate([v_refs[j][0, :, lo:lo + LANES] for j in range(NKB)], axis=0)
            return jnp.dot(p, v, preferred_element_type=F32) / l

        outs = []
        ss_next = scores(0)
        for h in range(N_HEADS):
            ss = ss_next
            if h + 1 < N_HEADS:
                ss_next = scores(h + 1)
            outs.append(softmax_pv(h, ss))
            if h % n_heads_per_slab == n_heads_per_slab - 1:
                lo = (h // n_heads_per_slab) * LANES
                o_ref[0, :, lo:lo + LANES] = jnp.where(low, outs[0], outs[1]).astype(BF16)
                outs = []

    @pl.when(i >= NKB - 1)
    def _():
        body(None)

    @pl.when(i < NKB - 1)
    def _():
        body([jnp.where(i - (NKB - 1) + j >= 0, 0.0, NEG_INF).astype(F32) for j in range(NKB)])


def _attn(q, k, v, bias):
    bsz, s, da = q.shape
    kspec = lambda j: pl.BlockSpec(
        (1, TQ, da), lambda b, i: (b, jnp.maximum(i - (NKB - 1) + j, 0), 0))
    return pl.pallas_call(
        _attn_kernel,
        grid=(bsz, s // TQ),
        in_specs=[pl.BlockSpec((1, TQ, da), lambda b, i: (b, i, 0)),
                  kspec(0), kspec(1), kspec(2), kspec(0), kspec(1), kspec(2),
                  pl.BlockSpec((N_HEADS, TQ, NKB * TQ), lambda b, i: (0, 0, 0))],
        out_specs=pl.BlockSpec((1, TQ, da), lambda b, i: (b, i, 0)),
        out_shape=jax.ShapeDtypeStruct((bsz, s, da), BF16),
        compiler_params=_cparams(("arbitrary", "arbitrary")),
        name="attn",
    )(q, k, k, k, v, v, v, bias)


def _bias_table(rel_bias, peer_wq):
    nk = NKB * TQ
    lw = TQ + nk - 1
    n_lo = (TQ - 1) - REL_CLIP
    n_hi = (nk - 1) - REL_CLIP
    w = jnp.concatenate([jnp.repeat(rel_bias[:, :1], n_lo, axis=1), rel_bias,
                         jnp.repeat(rel_bias[:, -1:], n_hi, axis=1)], axis=1).astype(F32)
    assert BIAS_ROW >= lw
    base = jnp.concatenate([w[:, nk - 1::-1], jnp.zeros((w.shape[0], BIAS_ROW - lw), F32),
                            w[:, lw - 1:nk - 1:-1]], axis=1) * LOG2E
    return pl.pallas_call(
        _bias_kernel,
        out_shape=[jax.ShapeDtypeStruct((N_HEADS, TQ, nk), F32),
                   jax.ShapeDtypeStruct(peer_wq.shape[::-1], BF16)],
        compiler_params=pltpu.CompilerParams(vmem_limit_bytes=VMEM_LIMIT),
        name="bias",
    )(base, peer_wq)


BIAS_ROW = 1024


def _bias_kernel(base_ref, wq_ref, o_ref, wqt_ref):
    wqt_ref[...] = wq_ref[...].T.astype(BF16)
    nk = NKB * TQ
    qi = lax.broadcasted_iota(jnp.int32, (TQ, nk), 0)
    kj = lax.broadcasted_iota(jnp.int32, (TQ, nk), 1)
    qc = qi // CHUNK + (NKB - 1) * TQ // CHUNK
    kc = kj // CHUNK
    band = (kc >= qc - LEFT_CHUNKS) & (kc <= qc)
    for h in range(N_HEADS):
        x = jnp.broadcast_to(base_ref[h:h + 1, :], (TQ, BIAS_ROW))
        y = pltpu.roll(x, 0, 1, stride=1, stride_axis=0)
        o_ref[h] = jnp.where(band, y[:, :nk], NEG_INF)


def _merge_kernel(o_ref, mc_ref, ga_ref, x_ref, wa_ref, wo_ref, ga1_ref, g2_ref, sc_ref, sh_ref,
                  x1_ref, h2t_ref):
    ya = jnp.dot(o_ref[0], wa_ref[...], preferred_element_type=F32)
    merged = mc_ref[0] + ga_ref[0] * ya
    y = jnp.dot(merged.astype(BF16), wo_ref[...], preferred_element_type=F32)
    x1 = x_ref[0] + ga1_ref[0] * y
    x1_ref[0] = x1
    ms = jnp.mean(x1 * x1, axis=-1, keepdims=True)
    h2 = (x1 * lax.rsqrt(ms + EPS)) * g2_ref[...]
    h2 = h2 * (1.0 + sc_ref[0]) + sh_ref[0]
    h2t_ref[...] = h2.T.astype(BF16)


def _merge(o, mc, ga, x, wa, wo, ga1, g2, sc2, sh2):
    bsz, s, d = x.shape
    tm = TM_MERGE
    nb = s // tm
    row = lambda w: pl.BlockSpec((1, tm, w), lambda b, i: (b, i, 0))
    mod = pl.BlockSpec((1, 1, d), lambda b, i: (b, 0, 0))
    return pl.pallas_call(
        _merge_kernel,
        grid=(bsz, nb),
        in_specs=[row(D_ATTN), row(d), row(d), row(d),
                  pl.BlockSpec((D_ATTN, d), lambda b, i: (0, 0)),
                  pl.BlockSpec((d, d), lambda b, i: (0, 0)),
                  mod, pl.BlockSpec((1, d), lambda b, i: (0, 0)), mod, mod],
        out_specs=[row(d), pl.BlockSpec((d, tm), lambda b, i: (0, b * nb + i))],
        out_shape=[jax.ShapeDtypeStruct((bsz, s, d), F32),
                   jax.ShapeDtypeStruct((d, bsz * s), BF16)],
        compiler_params=_cparams(("arbitrary", "arbitrary")),
        name="merge",
    )(o, mc, ga, x, wa, wo, ga1, g2, sc2, sh2)


_CAND = [(a, b) for a in range(PEER_TOPK) for b in range(PEER_TOPK)
         if (a + 1) * (b + 1) <= PEER_TOPK]


def _sort16_network():
    n, pairs, p = PEER_TOPK, [], 1
    while p < n:
        k = p
        while k >= 1:
            for j in range(k % p, n - k, 2 * k):
                for i in range(min(k, n - j - k)):
                    if (i + j) // (2 * p) == (i + j + k) // (2 * p):
                        pairs.append((i + j, i + j + k))
            k //= 2
        p *= 2
    return pairs


_SORT16 = _sort16_network()


def _top16_sorted(problems, store_row):
    sub = 8
    lvs = [[s[g * sub:(g + 1) * sub, :] for g in range(PEER_NKEYS // sub)] for s in problems]
    for a, b in _SORT16:
        for lv in lvs:
            lv[a], lv[b] = jnp.maximum(lv[a], lv[b]), jnp.minimum(lv[a], lv[b])
    for r in range(PEER_TOPK):
        for p, lv in enumerate(lvs):
            m = jnp.max(lv[0], axis=0, keepdims=True)
            store_row(p, r, m)
            eq = lv[0] == m
            for k in range(PEER_TOPK - 1 - r):
                lv[k] = jnp.where(eq, lv[k + 1], lv[k])


def _route_kernel(h2t_ref, wqt_ref, keys_ref, s1_ref, theta_ref, e0_ref, e1_ref,
                  s_scr, v_scr, tau_scr, zi_scr):
    tb = h2t_ref.shape[1]
    ncol = tb // LANES

    def scores(h):
        rows = slice(2 * h * PEER_DH, 2 * (h + 1) * PEER_DH)
        qt = jnp.dot(wqt_ref[rows, :], h2t_ref[...], preferred_element_type=F32).astype(BF16)
        for p in range(2):
            s_scr[2 * h + p] = jnp.dot(keys_ref[2 * h + p], qt[p * PEER_DH:(p + 1) * PEER_DH, :],
                                       preferred_element_type=F32)

    def top16(h):
        for col in range(ncol):
            cs = slice(col * LANES, (col + 1) * LANES)
            rw = slice(h * ncol + col, h * ncol + col + 1)

            def store_row(p, r, m, rw=rw):
                v_scr[p, r, rw, :] = m

            _top16_sorted([s_scr[2 * h + p, :, cs] for p in range(2)], store_row)

    scores(0)
    for h in range(PEER_HEADS):
        if h + 1 < PEER_HEADS:
            scores(h + 1)
        top16(h)

    v0 = [v_scr[0, a] for a in range(PEER_TOPK)]
    v1 = [v_scr[1, b] for b in range(PEER_TOPK)]
    cand = [v0[a] + v1[b] for (a, b) in _CAND]
    top = cand[0]
    work = list(cand)
    for r in range(PEER_TOPK):
        c16 = functools.reduce(jnp.maximum, work)
        work = [jnp.where(w == c16, -jnp.inf, w) for w in work]
    c17 = functools.reduce(jnp.maximum, work)
    z = jnp.zeros_like(top)
    for c in cand:
        z = z + jnp.where(c >= c16, jnp.exp(c - top), 0.0)
    tau_scr[...] = 0.5 * (c16 + c17)
    zi_scr[...] = _INV_SQRT2 / z

    def stage3(it, carry):
        h = it // ncol
        col = it % ncol
        cs = pl.ds(pl.multiple_of(col * LANES, LANES), LANES)
        rw = pl.ds(h * ncol + col, 1)
        s0 = s_scr[2 * h, :, cs]
        s1 = s_scr[2 * h + 1, :, cs]
        theta = jnp.where(s0 >= v_scr[0, PEER_TOPK - 1, rw, :], tau_scr[rw, :] - s0, jnp.inf)
        e0 = jnp.exp(s0 - v_scr[0, 0, rw, :])
        for g in range(PEER_NKEYS // ROWS_PER_STEP):
            rows = slice(g * ROWS_PER_STEP, (g + 1) * ROWS_PER_STEP)
            theta_ref[h, g, :, cs] = theta[rows]
            e0_ref[h, g, :, cs] = e0[rows]
        s1_ref[h, :, cs] = jnp.where(s1 >= v_scr[1, PEER_TOPK - 1, rw, :], s1, -jnp.inf)
        e1_ref[h, :, cs] = jnp.exp(s1 - v_scr[1, 0, rw, :]) * zi_scr[rw, :]
        return carry

    lax.fori_loop(0, PEER_HEADS * ncol, stage3, 0)


def _route(h2t, wqt, keys):
    d, t = h2t.shape
    tb = TB_ROUTE
    ncol = tb // LANES
    n_hp = 2 * PEER_HEADS
    n_grp = PEER_NKEYS // ROWS_PER_STEP
    tab = pl.BlockSpec((PEER_HEADS, PEER_NKEYS, tb), lambda i: (0, 0, i))
    rowtab = pl.BlockSpec((PEER_HEADS, n_grp, ROWS_PER_STEP, tb), lambda i: (0, 0, 0, i))
    tab_shape = jax.ShapeDtypeStruct((PEER_HEADS, PEER_NKEYS, t), F32)
    rowtab_shape = jax.ShapeDtypeStruct((PEER_HEADS, n_grp, ROWS_PER_STEP, t), F32)
    return pl.pallas_call(
        _route_kernel,
        grid=(t // tb,),
        in_specs=[pl.BlockSpec((d, tb), lambda i: (0, i)),
                  pl.BlockSpec(wqt.shape, lambda i: (0, 0)),
                  pl.BlockSpec(keys.shape, lambda i: (0, 0, 0))],
        out_specs=[tab, rowtab, rowtab, tab],
        out_shape=[tab_shape, rowtab_shape, rowtab_shape, tab_shape],
        scratch_shapes=[pltpu.VMEM((n_hp, PEER_NKEYS, tb), F32),
                        pltpu.VMEM((2, PEER_TOPK, PEER_HEADS * ncol, LANES), F32),
                        pltpu.VMEM((PEER_HEADS * ncol, LANES), F32),
                        pltpu.VMEM((PEER_HEADS * ncol, LANES), F32)],
        compiler_params=_cparams(("arbitrary",)),
        name="route",
    )(h2t, wqt, keys)


_INV_SQRT2 = float(1.0 / np.sqrt(2.0))
K_PIECE = 256
K_PIECE_A = 256
ROW_BLOCK = 4
JG_BLOCK = 4


def _experts_kernel(h2t_ref, u_ref, vt_ref, s1_ref, theta_ref, e0_ref, e1_ref,
                    x1_ref, ga2_ref, o_ref, a0_scr, a1_scr, p0_scr, p1_scr, acc_ref):
    c = pl.program_id(1)
    tb = h2t_ref.shape[1]

    @pl.when(c == 0)
    def _():
        acc_ref[...] = jnp.zeros_like(acc_ref)
        p1_scr[...] = jnp.zeros_like(p1_scr)

    sub = (8, LANES)
    n_jg = PEER_NKEYS // sub[0]
    zero = jnp.zeros(sub, F32)

    def pair_body(k, refs, stages):
        a_new, a_old, p_new, p_old = refs
        base = k * MXU_N
        ps = pl.ds(base, MXU_N)

        def mm_a(kc):
            ks = slice(kc * K_PIECE_A, (kc + 1) * K_PIECE_A)
            part = jnp.dot(u_ref[:, ks], h2t_ref[ks, ps], preferred_element_type=F32)
            if kc == 0:
                a_new[:, ps] = part
            else:
                a_new[:, ps] += part

        def mm_acc(kc, mh):
            ks = slice(kc * K_PIECE, (kc + 1) * K_PIECE)
            ms = slice(mh * EC, (mh + 1) * EC)
            acc_ref[ms, ps] += jnp.dot(vt_ref[ms, ks], p_old[ks, ps],
                                       preferred_element_type=F32)

        a_pieces = [functools.partial(mm_a, kc) for kc in range(D_MODEL // K_PIECE_A)]
        acc_pieces = [functools.partial(mm_acc, kc, mh) for kc in range(EC // K_PIECE)
                      for mh in range(D_MODEL // EC)]
        if stages == "fill":
            for piece in a_pieces:
                piece()
            return
        if stages == "drain":
            for piece in acc_pieces:
                piece()
            return
        mm_pieces = a_pieces + acc_pieces
        blocks = [(half, ip, j0) for half in range(MXU_N // LANES)
                  for ip in range(ROWS_PER_STEP // ROW_BLOCK)
                  for j0 in range(0, n_jg, JG_BLOCK)]
        every = len(blocks) // len(mm_pieces)
        for bi, (half, ip, j0) in enumerate(blocks):
            if bi % every == 0 and bi // every < len(mm_pieces):
                mm_pieces[bi // every]()
            cs = pl.ds(base + half * LANES, LANES)
            rows = tuple(range(ROW_BLOCK * ip, ROW_BLOCK * (ip + 1)))
            g = [[None] * JG_BLOCK for _ in rows]
            for h in range(PEER_HEADS):
                th = [jnp.broadcast_to(theta_ref[h, 0, r:r + 1, cs], sub) for r in rows]
                e0 = [jnp.broadcast_to(e0_ref[h, 0, r:r + 1, cs], sub) for r in rows]
                for jg in range(JG_BLOCK):
                    js = slice((j0 + jg) * sub[0], (j0 + jg + 1) * sub[0])
                    s1 = s1_ref[h, js, cs]
                    e1 = e1_ref[h, js, cs]
                    for q in range(ROW_BLOCK):
                        term = jnp.where(s1 >= th[q], e1, zero) * e0[q]
                        g[q][jg] = term if h == 0 else g[q][jg] + term
            for q, r in enumerate(rows):
                for jg in range(0, JG_BLOCK, 2):
                    lo = r * PEER_NKEYS + (j0 + jg) * sub[0]
                    a = a_old[lo:lo + 2 * sub[0], cs]
                    act = a + a * lax.erf(a)
                    gg = jnp.concatenate([g[q][jg], g[q][jg + 1]], axis=0)
                    p_new[lo:lo + 2 * sub[0], cs] = (act * gg).astype(BF16)

    even = (a0_scr, a1_scr, p1_scr, p0_scr)
    odd = (a1_scr, a0_scr, p0_scr, p1_scr)
    last = pl.num_programs(1) - 1

    def run(refs, stages):
        for k in range(tb // MXU_N):
            pair_body(k, refs, stages)

    @pl.when(c == 0)
    def _():
        run(even, "fill")

    @pl.when(jnp.logical_and(c > 0, c % 2 == 0))
    def _():
        run(even, "all")

    @pl.when(jnp.logical_and(c < last, c % 2 == 1))
    def _():
        run(odd, "all")

    @pl.when(c == last)
    def _():
        run(odd, "drain")
        o_ref[...] = x1_ref[...] + ga2_ref[0] * acc_ref[...].T


def _experts(h2t, u, vt, s1, theta, e0, e1, x1, ga2, seq):
    d, t = h2t.shape
    tb = TB_EXP
    n_chunks = u.shape[0] // EC
    last = n_chunks - 1
    per_batch = seq // tb
    once = dict(pipeline_mode=pl.Buffered(1))
    tab = pl.BlockSpec((PEER_HEADS, PEER_NKEYS, tb), lambda i, c: (0, 0, i), **once)
    rowtab = pl.BlockSpec((PEER_HEADS, 1, ROWS_PER_STEP, tb),
                          lambda i, c: (0, jnp.clip(c - 1, 0, last), 0, i))
    return pl.pallas_call(
        _experts_kernel,
        grid=(t // tb, n_chunks + 2),
        in_specs=[pl.BlockSpec((d, tb), lambda i, c: (0, i)),
                  pl.BlockSpec((EC, d), lambda i, c: (jnp.minimum(c, last), 0)),
                  pl.BlockSpec((d, EC), lambda i, c: (0, jnp.clip(c - 2, 0, last))),
                  tab, rowtab, rowtab, tab,
                  pl.BlockSpec((tb, d), lambda i, c: (i, 0), **once),
                  pl.BlockSpec((1, 1, d), lambda i, c: (i // per_batch, 0, 0))],
        out_specs=pl.BlockSpec((tb, d), lambda i, c: (i, 0)),
        out_shape=jax.ShapeDtypeStruct((t, d), F32),
        scratch_shapes=[pltpu.VMEM((EC, tb), F32), pltpu.VMEM((EC, tb), F32),
                        pltpu.VMEM((EC, tb), BF16), pltpu.VMEM((EC, tb), BF16),
                        pltpu.VMEM((d, tb), F32)],
        compiler_params=_cparams(("arbitrary", "arbitrary")),
        name="experts",
    )(h2t, u, vt, s1, theta, e0, e1, x1, ga2)


def kernel(x, c, w_ada, b_ada, norm1_g, norm2_g, w_in, conv_dw, conv_b, conv_ln_g, conv_ln_b,
           w_conv_out, q_norm_g, k_norm_g, rel_bias, w_attn_out, w_out, peer_wq, peer_keys,
           peer_u, peer_v):
    bsz, s, d = x.shape
    depth = w_ada.shape[0]
    bd = jnp.asarray(np.kron(np.eye(N_HEADS), np.full((HEAD_DIM, HEAD_DIM), 1.0 / HEAD_DIM)), BF16)
    for l in range(depth):
        mod, w_in_bf = _ada(c, w_ada[l], b_ada[l], w_in[l])
        sh1, sc1, ga1, sh2, sc2, ga2 = [m.reshape(bsz, 1, d) for m in jnp.split(mod, 6, axis=-1)]
        u, q, k, v, gc, ga, pu_bf, pvt_bf = _inproj(
            x, norm1_g[l].reshape(1, d), sc1, sh1, w_in_bf, bd,
            jnp.tile(q_norm_g[l], N_HEADS).reshape(1, D_ATTN),
            jnp.tile(k_norm_g[l], N_HEADS).reshape(1, D_ATTN), peer_u[l], peer_v[l])
        mc = _conv(u, conv_dw[l], conv_b[l].reshape(1, D_CONV), conv_ln_g[l].reshape(1, D_CONV),
                   conv_ln_b[l].reshape(1, D_CONV), w_conv_out[l].astype(BF16), gc)
        bias, wqt = _bias_table(rel_bias[l], peer_wq[l])
        o = _attn(q, k, v, bias)
        x1, h2t = _merge(o, mc, ga, x, w_attn_out[l].astype(BF16), w_out[l].astype(BF16),
                         ga1, norm2_g[l].reshape(1, d), sc2, sh2)
        keys = peer_keys[l].reshape(2 * PEER_HEADS, PEER_NKEYS, PEER_DH).astype(BF16)
        s1, theta, e0, e1 = _route(h2t, wqt, keys)
        out = _experts(h2t, pu_bf, pvt_bf, s1, theta, e0, e1, x1.reshape(bsz * s, d), ga2, s)
        x = out.reshape(bsz, s, d)
    return x
```

```python
import functools

import jax
import jax.numpy as jnp
import numpy as np
from jax import lax
from jax.experimental import pallas as pl
from jax.experimental.pallas import tpu as pltpu

F32 = jnp.float32
BF16 = jnp.bfloat16

D_MODEL = 1024
CHUNK = 64
N_HEADS = 8
HEAD_DIM = 64
D_ATTN = N_HEADS * HEAD_DIM
LEFT_CHUNKS = 8
REL_CLIP = 128
D_CONV = D_MODEL // 2
CONV_W = 31
PEER_HEADS = 8
PEER_NKEYS = 128
PEER_N = PEER_NKEYS * PEER_NKEYS
PEER_DH = 128
PEER_TOPK = 16
EPS = 1e-6
NEG_INF = -1e30
LOG2E = float(np.log2(np.e))

LANES = 128
SUBLANES = 8
MXU_N = 256
VMEM_LIMIT = 56 * 1024 * 1024

TM_IN = 512
TS_CONV = 512
HALO = 32
TQ = 256
NKB = 3
TM_MERGE = 512
TB_ROUTE = 512
TB_EXP = 1024
EC = 512
ROWS_PER_STEP = EC // PEER_NKEYS

def _cparams(sem):
    return pltpu.CompilerParams(dimension_semantics=sem, vmem_limit_bytes=VMEM_LIMIT)


def _ada_kernel(c_ref, w_ref, b_ref, win_ref, o_ref, winb_ref):
    c = c_ref[...]
    cond = c * jax.nn.sigmoid(c)
    o_ref[...] = jnp.dot(cond.astype(BF16), w_ref[...].astype(BF16),
                         preferred_element_type=F32) + b_ref[...]
    winb_ref[...] = win_ref[...].astype(BF16)


def _ada(c, w, b, w_in):
    bsz, d = c.shape
    n = w.shape[1]
    tn = 1024
    steps = n // tn
    n_in = w_in.shape[1]
    tin = n_in // steps
    return pl.pallas_call(
        _ada_kernel,
        grid=(steps,),
        in_specs=[pl.BlockSpec((bsz, d), lambda j: (0, 0)),
                  pl.BlockSpec((d, tn), lambda j: (0, j)),
                  pl.BlockSpec((1, tn), lambda j: (0, j)),
                  pl.BlockSpec((d, tin), lambda j: (0, j))],
        out_specs=[pl.BlockSpec((bsz, tn), lambda j: (0, j)),
                   pl.BlockSpec((d, tin), lambda j: (0, j))],
        out_shape=[jax.ShapeDtypeStruct((bsz, n), F32),
                   jax.ShapeDtypeStruct(w_in.shape, BF16)],
        compiler_params=_cparams(("arbitrary",)),
        name="ada",
    )(c, w, b.reshape(1, n), w_in)


def _head_rms(t, bd_ref, gain):
    t2 = t * t
    hi = t2.astype(BF16)
    lo = (t2 - hi.astype(F32)).astype(BF16)
    ms = (jnp.dot(hi, bd_ref[...], preferred_element_type=F32)
          + jnp.dot(lo, bd_ref[...], preferred_element_type=F32))
    return t * lax.rsqrt(ms + EPS) * gain


def _inproj_kernel(x_ref, g_ref, sc_ref, sh_ref, w_ref, bd_ref, qg_ref, kg_ref, pu_ref, pv_ref,
                   u_ref, q_ref, k_ref, v_ref, gc_ref, ga_ref, pub_ref, pvt_ref):
    x = x_ref[0]
    ms = jnp.mean(x * x, axis=-1, keepdims=True)
    h = (x * lax.rsqrt(ms + EPS)) * g_ref[...]
    h = h * (1.0 + sc_ref[0]) + sh_ref[0]
    hb = h.astype(BF16)

    def seg(lo, hi):
        return jnp.dot(hb, w_ref[:, lo:hi], preferred_element_type=F32)

    bounds = np.cumsum([0, D_CONV, D_CONV, D_ATTN, D_ATTN, D_ATTN, D_MODEL, D_MODEL])
    lo, hi = bounds[:-1], bounds[1:]
    a = seg(lo[0], hi[0])
    b = seg(lo[1], hi[1])
    q = seg(lo[2], hi[2])
    u_ref[0] = a * jax.nn.sigmoid(b)
    k = seg(lo[3], hi[3])
    q_ref[0] = (_head_rms(q, bd_ref, qg_ref[...]) * (HEAD_DIM ** -0.5 * LOG2E)).astype(BF16)
    v = seg(lo[4], hi[4])
    k_ref[0] = _head_rms(k, bd_ref, kg_ref[...]).astype(BF16)
    gc = seg(lo[5], hi[5])
    v_ref[0] = v.astype(BF16)
    pub_ref[...] = (pu_ref[...] * _INV_SQRT2).astype(BF16)
    ga = seg(lo[6], hi[6])
    pvt_ref[...] = pv_ref[...].T.astype(BF16)
    gc_ref[0] = jax.nn.sigmoid(gc)
    ga_ref[0] = jax.nn.sigmoid(ga)


def _inproj(x, g1, sc1, sh1, w_in, bd, qg, kg, pu, pv):
    bsz, s, d = x.shape
    tm = TM_IN
    n_in = w_in.shape[1]
    nb = s // tm
    n_e = pu.shape[0]
    er = n_e // (bsz * nb)
    row = lambda w: pl.BlockSpec((1, tm, w), lambda b, i: (b, i, 0))
    vec = lambda w: pl.BlockSpec((1, w), lambda b, i: (0, 0))
    mod = pl.BlockSpec((1, 1, d), lambda b, i: (b, 0, 0))
    erow = pl.BlockSpec((er, d), lambda b, i: (b * nb + i, 0))
    return pl.pallas_call(
        _inproj_kernel,
        grid=(bsz, nb),
        in_specs=[row(d), vec(d), mod, mod,
                  pl.BlockSpec((d, n_in), lambda b, i: (0, 0), pipeline_mode=pl.Buffered(1)),
                  pl.BlockSpec((D_ATTN, D_ATTN), lambda b, i: (0, 0)),
                  vec(D_ATTN), vec(D_ATTN), erow, erow],
        out_specs=[row(D_CONV), row(D_ATTN), row(D_ATTN), row(D_ATTN), row(d), row(d),
                   erow, pl.BlockSpec((d, er), lambda b, i: (0, b * nb + i))],
        out_shape=[jax.ShapeDtypeStruct((bsz, s, D_CONV), F32),
                   jax.ShapeDtypeStruct((bsz, s, D_ATTN), BF16),
                   jax.ShapeDtypeStruct((bsz, s, D_ATTN), BF16),
                   jax.ShapeDtypeStruct((bsz, s, D_ATTN), BF16),
                   jax.ShapeDtypeStruct((bsz, s, d), F32),
                   jax.ShapeDtypeStruct((bsz, s, d), F32),
                   jax.ShapeDtypeStruct((n_e, d), BF16),
                   jax.ShapeDtypeStruct((d, n_e), BF16)],
        compiler_params=_cparams(("arbitrary", "arbitrary")),
        name="inproj",
    )(x, g1, sc1, sh1, w_in, bd, qg, kg, pu, pv)


CONV_ROWS = 64


def _conv_kernel(u_ref, up_ref, dw_ref, cb_ref, lg_ref, lb_ref, w_ref, gc_ref,
                 o_ref, ext_ref, y_ref):
    i = pl.program_id(1)
    ts = u_ref.shape[1]
    prev = up_ref[0, ts - HALO:, :]
    ext_ref[0, 0:HALO, :] = jnp.where(i > 0, prev, 0.0)
    ext_ref[0, HALO:, :] = u_ref[0]
    n_sh = ts + HALO - SUBLANES
    for s in range(1, SUBLANES):
        for r0 in range(0, n_sh, CONV_ROWS):
            n = min(CONV_ROWS, n_sh - r0)
            ext_ref[s, r0:r0 + n, :] = ext_ref[0, r0 + s:r0 + s + n, :]
    base = HALO - (CONV_W - 1)
    for r0 in range(0, ts, CONV_ROWS):
        acc = jnp.zeros((CONV_ROWS, D_CONV), F32) + cb_ref[...]
        for w in range(CONV_W):
            s = (base + w) % SUBLANES
            a = r0 + base + w - s
            acc = acc + ext_ref[s, a:a + CONV_ROWS, :] * dw_ref[w:w + 1, :]
        y_ref[r0:r0 + CONV_ROWS, :] = acc
    y = y_ref[...]
    mu = jnp.mean(y, axis=-1, keepdims=True)
    yc = y - mu
    var = jnp.mean(yc * yc, axis=-1, keepdims=True)
    z = yc * lax.rsqrt(var + EPS) * lg_ref[...] + lb_ref[...]
    z = z * jax.nn.sigmoid(z)
    o = jnp.dot(z.astype(BF16), w_ref[...], preferred_element_type=F32)
    o_ref[0] = gc_ref[0] * o


def _conv(u, dw, cb, lg, lb, w_co, gc):
    bsz, s, dc = u.shape
    d = w_co.shape[1]
    ts = TS_CONV
    vec = lambda w: pl.BlockSpec((1, w), lambda b, i: (0, 0))
    return pl.pallas_call(
        _conv_kernel,
        grid=(bsz, s // ts),
        in_specs=[pl.BlockSpec((1, ts, dc), lambda b, i: (b, i, 0)),
                  pl.BlockSpec((1, ts, dc), lambda b, i: (b, jnp.maximum(i - 1, 0), 0)),
                  pl.BlockSpec((CONV_W, dc), lambda b, i: (0, 0)),
                  vec(dc), vec(dc), vec(dc),
                  pl.BlockSpec((dc, d), lambda b, i: (0, 0)),
                  pl.BlockSpec((1, ts, d), lambda b, i: (b, i, 0))],
        out_specs=pl.BlockSpec((1, ts, d), lambda b, i: (b, i, 0)),
        out_shape=jax.ShapeDtypeStruct((bsz, s, d), F32),
        scratch_shapes=[pltpu.VMEM((SUBLANES, ts + HALO, dc), F32), pltpu.VMEM((ts, dc), F32)],
        compiler_params=_cparams(("arbitrary", "arbitrary")),
        name="conv",
    )(u, u, dw, cb, lg, lb, w_co, gc)


def _attn_kernel(q_ref, k0_ref, k1_ref, k2_ref, v0_ref, v1_ref, v2_ref, bias_ref, o_ref):
    i = pl.program_id(1)
    k_refs = (k0_ref, k1_ref, k2_ref)
    v_refs = (v0_ref, v1_ref, v2_ref)
    low = lax.broadcasted_iota(jnp.int32, (TQ, LANES), 1) < HEAD_DIM

    n_heads_per_slab = LANES // HEAD_DIM

    def body(pens):
        def scores(h):
            lo = (h // n_heads_per_slab) * LANES
            q2 = q_ref[0, :, lo:lo + LANES]
            keep = low if h % n_heads_per_slab == 0 else jnp.logical_not(low)
            qh = jnp.where(keep, q2, jnp.zeros_like(q2))
            ss = []
            for j in range(NKB):
                s = lax.dot_general(qh, k_refs[j][0, :, lo:lo + LANES],
                                    (((1,), (1,)), ((), ())), preferred_element_type=F32)
                s = s + bias_ref[h, :, j * TQ:(j + 1) * TQ]
                ss.append(s if pens is None else s + pens[j])
            return ss

        def softmax_pv(h, ss):
            lo = (h // n_heads_per_slab) * LANES
            m = jnp.maximum(jnp.maximum(jnp.max(ss[0], axis=-1, keepdims=True),
                                        jnp.max(ss[1], axis=-1, keepdims=True)),
                            jnp.max(ss[2], axis=-1, keepdims=True))
            ps = [jnp.exp2(ss[j] - m) for j in range(NKB)]
            l = jnp.sum(ps[0] + ps[1] + ps[2], axis=-1, keepdims=True)
            p = jnp.concatenate([pj.astype(BF16) for pj in ps], axis=1)
            v = jnp.concatenate([v_refs[j][0, :, lo:lo + LANES] for j in range(NKB)], axis=0)
            return jnp.dot(p, v, preferred_element_type=F32) / l

        outs = []
        ss_next = scores(0)
        for h in range(N_HEADS):
            ss = ss_next
            if h + 1 < N_HEADS:
                ss_next = scores(h + 1)
            outs.append(softmax_pv(h, ss))
            if h % n_heads_per_slab == n_heads_per_slab - 1:
                lo = (h // n_heads_per_slab) * LANES
                o_ref[0, :, lo:lo + LANES] = jnp.where(low, outs[0], outs[1]).astype(BF16)
                outs = []

    @pl.when(i >= NKB - 1)
    def _():
        body(None)

    @pl.when(i < NKB - 1)
    def _():
        body([jnp.where(i - (NKB - 1) + j >= 0, 0.0, NEG_INF).astype(F32) for j in range(NKB)])


def _attn(q, k, v, bias):
    bsz, s, da = q.shape
    kspec = lambda j: pl.BlockSpec(
        (1, TQ, da), lambda b, i: (b, jnp.maximum(i - (NKB - 1) + j, 0), 0))
    return pl.pallas_call(
        _attn_kernel,
        grid=(bsz, s // TQ),
        in_specs=[pl.BlockSpec((1, TQ, da), lambda b, i: (b, i, 0)),
                  kspec(0), kspec(1), kspec(2), kspec(0), kspec(1), kspec(2),
                  pl.BlockSpec((N_HEADS, TQ, NKB * TQ), lambda b, i: (0, 0, 0))],
        out_specs=pl.BlockSpec((1, TQ, da), lambda b, i: (b, i, 0)),
        out_shape=jax.ShapeDtypeStruct((bsz, s, da), BF16),
        compiler_params=_cparams(("arbitrary", "arbitrary")),
        name="attn",
    )(q, k, k, k, v, v, v, bias)


def _bias_table(rel_bias, peer_wq):
    nk = NKB * TQ
    lw = TQ + nk - 1
    n_lo = (TQ - 1) - REL_CLIP
    n_hi = (nk - 1) - REL_CLIP
    w = jnp.concatenate([jnp.repeat(rel_bias[:, :1], n_lo, axis=1), rel_bias,
                         jnp.repeat(rel_bias[:, -1:], n_hi, axis=1)], axis=1).astype(F32)
    assert BIAS_ROW >= lw
    base = jnp.concatenate([w[:, nk - 1::-1], jnp.zeros((w.shape[0], BIAS_ROW - lw), F32),
                            w[:, lw - 1:nk - 1:-1]], axis=1) * LOG2E
    return pl.pallas_call(
        _bias_kernel,
        out_shape=[jax.ShapeDtypeStruct((N_HEADS, TQ, nk), F32),
                   jax.ShapeDtypeStruct(peer_wq.shape[::-1], BF16)],
        compiler_params=pltpu.CompilerParams(vmem_limit_bytes=VMEM_LIMIT),
        name="bias",
    )(base, peer_wq)


BIAS_ROW = 1024


def _bias_kernel(base_ref, wq_ref, o_ref, wqt_ref):
    wqt_ref[...] = wq_ref[...].T.astype(BF16)
    nk = NKB * TQ
    qi = lax.broadcasted_iota(jnp.int32, (TQ, nk), 0)
    kj = lax.broadcasted_iota(jnp.int32, (TQ, nk), 1)
    qc = qi // CHUNK + (NKB - 1) * TQ // CHUNK
    kc = kj // CHUNK
    band = (kc >= qc - LEFT_CHUNKS) & (kc <= qc)
    for h in range(N_HEADS):
        x = jnp.broadcast_to(base_ref[h:h + 1, :], (TQ, BIAS_ROW))
        y = pltpu.roll(x, 0, 1, stride=1, stride_axis=0)
        o_ref[h] = jnp.where(band, y[:, :nk], NEG_INF)


def _merge_kernel(o_ref, mc_ref, ga_ref, x_ref, wa_ref, wo_ref, ga1_ref, g2_ref, sc_ref, sh_ref,
                  x1_ref, h2t_ref):
    ya = jnp.dot(o_ref[0], wa_ref[...], preferred_element_type=F32)
    merged = mc_ref[0] + ga_ref[0] * ya
    y = jnp.dot(merged.astype(BF16), wo_ref[...], preferred_element_type=F32)
    x1 = x_ref[0] + ga1_ref[0] * y
    x1_ref[0] = x1
    ms = jnp.mean(x1 * x1, axis=-1, keepdims=True)
    h2 = (x1 * lax.rsqrt(ms + EPS)) * g2_ref[...]
    h2 = h2 * (1.0 + sc_ref[0]) + sh_ref[0]
    h2t_ref[...] = h2.T.astype(BF16)


def _merge(o, mc, ga, x, wa, wo, ga1, g2, sc2, sh2):
    bsz, s, d = x.shape
    tm = TM_MERGE
    nb = s // tm
    row = lambda w: pl.BlockSpec((1, tm, w), lambda b, i: (b, i, 0))
    mod = pl.BlockSpec((1, 1, d), lambda b, i: (b, 0, 0))
    return pl.pallas_call(
        _merge_kernel,
        grid=(bsz, nb),
        in_specs=[row(D_ATTN), row(d), row(d), row(d),
                  pl.BlockSpec((D_ATTN, d), lambda b, i: (0, 0)),
                  pl.BlockSpec((d, d), lambda b, i: (0, 0)),
                  mod, pl.BlockSpec((1, d), lambda b, i: (0, 0)), mod, mod],
        out_specs=[row(d), pl.BlockSpec((d, tm), lambda b, i: (0, b * nb + i))],
        out_shape=[jax.ShapeDtypeStruct((bsz, s, d), F32),
                   jax.ShapeDtypeStruct((d, bsz * s), BF16)],
        compiler_params=_cparams(("arbitrary", "arbitrary")),
        name="merge",
    )(o, mc, ga, x, wa, wo, ga1, g2, sc2, sh2)


_CAND = [(a, b) for a in range(PEER_TOPK) for b in range(PEER_TOPK)
         if (a + 1) * (b + 1) <= PEER_TOPK]


def _sort16_network():
    n, pairs, p = PEER_TOPK, [], 1
    while p < n:
        k = p
        while k >= 1:
            for j in range(k % p, n - k, 2 * k):
                for i in range(min(k, n - j - k)):
                    if (i + j) // (2 * p) == (i + j + k) // (2 * p):
                        pairs.append((i + j, i + j + k))
            k //= 2
        p *= 2
    return pairs


_SORT16 = _sort16_network()


def _top16_sorted(problems, store_row):
    sub = 8
    lvs = [[s[g * sub:(g + 1) * sub, :] for g in range(PEER_NKEYS // sub)] for s in problems]
    for a, b in _SORT16:
        for lv in lvs:
            lv[a], lv[b] = jnp.maximum(lv[a], lv[b]), jnp.minimum(lv[a], lv[b])
    for r in range(PEER_TOPK):
        for p, lv in enumerate(lvs):
            m = jnp.max(lv[0], axis=0, keepdims=True)
            store_row(p, r, m)
            eq = lv[0] == m
            for k in range(PEER_TOPK - 1 - r):
                lv[k] = jnp.where(eq, lv[k + 1], lv[k])


def _route_kernel(h2t_ref, wqt_ref, keys_ref, s1_ref, theta_ref, e0_ref, e1_ref,
                  s_scr, v_scr, tau_scr, zi_scr):
    tb = h2t_ref.shape[1]
    ncol = tb // LANES

    def scores(h):
        rows = slice(2 * h * PEER_DH, 2 * (h + 1) * PEER_DH)
        qt = jnp.dot(wqt_ref[rows, :], h2t_ref[...], preferred_element_type=F32).astype(BF16)
        for p in range(2):
            s_scr[2 * h + p] = jnp.dot(keys_ref[2 * h + p], qt[p * PEER_DH:(p + 1) * PEER_DH, :],
                                       preferred_element_type=F32)

    def top16(h):
        for col in range(ncol):
            cs = slice(col * LANES, (col + 1) * LANES)
            rw = slice(h * ncol + col, h * ncol + col + 1)

            def store_row(p, r, m, rw=rw):
                v_scr[p, r, rw, :] = m

            _top16_sorted([s_scr[2 * h + p, :, cs] for p in range(2)], store_row)

    scores(0)
    for h in range(PEER_HEADS):
        if h + 1 < PEER_HEADS:
            scores(h + 1)
        top16(h)

    v0 = [v_scr[0, a] for a in range(PEER_TOPK)]
    v1 = [v_scr[1, b] for b in range(PEER_TOPK)]
    cand = [v0[a] + v1[b] for (a, b) in _CAND]
    top = cand[0]
    work = list(cand)
    for r in range(PEER_TOPK):
        live = [n for n, (a, b) in enumerate(_CAND) if (a + 1) * (b + 1) <= r + 1]
        c16 = functools.reduce(jnp.maximum, [work[n] for n in live])
        for n in live:
            work[n] = jnp.where(work[n] == c16, -jnp.inf, work[n])
    c17 = functools.reduce(jnp.maximum, work)
    z = jnp.zeros_like(top)
    for c in cand:
        z = z + jnp.where(c >= c16, jnp.exp(c - top), 0.0)
    tau_scr[...] = 0.5 * (c16 + c17)
    zi_scr[...] = _INV_SQRT2 / z

    def stage3(it, carry):
        h = it // ncol
        col = it % ncol
        cs = pl.ds(pl.multiple_of(col * LANES, LANES), LANES)
        rw = pl.ds(h * ncol + col, 1)
        s0 = s_scr[2 * h, :, cs]
        s1 = s_scr[2 * h + 1, :, cs]
        theta = jnp.where(s0 >= v_scr[0, PEER_TOPK - 1, rw, :], tau_scr[rw, :] - s0, jnp.inf)
        e0 = jnp.exp(s0 - v_scr[0, 0, rw, :])
        for g in range(PEER_NKEYS // ROWS_PER_STEP):
            rows = slice(g * ROWS_PER_STEP, (g + 1) * ROWS_PER_STEP)
            theta_ref[h, g, :, cs] = theta[rows]
            e0_ref[h, g, :, cs] = e0[rows]
        s1_ref[h, :, cs] = jnp.where(s1 >= v_scr[1, PEER_TOPK - 1, rw, :], s1, -jnp.inf)
        e1_ref[h, :, cs] = jnp.exp(s1 - v_scr[1, 0, rw, :]) * zi_scr[rw, :]
        return carry

    lax.fori_loop(0, PEER_HEADS * ncol, stage3, 0)


def _route(h2t, wqt, keys):
    d, t = h2t.shape
    tb = TB_ROUTE
    ncol = tb // LANES
    n_hp = 2 * PEER_HEADS
    n_grp = PEER_NKEYS // ROWS_PER_STEP
    tab = pl.BlockSpec((PEER_HEADS, PEER_NKEYS, tb), lambda i: (0, 0, i))
    rowtab = pl.BlockSpec((PEER_HEADS, n_grp, ROWS_PER_STEP, tb), lambda i: (0, 0, 0, i))
    tab_shape = jax.ShapeDtypeStruct((PEER_HEADS, PEER_NKEYS, t), F32)
    rowtab_shape = jax.ShapeDtypeStruct((PEER_HEADS, n_grp, ROWS_PER_STEP, t), F32)
    return pl.pallas_call(
        _route_kernel,
        grid=(t // tb,),
        in_specs=[pl.BlockSpec((d, tb), lambda i: (0, i)),
                  pl.BlockSpec(wqt.shape, lambda i: (0, 0)),
                  pl.BlockSpec(keys.shape, lambda i: (0, 0, 0))],
        out_specs=[tab, rowtab, rowtab, tab],
        out_shape=[tab_shape, rowtab_shape, rowtab_shape, tab_shape],
        scratch_shapes=[pltpu.VMEM((n_hp, PEER_NKEYS, tb), F32),
                        pltpu.VMEM((2, PEER_TOPK, PEER_HEADS * ncol, LANES), F32),
                        pltpu.VMEM((PEER_HEADS * ncol, LANES), F32),
                        pltpu.VMEM((PEER_HEADS * ncol, LANES), F32)],
        compiler_params=_cparams(("arbitrary",)),
        name="route",
    )(h2t, wqt, keys)


_INV_SQRT2 = float(1.0 / np.sqrt(2.0))
K_PIECE = 256
K_PIECE_A = 256
ROW_BLOCK = 4
JG_BLOCK = 4


def _experts_kernel(h2t_ref, u_ref, vt_ref, s1_ref, theta_ref, e0_ref, e1_ref,
                    x1_ref, ga2_ref, o_ref, a0_scr, a1_scr, p0_scr, p1_scr, acc_ref):
    c = pl.program_id(1)
    tb = h2t_ref.shape[1]

    @pl.when(c == 0)
    def _():
        acc_ref[...] = jnp.zeros_like(acc_ref)
        p1_scr[...] = jnp.zeros_like(p1_scr)

    sub = (8, LANES)
    n_jg = PEER_NKEYS // sub[0]
    zero = jnp.zeros(sub, F32)

    def pair_body(k, refs, stages):
        a_new, a_old, p_new, p_old = refs
        base = k * MXU_N
        ps = pl.ds(base, MXU_N)

        def mm_a(kc):
            ks = slice(kc * K_PIECE_A, (kc + 1) * K_PIECE_A)
            part = jnp.dot(u_ref[:, ks], h2t_ref[ks, ps], preferred_element_type=F32)
            if kc == 0:
                a_new[:, ps] = part
            else:
                a_new[:, ps] += part

        def mm_acc(kc, mh):
            ks = slice(kc * K_PIECE, (kc + 1) * K_PIECE)
            ms = slice(mh * EC, (mh + 1) * EC)
            acc_ref[ms, ps] += jnp.dot(vt_ref[ms, ks], p_old[ks, ps],
                                       preferred_element_type=F32)

        a_pieces = [functools.partial(mm_a, kc) for kc in range(D_MODEL // K_PIECE_A)]
        acc_pieces = [functools.partial(mm_acc, kc, mh) for kc in range(EC // K_PIECE)
                      for mh in range(D_MODEL // EC)]
        if stages == "fill":
            for piece in a_pieces:
                piece()
            return
        if stages == "drain":
            for piece in acc_pieces:
                piece()
            return
        mm_pieces = a_pieces + acc_pieces
        blocks = [(half, ip, j0) for half in range(MXU_N // LANES)
                  for ip in range(ROWS_PER_STEP // ROW_BLOCK)
                  for j0 in range(0, n_jg, JG_BLOCK)]
        every = len(blocks) // len(mm_pieces)
        for bi, (half, ip, j0) in enumerate(blocks):
            if bi % every == 0 and bi // every < len(mm_pieces):
                mm_pieces[bi // every]()
            cs = pl.ds(base + half * LANES, LANES)
            rows = tuple(range(ROW_BLOCK * ip, ROW_BLOCK * (ip + 1)))
            g = [[zero] * JG_BLOCK for _ in rows]
            for h in range(PEER_HEADS):
                th = [jnp.broadcast_to(theta_ref[h, 0, r:r + 1, cs], sub) for r in rows]
                e0 = [jnp.broadcast_to(e0_ref[h, 0, r:r + 1, cs], sub) for r in rows]
                for jg in range(JG_BLOCK):
                    js = slice((j0 + jg) * sub[0], (j0 + jg + 1) * sub[0])
                    s1 = s1_ref[h, js, cs]
                    e1 = e1_ref[h, js, cs]
                    for q in range(ROW_BLOCK):
                        g[q][jg] = g[q][jg] + jnp.where(s1 >= th[q], e1, zero) * e0[q]
            for q, r in enumerate(rows):
                for jg in range(0, JG_BLOCK, 2):
                    lo = r * PEER_NKEYS + (j0 + jg) * sub[0]
                    a = a_old[lo:lo + 2 * sub[0], cs]
                    act = a + a * lax.erf(a)
                    gg = jnp.concatenate([g[q][jg], g[q][jg + 1]], axis=0)
                    p_new[lo:lo + 2 * sub[0], cs] = (act * gg).astype(BF16)

    even = (a0_scr, a1_scr, p1_scr, p0_scr)
    odd = (a1_scr, a0_scr, p0_scr, p1_scr)
    last = pl.num_programs(1) - 1

    def run(refs, stages):
        for k in range(tb // MXU_N):
            pair_body(k, refs, stages)

    @pl.when(c == 0)
    def _():
        run(even, "fill")

    @pl.when(jnp.logical_and(c > 0, c % 2 == 0))
    def _():
        run(even, "all")

    @pl.when(jnp.logical_and(c < last, c % 2 == 1))
    def _():
        run(odd, "all")

    @pl.when(c == last)
    def _():
        run(odd, "drain")
        o_ref[...] = x1_ref[...] + ga2_ref[0] * acc_ref[...].T


def _experts(h2t, u, vt, s1, theta, e0, e1, x1, ga2, seq):
    d, t = h2t.shape
    tb = TB_EXP
    n_chunks = u.shape[0] // EC
    last = n_chunks - 1
    per_batch = seq // tb
    once = dict(pipeline_mode=pl.Buffered(1))
    tab = pl.BlockSpec((PEER_HEADS, PEER_NKEYS, tb), lambda i, c: (0, 0, i), **once)
    rowtab = pl.BlockSpec((PEER_HEADS, 1, ROWS_PER_STEP, tb),
                          lambda i, c: (0, jnp.clip(c - 1, 0, last), 0, i))
    return pl.pallas_call(
        _experts_kernel,
        grid=(t // tb, n_chunks + 2),
        in_specs=[pl.BlockSpec((d, tb), lambda i, c: (0, i)),
                  pl.BlockSpec((EC, d), lambda i, c: (jnp.minimum(c, last), 0)),
                  pl.BlockSpec((d, EC), lambda i, c: (0, jnp.clip(c - 2, 0, last))),
                  tab, rowtab, rowtab, tab,
                  pl.BlockSpec((tb, d), lambda i, c: (i, 0), **once),
                  pl.BlockSpec((1, 1, d), lambda i, c: (i // per_batch, 0, 0))],
        out_specs=pl.BlockSpec((tb, d), lambda i, c: (i, 0)),
        out_shape=jax.ShapeDtypeStruct((t, d), F32),
        scratch_shapes=[pltpu.VMEM((EC, tb), F32), pltpu.VMEM((EC, tb), F32),
                        pltpu.VMEM((EC, tb), BF16), pltpu.VMEM((EC, tb), BF16),
                        pltpu.VMEM((d, tb), F32)],
        compiler_params=_cparams(("arbitrary", "arbitrary")),
        name="experts",
    )(h2t, u, vt, s1, theta, e0, e1, x1, ga2)


def kernel(x, c, w_ada, b_ada, norm1_g, norm2_g, w_in, conv_dw, conv_b, conv_ln_g, conv_ln_b,
           w_conv_out, q_norm_g, k_norm_g, rel_bias, w_attn_out, w_out, peer_wq, peer_keys,
           peer_u, peer_v):
    bsz, s, d = x.shape
    depth = w_ada.shape[0]
    bd = jnp.asarray(np.kron(np.eye(N_HEADS), np.full((HEAD_DIM, HEAD_DIM), 1.0 / HEAD_DIM)), BF16)
    for l in range(depth):
        mod, w_in_bf = _ada(c, w_ada[l], b_ada[l], w_in[l])
        sh1, sc1, ga1, sh2, sc2, ga2 = [m.reshape(bsz, 1, d) for m in jnp.split(mod, 6, axis=-1)]
        u, q, k, v, gc, ga, pu_bf, pvt_bf = _inproj(
            x, norm1_g[l].reshape(1, d), sc1, sh1, w_in_bf, bd,
            jnp.tile(q_norm_g[l], N_HEADS).reshape(1, D_ATTN),
            jnp.tile(k_norm_g[l], N_HEADS).reshape(1, D_ATTN), peer_u[l], peer_v[l])
        mc = _conv(u, conv_dw[l], conv_b[l].reshape(1, D_CONV), conv_ln_g[l].reshape(1, D_CONV),
                   conv_ln_b[l].reshape(1, D_CONV), w_conv_out[l].astype(BF16), gc)
        bias, wqt = _bias_table(rel_bias[l], peer_wq[l])
        o = _attn(q, k, v, bias)
        x1, h2t = _merge(o, mc, ga, x, w_attn_out[l].astype(BF16), w_out[l].astype(BF16),
                         ga1, norm2_g[l].reshape(1, d), sc2, sh2)
        keys = peer_keys[l].reshape(2 * PEER_HEADS, PEER_NKEYS, PEER_DH).astype(BF16)
        s1, theta, e0, e1 = _route(h2t, wqt, keys)
        out = _experts(h2t, pu_bf, pvt_bf, s1, theta, e0, e1, x1.reshape(bsz * s, d), ga2, s)
        x = out.reshape(bsz, s, d)
    return x
```
